```python
import math
import jax, jax.numpy as jnp
from jax import lax
import numpy as np

D_MODEL = 1024
BATCH = 16
SEQ = 256
DEPTH = 2
DEC_BATCH = 8
DEC_SEQ = 1024
PAST_LEN = 512

GRID_W = 64
N_MIXERS = 2
N_ATTN_LAYERS = (DEPTH + 1) // 2
N_RET_LAYERS = DEPTH // 2
N_HEADS = 8
N_KV_HEADS = 2
HEAD_DIM = D_MODEL // N_HEADS
GROUP = N_HEADS // N_KV_HEADS
ROPE_BASE = 10000.0
Q_BLOCK = 128
RET_HEADS = 4
RET_DK = D_MODEL // RET_HEADS
RET_DV = D_MODEL // RET_HEADS
RET_CHUNK = 128
D_FF = 4 * D_MODEL
EPS = 1e-6

kernel_name = 'hybrid_flow_gqa_retention_prefix_step'


def rms_norm(x, g):
    x32 = x.astype(jnp.float32)
    y = x32 * lax.rsqrt(jnp.mean(x32 * x32, axis=-1, keepdims=True) + EPS)
    return (y * g.astype(jnp.float32)).astype(x.dtype)


def adaln_params(cond, w, b):
    m = (jax.nn.silu(cond) @ w + b)[:, None, :]
    return jnp.split(m, 6, axis=-1)


def modulate(h, shift, scale):
    return h * (1.0 + scale) + shift


def axial_rope(x):
    n, d = x.shape[1], x.shape[-1]
    quarter = d // 4
    n_rows = n // GRID_W
    rows = jnp.broadcast_to(jnp.arange(n_rows)[:, None], (n_rows, GRID_W)).reshape(-1)
    cols = jnp.broadcast_to(jnp.arange(GRID_W)[None, :], (n_rows, GRID_W)).reshape(-1)
    freqs = ROPE_BASE ** (-jnp.arange(quarter, dtype=jnp.float32) / quarter)
    ang = jnp.stack([rows.astype(jnp.float32)[:, None] * freqs,
                     cols.astype(jnp.float32)[:, None] * freqs], axis=1)
    cos = jnp.cos(ang)[None, :, None]
    sin = jnp.sin(ang)[None, :, None]
    xs = x.astype(jnp.float32).reshape(x.shape[:-1] + (2, 2, quarter))
    x1 = xs[..., 0, :]
    x2 = xs[..., 1, :]
    out = jnp.stack([x1 * cos - x2 * sin, x2 * cos + x1 * sin], axis=-2)
    return out.reshape(x.shape).astype(x.dtype)


def attn_qkv(h, w_qkv, q_norm, k_norm, use_rope):
    b, n, _ = h.shape
    qkv = h @ w_qkv
    nq = N_HEADS * HEAD_DIM
    nk = N_KV_HEADS * HEAD_DIM
    q = qkv[..., :nq].reshape(b, n, N_HEADS, HEAD_DIM)
    k = qkv[..., nq:nq + nk].reshape(b, n, N_KV_HEADS, HEAD_DIM)
    v = qkv[..., nq + nk:].reshape(b, n, N_KV_HEADS, HEAD_DIM)
    q = rms_norm(q, q_norm)
    k = rms_norm(k, k_norm)
    if use_rope:
        q = axial_rope(q)
        k = axial_rope(k)
    return q, k, v


def block_attention(q, k, v):
    b, nq = q.shape[:2]
    nb = nq // Q_BLOCK
    qb = q.reshape(b, nb, Q_BLOCK, N_KV_HEADS, GROUP, HEAD_DIM).transpose(1, 0, 2, 3, 4, 5)
    scale = HEAD_DIM ** -0.5
    k32 = k.astype(jnp.float32)
    v32 = v.astype(jnp.float32)

    def one_block(qi):
        s = jnp.einsum('bqkgd,bskd->bkgqs', qi.astype(jnp.float32), k32) * scale
        p = jax.nn.softmax(s, axis=-1)
        return jnp.einsum('bkgqs,bskd->bqkgd', p, v32).astype(q.dtype)

    o = lax.map(one_block, qb)
    return o.transpose(1, 0, 2, 3, 4, 5).reshape(b, nq, N_HEADS * HEAD_DIM)


def retention_chunkwise(q, k, v, log_gamma, s0, strict):
    b, h, n, dk = q.shape
    dv = v.shape[-1]
    nc = n // RET_CHUNK
    idx = jnp.arange(RET_CHUNK, dtype=jnp.float32)
    diff = idx[:, None] - idx[None, :]
    mask = (diff > 0) if strict else (diff >= 0)
    lg = log_gamma.astype(jnp.float32)
    decay_mask = jnp.where(mask[None], jnp.exp(lg[:, None, None] * jnp.where(mask, diff, 0.0)[None]), 0.0)
    q_decay = jnp.exp(lg[:, None] * (idx + 1.0)[None])
    k_decay = jnp.exp(lg[:, None] * (RET_CHUNK - 1.0 - idx)[None])
    chunk_decay = jnp.exp(lg * RET_CHUNK)

    def to_chunks(x):
        return x.astype(jnp.float32).reshape(b, h, nc, RET_CHUNK, x.shape[-1]).transpose(2, 0, 1, 3, 4)

    def step(s, inp):
        qc, kc, vc = inp
        scores = jnp.einsum('bhid,bhjd->bhij', qc, kc) * decay_mask[None]
        intra = jnp.einsum('bhij,bhjv->bhiv', scores, vc)
        inter = jnp.einsum('bhid,bhdv->bhiv', qc, s) * q_decay[None, :, :, None]
        s_new = s * chunk_decay[None, :, None, None] + jnp.einsum(
            'bhjd,bhjv->bhdv', kc * k_decay[None, :, :, None], vc)
        return s_new, intra + inter

    s_fin, o = lax.scan(step, s0.astype(jnp.float32), (to_chunks(q), to_chunks(k), to_chunks(v)))
    o = o.transpose(1, 2, 0, 3, 4).reshape(b, h, n, dv)
    return o, s_fin


def retention_mixer(h, w_qkvg, decay_logit, gn_w, w_o, s0_fwd, s0_bwd, use_rope):
    b, n, _ = h.shape
    proj = h @ w_qkvg
    dk_tot = RET_HEADS * RET_DK
    dv_tot = RET_HEADS * RET_DV
    q = proj[..., :dk_tot].reshape(b, n, RET_HEADS, RET_DK)
    k = proj[..., dk_tot:2 * dk_tot].reshape(b, n, RET_HEADS, RET_DK)
    v = proj[..., 2 * dk_tot:2 * dk_tot + dv_tot].reshape(b, n, RET_HEADS, RET_DV)
    g = proj[..., 2 * dk_tot + dv_tot:]
    if use_rope:
        q = axial_rope(q)
        k = axial_rope(k)
    q = q * (RET_DK ** -0.5)
    q = q.transpose(0, 2, 1, 3)
    k = k.transpose(0, 2, 1, 3)
    v = v.transpose(0, 2, 1, 3)
    log_gamma = jax.nn.log_sigmoid(decay_logit.astype(jnp.float32))
    o_f, s_f = retention_chunkwise(q, k, v, log_gamma[0], s0_fwd, False)
    o_b, s_b = retention_chunkwise(jnp.flip(q, 2), jnp.flip(k, 2), jnp.flip(v, 2),
                                   log_gamma[1], s0_bwd, True)
    o = o_f + jnp.flip(o_b, 2)
    mu = jnp.mean(o, axis=-1, keepdims=True)
    var = jnp.mean(jnp.square(o - mu), axis=-1, keepdims=True)
    o = (o - mu) * lax.rsqrt(var + EPS)
    o = o.transpose(0, 2, 1, 3).reshape(b, n, dv_tot) * gn_w.astype(jnp.float32)
    out = (jax.nn.silu(g) * o.astype(h.dtype)) @ w_o
    return out, s_f, s_b


def sq_relu_mlp(h, w1, w2):
    return jnp.square(jax.nn.relu(h @ w1)) @ w2


def setup_inputs(seed: int = 0) -> dict:
    key = jax.random.key(seed)
    ks = jax.random.split(key, 24)
    f32 = jnp.float32
    D = D_MODEL
    qkv_w = (N_HEADS + 2 * N_KV_HEADS) * HEAD_DIM
    ret_w = 2 * RET_HEADS * RET_DK + 2 * RET_HEADS * RET_DV
    gamma = 1.0 - 2.0 ** (-5.0 - jnp.arange(RET_HEADS, dtype=f32))
    base_logit = jnp.log(gamma) - jnp.log1p(-gamma)
    return {
        'x_prompt': jax.random.normal(ks[0], (BATCH, SEQ, D), f32),
        'x_sample': jax.random.normal(ks[1], (DEC_BATCH, DEC_SEQ, D), f32),
        'cache_k': jax.random.normal(ks[2], (DEC_BATCH, N_ATTN_LAYERS, PAST_LEN, N_KV_HEADS, HEAD_DIM), f32),
        'cache_v': jax.random.normal(ks[3], (DEC_BATCH, N_ATTN_LAYERS, PAST_LEN, N_KV_HEADS, HEAD_DIM), f32),
        'state_ret': 2.0 * jax.random.normal(ks[4], (DEC_BATCH, N_RET_LAYERS, 2, RET_HEADS, RET_DK, RET_DV), f32),
        'c': jax.random.normal(ks[5], (DEC_BATCH, D), f32),
        'c_ctx': jax.random.normal(ks[6], (D,), f32),
        'w_mod': 0.5 * D ** -0.5 * jax.random.normal(ks[7], (DEPTH, D, 6 * D), f32),
        'b_mod': 0.02 * jax.random.normal(ks[8], (DEPTH, 6 * D), f32),
        'norm_g': 1.0 + 0.02 * jax.random.normal(ks[9], (DEPTH, 2, D), f32),
        'attn_w_qkv': D ** -0.5 * jax.random.normal(ks[10], (N_ATTN_LAYERS, D, qkv_w), f32),
        'attn_q_norm': 1.0 + 0.02 * jax.random.normal(ks[11], (N_ATTN_LAYERS, HEAD_DIM), f32),
        'attn_k_norm': 1.0 + 0.02 * jax.random.normal(ks[12], (N_ATTN_LAYERS, HEAD_DIM), f32),
        'attn_w_o': D ** -0.5 * jax.random.normal(ks[13], (N_ATTN_LAYERS, N_HEADS * HEAD_DIM, D), f32),
        'ret_w_qkvg': D ** -0.5 * jax.random.normal(ks[14], (N_RET_LAYERS, D, ret_w), f32),
        'ret_decay_logit': base_logit[None, None, :] + 0.1 * jax.random.normal(ks[15], (N_RET_LAYERS, 2, RET_HEADS), f32),
        'ret_gn_w': 1.0 + 0.02 * jax.random.normal(ks[16], (N_RET_LAYERS, RET_HEADS * RET_DV), f32),
        'ret_w_o': (RET_HEADS * RET_DV) ** -0.5 * jax.random.normal(ks[17], (N_RET_LAYERS, RET_HEADS * RET_DV, D), f32),
        'mlp_w1': D ** -0.5 * jax.random.normal(ks[18], (DEPTH, D, D_FF), f32),
        'mlp_w2': D_FF ** -0.5 * jax.random.normal(ks[19], (DEPTH, D_FF, D), f32),
        'final_norm_g': 1.0 + 0.02 * jax.random.normal(ks[20], (D,), f32),
    }


def reference(x_prompt, x_sample, cache_k, cache_v, state_ret, c, c_ctx,
              w_mod, b_mod, norm_g, attn_w_qkv, attn_q_norm, attn_k_norm, attn_w_o,
              ret_w_qkvg, ret_decay_logit, ret_gn_w, ret_w_o, mlp_w1, mlp_w2, final_norm_g):
    xp = x_prompt
    xs = x_sample
    new_k, new_v, new_s = [], [], []
    for i in range(DEPTH):
        mc = adaln_params(c_ctx[None, :], w_mod[i], b_mod[i])
        ms = adaln_params(c, w_mod[i], b_mod[i])
        hp = modulate(rms_norm(xp, norm_g[i, 0]), mc[0], mc[1])
        hs = modulate(rms_norm(xs, norm_g[i, 0]), ms[0], ms[1])
        j = i // N_MIXERS
        if i % N_MIXERS == 0:
            qp, kp, vp = attn_qkv(hp, attn_w_qkv[j], attn_q_norm[j], attn_k_norm[j], False)
            op = block_attention(qp, kp, vp) @ attn_w_o[j]
            new_k.append(kp)
            new_v.append(vp)
            qs, ks_, vs = attn_qkv(hs, attn_w_qkv[j], attn_q_norm[j], attn_k_norm[j], True)
            k_all = jnp.concatenate([cache_k[:, j].astype(ks_.dtype), ks_], axis=1)
            v_all = jnp.concatenate([cache_v[:, j].astype(vs.dtype), vs], axis=1)
            os_ = block_attention(qs, k_all, v_all) @ attn_w_o[j]
        else:
            zero = jnp.zeros((xp.shape[0], RET_HEADS, RET_DK, RET_DV), jnp.float32)
            op, sf, sb = retention_mixer(hp, ret_w_qkvg[j], ret_decay_logit[j], ret_gn_w[j], ret_w_o[j],
                                         zero, zero, False)
            new_s.append(jnp.stack([sf, sb], axis=1))
            os_, _, _ = retention_mixer(hs, ret_w_qkvg[j], ret_decay_logit[j], ret_gn_w[j], ret_w_o[j],
                                        state_ret[:, j, 0], state_ret[:, j, 1], True)
        xp = xp + mc[2] * op
        xs = xs + ms[2] * os_
        hp = modulate(rms_norm(xp, norm_g[i, 1]), mc[3], mc[4])
        hs = modulate(rms_norm(xs, norm_g[i, 1]), ms[3], ms[4])
        xp = xp + mc[5] * sq_relu_mlp(hp, mlp_w1[i], mlp_w2[i])
        xs = xs + ms[5] * sq_relu_mlp(hs, mlp_w1[i], mlp_w2[i])
    y_prompt = rms_norm(xp, final_norm_g)
    y_sample = rms_norm(xs, final_norm_g)
    new_cache_k = jnp.stack(new_k, axis=1)
    new_cache_v = jnp.stack(new_v, axis=1)
    new_state_ret = jnp.stack(new_s, axis=1)
    return (y_prompt, y_sample, new_cache_k, new_cache_v, new_state_ret)
```

```python
import functools

import numpy as np
import jax
import jax.numpy as jnp
from jax import lax
from jax.experimental import pallas as pl
from jax.experimental.pallas import tpu as pltpu

F32 = jnp.float32
BF16 = jnp.bfloat16

D_MODEL = 1024
GRID_W = 64
N_HEADS = 8
N_KV_HEADS = 2
HEAD_DIM = 128
GROUP = N_HEADS // N_KV_HEADS
ROPE_BASE = 10000.0
RET_HEADS = 4
RET_DK = 256
RET_DV = 256
RET_CHUNK = 128
D_FF = 4 * D_MODEL
EPS = 1e-6
QKV_W = (N_HEADS + 2 * N_KV_HEADS) * HEAD_DIM
RET_W = 2 * RET_HEADS * RET_DK + 2 * RET_HEADS * RET_DV

V7X_LANES = 128
V7X_SUBLANES = 8
MOD_ROWS = 16
ROW_TILE = 512
ATTN_Q_TILE = 256
VMEM_LIMIT = 56 * 1024 * 1024


def _params(sem, vmem=VMEM_LIMIT):
    return pltpu.CompilerParams(dimension_semantics=sem, vmem_limit_bytes=vmem)


def _const_spec(shape):
    nd = len(shape)
    return pl.BlockSpec(shape, lambda *_: (0,) * nd, pipeline_mode=pl.Buffered(1))


def _sigmoid(x):
    return 1.0 / (1.0 + jnp.exp(-x))


def _mod_kernel(cond_ref, w_ref, b_ref, o_ref):
    s = cond_ref[...]
    s = (s * _sigmoid(s)).astype(BF16)
    o_ref[0] = jnp.dot(s, w_ref[0].astype(BF16), preferred_element_type=F32) + b_ref[0]


def _modulation(cond, w_mod, b_mod):
    depth = w_mod.shape[0]
    tn = 1024
    return pl.pallas_call(
        _mod_kernel,
        out_shape=jax.ShapeDtypeStruct((depth, MOD_ROWS, 6 * D_MODEL), F32),
        grid=(depth, 6 * D_MODEL // tn),
        in_specs=[
            pl.BlockSpec((MOD_ROWS, D_MODEL), lambda i, j: (0, 0)),
            pl.BlockSpec((1, D_MODEL, tn), lambda i, j: (i, 0, j)),
            pl.BlockSpec((1, 1, tn), lambda i, j: (i, 0, j)),
        ],
        out_specs=pl.BlockSpec((1, MOD_ROWS, tn), lambda i, j: (i, 0, j)),
        compiler_params=_params(("parallel", "parallel")),
        name="modulation",
    )(cond, w_mod, b_mod.reshape(depth, 1, 6 * D_MODEL))


def _norm_modulate(x, g, mod_ref, which):
    y = x * lax.rsqrt(jnp.mean(x * x, axis=-1, keepdims=True) + EPS) * g
    shift = mod_ref[0, :, (3 * which) * D_MODEL:(3 * which + 1) * D_MODEL]
    scale = mod_ref[0, :, (3 * which + 1) * D_MODEL:(3 * which + 2) * D_MODEL]
    return y * (1.0 + scale) + shift


def _mod_spec(rows_per_mod_row, first_row):
    tiles = rows_per_mod_row // ROW_TILE if rows_per_mod_row else 0
    if tiles:
        return pl.BlockSpec((1, 1, 6 * D_MODEL), lambda i: (first_row + i // tiles, 0, 0))
    return pl.BlockSpec((1, 1, 6 * D_MODEL), lambda i: (first_row, 0, 0))


def _rope_tables(n, dim):
    quarter = dim // 4
    rows = np.repeat(np.arange(n // GRID_W), GRID_W).astype(np.float64)
    cols = np.tile(np.arange(GRID_W), n // GRID_W).astype(np.float64)
    freqs = ROPE_BASE ** (-np.arange(quarter, dtype=np.float64) / quarter)
    ar = rows[:, None] * freqs
    ac = cols[:, None] * freqs
    cos = np.concatenate([np.cos(ar), np.cos(ar), np.cos(ac), np.cos(ac)], axis=1)
    sin = np.concatenate([-np.sin(ar), np.sin(ar), -np.sin(ac), np.sin(ac)], axis=1)
    return jnp.asarray(cos, F32), jnp.asarray(sin, F32)


def _swap_halves(t, quarter):
    if 2 * quarter == V7X_LANES:
        return pltpu.roll(t, quarter, axis=1)
    lane = lax.broadcasted_iota(jnp.int32, t.shape, 1)
    return jnp.where((lane & quarter) == 0,
                     pltpu.roll(t, V7X_LANES - quarter, axis=1),
                     pltpu.roll(t, quarter, axis=1))


def _attn_in_kernel(x_ref, mod_ref, g_ref, w_ref, qn_ref, kn_ref, *rest, use_rope):
    if use_rope:
        cos_ref, sin_ref, q_ref, k_ref, v_ref = rest
    else:
        q_ref, k_ref, v_ref = rest
    h = _norm_modulate(x_ref[...], g_ref[...], mod_ref, 0).astype(BF16)
    qkv = jnp.dot(h, w_ref[...], preferred_element_type=F32)
    nq = N_HEADS * HEAD_DIM
    nk = N_KV_HEADS * HEAD_DIM
    for head in range(N_HEADS + N_KV_HEADS):
        t = qkv[:, head * HEAD_DIM:(head + 1) * HEAD_DIM]
        w = qn_ref[...] if head < N_HEADS else kn_ref[...]
        t = t * lax.rsqrt(jnp.mean(t * t, axis=-1, keepdims=True) + EPS) * w
        if use_rope:
            t = t * cos_ref[...] + _swap_halves(t, HEAD_DIM // 4) * sin_ref[...]
        if head < N_HEADS:
            q_ref[:, head * HEAD_DIM:(head + 1) * HEAD_DIM] = (t * HEAD_DIM ** -0.5).astype(q_ref.dtype)
        else:
            kh = head - N_HEADS
            k_ref[:, kh * HEAD_DIM:(kh + 1) * HEAD_DIM] = t.astype(k_ref.dtype)
    v_ref[...] = qkv[:, nq + nk:].astype(v_ref.dtype)


def _attn_in(x, mod, mod_spec, g, w, qn, kn, rope, kv_dtype):
    rows = x.shape[0]
    nk = N_KV_HEADS * HEAD_DIM
    in_specs = [
        pl.BlockSpec((ROW_TILE, D_MODEL), lambda i: (i, 0)),
        mod_spec,
        _const_spec((1, D_MODEL)),
        _const_spec((D_MODEL, QKV_W)),
        _const_spec((1, HEAD_DIM)),
        _const_spec((1, HEAD_DIM)),
    ]
    args = [x, mod, g, w, qn, kn]
    if rope is not None:
        seq_tiles = rope[0].shape[0] // ROW_TILE
        in_specs += [pl.BlockSpec((ROW_TILE, HEAD_DIM), lambda i: (i % seq_tiles, 0))] * 2
        args += list(rope)
    return pl.pallas_call(
        functools.partial(_attn_in_kernel, use_rope=rope is not None),
        out_shape=(jax.ShapeDtypeStruct((rows, N_HEADS * HEAD_DIM), BF16),
                   jax.ShapeDtypeStruct((rows, nk), kv_dtype),
                   jax.ShapeDtypeStruct((rows, nk), kv_dtype)),
        grid=(rows // ROW_TILE,),
        in_specs=in_specs,
        out_specs=(pl.BlockSpec((ROW_TILE, N_HEADS * HEAD_DIM), lambda i: (i, 0)),
                   pl.BlockSpec((ROW_TILE, nk), lambda i: (i, 0)),
                   pl.BlockSpec((ROW_TILE, nk), lambda i: (i, 0))),
        compiler_params=_params(("parallel",)),
        name="attn_in_rope" if rope is not None else "attn_in",
    )(*args)


def _attn_kernel(q_ref, k_ref, v_ref, *rest, cached):
    if cached:
        ck_ref, cv_ref, o_ref = rest
        ck = ck_ref[0].astype(BF16)
        cv = cv_ref[0].astype(BF16)
    else:
        (o_ref,) = rest
    k = k_ref[0].astype(BF16)
    v = v_ref[0].astype(BF16)
    nt = (((1,), (1,)), ((), ()))
    for g in range(GROUP):
        q = q_ref[0, :, g * HEAD_DIM:(g + 1) * HEAD_DIM]
        s = lax.dot_general(q, k, nt, preferred_element_type=F32)
        m = jnp.max(s, axis=-1, keepdims=True)
        if cached:
            sc = lax.dot_general(q, ck, nt, preferred_element_type=F32)
            m = jnp.maximum(m, jnp.max(sc, axis=-1, keepdims=True))
        p = jnp.exp(s - m)
        l = jnp.sum(p, axis=-1, keepdims=True)
        o = jnp.dot(p.astype(BF16), v, preferred_element_type=F32)
        if cached:
            pc = jnp.exp(sc - m)
            l = l + jnp.sum(pc, axis=-1, keepdims=True)
            o = o + jnp.dot(pc.astype(BF16), cv, preferred_element_type=F32)
        o_ref[0, :, g * HEAD_DIM:(g + 1) * HEAD_DIM] = (o / l).astype(o_ref.dtype)


def _attention(q, k, v, cache=None):
    b, n, _ = q.shape
    tq = min(ATTN_Q_TILE, n)
    gw = GROUP * HEAD_DIM
    in_specs = [
        pl.BlockSpec((1, tq, gw), lambda bi, kv, qi: (bi, qi, kv)),
        pl.BlockSpec((1, n, HEAD_DIM), lambda bi, kv, qi: (bi, 0, kv)),
        pl.BlockSpec((1, n, HEAD_DIM), lambda bi, kv, qi: (bi, 0, kv)),
    ]
    args = [q, k, v]
    if cache is not None:
        p = cache[0].shape[1]
        in_specs += [pl.BlockSpec((1, p, HEAD_DIM), lambda bi, kv, qi: (bi, 0, kv))] * 2
        args += list(cache)
    return pl.pallas_call(
        functools.partial(_attn_kernel, cached=cache is not None),
        out_shape=jax.ShapeDtypeStruct((b, n, N_HEADS * HEAD_DIM), BF16),
        grid=(b, N_KV_HEADS, n // tq),
        in_specs=in_specs,
        out_specs=pl.BlockSpec((1, tq, gw), lambda bi, kv, qi: (bi, qi, kv)),
        compiler_params=_params(("parallel", "parallel", "parallel")),
        name="attention_cached" if cache is not None else "attention",
    )(*args)


def _post_kernel(a_ref, x_ref, mod_ref, g_ref, wo_ref, w1_ref, w2_ref, *rest, final):
    if final:
        gf_ref, o_ref = rest
    else:
        (o_ref,) = rest
    gate1 = mod_ref[0, :, 2 * D_MODEL:3 * D_MODEL]
    gate2 = mod_ref[0, :, 5 * D_MODEL:6 * D_MODEL]
    x = x_ref[...] + gate1 * jnp.dot(a_ref[...], wo_ref[...], preferred_element_type=F32)
    h = _norm_modulate(x, g_ref[...], mod_ref, 1).astype(BF16)
    ff_tile = D_FF // 4
    y = None
    for f in range(D_FF // ff_tile):
        u = jnp.dot(h, w1_ref[:, f * ff_tile:(f + 1) * ff_tile], preferred_element_type=F32)
        u = jnp.square(jnp.maximum(u, 0.0)).astype(BF16)
        part = jnp.dot(u, w2_ref[f * ff_tile:(f + 1) * ff_tile, :], preferred_element_type=F32)
        y = part if y is None else y + part
    x = x + gate2 * y
    if final:
        x = x * lax.rsqrt(jnp.mean(x * x, axis=-1, keepdims=True) + EPS) * gf_ref[...]
    o_ref[...] = x


def _post(a, x, mod, mod_spec, g, wo, w1, w2, gf=None):
    rows = x.shape[0]
    in_specs = [
        pl.BlockSpec((ROW_TILE, D_MODEL), lambda i: (i, 0)),
        pl.BlockSpec((ROW_TILE, D_MODEL), lambda i: (i, 0)),
        mod_spec,
        _const_spec((1, D_MODEL)),
        _const_spec((D_MODEL, D_MODEL)),
        _const_spec((D_MODEL, D_FF)),
        _const_spec((D_FF, D_MODEL)),
    ]
    args = [a, x, mod, g, wo, w1, w2]
    if gf is not None:
        in_specs.append(_const_spec((1, D_MODEL)))
        args.append(gf)
    return pl.pallas_call(
        functools.partial(_post_kernel, final=gf is not None),
        out_shape=jax.ShapeDtypeStruct((rows, D_MODEL), F32),
        grid=(rows // ROW_TILE,),
        in_specs=in_specs,
        out_specs=pl.BlockSpec((ROW_TILE, D_MODEL), lambda i: (i, 0)),
        compiler_params=_params(("parallel",)),
        name="post_final" if gf is not None else "post",
    )(*args)


def _ret_in_kernel(x_ref, mod_ref, g_ref, w_ref, *rest, use_rope):
    if use_rope:
        cos_ref, sin_ref, q_ref, k_ref, v_ref, gate_ref = rest
    else:
        q_ref, k_ref, v_ref, gate_ref = rest
    h = _norm_modulate(x_ref[...], g_ref[...], mod_ref, 0).astype(BF16)
    width = RET_HEADS * RET_DK
    for part, ref in enumerate((q_ref, k_ref, v_ref, gate_ref)):
        t = jnp.dot(h, w_ref[:, part * width:(part + 1) * width], preferred_element_type=F32)
        if part < 2 and use_rope:
            for c in range(width // V7X_LANES):
                half = (c % (RET_DK // V7X_LANES)) * V7X_LANES
                tc = t[:, c * V7X_LANES:(c + 1) * V7X_LANES]
                tc = (tc * cos_ref[:, half:half + V7X_LANES]
                      + _swap_halves(tc, RET_DK // 4) * sin_ref[:, half:half + V7X_LANES])
                if part == 0:
                    tc = tc * RET_DK ** -0.5
                ref[:, c * V7X_LANES:(c + 1) * V7X_LANES] = tc.astype(ref.dtype)
        else:
            if part == 0:
                t = t * RET_DK ** -0.5
            ref[...] = t.astype(ref.dtype)


def _ret_in(x, mod, mod_spec, g, w, rope):
    rows = x.shape[0]
    width = RET_HEADS * RET_DK
    in_specs = [
        pl.BlockSpec((ROW_TILE, D_MODEL), lambda i: (i, 0)),
        mod_spec,
        _const_spec((1, D_MODEL)),
        _const_spec((D_MODEL, RET_W)),
    ]
    args = [x, mod, g, w]
    if rope is not None:
        seq_tiles = rope[0].shape[0] // ROW_TILE
        in_specs += [pl.BlockSpec((ROW_TILE, RET_DK), lambda i: (i % seq_tiles, 0))] * 2
        args += list(rope)
    out_block = pl.BlockSpec((ROW_TILE, width), lambda i: (i, 0))
    return pl.pallas_call(
        functools.partial(_ret_in_kernel, use_rope=rope is not None),
        out_shape=(jax.ShapeDtypeStruct((rows, width), BF16),
                   jax.ShapeDtypeStruct((rows, width), BF16),
                   jax.ShapeDtypeStruct((rows, width), BF16),
                   jax.ShapeDtypeStruct((rows, width), F32)),
        grid=(rows // ROW_TILE,),
        in_specs=in_specs,
        out_specs=(out_block, out_block, out_block, out_block),
        compiler_params=_params(("parallel",)),
        name="ret_in_rope" if rope is not None else "ret_in",
    )(*args)


def _log_sigmoid(x):
    return jnp.minimum(x, 0.0) - jnp.log(1.0 + jnp.exp(-jnp.abs(x)))


def _ret_kernel(q_ref, k_ref, v_ref, gate_ref, lg_ref, gn_ref, *rest, n_chunks, has_state):
    if has_state:
        s0_ref, y_ref, sf_ref, sb_ref, ob_ref = rest
    else:
        y_ref, st_ref, sf_ref, sb_ref, ob_ref = rest
    c_ = RET_CHUNK
    lg_f = _log_sigmoid(lg_ref[0, 0, 0:1, :])
    lg_b = _log_sigmoid(lg_ref[0, 1, 0:1, :])
    row = lax.broadcasted_iota(jnp.int32, (c_, RET_DV), 0).astype(F32)
    qdec_f = jnp.exp(lg_f * (row + 1.0))
    qdec_b = jnp.exp(lg_b * (c_ - row))
    kdec_f = jnp.exp(lg_f * (c_ - 1.0 - row))
    kdec_b = jnp.exp(lg_b * row)
    cdec_f = jnp.exp(lg_f * c_)
    cdec_b = jnp.exp(lg_b * c_)
    ii = lax.broadcasted_iota(jnp.int32, (c_, c_), 0)
    jj = lax.broadcasted_iota(jnp.int32, (c_, c_), 1)
    diff = (ii - jj).astype(F32)
    dmask = jnp.where(ii >= jj,
                      jnp.exp(lg_f[:, :c_] * jnp.maximum(diff, 0.0)),
                      jnp.exp(lg_b[:, :c_] * jnp.maximum(-diff, 0.0)))
    tn = (((0,), (0,)), ((), ()))
    nt = (((1,), (1,)), ((), ()))

    if has_state:
        sf_ref[...] = s0_ref[0, 0, 0]
        sb_ref[...] = s0_ref[0, 1, 0]
    else:
        sf_ref[...] = jnp.zeros_like(sf_ref)
        sb_ref[...] = jnp.zeros_like(sb_ref)

    for c in reversed(range(n_chunks)):
        sl = pl.ds(c * c_, c_)
        qc = q_ref[0, sl, :]
        ob_ref[sl, :] = jnp.dot(qc, sb_ref[...].astype(BF16), preferred_element_type=F32) * qdec_b
        kd = (k_ref[0, sl, :].astype(F32) * kdec_b).astype(BF16)
        sb_ref[...] = sb_ref[...] * cdec_b + lax.dot_general(kd, v_ref[0, sl, :], tn,
                                                             preferred_element_type=F32)

    for c in range(n_chunks):
        sl = pl.ds(c * c_, c_)
        qc = q_ref[0, sl, :]
        kc = k_ref[0, sl, :]
        vc = v_ref[0, sl, :]
        scores = lax.dot_general(qc, kc, nt, preferred_element_type=F32) * dmask
        o = (jnp.dot(scores.astype(BF16), vc, preferred_element_type=F32)
             + jnp.dot(qc, sf_ref[...].astype(BF16), preferred_element_type=F32) * qdec_f
             + ob_ref[sl, :])
        kd = (kc.astype(F32) * kdec_f).astype(BF16)
        sf_ref[...] = sf_ref[...] * cdec_f + lax.dot_general(kd, vc, tn, preferred_element_type=F32)
        mu = jnp.mean(o, axis=-1, keepdims=True)
        oc = o - mu
        var = jnp.mean(oc * oc, axis=-1, keepdims=True)
        on = oc * lax.rsqrt(var + EPS) * gn_ref[...]
        gt = gate_ref[0, sl, :]
        y_ref[0, sl, :] = (gt * _sigmoid(gt) * on).astype(y_ref.dtype)

    if not has_state:
        st_ref[0, 0, 0] = sf_ref[...]
        st_ref[0, 1, 0] = sb_ref[...]


def _retention(q, k, v, gate, lg, gn, state0):
    b, n, _ = q.shape
    tok = pl.BlockSpec((1, n, RET_DK), lambda bi, h: (bi, 0, h))
    st = pl.BlockSpec((1, 2, 1, RET_DK, RET_DV), lambda bi, h: (bi, 0, h, 0, 0))
    in_specs = [tok, tok, tok, tok,
                pl.BlockSpec((1, 2, V7X_SUBLANES, RET_DV), lambda bi, h: (h, 0, 0, 0)),
                pl.BlockSpec((1, RET_DV), lambda bi, h: (0, h))]
    args = [q, k, v, gate, lg, gn]
    y_shape = jax.ShapeDtypeStruct((b, n, RET_HEADS * RET_DV), BF16)
    if state0 is not None:
        in_specs.append(st)
        args.append(state0)
        out_shape, out_specs = y_shape, tok
    else:
        out_shape = (y_shape, jax.ShapeDtypeStruct((b, 2, RET_HEADS, RET_DK, RET_DV), F32))
        out_specs = (tok, st)
    return pl.pallas_call(
        functools.partial(_ret_kernel, n_chunks=n // RET_CHUNK, has_state=state0 is not None),
        out_shape=out_shape,
        grid=(b, RET_HEADS),
        in_specs=in_specs,
        out_specs=out_specs,
        scratch_shapes=[pltpu.VMEM((RET_DK, RET_DV), F32),
                        pltpu.VMEM((RET_DK, RET_DV), F32),
                        pltpu.VMEM((n, RET_DV), F32)],
        compiler_params=_params(("parallel", "parallel")),
        name="retention_state" if state0 is not None else "retention",
    )(*args)


def kernel(x_prompt, x_sample, cache_k, cache_v, state_ret, c, c_ctx, w_mod, b_mod, norm_g,
           attn_w_qkv, attn_q_norm, attn_k_norm, attn_w_o, ret_w_qkvg, ret_decay_logit, ret_gn_w,
           ret_w_o, mlp_w1, mlp_w2, final_norm_g):
    bp, sp, d = x_prompt.shape
    bs, ss, _ = x_sample.shape
    depth = w_mod.shape[0]
    assert d == D_MODEL and depth == 2 and 1 + bs <= MOD_ROWS
    assert (bp * sp) % ROW_TILE == 0 and ss % ROW_TILE == 0

    cond = jnp.concatenate([c_ctx[None, :], c, jnp.zeros((MOD_ROWS - 1 - bs, d), F32)], axis=0)
    mod = _modulation(cond, w_mod, b_mod)
    mod = mod.reshape(depth, MOD_ROWS, 1, 6 * d)
    ctx_mod = _mod_spec(0, 0)
    lat_mod = _mod_spec(ss, 1)

    xp = x_prompt.reshape(bp * sp, d)
    xs = x_sample.reshape(bs * ss, d)
    nkv = N_KV_HEADS * HEAD_DIM

    g0 = norm_g[0, 0][None, :]
    g1 = norm_g[0, 1][None, :]
    wqkv = attn_w_qkv[0].astype(BF16)
    qn = attn_q_norm[0][None, :]
    kn = attn_k_norm[0][None, :]
    wo = attn_w_o[0].astype(BF16)
    w1 = mlp_w1[0].astype(BF16)
    w2 = mlp_w2[0].astype(BF16)

    qp, kp, vp = _attn_in(xp, mod[0], ctx_mod, g0, wqkv, qn, kn, None, F32)
    ap = _attention(qp.reshape(bp, sp, -1), kp.reshape(bp, sp, nkv), vp.reshape(bp, sp, nkv))
    xp = _post(ap.reshape(bp * sp, -1), xp, mod[0], ctx_mod, g1, wo, w1, w2)

    qs, ks, vs = _attn_in(xs, mod[0], lat_mod, g0, wqkv, qn, kn, _rope_tables(ss, HEAD_DIM), BF16)
    past = cache_k.shape[2]
    cache = (cache_k[:, 0].reshape(bs, past, nkv), cache_v[:, 0].reshape(bs, past, nkv))
    as_ = _attention(qs.reshape(bs, ss, -1), ks.reshape(bs, ss, nkv), vs.reshape(bs, ss, nkv), cache)
    xs = _post(as_.reshape(bs * ss, -1), xs, mod[0], lat_mod, g1, wo, w1, w2)

    g0 = norm_g[1, 0][None, :]
    g1 = norm_g[1, 1][None, :]
    wr = ret_w_qkvg[0].astype(BF16)
    wo = ret_w_o[0].astype(BF16)
    w1 = mlp_w1[1].astype(BF16)
    w2 = mlp_w2[1].astype(BF16)
    gn = ret_gn_w[0][None, :]
    gf = final_norm_g[None, :]
    lg = jnp.broadcast_to(ret_decay_logit[0].T[:, :, None, None],
                          (RET_HEADS, 2, V7X_SUBLANES, RET_DV))
    hw = RET_HEADS * RET_DK

    q, k, v, gate = _ret_in(xp, mod[1], ctx_mod, g0, wr, None)
    yp, new_state = _retention(q.reshape(bp, sp, hw), k.reshape(bp, sp, hw), v.reshape(bp, sp, hw),
                               gate.reshape(bp, sp, hw), lg, gn, None)
    y_prompt = _post(yp.reshape(bp * sp, hw), xp, mod[1], ctx_mod, g1, wo, w1, w2, gf)

    q, k, v, gate = _ret_in(xs, mod[1], lat_mod, g0, wr, _rope_tables(ss, RET_DK))
    ys = _retention(q.reshape(bs, ss, hw), k.reshape(bs, ss, hw), v.reshape(bs, ss, hw),
                    gate.reshape(bs, ss, hw), lg, gn, state_ret[:, 0])
    y_sample = _post(ys.reshape(bs * ss, hw), xs, mod[1], lat_mod, g1, wo, w1, w2, gf)

    return (y_prompt.reshape(bp, sp, d),
            y_sample.reshape(bs, ss, d),
            kp.reshape(bp, 1, sp, N_KV_HEADS, HEAD_DIM),
            vp.reshape(bp, 1, sp, N_KV_HEADS, HEAD_DIM),
            new_state.reshape(bp, 1, 2, RET_HEADS, RET_DK, RET_DV))
```

```python
import functools

import numpy as np
import jax
import jax.numpy as jnp
from jax import lax
from jax.experimental import pallas as pl
from jax.experimental.pallas import tpu as pltpu

F32 = jnp.float32
BF16 = jnp.bfloat16

D_MODEL = 1024
GRID_W = 64
N_HEADS = 8
N_KV_HEADS = 2
HEAD_DIM = 128
GROUP = N_HEADS // N_KV_HEADS
ROPE_BASE = 10000.0
RET_HEADS = 4
RET_DK = 256
RET_DV = 256
RET_CHUNK = 128
D_FF = 4 * D_MODEL
EPS = 1e-6
QKV_W = (N_HEADS + 2 * N_KV_HEADS) * HEAD_DIM
RET_W = 2 * RET_HEADS * RET_DK + 2 * RET_HEADS * RET_DV

V7X_LANES = 128
V7X_SUBLANES = 8
MOD_ROWS = 16
ROW_TILE = 512
ATTN_Q_TILE = 256
RET_Q_TILE = 512
VMEM_LIMIT = 56 * 1024 * 1024


def _params(sem, vmem=VMEM_LIMIT):
    return pltpu.CompilerParams(dimension_semantics=sem, vmem_limit_bytes=vmem)


def _const_spec(shape):
    nd = len(shape)
    return pl.BlockSpec(shape, lambda *_: (0,) * nd, pipeline_mode=pl.Buffered(1))


def _sigmoid(x):
    return 1.0 / (1.0 + jnp.exp(-x))


def _mod_kernel(cond_ref, w_ref, b_ref, o_ref):
    s = cond_ref[...]
    s = (s * _sigmoid(s)).astype(BF16)
    o_ref[0] = jnp.dot(s, w_ref[0].astype(BF16), preferred_element_type=F32) + b_ref[0]


def _modulation(cond, w_mod, b_mod):
    depth = w_mod.shape[0]
    tn = 1024
    return pl.pallas_call(
        _mod_kernel,
        out_shape=jax.ShapeDtypeStruct((depth, MOD_ROWS, 6 * D_MODEL), F32),
        grid=(depth, 6 * D_MODEL // tn),
        in_specs=[
            pl.BlockSpec((MOD_ROWS, D_MODEL), lambda i, j: (0, 0)),
            pl.BlockSpec((1, D_MODEL, tn), lambda i, j: (i, 0, j)),
            pl.BlockSpec((1, 1, tn), lambda i, j: (i, 0, j)),
        ],
        out_specs=pl.BlockSpec((1, MOD_ROWS, tn), lambda i, j: (i, 0, j)),
        compiler_params=_params(("parallel", "parallel")),
        name="modulation",
    )(cond, w_mod, b_mod.reshape(depth, 1, 6 * D_MODEL))


def _norm_modulate(x, g, mod_ref, which):
    y = x * lax.rsqrt(jnp.mean(x * x, axis=-1, keepdims=True) + EPS) * g
    shift = mod_ref[0, :, (3 * which) * D_MODEL:(3 * which + 1) * D_MODEL]
    scale = mod_ref[0, :, (3 * which + 1) * D_MODEL:(3 * which + 2) * D_MODEL]
    return y * (1.0 + scale) + shift


def _mod_spec(rows_per_mod_row, first_row):
    tiles = rows_per_mod_row // ROW_TILE if rows_per_mod_row else 0
    if tiles:
        return pl.BlockSpec((1, 1, 6 * D_MODEL), lambda i: (first_row + i // tiles, 0, 0))
    return pl.BlockSpec((1, 1, 6 * D_MODEL), lambda i: (first_row, 0, 0))


def _rope_tables(n, dim):
    quarter = dim // 4
    rows = np.repeat(np.arange(n // GRID_W), GRID_W).astype(np.float64)
    cols = np.tile(np.arange(GRID_W), n // GRID_W).astype(np.float64)
    freqs = ROPE_BASE ** (-np.arange(quarter, dtype=np.float64) / quarter)
    ar = rows[:, None] * freqs
    ac = cols[:, None] * freqs
    cos = np.concatenate([np.cos(ar), np.cos(ar), np.cos(ac), np.cos(ac)], axis=1)
    sin = np.concatenate([-np.sin(ar), np.sin(ar), -np.sin(ac), np.sin(ac)], axis=1)
    return jnp.asarray(cos, F32), jnp.asarray(sin, F32)


def _swap_halves(t, quarter):
    if 2 * quarter == V7X_LANES:
        return pltpu.roll(t, quarter, axis=1)
    lane = lax.broadcasted_iota(jnp.int32, t.shape, 1)
    return jnp.where((lane & quarter) == 0,
                     pltpu.roll(t, V7X_LANES - quarter, axis=1),
                     pltpu.roll(t, quarter, axis=1))


def _attn_in_kernel(x_ref, mod_ref, g_ref, w_ref, qn_ref, kn_ref, *rest, use_rope):
    if use_rope:
        cos_ref, sin_ref, q_ref, k_ref, v_ref = rest
    else:
        q_ref, k_ref, v_ref = rest
    h = _norm_modulate(x_ref[...], g_ref[...], mod_ref, 0).astype(BF16)
    qkv = jnp.dot(h, w_ref[...], preferred_element_type=F32)
    nq = N_HEADS * HEAD_DIM
    nk = N_KV_HEADS * HEAD_DIM
    for head in range(N_HEADS + N_KV_HEADS):
        t = qkv[:, head * HEAD_DIM:(head + 1) * HEAD_DIM]
        w = qn_ref[...] if head < N_HEADS else kn_ref[...]
        t = t * lax.rsqrt(jnp.mean(t * t, axis=-1, keepdims=True) + EPS) * w
        if use_rope:
            t = t * cos_ref[...] + _swap_halves(t, HEAD_DIM // 4) * sin_ref[...]
        if head < N_HEADS:
            q_ref[:, head * HEAD_DIM:(head + 1) * HEAD_DIM] = (t * HEAD_DIM ** -0.5).astype(q_ref.dtype)
        else:
            kh = head - N_HEADS
            k_ref[:, kh * HEAD_DIM:(kh + 1) * HEAD_DIM] = t.astype(k_ref.dtype)
    v_ref[...] = qkv[:, nq + nk:].astype(v_ref.dtype)


def _attn_in(x, mod, mod_spec, g, w, qn, kn, rope, kv_dtype):
    rows = x.shape[0]
    nk = N_KV_HEADS * HEAD_DIM
    in_specs = [
        pl.BlockSpec((ROW_TILE, D_MODEL), lambda i: (i, 0)),
        mod_spec,
        _const_spec((1, D_MODEL)),
        _const_spec((D_MODEL, QKV_W)),
        _const_spec((1, HEAD_DIM)),
        _const_spec((1, HEAD_DIM)),
    ]
    args = [x, mod, g, w, qn, kn]
    if rope is not None:
        seq_tiles = rope[0].shape[0] // ROW_TILE
        in_specs += [pl.BlockSpec((ROW_TILE, HEAD_DIM), lambda i: (i % seq_tiles, 0))] * 2
        args += list(rope)
    return pl.pallas_call(
        functools.partial(_attn_in_kernel, use_rope=rope is not None),
        out_shape=(jax.ShapeDtypeStruct((rows, N_HEADS * HEAD_DIM), BF16),
                   jax.ShapeDtypeStruct((rows, nk), kv_dtype),
                   jax.ShapeDtypeStruct((rows, nk), kv_dtype)),
        grid=(rows // ROW_TILE,),
        in_specs=in_specs,
        out_specs=(pl.BlockSpec((ROW_TILE, N_HEADS * HEAD_DIM), lambda i: (i, 0)),
                   pl.BlockSpec((ROW_TILE, nk), lambda i: (i, 0)),
                   pl.BlockSpec((ROW_TILE, nk), lambda i: (i, 0))),
        compiler_params=_params(("parallel",)),
        name="attn_in_rope" if rope is not None else "attn_in",
    )(*args)


def _attn_kernel(q_ref, k_ref, v_ref, *rest, cached):
    if cached:
        ck_ref, cv_ref, o_ref = rest
        ck = ck_ref[0].astype(BF16)
        cv = cv_ref[0].astype(BF16)
    else:
        (o_ref,) = rest
    k = k_ref[0].astype(BF16)
    v = v_ref[0].astype(BF16)
    nt = (((1,), (1,)), ((), ()))
    for g in range(GROUP):
        q = q_ref[0, :, g * HEAD_DIM:(g + 1) * HEAD_DIM]
        s = lax.dot_general(q, k, nt, preferred_element_type=F32)
        m = jnp.max(s, axis=-1, keepdims=True)
        if cached:
            sc = lax.dot_general(q, ck, nt, preferred_element_type=F32)
            m = jnp.maximum(m, jnp.max(sc, axis=-1, keepdims=True))
        p = jnp.exp(s - m)
        l = jnp.sum(p, axis=-1, keepdims=True)
        o = jnp.dot(p.astype(BF16), v, preferred_element_type=F32)
        if cached:
            pc = jnp.exp(sc - m)
            l = l + jnp.sum(pc, axis=-1, keepdims=True)
            o = o + jnp.dot(pc.astype(BF16), cv, preferred_element_type=F32)
        o_ref[0, :, g * HEAD_DIM:(g + 1) * HEAD_DIM] = (o / l).astype(o_ref.dtype)


def _attention(q, k, v, cache=None):
    b, n, _ = q.shape
    tq = min(ATTN_Q_TILE, n)
    gw = GROUP * HEAD_DIM
    in_specs = [
        pl.BlockSpec((1, tq, gw), lambda bi, kv, qi: (bi, qi, kv)),
        pl.BlockSpec((1, n, HEAD_DIM), lambda bi, kv, qi: (bi, 0, kv)),
        pl.BlockSpec((1, n, HEAD_DIM), lambda bi, kv, qi: (bi, 0, kv)),
    ]
    args = [q, k, v]
    if cache is not None:
        p = cache[0].shape[1]
        in_specs += [pl.BlockSpec((1, p, HEAD_DIM), lambda bi, kv, qi: (bi, 0, kv))] * 2
        args += list(cache)
    return pl.pallas_call(
        functools.partial(_attn_kernel, cached=cache is not None),
        out_shape=jax.ShapeDtypeStruct((b, n, N_HEADS * HEAD_DIM), BF16),
        grid=(b, N_KV_HEADS, n // tq),
        in_specs=in_specs,
        out_specs=pl.BlockSpec((1, tq, gw), lambda bi, kv, qi: (bi, qi, kv)),
        compiler_params=_params(("parallel", "parallel", "parallel")),
        name="attention_cached" if cache is not None else "attention",
    )(*args)


def _post_kernel(a_ref, x_ref, mod_ref, g_ref, wo_ref, w1_ref, w2_ref, *rest, final):
    if final:
        gf_ref, o_ref = rest
    else:
        (o_ref,) = rest
    gate1 = mod_ref[0, :, 2 * D_MODEL:3 * D_MODEL]
    gate2 = mod_ref[0, :, 5 * D_MODEL:6 * D_MODEL]
    x = x_ref[...] + gate1 * jnp.dot(a_ref[...], wo_ref[...], preferred_element_type=F32)
    h = _norm_modulate(x, g_ref[...], mod_ref, 1).astype(BF16)
    ff_tile = D_FF // 4
    y = None
    for f in range(D_FF // ff_tile):
        u = jnp.dot(h, w1_ref[:, f * ff_tile:(f + 1) * ff_tile], preferred_element_type=F32)
        u = jnp.square(jnp.maximum(u, 0.0)).astype(BF16)
        part = jnp.dot(u, w2_ref[f * ff_tile:(f + 1) * ff_tile, :], preferred_element_type=F32)
        y = part if y is None else y + part
    x = x + gate2 * y
    if final:
        x = x * lax.rsqrt(jnp.mean(x * x, axis=-1, keepdims=True) + EPS) * gf_ref[...]
    o_ref[...] = x


def _post(a, x, mod, mod_spec, g, wo, w1, w2, gf=None):
    rows = x.shape[0]
    in_specs = [
        pl.BlockSpec((ROW_TILE, D_MODEL), lambda i: (i, 0)),
        pl.BlockSpec((ROW_TILE, D_MODEL), lambda i: (i, 0)),
        mod_spec,
        _const_spec((1, D_MODEL)),
        _const_spec((D_MODEL, D_MODEL)),
        _const_spec((D_MODEL, D_FF)),
        _const_spec((D_FF, D_MODEL)),
    ]
    args = [a, x, mod, g, wo, w1, w2]
    if gf is not None:
        in_specs.append(_const_spec((1, D_MODEL)))
        args.append(gf)
    return pl.pallas_call(
        functools.partial(_post_kernel, final=gf is not None),
        out_shape=jax.ShapeDtypeStruct((rows, D_MODEL), F32),
        grid=(rows // ROW_TILE,),
        in_specs=in_specs,
        out_specs=pl.BlockSpec((ROW_TILE, D_MODEL), lambda i: (i, 0)),
        compiler_params=_params(("parallel",)),
        name="post_final" if gf is not None else "post",
    )(*args)


def _ret_in_kernel(x_ref, mod_ref, g_ref, w_ref, *rest, use_rope):
    if use_rope:
        cos_ref, sin_ref, q_ref, k_ref, v_ref, gate_ref = rest
    else:
        q_ref, k_ref, v_ref, gate_ref = rest
    h = _norm_modulate(x_ref[...], g_ref[...], mod_ref, 0).astype(BF16)
    width = RET_HEADS * RET_DK
    for part, ref in enumerate((q_ref, k_ref, v_ref, gate_ref)):
        t = jnp.dot(h, w_ref[:, part * width:(part + 1) * width], preferred_element_type=F32)
        if part < 2 and use_rope:
            for c in range(width // V7X_LANES):
                half = (c % (RET_DK // V7X_LANES)) * V7X_LANES
                tc = t[:, c * V7X_LANES:(c + 1) * V7X_LANES]
                tc = (tc * cos_ref[:, half:half + V7X_LANES]
                      + _swap_halves(tc, RET_DK // 4) * sin_ref[:, half:half + V7X_LANES])
                if part == 0:
                    tc = tc * RET_DK ** -0.5
                ref[:, c * V7X_LANES:(c + 1) * V7X_LANES] = tc.astype(ref.dtype)
        else:
            if part == 0:
                t = t * RET_DK ** -0.5
            ref[...] = t.astype(ref.dtype)


def _ret_in(x, mod, mod_spec, g, w, rope):
    rows = x.shape[0]
    width = RET_HEADS * RET_DK
    in_specs = [
        pl.BlockSpec((ROW_TILE, D_MODEL), lambda i: (i, 0)),
        mod_spec,
        _const_spec((1, D_MODEL)),
        _const_spec((D_MODEL, RET_W)),
    ]
    args = [x, mod, g, w]
    if rope is not None:
        seq_tiles = rope[0].shape[0] // ROW_TILE
        in_specs += [pl.BlockSpec((ROW_TILE, RET_DK), lambda i: (i % seq_tiles, 0))] * 2
        args += list(rope)
    out_block = pl.BlockSpec((ROW_TILE, width), lambda i: (i, 0))
    return pl.pallas_call(
        functools.partial(_ret_in_kernel, use_rope=rope is not None),
        out_shape=(jax.ShapeDtypeStruct((rows, width), BF16),
                   jax.ShapeDtypeStruct((rows, width), BF16),
                   jax.ShapeDtypeStruct((rows, width), BF16),
                   jax.ShapeDtypeStruct((rows, width), F32)),
        grid=(rows // ROW_TILE,),
        in_specs=in_specs,
        out_specs=(out_block, out_block, out_block, out_block),
        compiler_params=_params(("parallel",)),
        name="ret_in_rope" if rope is not None else "ret_in",
    )(*args)


def _log_sigmoid(x):
    return jnp.minimum(x, 0.0) - jnp.log(1.0 + jnp.exp(-jnp.abs(x)))


def _ret_kernel(q_ref, k_ref, v_ref, gate_ref, lg_ref, gn_ref, *rest, n, tq, heads, has_state):
    if has_state:
        s0_ref, y_ref, d_ref = rest
    else:
        y_ref, st_ref, d_ref = rest
    b = pl.program_id(1)
    qi = pl.program_id(2)
    r0 = qi * tq
    tn = (((0,), (0,)), ((), ()))
    nt = (((1,), (1,)), ((), ()))
    i_col = (r0 + lax.broadcasted_iota(jnp.int32, (tq, 1), 0)).astype(F32)
    for hh in range(heads):
        lg_f = _log_sigmoid(lg_ref[hh, 0, 0:1, 0:1])
        lg_b = _log_sigmoid(lg_ref[hh, 1, 0:1, 0:1])

        @pl.when(b == 0)
        def _():
            ii = r0 + lax.broadcasted_iota(jnp.int32, (tq, n), 0)
            jj = lax.broadcasted_iota(jnp.int32, (tq, n), 1)
            diff = (ii - jj).astype(F32)
            d_ref[hh, qi] = jnp.exp(jnp.where(diff >= 0.0, lg_f, -lg_b) * diff)

        cols = slice(hh * RET_DK, (hh + 1) * RET_DK)
        q = q_ref[0, :, cols]
        k = k_ref[0, :, cols]
        v = v_ref[0, :, cols]
        s = lax.dot_general(q, k, nt, preferred_element_type=F32) * d_ref[hh, qi]
        o = jnp.dot(s.astype(BF16), v, preferred_element_type=F32)
        if has_state:
            o = o + (jnp.dot(q, s0_ref[0, 0, hh].astype(BF16), preferred_element_type=F32)
                     * jnp.exp(lg_f * (i_col + 1.0))
                     + jnp.dot(q, s0_ref[0, 1, hh].astype(BF16), preferred_element_type=F32)
                     * jnp.exp(lg_b * (n - i_col)))
        else:
            kf = k.astype(F32)
            st_ref[0, 0, hh] = lax.dot_general((kf * jnp.exp(lg_f * (n - 1.0 - i_col))).astype(BF16), v, tn,
                                               preferred_element_type=F32)
            st_ref[0, 1, hh] = lax.dot_general((kf * jnp.exp(lg_b * i_col)).astype(BF16), v, tn,
                                               preferred_element_type=F32)
        mu = jnp.mean(o, axis=-1, keepdims=True)
        oc = o - mu
        var = jnp.mean(oc * oc, axis=-1, keepdims=True)
        on = oc * lax.rsqrt(var + EPS) * gn_ref[:, cols]
        gt = gate_ref[0, :, cols]
        y_ref[0, :, cols] = (gt * _sigmoid(gt) * on).astype(y_ref.dtype)


def _retention(q, k, v, gate, lg, gn, state0, heads, tq):
    b, n, _ = q.shape
    assert n % tq == 0 and RET_HEADS % heads == 0 and (state0 is not None or tq == n)
    width = heads * RET_DK
    q_tile = pl.BlockSpec((1, tq, width), lambda hg, bi, qi: (bi, qi, hg))
    seq = pl.BlockSpec((1, n, width), lambda hg, bi, qi: (bi, 0, hg))
    st = pl.BlockSpec((1, 2, heads, RET_DK, RET_DV), lambda hg, bi, qi: (bi, 0, hg, 0, 0))
    in_specs = [q_tile, seq, seq, q_tile,
                pl.BlockSpec((heads, 2, V7X_SUBLANES, V7X_LANES), lambda hg, bi, qi: (hg, 0, 0, 0)),
                pl.BlockSpec((1, width), lambda hg, bi, qi: (0, hg))]
    args = [q, k, v, gate, lg, gn]
    y_shape = jax.ShapeDtypeStruct((b, n, RET_HEADS * RET_DV), BF16)
    if state0 is not None:
        in_specs.append(st)
        args.append(state0)
        out_shape, out_specs = y_shape, q_tile
    else:
        out_shape = (y_shape, jax.ShapeDtypeStruct((b, 2, RET_HEADS, RET_DK, RET_DV), F32))
        out_specs = (q_tile, st)
    return pl.pallas_call(
        functools.partial(_ret_kernel, n=n, tq=tq, heads=heads, has_state=state0 is not None),
        out_shape=out_shape,
        grid=(RET_HEADS // heads, b, n // tq),
        in_specs=in_specs,
        out_specs=out_specs,
        scratch_shapes=[pltpu.VMEM((heads, n // tq, tq, n), F32)],
        compiler_params=_params(("arbitrary", "arbitrary", "arbitrary")),
        name="retention_state" if state0 is not None else "retention",
    )(*args)


def kernel(x_prompt, x_sample, cache_k, cache_v, state_ret, c, c_ctx, w_mod, b_mod, norm_g,
           attn_w_qkv, attn_q_norm, attn_k_norm, attn_w_o, ret_w_qkvg, ret_decay_logit, ret_gn_w,
           ret_w_o, mlp_w1, mlp_w2, final_norm_g):
    bp, sp, d = x_prompt.shape
    bs, ss, _ = x_sample.shape
    depth = w_mod.shape[0]
    assert d == D_MODEL and depth == 2 and 1 + bs <= MOD_ROWS
    assert (bp * sp) % ROW_TILE == 0 and ss % ROW_TILE == 0

    cond = jnp.concatenate([c_ctx[None, :], c, jnp.zeros((MOD_ROWS - 1 - bs, d), F32)], axis=0)
    mod = _modulation(cond, w_mod, b_mod)
    mod = mod.reshape(depth, MOD_ROWS, 1, 6 * d)
    ctx_mod = _mod_spec(0, 0)
    lat_mod = _mod_spec(ss, 1)

    xp = x_prompt.reshape(bp * sp, d)
    xs = x_sample.reshape(bs * ss, d)
    nkv = N_KV_HEADS * HEAD_DIM

    g0 = norm_g[0, 0][None, :]
    g1 = norm_g[0, 1][None, :]
    wqkv = attn_w_qkv[0].astype(BF16)
    qn = attn_q_norm[0][None, :]
    kn = attn_k_norm[0][None, :]
    wo = attn_w_o[0].astype(BF16)
    w1 = mlp_w1[0].astype(BF16)
    w2 = mlp_w2[0].astype(BF16)

    qp, kp, vp = _attn_in(xp, mod[0], ctx_mod, g0, wqkv, qn, kn, None, F32)
    ap = _attention(qp.reshape(bp, sp, -1), kp.reshape(bp, sp, nkv), vp.reshape(bp, sp, nkv))
    xp = _post(ap.reshape(bp * sp, -1), xp, mod[0], ctx_mod, g1, wo, w1, w2)

    qs, ks, vs = _attn_in(xs, mod[0], lat_mod, g0, wqkv, qn, kn, _rope_tables(ss, HEAD_DIM), BF16)
    past = cache_k.shape[2]
    cache = (cache_k[:, 0].reshape(bs, past, nkv), cache_v[:, 0].reshape(bs, past, nkv))
    as_ = _attention(qs.reshape(bs, ss, -1), ks.reshape(bs, ss, nkv), vs.reshape(bs, ss, nkv), cache)
    xs = _post(as_.reshape(bs * ss, -1), xs, mod[0], lat_mod, g1, wo, w1, w2)

    g0 = norm_g[1, 0][None, :]
    g1 = norm_g[1, 1][None, :]
    wr = ret_w_qkvg[0].astype(BF16)
    wo = ret_w_o[0].astype(BF16)
    w1 = mlp_w1[1].astype(BF16)
    w2 = mlp_w2[1].astype(BF16)
    gn = ret_gn_w[0][None, :]
    gf = final_norm_g[None, :]
    lg = jnp.broadcast_to(ret_decay_logit[0].T[:, :, None, None],
                          (RET_HEADS, 2, V7X_SUBLANES, V7X_LANES))
    hw = RET_HEADS * RET_DK

    q, k, v, gate = _ret_in(xp, mod[1], ctx_mod, g0, wr, None)
    yp, new_state = _retention(q.reshape(bp, sp, hw), k.reshape(bp, sp, hw), v.reshape(bp, sp, hw),
                               gate.reshape(bp, sp, hw), lg, gn, None, RET_HEADS, sp)
    y_prompt = _post(yp.reshape(bp * sp, hw), xp, mod[1], ctx_mod, g1, wo, w1, w2, gf)

    q, k, v, gate = _ret_in(xs, mod[1], lat_mod, g0, wr, _rope_tables(ss, RET_DK))
    ys = _retention(q.reshape(bs, ss, hw), k.reshape(bs, ss, hw), v.reshape(bs, ss, hw),
                    gate.reshape(bs, ss, hw), lg, gn, state_ret[:, 0], 1, RET_Q_TILE)
    y_sample = _post(ys.reshape(bs * ss, hw), xs, mod[1], lat_mod, g1, wo, w1, w2, gf)

    return (y_prompt.reshape(bp, sp, d),
            y_sample.reshape(bs, ss, d),
            kp.reshape(bp, 1, sp, N_KV_HEADS, HEAD_DIM),
            vp.reshape(bp, 1, sp, N_KV_HEADS, HEAD_DIM),
            new_state.reshape(bp, 1, 2, RET_HEADS, RET_DK, RET_DV))
```

```python
import functools

import numpy as np
import jax
import jax.numpy as jnp
from jax import lax
from jax.experimental import pallas as pl
from jax.experimental.pallas import tpu as pltpu

F32 = jnp.float32
BF16 = jnp.bfloat16

D_MODEL = 1024
GRID_W = 64
N_HEADS = 8
N_KV_HEADS = 2
HEAD_DIM = 128
GROUP = N_HEADS // N_KV_HEADS
ROPE_BASE = 10000.0
RET_HEADS = 4
RET_DK = 256
RET_DV = 256
D_FF = 4 * D_MODEL
Q_SCALE = HEAD_DIM ** -0.5 * float(np.log2(np.e))
EPS = 1e-6
QKV_W = (N_HEADS + 2 * N_KV_HEADS) * HEAD_DIM
RET_W = 2 * RET_HEADS * RET_DK + 2 * RET_HEADS * RET_DV

V7X_LANES = 128
V7X_SUBLANES = 8
MOD_ROWS = 16
ROW_TILE = 512
ATTN_Q_TILE = 256
RET_Q_TILE = 512
VMEM_LIMIT = 56 * 1024 * 1024


def _params(sem, vmem=VMEM_LIMIT):
    return pltpu.CompilerParams(dimension_semantics=sem, vmem_limit_bytes=vmem)


def _const_spec(shape):
    nd = len(shape)
    return pl.BlockSpec(shape, lambda *_: (0,) * nd, pipeline_mode=pl.Buffered(1))


def _sigmoid(x):
    return 1.0 / (1.0 + jnp.exp(-x))


def _mod_kernel(cond_ref, w_ref, b_ref, o_ref):
    s = cond_ref[...]
    s = (s * _sigmoid(s)).astype(BF16)
    o_ref[0] = jnp.dot(s, w_ref[0].astype(BF16), preferred_element_type=F32) + b_ref[0]


def _modulation(cond, w_mod, b_mod):
    depth = w_mod.shape[0]
    tn = 1024
    return pl.pallas_call(
        _mod_kernel,
        out_shape=jax.ShapeDtypeStruct((depth, MOD_ROWS, 6 * D_MODEL), F32),
        grid=(depth, 6 * D_MODEL // tn),
        in_specs=[
            pl.BlockSpec((MOD_ROWS, D_MODEL), lambda i, j: (0, 0)),
            pl.BlockSpec((1, D_MODEL, tn), lambda i, j: (i, 0, j)),
            pl.BlockSpec((1, 1, tn), lambda i, j: (i, 0, j)),
        ],
        out_specs=pl.BlockSpec((1, MOD_ROWS, tn), lambda i, j: (i, 0, j)),
        compiler_params=_params(("parallel", "parallel")),
        name="modulation",
    )(cond, w_mod, b_mod.reshape(depth, 1, 6 * D_MODEL))


def _norm_modulate(x, g, mod_ref, which):
    y = x * lax.rsqrt(jnp.mean(x * x, axis=-1, keepdims=True) + EPS) * g
    shift = mod_ref[0, :, (3 * which) * D_MODEL:(3 * which + 1) * D_MODEL]
    scale = mod_ref[0, :, (3 * which + 1) * D_MODEL:(3 * which + 2) * D_MODEL]
    return y * (1.0 + scale) + shift


def _mod_spec(rows_per_mod_row, first_row):
    tiles = rows_per_mod_row // ROW_TILE if rows_per_mod_row else 0
    if tiles:
        return pl.BlockSpec((1, 1, 6 * D_MODEL), lambda i: (first_row + i // tiles, 0, 0))
    return pl.BlockSpec((1, 1, 6 * D_MODEL), lambda i: (first_row, 0, 0))


def _rope_tables(n, dim):
    quarter = dim // 4
    rows = np.repeat(np.arange(n // GRID_W), GRID_W).astype(np.float64)
    cols = np.tile(np.arange(GRID_W), n // GRID_W).astype(np.float64)
    freqs = ROPE_BASE ** (-np.arange(quarter, dtype=np.float64) / quarter)
    ar = rows[:, None] * freqs
    ac = cols[:, None] * freqs
    cos = np.concatenate([np.cos(ar), np.cos(ar), np.cos(ac), np.cos(ac)], axis=1)
    sin = np.concatenate([-np.sin(ar), np.sin(ar), -np.sin(ac), np.sin(ac)], axis=1)
    return jnp.asarray(cos, F32), jnp.asarray(sin, F32)


def _swap_halves(t, quarter):
    if 2 * quarter == V7X_LANES:
        return pltpu.roll(t, quarter, axis=1)
    lane = lax.broadcasted_iota(jnp.int32, t.shape, 1)
    return jnp.where((lane & quarter) == 0,
                     pltpu.roll(t, V7X_LANES - quarter, axis=1),
                     pltpu.roll(t, quarter, axis=1))


def _attn_in_kernel(x_ref, mod_ref, g_ref, w_ref, qn_ref, kn_ref, *rest, use_rope):
    if use_rope:
        cos_ref, sin_ref, q_ref, k_ref, v_ref = rest
    else:
        q_ref, k_ref, v_ref = rest
    h = _norm_modulate(x_ref[...], g_ref[...], mod_ref, 0).astype(BF16)
    qkv = jnp.dot(h, w_ref[...], preferred_element_type=F32)
    nq = N_HEADS * HEAD_DIM
    nk = N_KV_HEADS * HEAD_DIM
    tm = x_ref.shape[0]

    def put_kv(ref, kh, val):
        if use_rope:
            ref[:, kh * HEAD_DIM:(kh + 1) * HEAD_DIM] = val.astype(ref.dtype)
        else:
            ref[pl.ds(kh, tm, stride=N_KV_HEADS), :] = val.astype(ref.dtype)

    for head in range(N_HEADS + N_KV_HEADS):
        t = qkv[:, head * HEAD_DIM:(head + 1) * HEAD_DIM]
        w = qn_ref[...] if head < N_HEADS else kn_ref[...]
        t = t * lax.rsqrt(jnp.mean(t * t, axis=-1, keepdims=True) + EPS) * w
        if use_rope:
            t = t * cos_ref[...] + _swap_halves(t, HEAD_DIM // 4) * sin_ref[...]
        if head < N_HEADS:
            q_ref[:, head * HEAD_DIM:(head + 1) * HEAD_DIM] = (t * Q_SCALE).astype(q_ref.dtype)
        else:
            put_kv(k_ref, head - N_HEADS, t)
    for kh in range(N_KV_HEADS):
        put_kv(v_ref, kh, qkv[:, nq + nk + kh * HEAD_DIM:nq + nk + (kh + 1) * HEAD_DIM])


def _attn_in(x, mod, mod_spec, g, w, qn, kn, rope):
    rows = x.shape[0]
    nk = N_KV_HEADS * HEAD_DIM
    if rope is None:
        kv_shape = jax.ShapeDtypeStruct((rows * N_KV_HEADS, HEAD_DIM), F32)
        kv_spec = pl.BlockSpec((ROW_TILE * N_KV_HEADS, HEAD_DIM), lambda i: (i, 0))
    else:
        kv_shape = jax.ShapeDtypeStruct((rows, nk), BF16)
        kv_spec = pl.BlockSpec((ROW_TILE, nk), lambda i: (i, 0))
    in_specs = [
        pl.BlockSpec((ROW_TILE, D_MODEL), lambda i: (i, 0)),
        mod_spec,
        _const_spec((1, D_MODEL)),
        _const_spec((D_MODEL, QKV_W)),
        _const_spec((1, HEAD_DIM)),
        _const_spec((1, HEAD_DIM)),
    ]
    args = [x, mod, g, w, qn, kn]
    if rope is not None:
        seq_tiles = rope[0].shape[0] // ROW_TILE
        in_specs += [pl.BlockSpec((ROW_TILE, HEAD_DIM), lambda i: (i % seq_tiles, 0))] * 2
        args += list(rope)
    return pl.pallas_call(
        functools.partial(_attn_in_kernel, use_rope=rope is not None),
        out_shape=(jax.ShapeDtypeStruct((rows, N_HEADS * HEAD_DIM), BF16), kv_shape, kv_shape),
        grid=(rows // ROW_TILE,),
        in_specs=in_specs,
        out_specs=(pl.BlockSpec((ROW_TILE, N_HEADS * HEAD_DIM), lambda i: (i, 0)), kv_spec, kv_spec),
        compiler_params=_params(("parallel",)),
        name="attn_in_rope" if rope is not None else "attn_in",
    )(*args)


def _kv_head(ref, kv):
    if ref.dtype == BF16:
        return ref[0, :, kv * HEAD_DIM:(kv + 1) * HEAD_DIM]
    tokens = ref.shape[1] // N_KV_HEADS
    return ref[0, pl.ds(kv, tokens, stride=N_KV_HEADS), :].astype(BF16)


def _attn_kernel(q_ref, k_ref, v_ref, *rest, cached):
    if cached:
        ck_ref, cv_ref, o_ref = rest
    else:
        (o_ref,) = rest
    nt = (((1,), (1,)), ((), ()))
    for kv in range(N_KV_HEADS):
        k = _kv_head(k_ref, kv)
        v = _kv_head(v_ref, kv)
        if cached:
            ck = _kv_head(ck_ref, kv)
            cv = _kv_head(cv_ref, kv)
        for g in range(GROUP):
            cols = slice((kv * GROUP + g) * HEAD_DIM, (kv * GROUP + g + 1) * HEAD_DIM)
            q = q_ref[0, :, cols]
            s = lax.dot_general(q, k, nt, preferred_element_type=F32)
            m = jnp.max(s, axis=-1, keepdims=True)
            if cached:
                sc = lax.dot_general(q, ck, nt, preferred_element_type=F32)
                m = jnp.maximum(m, jnp.max(sc, axis=-1, keepdims=True))
            p = jnp.exp2(s - m)
            l = jnp.sum(p, axis=-1, keepdims=True)
            o = jnp.dot(p.astype(BF16), v, preferred_element_type=F32)
            if cached:
                pc = jnp.exp2(sc - m)
                l = l + jnp.sum(pc, axis=-1, keepdims=True)
                o = o + jnp.dot(pc.astype(BF16), cv, preferred_element_type=F32)
            o_ref[0, :, cols] = (o / l).astype(o_ref.dtype)


def _attention(q, k, v, cache=None):
    b, n, width = q.shape
    tq = min(ATTN_Q_TILE, n)
    in_specs = [pl.BlockSpec((1, tq, width), lambda bi, qi: (bi, qi, 0))]
    in_specs += [pl.BlockSpec((1,) + k.shape[1:], lambda bi, qi: (bi, 0, 0))] * 2
    args = [q, k, v]
    if cache is not None:
        in_specs += [pl.BlockSpec((1,) + cache[0].shape[1:], lambda bi, qi: (bi, 0, 0))] * 2
        args += list(cache)
    return pl.pallas_call(
        functools.partial(_attn_kernel, cached=cache is not None),
        out_shape=jax.ShapeDtypeStruct((b, n, width), BF16),
        grid=(b, n // tq),
        in_specs=in_specs,
        out_specs=pl.BlockSpec((1, tq, width), lambda bi, qi: (bi, qi, 0)),
        compiler_params=_params(("parallel", "parallel")),
        name="attention_cached" if cache is not None else "attention",
    )(*args)


def _post_kernel(a_ref, x_ref, mod_ref, g_ref, wo_ref, w1_ref, w2_ref, *rest, final):
    if final:
        gf_ref, o_ref = rest
    else:
        (o_ref,) = rest
    gate1 = mod_ref[0, :, 2 * D_MODEL:3 * D_MODEL]
    gate2 = mod_ref[0, :, 5 * D_MODEL:6 * D_MODEL]
    x = x_ref[...] + gate1 * jnp.dot(a_ref[...], wo_ref[...], preferred_element_type=F32)
    h = _norm_modulate(x, g_ref[...], mod_ref, 1).astype(BF16)
    ff_tile = D_FF // 4
    y = None
    for f in range(D_FF // ff_tile):
        u = jnp.dot(h, w1_ref[:, f * ff_tile:(f + 1) * ff_tile], preferred_element_type=F32)
        u = jnp.square(jnp.maximum(u, 0.0)).astype(BF16)
        part = jnp.dot(u, w2_ref[f * ff_tile:(f + 1) * ff_tile, :], preferred_element_type=F32)
        y = part if y is None else y + part
    x = x + gate2 * y
    if final:
        x = x * lax.rsqrt(jnp.mean(x * x, axis=-1, keepdims=True) + EPS) * gf_ref[...]
    o_ref[...] = x


def _post(a, x, mod, mod_spec, g, wo, w1, w2, gf=None):
    rows = x.shape[0]
    in_specs = [
        pl.BlockSpec((ROW_TILE, D_MODEL), lambda i: (i, 0)),
        pl.BlockSpec((ROW_TILE, D_MODEL), lambda i: (i, 0)),
        mod_spec,
        _const_spec((1, D_MODEL)),
        _const_spec((D_MODEL, D_MODEL)),
        _const_spec((D_MODEL, D_FF)),
        _const_spec((D_FF, D_MODEL)),
    ]
    args = [a, x, mod, g, wo, w1, w2]
    if gf is not None:
        in_specs.append(_const_spec((1, D_MODEL)))
        args.append(gf)
    return pl.pallas_call(
        functools.partial(_post_kernel, final=gf is not None),
        out_shape=jax.ShapeDtypeStruct((rows, D_MODEL), F32),
        grid=(rows // ROW_TILE,),
        in_specs=in_specs,
        out_specs=pl.BlockSpec((ROW_TILE, D_MODEL), lambda i: (i, 0)),
        compiler_params=_params(("parallel",)),
        name="post_final" if gf is not None else "post",
    )(*args)


def _ret_in_kernel(x_ref, mod_ref, g_ref, w_ref, *rest, use_rope):
    if use_rope:
        cos_ref, sin_ref, q_ref, k_ref, v_ref, gate_ref = rest
    else:
        q_ref, k_ref, v_ref, gate_ref = rest
    h = _norm_modulate(x_ref[...], g_ref[...], mod_ref, 0).astype(BF16)
    width = RET_HEADS * RET_DK
    for part, ref in enumerate((q_ref, k_ref, v_ref, gate_ref)):
        t = jnp.dot(h, w_ref[:, part * width:(part + 1) * width], preferred_element_type=F32)
        if part < 2 and use_rope:
            for c in range(width // V7X_LANES):
                half = (c % (RET_DK // V7X_LANES)) * V7X_LANES
                tc = t[:, c * V7X_LANES:(c + 1) * V7X_LANES]
                tc = (tc * cos_ref[:, half:half + V7X_LANES]
                      + _swap_halves(tc, RET_DK // 4) * sin_ref[:, half:half + V7X_LANES])
                if part == 0:
                    tc = tc * RET_DK ** -0.5
                ref[:, c * V7X_LANES:(c + 1) * V7X_LANES] = tc.astype(ref.dtype)
        else:
            if part == 0:
                t = t * RET_DK ** -0.5
            ref[...] = t.astype(ref.dtype)


def _ret_in(x, mod, mod_spec, g, w, rope):
    rows = x.shape[0]
    width = RET_HEADS * RET_DK
    in_specs = [
        pl.BlockSpec((ROW_TILE, D_MODEL), lambda i: (i, 0)),
        mod_spec,
        _const_spec((1, D_MODEL)),
        _const_spec((D_MODEL, RET_W)),
    ]
    args = [x, mod, g, w]
    if rope is not None:
        seq_tiles = rope[0].shape[0] // ROW_TILE
        in_specs += [pl.BlockSpec((ROW_TILE, RET_DK), lambda i: (i % seq_tiles, 0))] * 2
        args += list(rope)
    out_block = pl.BlockSpec((ROW_TILE, width), lambda i: (i, 0))
    return pl.pallas_call(
        functools.partial(_ret_in_kernel, use_rope=rope is not None),
        out_shape=(jax.ShapeDtypeStruct((rows, width), BF16),
                   jax.ShapeDtypeStruct((rows, width), BF16),
                   jax.ShapeDtypeStruct((rows, width), BF16),
                   jax.ShapeDtypeStruct((rows, width), F32)),
        grid=(rows // ROW_TILE,),
        in_specs=in_specs,
        out_specs=(out_block, out_block, out_block, out_block),
        compiler_params=_params(("parallel",)),
        name="ret_in_rope" if rope is not None else "ret_in",
    )(*args)


def _log_sigmoid(x):
    return jnp.minimum(x, 0.0) - jnp.log(1.0 + jnp.exp(-jnp.abs(x)))


def _ret_kernel(q_ref, k_ref, v_ref, gate_ref, lg_ref, gn_ref, *rest, n, tq, heads, has_state):
    if has_state:
        s0_ref, y_ref, d_ref = rest
    else:
        y_ref, st_ref, d_ref = rest
    b = pl.program_id(1)
    qi = pl.program_id(2)
    r0 = qi * tq
    tn = (((0,), (0,)), ((), ()))
    nt = (((1,), (1,)), ((), ()))
    i_col = (r0 + lax.broadcasted_iota(jnp.int32, (tq, 1), 0)).astype(F32)
    for hh in range(heads):
        lg_f = _log_sigmoid(lg_ref[hh, 0, 0:1, 0:1])
        lg_b = _log_sigmoid(lg_ref[hh, 1, 0:1, 0:1])

        @pl.when(b == 0)
        def _():
            ii = r0 + lax.broadcasted_iota(jnp.int32, (tq, n), 0)
            jj = lax.broadcasted_iota(jnp.int32, (tq, n), 1)
            diff = (ii - jj).astype(F32)
            d_ref[hh, qi] = jnp.exp(jnp.where(diff >= 0.0, lg_f, -lg_b) * diff)

        cols = slice(hh * RET_DK, (hh + 1) * RET_DK)
        q = q_ref[0, :, cols]
        k = k_ref[0, :, cols]
        v = v_ref[0, :, cols]
        s = lax.dot_general(q, k, nt, preferred_element_type=F32) * d_ref[hh, qi]
        o = jnp.dot(s.astype(BF16), v, preferred_element_type=F32)
        if has_state:
            o = o + (jnp.dot(q, s0_ref[0, 0, 0, hh].astype(BF16), preferred_element_type=F32)
                     * jnp.exp(lg_f * (i_col + 1.0))
                     + jnp.dot(q, s0_ref[0, 0, 1, hh].astype(BF16), preferred_element_type=F32)
                     * jnp.exp(lg_b * (n - i_col)))
        else:
            kf = k.astype(F32)
            st_ref[0, 0, 0, hh] = lax.dot_general((kf * jnp.exp(lg_f * (n - 1.0 - i_col))).astype(BF16), v, tn,
                                                  preferred_element_type=F32)
            st_ref[0, 0, 1, hh] = lax.dot_general((kf * jnp.exp(lg_b * i_col)).astype(BF16), v, tn,
                                                  preferred_element_type=F32)
        mu = jnp.mean(o, axis=-1, keepdims=True)
        oc = o - mu
        var = jnp.mean(oc * oc, axis=-1, keepdims=True)
        on = oc * lax.rsqrt(var + EPS) * gn_ref[:, cols]
        gt = gate_ref[0, :, cols]
        y_ref[0, :, cols] = (gt * _sigmoid(gt) * on).astype(y_ref.dtype)


def _retention(q, k, v, gate, lg, gn, state0, heads, tq):
    b, n, _ = q.shape
    assert n % tq == 0 and RET_HEADS % heads == 0 and (state0 is not None or tq == n)
    width = heads * RET_DK
    q_tile = pl.BlockSpec((1, tq, width), lambda hg, bi, qi: (bi, qi, hg))
    seq = pl.BlockSpec((1, n, width), lambda hg, bi, qi: (bi, 0, hg))
    st = pl.BlockSpec((1, 1, 2, heads, RET_DK, RET_DV), lambda hg, bi, qi: (bi, 0, 0, hg, 0, 0))
    in_specs = [q_tile, seq, seq, q_tile,
                pl.BlockSpec((heads, 2, V7X_SUBLANES, V7X_LANES), lambda hg, bi, qi: (hg, 0, 0, 0)),
                pl.BlockSpec((1, width), lambda hg, bi, qi: (0, hg))]
    args = [q, k, v, gate, lg, gn]
    y_shape = jax.ShapeDtypeStruct((b, n, RET_HEADS * RET_DV), BF16)
    if state0 is not None:
        in_specs.append(st)
        args.append(state0)
        out_shape, out_specs = y_shape, q_tile
    else:
        out_shape = (y_shape, jax.ShapeDtypeStruct((b, 1, 2, RET_HEADS, RET_DK, RET_DV), F32))
        out_specs = (q_tile, st)
    return pl.pallas_call(
        functools.partial(_ret_kernel, n=n, tq=tq, heads=heads, has_state=state0 is not None),
        out_shape=out_shape,
        grid=(RET_HEADS // heads, b, n // tq),
        in_specs=in_specs,
        out_specs=out_specs,
        scratch_shapes=[pltpu.VMEM((heads, n // tq, tq, n), F32)],
        compiler_params=_params(("arbitrary", "arbitrary", "arbitrary")),
        name="retention_state" if state0 is not None else "retention",
    )(*args)


def kernel(x_prompt, x_sample, cache_k, cache_v, state_ret, c, c_ctx, w_mod, b_mod, norm_g,
           attn_w_qkv, attn_q_norm, attn_k_norm, attn_w_o, ret_w_qkvg, ret_decay_logit, ret_gn_w,
           ret_w_o, mlp_w1, mlp_w2, final_norm_g):
    bp, sp, d = x_prompt.shape
    bs, ss, _ = x_sample.shape
    depth = w_mod.shape[0]
    assert d == D_MODEL and depth == 2 and 1 + bs <= MOD_ROWS
    assert (bp * sp) % ROW_TILE == 0 and ss % ROW_TILE == 0

    cond = jnp.concatenate([c_ctx[None, :], c, jnp.zeros((MOD_ROWS - 1 - bs, d), F32)], axis=0)
    mod = _modulation(cond, w_mod, b_mod)
    mod = mod.reshape(depth, MOD_ROWS, 1, 6 * d)
    ctx_mod = _mod_spec(0, 0)
    lat_mod = _mod_spec(ss, 1)

    xp = x_prompt.reshape(bp * sp, d)
    xs = x_sample.reshape(bs * ss, d)
    nkv = N_KV_HEADS * HEAD_DIM

    g0 = norm_g[0, 0][None, :]
    g1 = norm_g[0, 1][None, :]
    wqkv = attn_w_qkv[0].astype(BF16)
    qn = attn_q_norm[0][None, :]
    kn = attn_k_norm[0][None, :]
    wo = attn_w_o[0].astype(BF16)
    w1 = mlp_w1[0].astype(BF16)
    w2 = mlp_w2[0].astype(BF16)

    qp, kp, vp = _attn_in(xp, mod[0], ctx_mod, g0, wqkv, qn, kn, None)
    ap = _attention(qp.reshape(bp, sp, -1), kp.reshape(bp, sp * N_KV_HEADS, HEAD_DIM),
                    vp.reshape(bp, sp * N_KV_HEADS, HEAD_DIM))
    xp = _post(ap.reshape(bp * sp, -1), xp, mod[0], ctx_mod, g1, wo, w1, w2)

    qs, ks, vs = _attn_in(xs, mod[0], lat_mod, g0, wqkv, qn, kn, _rope_tables(ss, HEAD_DIM))
    past = cache_k.shape[2]
    cache = (cache_k[:, 0].reshape(bs, past * N_KV_HEADS, HEAD_DIM),
             cache_v[:, 0].reshape(bs, past * N_KV_HEADS, HEAD_DIM))
    as_ = _attention(qs.reshape(bs, ss, -1), ks.reshape(bs, ss, nkv), vs.reshape(bs, ss, nkv), cache)
    xs = _post(as_.reshape(bs * ss, -1), xs, mod[0], lat_mod, g1, wo, w1, w2)

    g0 = norm_g[1, 0][None, :]
    g1 = norm_g[1, 1][None, :]
    wr = ret_w_qkvg[0].astype(BF16)
    wo = ret_w_o[0].astype(BF16)
    w1 = mlp_w1[1].astype(BF16)
    w2 = mlp_w2[1].astype(BF16)
    gn = ret_gn_w[0][None, :]
    gf = final_norm_g[None, :]
    lg = jnp.broadcast_to(ret_decay_logit[0].T[:, :, None, None],
                          (RET_HEADS, 2, V7X_SUBLANES, V7X_LANES))
    hw = RET_HEADS * RET_DK

    q, k, v, gate = _ret_in(xp, mod[1], ctx_mod, g0, wr, None)
    yp, new_state = _retention(q.reshape(bp, sp, hw), k.reshape(bp, sp, hw), v.reshape(bp, sp, hw),
                               gate.reshape(bp, sp, hw), lg, gn, None, RET_HEADS, sp)
    y_prompt = _post(yp.reshape(bp * sp, hw), xp, mod[1], ctx_mod, g1, wo, w1, w2, gf)

    q, k, v, gate = _ret_in(xs, mod[1], lat_mod, g0, wr, _rope_tables(ss, RET_DK))
    ys = _retention(q.reshape(bs, ss, hw), k.reshape(bs, ss, hw), v.reshape(bs, ss, hw),
                    gate.reshape(bs, ss, hw), lg, gn, state_ret, 1, RET_Q_TILE)
    y_sample = _post(ys.reshape(bs * ss, hw), xs, mod[1], lat_mod, g1, wo, w1, w2, gf)

    return (y_prompt.reshape(bp, sp, d),
            y_sample.reshape(bs, ss, d),
            kp.reshape(bp, 1, sp, N_KV_HEADS, HEAD_DIM),
            vp.reshape(bp, 1, sp, N_KV_HEADS, HEAD_DIM),
            new_state)
```

```python
import functools

import numpy as np
import jax
import jax.numpy as jnp
from jax import lax
from jax.experimental import pallas as pl
from jax.experimental.pallas import tpu as pltpu

F32 = jnp.float32
BF16 = jnp.bfloat16

D_MODEL = 1024
GRID_W = 64
N_HEADS = 8
N_KV_HEADS = 2
HEAD_DIM = 128
GROUP = N_HEADS // N_KV_HEADS
ROPE_BASE = 10000.0
RET_HEADS = 4
RET_DK = 256
RET_DV = 256
D_FF = 4 * D_MODEL
Q_SCALE = HEAD_DIM ** -0.5 * float(np.log2(np.e))
EPS = 1e-6
QKV_W = (N_HEADS + 2 * N_KV_HEADS) * HEAD_DIM
RET_W = 2 * RET_HEADS * RET_DK + 2 * RET_HEADS * RET_DV

V7X_LANES = 128
V7X_SUBLANES = 8
MOD_ROWS = 16
ROW_TILE = 512
ATTN_Q_TILE = 256
RET_Q_TILE = 1024
RET_SUB_TILE = 256
VMEM_LIMIT = 56 * 1024 * 1024


def _params(sem, vmem=VMEM_LIMIT):
    return pltpu.CompilerParams(dimension_semantics=sem, vmem_limit_bytes=vmem)


def _const_spec(shape):
    nd = len(shape)
    return pl.BlockSpec(shape, lambda *_: (0,) * nd, pipeline_mode=pl.Buffered(1))


def _sigmoid(x):
    return 1.0 / (1.0 + jnp.exp(-x))


def _mod_kernel(cond_ref, w_ref, b_ref, o_ref):
    s = cond_ref[...]
    s = (s * _sigmoid(s)).astype(BF16)
    o_ref[0] = jnp.dot(s, w_ref[0].astype(BF16), preferred_element_type=F32) + b_ref[0]


def _modulation(cond, w_mod, b_mod):
    depth = w_mod.shape[0]
    tn = 1024
    return pl.pallas_call(
        _mod_kernel,
        out_shape=jax.ShapeDtypeStruct((depth, MOD_ROWS, 6 * D_MODEL), F32),
        grid=(depth, 6 * D_MODEL // tn),
        in_specs=[
            pl.BlockSpec((MOD_ROWS, D_MODEL), lambda i, j: (0, 0)),
            pl.BlockSpec((1, D_MODEL, tn), lambda i, j: (i, 0, j)),
            pl.BlockSpec((1, 1, tn), lambda i, j: (i, 0, j)),
        ],
        out_specs=pl.BlockSpec((1, MOD_ROWS, tn), lambda i, j: (i, 0, j)),
        compiler_params=_params(("parallel", "parallel")),
        name="modulation",
    )(cond, w_mod, b_mod.reshape(depth, 1, 6 * D_MODEL))


def _norm_modulate(x, g, mod_ref, which):
    y = x * lax.rsqrt(jnp.mean(x * x, axis=-1, keepdims=True) + EPS) * g
    shift = mod_ref[0, :, (3 * which) * D_MODEL:(3 * which + 1) * D_MODEL]
    scale = mod_ref[0, :, (3 * which + 1) * D_MODEL:(3 * which + 2) * D_MODEL]
    return y * (1.0 + scale) + shift


def _mod_spec(rows_per_mod_row, first_row):
    tiles = rows_per_mod_row // ROW_TILE if rows_per_mod_row else 0
    if tiles:
        return pl.BlockSpec((1, 1, 6 * D_MODEL), lambda i: (first_row + i // tiles, 0, 0))
    return pl.BlockSpec((1, 1, 6 * D_MODEL), lambda i: (first_row, 0, 0))


def _rope_tables(n, dim, paired):
    quarter = dim // 4
    rows = np.repeat(np.arange(n // GRID_W), GRID_W).astype(np.float64)
    cols = np.tile(np.arange(GRID_W), n // GRID_W).astype(np.float64)
    freqs = ROPE_BASE ** (-np.arange(quarter, dtype=np.float64) / quarter)
    ar = rows[:, None] * freqs
    ac = cols[:, None] * freqs
    if paired:
        cos = np.concatenate([np.cos(ar), np.cos(ac), np.cos(ar), np.cos(ac)], axis=1)
        sin = np.concatenate([-np.sin(ar), -np.sin(ac), np.sin(ar), np.sin(ac)], axis=1)
    else:
        cos = np.concatenate([np.cos(ar), np.cos(ar), np.cos(ac), np.cos(ac)], axis=1)
        sin = np.concatenate([-np.sin(ar), np.sin(ar), -np.sin(ac), np.sin(ac)], axis=1)
    return jnp.asarray(cos, F32), jnp.asarray(sin, F32)


_Q = HEAD_DIM // 4
_PAIRED_ORDER = np.concatenate([np.arange(0, _Q), np.arange(2 * _Q, 3 * _Q),
                                np.arange(_Q, 2 * _Q), np.arange(3 * _Q, 4 * _Q)])


def _swap_halves(t):
    return pltpu.roll(t, V7X_LANES // 2, axis=1)


def _attn_in_kernel(x_ref, mod_ref, g_ref, w_ref, qn_ref, kn_ref, *rest, use_rope):
    if use_rope:
        cos_ref, sin_ref, q_ref, k_ref, v_ref = rest
    else:
        q_ref, k_ref, v_ref = rest
    h = _norm_modulate(x_ref[...], g_ref[...], mod_ref, 0).astype(BF16)
    nq = N_HEADS * HEAD_DIM
    nk = N_KV_HEADS * HEAD_DIM
    tm = x_ref.shape[0]
    pair_w = 2 * HEAD_DIM

    def put_kv(ref, kh, val):
        if use_rope:
            ref[:, kh * HEAD_DIM:(kh + 1) * HEAD_DIM] = val.astype(ref.dtype)
        else:
            ref[pl.ds(kh, tm, stride=N_KV_HEADS), :] = val.astype(ref.dtype)

    qw = qn_ref[...] * Q_SCALE
    kw = kn_ref[...]
    if use_rope:
        q_tabs = (qw * cos_ref[...], _swap_halves(qw) * sin_ref[...])
        k_tabs = (kw * cos_ref[...], _swap_halves(kw) * sin_ref[...])
        ra = lax.broadcasted_iota(jnp.int32, (pair_w, pair_w), 0) // HEAD_DIM
        rb = lax.broadcasted_iota(jnp.int32, (pair_w, pair_w), 1) // HEAD_DIM
        head_sum = jnp.where(ra == rb, 1.0, 0.0).astype(BF16)
    n_pairs = (N_HEADS + N_KV_HEADS) // 2
    t2_next = jnp.dot(h, w_ref[:, 0:pair_w], preferred_element_type=F32)
    for pair in range(n_pairs):
        t2 = t2_next
        if pair + 1 < n_pairs:
            t2_next = jnp.dot(h, w_ref[:, (pair + 1) * pair_w:(pair + 2) * pair_w], preferred_element_type=F32)
        else:
            t2_next = jnp.dot(h, w_ref[:, nq + nk:], preferred_element_type=F32)
        if use_rope:
            ss2 = jnp.dot((t2 * t2).astype(BF16), head_sum, preferred_element_type=F32)
            r2 = lax.rsqrt(ss2 * (1.0 / HEAD_DIM) + EPS)
        for j in range(2):
            head = 2 * pair + j
            t = t2[:, j * HEAD_DIM:(j + 1) * HEAD_DIM]
            if use_rope:
                r = r2[:, j * HEAD_DIM:(j + 1) * HEAD_DIM]
            else:
                r = lax.rsqrt(jnp.mean(t * t, axis=-1, keepdims=True) + EPS)
            if use_rope:
                wc, ws = q_tabs if head < N_HEADS else k_tabs
                t = (t * wc + _swap_halves(t) * ws) * r
            else:
                t = t * r * (qw if head < N_HEADS else kw)
            if head < N_HEADS:
                q_ref[:, head * HEAD_DIM:(head + 1) * HEAD_DIM] = t.astype(q_ref.dtype)
            else:
                put_kv(k_ref, head - N_HEADS, t)
    vv = t2_next
    for kh in range(N_KV_HEADS):
        put_kv(v_ref, kh, vv[:, kh * HEAD_DIM:(kh + 1) * HEAD_DIM])


def _attn_in(x, mod, mod_spec, g, w, qn, kn, rope):
    rows = x.shape[0]
    nk = N_KV_HEADS * HEAD_DIM
    if rope is None:
        kv_shape = jax.ShapeDtypeStruct((rows * N_KV_HEADS, HEAD_DIM), F32)
        kv_spec = pl.BlockSpec((ROW_TILE * N_KV_HEADS, HEAD_DIM), lambda i: (i, 0))
    else:
        kv_shape = jax.ShapeDtypeStruct((rows, nk), BF16)
        kv_spec = pl.BlockSpec((ROW_TILE, nk), lambda i: (i, 0))
    in_specs = [
        pl.BlockSpec((ROW_TILE, D_MODEL), lambda i: (i, 0)),
        mod_spec,
        _const_spec((1, D_MODEL)),
        _const_spec((D_MODEL, QKV_W)),
        _const_spec((1, HEAD_DIM)),
        _const_spec((1, HEAD_DIM)),
    ]
    args = [x, mod, g, w, qn, kn]
    if rope is not None:
        seq_tiles = rope[0].shape[0] // ROW_TILE
        in_specs += [pl.BlockSpec((ROW_TILE, HEAD_DIM), lambda i: (i % seq_tiles, 0))] * 2
        args += list(rope)
    return pl.pallas_call(
        functools.partial(_attn_in_kernel, use_rope=rope is not None),
        out_shape=(jax.ShapeDtypeStruct((rows, N_HEADS * HEAD_DIM), BF16), kv_shape, kv_shape),
        grid=(rows // ROW_TILE,),
        in_specs=in_specs,
        out_specs=(pl.BlockSpec((ROW_TILE, N_HEADS * HEAD_DIM), lambda i: (i, 0)), kv_spec, kv_spec),
        compiler_params=_params(("parallel",)),
        name="attn_in_rope" if rope is not None else "attn_in",
    )(*args)


def _kv_head(ref, kv):
    if ref.dtype == BF16:
        return ref[0, :, kv * HEAD_DIM:(kv + 1) * HEAD_DIM]
    if len(ref.shape) == 5:
        return ref[0, 0, :, kv, :].astype(BF16)
    tokens = ref.shape[1] // N_KV_HEADS
    return ref[0, pl.ds(kv, tokens, stride=N_KV_HEADS), :].astype(BF16)


def _attn_kernel(q_ref, k_ref, v_ref, *rest, cached):
    if cached:
        ck_ref, cv_ref, reorder_ref, o_ref, ck_scr, cv_scr = rest

        @pl.when(pl.program_id(1) == 0)
        def _():
            for kv in range(N_KV_HEADS):
                ck_scr[kv] = jnp.dot(_kv_head(ck_ref, kv), reorder_ref[...],
                                     preferred_element_type=F32).astype(BF16)
                cv_scr[kv] = _kv_head(cv_ref, kv)
    else:
        (o_ref,) = rest
    nt = (((1,), (1,)), ((), ()))
    ks = [_kv_head(k_ref, kv) for kv in range(N_KV_HEADS)]
    vs = [_kv_head(v_ref, kv) for kv in range(N_KV_HEADS)]

    def scores(head):
        kv = head // GROUP
        q = q_ref[0, :, head * HEAD_DIM:(head + 1) * HEAD_DIM]
        s = lax.dot_general(q, ks[kv], nt, preferred_element_type=F32)
        sc = lax.dot_general(q, ck_scr[kv], nt, preferred_element_type=F32) if cached else None
        return s, sc

    nxt = scores(0)
    for head in range(N_HEADS):
        kv = head // GROUP
        s, sc = nxt
        if head + 1 < N_HEADS:
            nxt = scores(head + 1)
        m = jnp.max(s, axis=-1, keepdims=True)
        if cached:
            m = jnp.maximum(m, jnp.max(sc, axis=-1, keepdims=True))
        p = jnp.exp2(s - m)
        l = jnp.sum(p, axis=-1, keepdims=True)
        o = jnp.dot(p.astype(BF16), vs[kv], preferred_element_type=F32)
        if cached:
            pc = jnp.exp2(sc - m)
            l = l + jnp.sum(pc, axis=-1, keepdims=True)
            o = o + jnp.dot(pc.astype(BF16), cv_scr[kv], preferred_element_type=F32)
        o_ref[0, :, head * HEAD_DIM:(head + 1) * HEAD_DIM] = (o / l).astype(o_ref.dtype)


def _attention(q, k, v, cache=None):
    b, n, width = q.shape
    tq = min(ATTN_Q_TILE, n)
    in_specs = [pl.BlockSpec((1, tq, width), lambda bi, qi: (bi, qi, 0))]
    in_specs += [pl.BlockSpec((1,) + k.shape[1:], lambda bi, qi: (bi, 0, 0))] * 2
    args = [q, k, v]
    scratch = []
    if cache is not None:
        in_specs += [pl.BlockSpec((1,) + cache[0].shape[1:], lambda bi, qi: (bi, 0, 0, 0, 0))] * 2
        in_specs.append(pl.BlockSpec((HEAD_DIM, HEAD_DIM), lambda bi, qi: (0, 0)))
        args += list(cache)
        scratch = [pltpu.VMEM((N_KV_HEADS, cache[0].shape[2], HEAD_DIM), BF16)] * 2
    return pl.pallas_call(
        functools.partial(_attn_kernel, cached=cache is not None),
        out_shape=jax.ShapeDtypeStruct((b, n, width), BF16),
        grid=(b, n // tq),
        in_specs=in_specs,
        out_specs=pl.BlockSpec((1, tq, width), lambda bi, qi: (bi, qi, 0)),
        scratch_shapes=scratch,
        compiler_params=_params(("arbitrary", "arbitrary")),
        name="attention_cached" if cache is not None else "attention",
    )(*args)


def _post_kernel(a_ref, x_ref, mod_ref, g_ref, wo_ref, w1_ref, w2_ref, *rest, final):
    if final:
        gf_ref, o_ref = rest
    else:
        (o_ref,) = rest
    gate1 = mod_ref[0, :, 2 * D_MODEL:3 * D_MODEL]
    gate2 = mod_ref[0, :, 5 * D_MODEL:6 * D_MODEL]
    x = x_ref[...] + gate1 * jnp.dot(a_ref[...], wo_ref[...], preferred_element_type=F32)
    h = _norm_modulate(x, g_ref[...], mod_ref, 1).astype(BF16)
    ff_tile = D_FF // 4
    y = None
    for f in range(D_FF // ff_tile):
        u = jnp.dot(h, w1_ref[:, f * ff_tile:(f + 1) * ff_tile], preferred_element_type=F32)
        u = jnp.square(jnp.maximum(u, 0.0)).astype(BF16)
        part = jnp.dot(u, w2_ref[f * ff_tile:(f + 1) * ff_tile, :], preferred_element_type=F32)
        y = part if y is None else y + part
    x = x + gate2 * y
    if final:
        x = x * lax.rsqrt(jnp.mean(x * x, axis=-1, keepdims=True) + EPS) * gf_ref[...]
    o_ref[...] = x


def _post(a, x, mod, mod_spec, g, wo, w1, w2, gf=None):
    rows = x.shape[0]
    in_specs = [
        pl.BlockSpec((ROW_TILE, D_MODEL), lambda i: (i, 0)),
        pl.BlockSpec((ROW_TILE, D_MODEL), lambda i: (i, 0)),
        mod_spec,
        _const_spec((1, D_MODEL)),
        _const_spec((D_MODEL, D_MODEL)),
        _const_spec((D_MODEL, D_FF)),
        _const_spec((D_FF, D_MODEL)),
    ]
    args = [a, x, mod, g, wo, w1, w2]
    if gf is not None:
        in_specs.append(_const_spec((1, D_MODEL)))
        args.append(gf)
    return pl.pallas_call(
        functools.partial(_post_kernel, final=gf is not None),
        out_shape=jax.ShapeDtypeStruct((rows, D_MODEL), F32),
        grid=(rows // ROW_TILE,),
        in_specs=in_specs,
        out_specs=pl.BlockSpec((ROW_TILE, D_MODEL), lambda i: (i, 0)),
        compiler_params=_params(("parallel",)),
        name="post_final" if gf is not None else "post",
    )(*args)


def _ret_in_kernel(x_ref, mod_ref, g_ref, w_ref, *rest, use_rope):
    if use_rope:
        cos_ref, sin_ref, q_ref, k_ref, v_ref, gate_ref = rest
    else:
        q_ref, k_ref, v_ref, gate_ref = rest
    h = _norm_modulate(x_ref[...], g_ref[...], mod_ref, 0).astype(BF16)
    width = RET_HEADS * RET_DK
    for part, ref in enumerate((q_ref, k_ref, v_ref, gate_ref)):
        t = jnp.dot(h, w_ref[:, part * width:(part + 1) * width], preferred_element_type=F32)
        if part < 2 and use_rope:
            for c in range(width // V7X_LANES):
                half = (c % (RET_DK // V7X_LANES)) * V7X_LANES
                tc = t[:, c * V7X_LANES:(c + 1) * V7X_LANES]
                tc = (tc * cos_ref[:, half:half + V7X_LANES]
                      + _swap_halves(tc) * sin_ref[:, half:half + V7X_LANES])
                if part == 0:
                    tc = tc * RET_DK ** -0.5
                ref[:, c * V7X_LANES:(c + 1) * V7X_LANES] = tc.astype(ref.dtype)
        else:
            if part == 0:
                t = t * RET_DK ** -0.5
            ref[...] = t.astype(ref.dtype)


def _ret_in(x, mod, mod_spec, g, w, rope):
    rows = x.shape[0]
    width = RET_HEADS * RET_DK
    in_specs = [
        pl.BlockSpec((ROW_TILE, D_MODEL), lambda i: (i, 0)),
        mod_spec,
        _const_spec((1, D_MODEL)),
        _const_spec((D_MODEL, RET_W)),
    ]
    args = [x, mod, g, w]
    if rope is not None:
        seq_tiles = rope[0].shape[0] // ROW_TILE
        in_specs += [pl.BlockSpec((ROW_TILE, RET_DK), lambda i: (i % seq_tiles, 0))] * 2
        args += list(rope)
    out_block = pl.BlockSpec((ROW_TILE, width), lambda i: (i, 0))
    return pl.pallas_call(
        functools.partial(_ret_in_kernel, use_rope=rope is not None),
        out_shape=(jax.ShapeDtypeStruct((rows, width), BF16),
                   jax.ShapeDtypeStruct((rows, width), BF16),
                   jax.ShapeDtypeStruct((rows, width), BF16),
                   jax.ShapeDtypeStruct((rows, width), F32)),
        grid=(rows // ROW_TILE,),
        in_specs=in_specs,
        out_specs=(out_block, out_block, out_block, out_block),
        compiler_params=_params(("parallel",)),
        name="ret_in_rope" if rope is not None else "ret_in",
    )(*args)


def _log_sigmoid(x):
    return jnp.minimum(x, 0.0) - jnp.log(1.0 + jnp.exp(-jnp.abs(x)))


def _ret_kernel(q_ref, k_ref, v_ref, gate_ref, lg_ref, gn_ref, *rest, n, tq, heads, has_state):
    if has_state:
        s0_ref, y_ref, d_ref = rest
    else:
        y_ref, st_ref, d_ref = rest
    b = pl.program_id(1)
    qi = pl.program_id(2)
    r0 = qi * tq
    tn = (((0,), (0,)), ((), ()))
    nt = (((1,), (1,)), ((), ()))
    sub = min(tq, RET_SUB_TILE)
    lgs = [(_log_sigmoid(lg_ref[hh, 0, 0:1, 0:1]), _log_sigmoid(lg_ref[hh, 1, 0:1, 0:1]))
           for hh in range(heads)]

    @pl.when(b == 0)
    def _():
        ii = r0 + lax.broadcasted_iota(jnp.int32, (tq, n), 0)
        jj = lax.broadcasted_iota(jnp.int32, (tq, n), 1)
        diff = (ii - jj).astype(F32)
        for hh in range(heads):
            lg_f, lg_b = lgs[hh]
            d_ref[hh, qi] = jnp.exp(jnp.where(diff >= 0.0, lg_f, -lg_b) * diff)

    def head_cols(hh):
        return slice(hh * RET_DK, (hh + 1) * RET_DK)

    def scores(unit):
        hh, si = unit
        return lax.dot_general(q_ref[0, si * sub:(si + 1) * sub, head_cols(hh)], k_ref[0, :, head_cols(hh)],
                               nt, preferred_element_type=F32)

    units = [(hh, si) for hh in range(heads) for si in range(tq // sub)]
    nxt = scores(units[0])
    for ui, (hh, si) in enumerate(units):
        s = nxt
        if ui + 1 < len(units):
            nxt = scores(units[ui + 1])
        lg_f, lg_b = lgs[hh]
        cols = head_cols(hh)
        rows = slice(si * sub, (si + 1) * sub)
        v = v_ref[0, :, cols]
        p = (s * d_ref[hh, qi, rows, :]).astype(BF16)
        o = jnp.dot(p, v, preferred_element_type=F32)
        if has_state:
            q = q_ref[0, rows, cols]
            i_col = (r0 + si * sub + lax.broadcasted_iota(jnp.int32, (sub, 1), 0)).astype(F32)
            o = o + (jnp.dot(q, s0_ref[0, 0, 0, hh].astype(BF16), preferred_element_type=F32)
                     * jnp.exp(lg_f * (i_col + 1.0))
                     + jnp.dot(q, s0_ref[0, 0, 1, hh].astype(BF16), preferred_element_type=F32)
                     * jnp.exp(lg_b * (n - i_col)))
        elif si == 0:
            j_col = lax.broadcasted_iota(jnp.int32, (n, 1), 0).astype(F32)
            kf = k_ref[0, :, cols].astype(F32)
            st_ref[0, 0, 0, hh] = lax.dot_general((kf * jnp.exp(lg_f * (n - 1.0 - j_col))).astype(BF16), v, tn,
                                                  preferred_element_type=F32)
            st_ref[0, 0, 1, hh] = lax.dot_general((kf * jnp.exp(lg_b * j_col)).astype(BF16), v, tn,
                                                  preferred_element_type=F32)
        mu = jnp.mean(o, axis=-1, keepdims=True)
        oc = o - mu
        var = jnp.mean(oc * oc, axis=-1, keepdims=True)
        on = oc * lax.rsqrt(var + EPS) * gn_ref[:, cols]
        gt = gate_ref[0, rows, cols]
        y_ref[0, rows, cols] = (gt * _sigmoid(gt) * on).astype(y_ref.dtype)


def _retention(q, k, v, gate, lg, gn, state0, heads, tq):
    b, n, _ = q.shape
    assert n % tq == 0 and RET_HEADS % heads == 0 and (state0 is not None or tq == n)
    width = heads * RET_DK
    q_tile = pl.BlockSpec((1, tq, width), lambda hg, bi, qi: (bi, qi, hg))
    seq = pl.BlockSpec((1, n, width), lambda hg, bi, qi: (bi, 0, hg))
    st = pl.BlockSpec((1, 1, 2, heads, RET_DK, RET_DV), lambda hg, bi, qi: (bi, 0, 0, hg, 0, 0))
    in_specs = [q_tile, seq, seq, q_tile,
                pl.BlockSpec((heads, 2, V7X_SUBLANES, V7X_LANES), lambda hg, bi, qi: (hg, 0, 0, 0)),
                pl.BlockSpec((1, width), lambda hg, bi, qi: (0, hg))]
    args = [q, k, v, gate, lg, gn]
    y_shape = jax.ShapeDtypeStruct((b, n, RET_HEADS * RET_DV), BF16)
    if state0 is not None:
        in_specs.append(st)
        args.append(state0)
        out_shape, out_specs = y_shape, q_tile
    else:
        out_shape = (y_shape, jax.ShapeDtypeStruct((b, 1, 2, RET_HEADS, RET_DK, RET_DV), F32))
        out_specs = (q_tile, st)
    return pl.pallas_call(
        functools.partial(_ret_kernel, n=n, tq=tq, heads=heads, has_state=state0 is not None),
        out_shape=out_shape,
        grid=(RET_HEADS // heads, b, n // tq),
        in_specs=in_specs,
        out_specs=out_specs,
        scratch_shapes=[pltpu.VMEM((heads, n // tq, tq, n), F32)],
        compiler_params=_params(("arbitrary", "arbitrary", "arbitrary")),
        name="retention_state" if state0 is not None else "retention",
    )(*args)


def kernel(x_prompt, x_sample, cache_k, cache_v, state_ret, c, c_ctx, w_mod, b_mod, norm_g,
           attn_w_qkv, attn_q_norm, attn_k_norm, attn_w_o, ret_w_qkvg, ret_decay_logit, ret_gn_w,
           ret_w_o, mlp_w1, mlp_w2, final_norm_g):
    bp, sp, d = x_prompt.shape
    bs, ss, _ = x_sample.shape
    depth = w_mod.shape[0]
    assert d == D_MODEL and depth == 2 and 1 + bs <= MOD_ROWS
    assert (bp * sp) % ROW_TILE == 0 and ss % ROW_TILE == 0

    cond = jnp.concatenate([c_ctx[None, :], c, jnp.zeros((MOD_ROWS - 1 - bs, d), F32)], axis=0)
    mod = _modulation(cond, w_mod, b_mod)
    mod = mod.reshape(depth, MOD_ROWS, 1, 6 * d)
    ctx_mod = _mod_spec(0, 0)
    lat_mod = _mod_spec(ss, 1)

    xp = x_prompt.reshape(bp * sp, d)
    xs = x_sample.reshape(bs * ss, d)
    nkv = N_KV_HEADS * HEAD_DIM

    g0 = norm_g[0, 0][None, :]
    g1 = norm_g[0, 1][None, :]
    wqkv = attn_w_qkv[0].astype(BF16)
    qn = attn_q_norm[0][None, :]
    kn = attn_k_norm[0][None, :]
    wo = attn_w_o[0].astype(BF16)
    w1 = mlp_w1[0].astype(BF16)
    w2 = mlp_w2[0].astype(BF16)

    qp, kp, vp = _attn_in(xp, mod[0], ctx_mod, g0, wqkv, qn, kn, None)
    ap = _attention(qp.reshape(bp, sp, -1), kp.reshape(bp, sp * N_KV_HEADS, HEAD_DIM),
                    vp.reshape(bp, sp * N_KV_HEADS, HEAD_DIM))
    xp = _post(ap.reshape(bp * sp, -1), xp, mod[0], ctx_mod, g1, wo, w1, w2)

    head_cols = (np.arange(N_HEADS + N_KV_HEADS)[:, None] * HEAD_DIM + _PAIRED_ORDER[None, :]).reshape(-1)
    col_order = np.concatenate([head_cols, np.arange(head_cols.size, QKV_W)])
    wqkv_paired = attn_w_qkv[0][:, col_order].astype(BF16)
    qs, ks, vs = _attn_in(xs, mod[0], lat_mod, g0, wqkv_paired, qn[:, _PAIRED_ORDER], kn[:, _PAIRED_ORDER],
                          _rope_tables(ss, HEAD_DIM, True))
    reorder = np.zeros((HEAD_DIM, HEAD_DIM), np.float32)
    reorder[_PAIRED_ORDER, np.arange(HEAD_DIM)] = 1.0
    cache = (cache_k, cache_v, jnp.asarray(reorder, BF16))
    as_ = _attention(qs.reshape(bs, ss, -1), ks.reshape(bs, ss, nkv), vs.reshape(bs, ss, nkv), cache)
    xs = _post(as_.reshape(bs * ss, -1), xs, mod[0], lat_mod, g1, wo, w1, w2)

    g0 = norm_g[1, 0][None, :]
    g1 = norm_g[1, 1][None, :]
    wr = ret_w_qkvg[0].astype(BF16)
    wo = ret_w_o[0].astype(BF16)
    w1 = mlp_w1[1].astype(BF16)
    w2 = mlp_w2[1].astype(BF16)
    gn = ret_gn_w[0][None, :]
    gf = final_norm_g[None, :]
    lg = jnp.broadcast_to(ret_decay_logit[0].T[:, :, None, None],
                          (RET_HEADS, 2, V7X_SUBLANES, V7X_LANES))
    hw = RET_HEADS * RET_DK

    q, k, v, gate = _ret_in(xp, mod[1], ctx_mod, g0, wr, None)
    yp, new_state = _retention(q.reshape(bp, sp, hw), k.reshape(bp, sp, hw), v.reshape(bp, sp, hw),
                               gate.reshape(bp, sp, hw), lg, gn, None, RET_HEADS, sp)
    y_prompt = _post(yp.reshape(bp * sp, hw), xp, mod[1], ctx_mod, g1, wo, w1, w2, gf)

    q, k, v, gate = _ret_in(xs, mod[1], lat_mod, g0, wr, _rope_tables(ss, RET_DK, False))
    ys = _retention(q.reshape(bs, ss, hw), k.reshape(bs, ss, hw), v.reshape(bs, ss, hw),
                    gate.reshape(bs, ss, hw), lg, gn, state_ret, 1, RET_Q_TILE)
    y_sample = _post(ys.reshape(bs * ss, hw), xs, mod[1], lat_mod, g1, wo, w1, w2, gf)

    return (y_prompt.reshape(bp, sp, d),
            y_sample.reshape(bs, ss, d),
            kp.reshape(bp, 1, sp, N_KV_HEADS, HEAD_DIM),
            vp.reshape(bp, 1, sp, N_KV_HEADS, HEAD_DIM),
            new_state)
```

```python
import functools

import numpy as np
import jax
import jax.numpy as jnp
from jax import lax
from jax.experimental import pallas as pl
from jax.experimental.pallas import tpu as pltpu

F32 = jnp.float32
BF16 = jnp.bfloat16

D_MODEL = 1024
GRID_W = 64
N_HEADS = 8
N_KV_HEADS = 2
HEAD_DIM = 128
GROUP = N_HEADS // N_KV_HEADS
ROPE_BASE = 10000.0
RET_HEADS = 4
RET_DK = 256
RET_DV = 256
D_FF = 4 * D_MODEL
Q_SCALE = HEAD_DIM ** -0.5 * float(np.log2(np.e))
EPS = 1e-6
QKV_W = (N_HEADS + 2 * N_KV_HEADS) * HEAD_DIM
RET_W = 2 * RET_HEADS * RET_DK + 2 * RET_HEADS * RET_DV

V7X_LANES = 128
V7X_SUBLANES = 8
MOD_ROWS = 16
ROW_TILE = 512
ATTN_Q_TILE = 256
RET_Q_TILE = 1024
RET_SUB_TILE = 256
VMEM_LIMIT = 56 * 1024 * 1024


def _params(sem, vmem=VMEM_LIMIT):
    return pltpu.CompilerParams(dimension_semantics=sem, vmem_limit_bytes=vmem)


def _const_spec(shape):
    nd = len(shape)
    return pl.BlockSpec(shape, lambda *_: (0,) * nd, pipeline_mode=pl.Buffered(1))


def _sigmoid(x):
    return 1.0 / (1.0 + jnp.exp(-x))


def _mod_kernel(cond_ref, w_ref, b_ref, o_ref):
    s = cond_ref[...]
    s = (s * _sigmoid(s)).astype(BF16)
    o_ref[0] = jnp.dot(s, w_ref[0].astype(BF16), preferred_element_type=F32) + b_ref[0]


def _modulation(cond, w_mod, b_mod):
    depth = w_mod.shape[0]
    tn = 1024
    return pl.pallas_call(
        _mod_kernel,
        out_shape=jax.ShapeDtypeStruct((depth, MOD_ROWS, 6 * D_MODEL), F32),
        grid=(depth, 6 * D_MODEL // tn),
        in_specs=[
            pl.BlockSpec((MOD_ROWS, D_MODEL), lambda i, j: (0, 0)),
            pl.BlockSpec((1, D_MODEL, tn), lambda i, j: (i, 0, j)),
            pl.BlockSpec((1, 1, tn), lambda i, j: (i, 0, j)),
        ],
        out_specs=pl.BlockSpec((1, MOD_ROWS, tn), lambda i, j: (i, 0, j)),
        compiler_params=_params(("parallel", "parallel")),
        name="modulation",
    )(cond, w_mod, b_mod.reshape(depth, 1, 6 * D_MODEL))


def _norm_modulate(x, g, mod_ref, which):
    y = x * lax.rsqrt(jnp.mean(x * x, axis=-1, keepdims=True) + EPS) * g
    shift = mod_ref[0, :, (3 * which) * D_MODEL:(3 * which + 1) * D_MODEL]
    scale = mod_ref[0, :, (3 * which + 1) * D_MODEL:(3 * which + 2) * D_MODEL]
    return y * (1.0 + scale) + shift


def _mod_spec(rows_per_mod_row, first_row):
    tiles = rows_per_mod_row // ROW_TILE if rows_per_mod_row else 0
    if tiles:
        return pl.BlockSpec((1, 1, 6 * D_MODEL), lambda i: (first_row + i // tiles, 0, 0))
    return pl.BlockSpec((1, 1, 6 * D_MODEL), lambda i: (first_row, 0, 0))


def _rope_tables(n, dim, paired):
    quarter = dim // 4
    rows = np.repeat(np.arange(n // GRID_W), GRID_W).astype(np.float64)
    cols = np.tile(np.arange(GRID_W), n // GRID_W).astype(np.float64)
    freqs = ROPE_BASE ** (-np.arange(quarter, dtype=np.float64) / quarter)
    ar = rows[:, None] * freqs
    ac = cols[:, None] * freqs
    if paired:
        cos = np.concatenate([np.cos(ar), np.cos(ac), np.cos(ar), np.cos(ac)], axis=1)
        sin = np.concatenate([-np.sin(ar), -np.sin(ac), np.sin(ar), np.sin(ac)], axis=1)
    else:
        cos = np.concatenate([np.cos(ar), np.cos(ar), np.cos(ac), np.cos(ac)], axis=1)
        sin = np.concatenate([-np.sin(ar), np.sin(ar), -np.sin(ac), np.sin(ac)], axis=1)
    return jnp.asarray(cos, F32), jnp.asarray(sin, F32)


_Q = HEAD_DIM // 4
_PAIRED_ORDER = np.concatenate([np.arange(0, _Q), np.arange(2 * _Q, 3 * _Q),
                                np.arange(_Q, 2 * _Q), np.arange(3 * _Q, 4 * _Q)])


def _swap_halves(t):
    return pltpu.roll(t, V7X_LANES // 2, axis=1)


def _attn_in_kernel(x_ref, mod_ref, g_ref, w_ref, qn_ref, kn_ref, *rest, use_rope):
    nq = N_HEADS * HEAD_DIM
    nk = N_KV_HEADS * HEAD_DIM
    if use_rope:
        cos_ref, sin_ref, reorder_ref, q_ref, k_ref, v_ref, wqk_ref = rest

        @pl.when(pl.program_id(0) == 0)
        def _():
            for head in range(N_HEADS + N_KV_HEADS):
                cols = slice(head * HEAD_DIM, (head + 1) * HEAD_DIM)
                wqk_ref[:, cols] = jnp.dot(w_ref[:, cols], reorder_ref[...],
                                           preferred_element_type=F32).astype(BF16)
    else:
        q_ref, k_ref, v_ref = rest
        wqk_ref = w_ref
    h = _norm_modulate(x_ref[...], g_ref[...], mod_ref, 0).astype(BF16)
    tm = x_ref.shape[0]
    pair_w = 2 * HEAD_DIM

    def put_kv(ref, kh, val):
        if use_rope:
            ref[:, kh * HEAD_DIM:(kh + 1) * HEAD_DIM] = val.astype(ref.dtype)
        else:
            ref[pl.ds(kh, tm, stride=N_KV_HEADS), :] = val.astype(ref.dtype)

    qw = qn_ref[...] * Q_SCALE
    kw = kn_ref[...]
    if use_rope:
        q_tabs = (qw * cos_ref[...], _swap_halves(qw) * sin_ref[...])
        k_tabs = (kw * cos_ref[...], _swap_halves(kw) * sin_ref[...])
        ra = lax.broadcasted_iota(jnp.int32, (pair_w, pair_w), 0) // HEAD_DIM
        rb = lax.broadcasted_iota(jnp.int32, (pair_w, pair_w), 1) // HEAD_DIM
        head_sum = jnp.where(ra == rb, 1.0, 0.0).astype(BF16)
    n_pairs = (N_HEADS + N_KV_HEADS) // 2
    t2_next = jnp.dot(h, wqk_ref[:, 0:pair_w], preferred_element_type=F32)
    for pair in range(n_pairs):
        t2 = t2_next
        if pair + 1 < n_pairs:
            t2_next = jnp.dot(h, wqk_ref[:, (pair + 1) * pair_w:(pair + 2) * pair_w], preferred_element_type=F32)
        else:
            t2_next = jnp.dot(h, w_ref[:, nq + nk:], preferred_element_type=F32)
        if use_rope:
            ss2 = jnp.dot((t2 * t2).astype(BF16), head_sum, preferred_element_type=F32)
            r2 = lax.rsqrt(ss2 * (1.0 / HEAD_DIM) + EPS)
        for j in range(2):
            head = 2 * pair + j
            t = t2[:, j * HEAD_DIM:(j + 1) * HEAD_DIM]
            if use_rope:
                r = r2[:, j * HEAD_DIM:(j + 1) * HEAD_DIM]
            else:
                r = lax.rsqrt(jnp.mean(t * t, axis=-1, keepdims=True) + EPS)
            if use_rope:
                wc, ws = q_tabs if head < N_HEADS else k_tabs
                t = (t * wc + _swap_halves(t) * ws) * r
            else:
                t = t * r * (qw if head < N_HEADS else kw)
            if head < N_HEADS:
                q_ref[:, head * HEAD_DIM:(head + 1) * HEAD_DIM] = t.astype(q_ref.dtype)
            else:
                put_kv(k_ref, head - N_HEADS, t)
    vv = t2_next
    for kh in range(N_KV_HEADS):
        put_kv(v_ref, kh, vv[:, kh * HEAD_DIM:(kh + 1) * HEAD_DIM])


def _attn_in(x, mod, mod_spec, g, w, qn, kn, rope):
    rows = x.shape[0]
    nk = N_KV_HEADS * HEAD_DIM
    if rope is None:
        kv_shape = jax.ShapeDtypeStruct((rows * N_KV_HEADS, HEAD_DIM), F32)
        kv_spec = pl.BlockSpec((ROW_TILE * N_KV_HEADS, HEAD_DIM), lambda i: (i, 0))
    else:
        kv_shape = jax.ShapeDtypeStruct((rows, nk), BF16)
        kv_spec = pl.BlockSpec((ROW_TILE, nk), lambda i: (i, 0))
    in_specs = [
        pl.BlockSpec((ROW_TILE, D_MODEL), lambda i: (i, 0)),
        mod_spec,
        _const_spec((1, D_MODEL)),
        _const_spec((D_MODEL, QKV_W)),
        _const_spec((1, HEAD_DIM)),
        _const_spec((1, HEAD_DIM)),
    ]
    args = [x, mod, g, w, qn, kn]
    scratch = []
    if rope is not None:
        cos, sin, reorder = rope
        seq_tiles = cos.shape[0] // ROW_TILE
        in_specs += [pl.BlockSpec((ROW_TILE, HEAD_DIM), lambda i: (i % seq_tiles, 0))] * 2
        in_specs.append(_const_spec((HEAD_DIM, HEAD_DIM)))
        args += [cos, sin, reorder]
        scratch = [pltpu.VMEM((D_MODEL, (N_HEADS + N_KV_HEADS) * HEAD_DIM), BF16)]
    return pl.pallas_call(
        functools.partial(_attn_in_kernel, use_rope=rope is not None),
        out_shape=(jax.ShapeDtypeStruct((rows, N_HEADS * HEAD_DIM), BF16), kv_shape, kv_shape),
        grid=(rows // ROW_TILE,),
        in_specs=in_specs,
        out_specs=(pl.BlockSpec((ROW_TILE, N_HEADS * HEAD_DIM), lambda i: (i, 0)), kv_spec, kv_spec),
        scratch_shapes=scratch,
        compiler_params=_params(("arbitrary",)),
        name="attn_in_rope" if rope is not None else "attn_in",
    )(*args)


def _kv_head(ref, kv):
    if ref.dtype == BF16:
        return ref[0, :, kv * HEAD_DIM:(kv + 1) * HEAD_DIM]
    if len(ref.shape) == 5:
        return ref[0, 0, :, kv, :].astype(BF16)
    tokens = ref.shape[1] // N_KV_HEADS
    return ref[0, pl.ds(kv, tokens, stride=N_KV_HEADS), :].astype(BF16)


def _attn_kernel(q_ref, k_ref, v_ref, *rest, cached):
    if cached:
        ck_ref, cv_ref, reorder_ref, o_ref, ck_scr, cv_scr = rest

        @pl.when(pl.program_id(1) == 0)
        def _():
            for kv in range(N_KV_HEADS):
                ck_scr[kv] = jnp.dot(_kv_head(ck_ref, kv), reorder_ref[...],
                                     preferred_element_type=F32).astype(BF16)
                cv_scr[kv] = _kv_head(cv_ref, kv)
    else:
        (o_ref,) = rest
    nt = (((1,), (1,)), ((), ()))
    ks = [_kv_head(k_ref, kv) for kv in range(N_KV_HEADS)]
    vs = [_kv_head(v_ref, kv) for kv in range(N_KV_HEADS)]

    def scores(head):
        kv = head // GROUP
        q = q_ref[0, :, head * HEAD_DIM:(head + 1) * HEAD_DIM]
        s = lax.dot_general(q, ks[kv], nt, preferred_element_type=F32)
        sc = lax.dot_general(q, ck_scr[kv], nt, preferred_element_type=F32) if cached else None
        return s, sc

    nxt = scores(0)
    for head in range(N_HEADS):
        kv = head // GROUP
        s, sc = nxt
        if head + 1 < N_HEADS:
            nxt = scores(head + 1)
        m = jnp.max(s, axis=-1, keepdims=True)
        if cached:
            m = jnp.maximum(m, jnp.max(sc, axis=-1, keepdims=True))
        p = jnp.exp2(s - m)
        l = jnp.sum(p, axis=-1, keepdims=True)
        o = jnp.dot(p.astype(BF16), vs[kv], preferred_element_type=F32)
        if cached:
            pc = jnp.exp2(sc - m)
            l = l + jnp.sum(pc, axis=-1, keepdims=True)
            o = o + jnp.dot(pc.astype(BF16), cv_scr[kv], preferred_element_type=F32)
        o_ref[0, :, head * HEAD_DIM:(head + 1) * HEAD_DIM] = (o / l).astype(o_ref.dtype)


def _attention(q, k, v, cache=None):
    b, n, width = q.shape
    tq = min(ATTN_Q_TILE, n)
    in_specs = [pl.BlockSpec((1, tq, width), lambda bi, qi: (bi, qi, 0))]
    in_specs += [pl.BlockSpec((1,) + k.shape[1:], lambda bi, qi: (bi, 0, 0))] * 2
    args = [q, k, v]
    scratch = []
    if cache is not None:
        in_specs += [pl.BlockSpec((1,) + cache[0].shape[1:], lambda bi, qi: (bi, 0, 0, 0, 0))] * 2
        in_specs.append(pl.BlockSpec((HEAD_DIM, HEAD_DIM), lambda bi, qi: (0, 0)))
        args += list(cache)
        scratch = [pltpu.VMEM((N_KV_HEADS, cache[0].shape[2], HEAD_DIM), BF16)] * 2
    return pl.pallas_call(
        functools.partial(_attn_kernel, cached=cache is not None),
        out_shape=jax.ShapeDtypeStruct((b, n, width), BF16),
        grid=(b, n // tq),
        in_specs=in_specs,
        out_specs=pl.BlockSpec((1, tq, width), lambda bi, qi: (bi, qi, 0)),
        scratch_shapes=scratch,
        compiler_params=_params(("arbitrary", "arbitrary")),
        name="attention_cached" if cache is not None else "attention",
    )(*args)


def _post_kernel(a_ref, x_ref, mod_ref, g_ref, wo_ref, w1_ref, w2_ref, *rest, final):
    if final:
        gf_ref, o_ref = rest
    else:
        (o_ref,) = rest
    gate1 = mod_ref[0, :, 2 * D_MODEL:3 * D_MODEL]
    gate2 = mod_ref[0, :, 5 * D_MODEL:6 * D_MODEL]
    x = x_ref[...] + gate1 * jnp.dot(a_ref[...], wo_ref[...], preferred_element_type=F32)
    h = _norm_modulate(x, g_ref[...], mod_ref, 1).astype(BF16)
    ff_tile = D_FF // 4
    y = None
    for f in range(D_FF // ff_tile):
        u = jnp.dot(h, w1_ref[0, :, f * ff_tile:(f + 1) * ff_tile], preferred_element_type=F32)
        u = jnp.square(jnp.maximum(u, 0.0)).astype(BF16)
        part = jnp.dot(u, w2_ref[0, f * ff_tile:(f + 1) * ff_tile, :], preferred_element_type=F32)
        y = part if y is None else y + part
    x = x + gate2 * y
    if final:
        x = x * lax.rsqrt(jnp.mean(x * x, axis=-1, keepdims=True) + EPS) * gf_ref[...]
    o_ref[...] = x


def _post(a, x, mod, mod_spec, g, wo, w1, w2, layer, gf=None):
    rows = x.shape[0]

    def layer_spec(shape):
        return pl.BlockSpec((1,) + shape, lambda i: (layer, 0, 0), pipeline_mode=pl.Buffered(1))

    in_specs = [
        pl.BlockSpec((ROW_TILE, D_MODEL), lambda i: (i, 0)),
        pl.BlockSpec((ROW_TILE, D_MODEL), lambda i: (i, 0)),
        mod_spec,
        _const_spec((1, D_MODEL)),
        _const_spec((D_MODEL, D_MODEL)),
        layer_spec((D_MODEL, D_FF)),
        layer_spec((D_FF, D_MODEL)),
    ]
    args = [a, x, mod, g, wo, w1, w2]
    if gf is not None:
        in_specs.append(_const_spec((1, D_MODEL)))
        args.append(gf)
    return pl.pallas_call(
        functools.partial(_post_kernel, final=gf is not None),
        out_shape=jax.ShapeDtypeStruct((rows, D_MODEL), F32),
        grid=(rows // ROW_TILE,),
        in_specs=in_specs,
        out_specs=pl.BlockSpec((ROW_TILE, D_MODEL), lambda i: (i, 0)),
        compiler_params=_params(("parallel",)),
        name="post_final" if gf is not None else "post",
    )(*args)


def _ret_in_kernel(x_ref, mod_ref, g_ref, w_ref, *rest, use_rope):
    if use_rope:
        cos_ref, sin_ref, q_ref, k_ref, v_ref, gate_ref = rest
    else:
        q_ref, k_ref, v_ref, gate_ref = rest
    h = _norm_modulate(x_ref[...], g_ref[...], mod_ref, 0).astype(BF16)
    width = RET_HEADS * RET_DK
    for part, ref in enumerate((q_ref, k_ref, v_ref, gate_ref)):
        t = jnp.dot(h, w_ref[:, part * width:(part + 1) * width], preferred_element_type=F32)
        if part < 2 and use_rope:
            for c in range(width // V7X_LANES):
                half = (c % (RET_DK // V7X_LANES)) * V7X_LANES
                tc = t[:, c * V7X_LANES:(c + 1) * V7X_LANES]
                tc = (tc * cos_ref[:, half:half + V7X_LANES]
                      + _swap_halves(tc) * sin_ref[:, half:half + V7X_LANES])
                if part == 0:
                    tc = tc * RET_DK ** -0.5
                ref[:, c * V7X_LANES:(c + 1) * V7X_LANES] = tc.astype(ref.dtype)
        else:
            if part == 0:
                t = t * RET_DK ** -0.5
            ref[...] = t.astype(ref.dtype)


def _ret_in(x, mod, mod_spec, g, w, rope):
    rows = x.shape[0]
    width = RET_HEADS * RET_DK
    in_specs = [
        pl.BlockSpec((ROW_TILE, D_MODEL), lambda i: (i, 0)),
        mod_spec,
        _const_spec((1, D_MODEL)),
        _const_spec((D_MODEL, RET_W)),
    ]
    args = [x, mod, g, w]
    if rope is not None:
        seq_tiles = rope[0].shape[0] // ROW_TILE
        in_specs += [pl.BlockSpec((ROW_TILE, RET_DK), lambda i: (i % seq_tiles, 0))] * 2
        args += list(rope)
    out_block = pl.BlockSpec((ROW_TILE, width), lambda i: (i, 0))
    return pl.pallas_call(
        functools.partial(_ret_in_kernel, use_rope=rope is not None),
        out_shape=(jax.ShapeDtypeStruct((rows, width), BF16),
                   jax.ShapeDtypeStruct((rows, width), BF16),
                   jax.ShapeDtypeStruct((rows, width), BF16),
                   jax.ShapeDtypeStruct((rows, width), F32)),
        grid=(rows // ROW_TILE,),
        in_specs=in_specs,
        out_specs=(out_block, out_block, out_block, out_block),
        compiler_params=_params(("parallel",)),
        name="ret_in_rope" if rope is not None else "ret_in",
    )(*args)


def _log_sigmoid(x):
    return jnp.minimum(x, 0.0) - jnp.log(1.0 + jnp.exp(-jnp.abs(x)))


def _ret_kernel(q_ref, k_ref, v_ref, gate_ref, lg_ref, gn_ref, *rest, n, tq, heads, has_state):
    if has_state:
        s0_ref, y_ref, d_ref = rest
    else:
        y_ref, st_ref, d_ref = rest
    b = pl.program_id(1)
    qi = pl.program_id(2)
    r0 = qi * tq
    nt = (((1,), (1,)), ((), ()))
    sub = min(tq, RET_SUB_TILE)
    lgs = [(_log_sigmoid(lg_ref[hh, 0, 0:1, 0:1]), _log_sigmoid(lg_ref[hh, 1, 0:1, 0:1]))
           for hh in range(heads)]

    @pl.when(b == 0)
    def _():
        ii = r0 + lax.broadcasted_iota(jnp.int32, (tq, n), 0)
        jj = lax.broadcasted_iota(jnp.int32, (tq, n), 1)
        diff = (ii - jj).astype(F32)
        for hh in range(heads):
            lg_f, lg_b = lgs[hh]
            d_ref[hh, qi] = jnp.exp(jnp.where(diff >= 0.0, lg_f, -lg_b) * diff)

    def head_cols(hh):
        return slice(hh * RET_DK, (hh + 1) * RET_DK)

    def scores(unit):
        hh, si = unit
        return lax.dot_general(q_ref[0, si * sub:(si + 1) * sub, head_cols(hh)], k_ref[0, :, head_cols(hh)],
                               nt, preferred_element_type=F32)

    units = [(hh, si) for hh in range(heads) for si in range(tq // sub)]
    nxt = scores(units[0])
    for ui, (hh, si) in enumerate(units):
        s = nxt
        if ui + 1 < len(units):
            nxt = scores(units[ui + 1])
        lg_f, lg_b = lgs[hh]
        cols = head_cols(hh)
        rows = slice(si * sub, (si + 1) * sub)
        v = v_ref[0, :, cols]
        p = (s * d_ref[hh, qi, rows, :]).astype(BF16)
        o = jnp.dot(p, v, preferred_element_type=F32)
        if has_state:
            q = q_ref[0, rows, cols]
            i_col = (r0 + si * sub + lax.broadcasted_iota(jnp.int32, (sub, 1), 0)).astype(F32)
            o = o + (jnp.dot(q, s0_ref[0, 0, 0, hh].astype(BF16), preferred_element_type=F32)
                     * jnp.exp(lg_f * (i_col + 1.0))
                     + jnp.dot(q, s0_ref[0, 0, 1, hh].astype(BF16), preferred_element_type=F32)
                     * jnp.exp(lg_b * (n - i_col)))
        elif si == 0:
            eye = jnp.where(lax.broadcasted_iota(jnp.int32, (RET_DK, RET_DK), 0)
                            == lax.broadcasted_iota(jnp.int32, (RET_DK, RET_DK), 1), 1.0, 0.0).astype(BF16)
            kt = lax.dot_general(eye, k_ref[0, :, cols], nt, preferred_element_type=F32)
            j_row = lax.broadcasted_iota(jnp.int32, (1, n), 1).astype(F32)
            st_ref[0, 0, 0, hh] = jnp.dot((kt * jnp.exp(lg_f * (n - 1.0 - j_row))).astype(BF16), v,
                                          preferred_element_type=F32)
            st_ref[0, 0, 1, hh] = jnp.dot((kt * jnp.exp(lg_b * j_row)).astype(BF16), v,
                                          preferred_element_type=F32)
        mu = jnp.mean(o, axis=-1, keepdims=True)
        oc = o - mu
        var = jnp.mean(oc * oc, axis=-1, keepdims=True)
        on = oc * lax.rsqrt(var + EPS) * gn_ref[:, cols]
        gt = gate_ref[0, rows, cols]
        y_ref[0, rows, cols] = (gt * _sigmoid(gt) * on).astype(y_ref.dtype)


def _retention(q, k, v, gate, lg, gn, state0, heads, tq):
    b, n, _ = q.shape
    assert n % tq == 0 and RET_HEADS % heads == 0 and (state0 is not None or tq == n)
    width = heads * RET_DK
    q_tile = pl.BlockSpec((1, tq, width), lambda hg, bi, qi: (bi, qi, hg))
    seq = pl.BlockSpec((1, n, width), lambda hg, bi, qi: (bi, 0, hg))
    st = pl.BlockSpec((1, 1, 2, heads, RET_DK, RET_DV), lambda hg, bi, qi: (bi, 0, 0, hg, 0, 0))
    in_specs = [q_tile, seq, seq, q_tile,
                pl.BlockSpec((heads, 2, V7X_SUBLANES, V7X_LANES), lambda hg, bi, qi: (hg, 0, 0, 0)),
                pl.BlockSpec((1, width), lambda hg, bi, qi: (0, hg))]
    args = [q, k, v, gate, lg, gn]
    y_shape = jax.ShapeDtypeStruct((b, n, RET_HEADS * RET_DV), BF16)
    if state0 is not None:
        in_specs.append(st)
        args.append(state0)
        out_shape, out_specs = y_shape, q_tile
    else:
        out_shape = (y_shape, jax.ShapeDtypeStruct((b, 1, 2, RET_HEADS, RET_DK, RET_DV), F32))
        out_specs = (q_tile, st)
    return pl.pallas_call(
        functools.partial(_ret_kernel, n=n, tq=tq, heads=heads, has_state=state0 is not None),
        out_shape=out_shape,
        grid=(RET_HEADS // heads, b, n // tq),
        in_specs=in_specs,
        out_specs=out_specs,
        scratch_shapes=[pltpu.VMEM((heads, n // tq, tq, n), F32)],
        compiler_params=_params(("arbitrary", "arbitrary", "arbitrary")),
        name="retention_state" if state0 is not None else "retention",
    )(*args)


def kernel(x_prompt, x_sample, cache_k, cache_v, state_ret, c, c_ctx, w_mod, b_mod, norm_g,
           attn_w_qkv, attn_q_norm, attn_k_norm, attn_w_o, ret_w_qkvg, ret_decay_logit, ret_gn_w,
           ret_w_o, mlp_w1, mlp_w2, final_norm_g):
    bp, sp, d = x_prompt.shape
    bs, ss, _ = x_sample.shape
    depth = w_mod.shape[0]
    assert d == D_MODEL and depth == 2 and 1 + bs <= MOD_ROWS
    assert (bp * sp) % ROW_TILE == 0 and ss % ROW_TILE == 0

    cond = jnp.concatenate([c_ctx[None, :], c, jnp.zeros((MOD_ROWS - 1 - bs, d), F32)], axis=0)
    mod = _modulation(cond, w_mod, b_mod)
    mod = mod.reshape(depth, MOD_ROWS, 1, 6 * d)
    ctx_mod = _mod_spec(0, 0)
    lat_mod = _mod_spec(ss, 1)

    xp = x_prompt.reshape(bp * sp, d)
    xs = x_sample.reshape(bs * ss, d)
    nkv = N_KV_HEADS * HEAD_DIM

    g0 = norm_g[0, 0][None, :]
    g1 = norm_g[0, 1][None, :]
    wqkv = attn_w_qkv[0].astype(BF16)
    qn = attn_q_norm[0][None, :]
    kn = attn_k_norm[0][None, :]
    wo = attn_w_o[0].astype(BF16)
    w1 = mlp_w1.astype(BF16)
    w2 = mlp_w2.astype(BF16)

    qp, kp, vp = _attn_in(xp, mod[0], ctx_mod, g0, wqkv, qn, kn, None)
    ap = _attention(qp.reshape(bp, sp, -1), kp.reshape(bp, sp * N_KV_HEADS, HEAD_DIM),
                    vp.reshape(bp, sp * N_KV_HEADS, HEAD_DIM))
    xp = _post(ap.reshape(bp * sp, -1), xp, mod[0], ctx_mod, g1, wo, w1, w2, 0)

    reorder = np.zeros((HEAD_DIM, HEAD_DIM), np.float32)
    reorder[_PAIRED_ORDER, np.arange(HEAD_DIM)] = 1.0
    reorder = jnp.asarray(reorder, BF16)
    qs, ks, vs = _attn_in(xs, mod[0], lat_mod, g0, wqkv, qn[:, _PAIRED_ORDER], kn[:, _PAIRED_ORDER],
                          _rope_tables(ss, HEAD_DIM, True) + (reorder,))
    cache = (cache_k, cache_v, reorder)
    as_ = _attention(qs.reshape(bs, ss, -1), ks.reshape(bs, ss, nkv), vs.reshape(bs, ss, nkv), cache)
    xs = _post(as_.reshape(bs * ss, -1), xs, mod[0], lat_mod, g1, wo, w1, w2, 0)

    g0 = norm_g[1, 0][None, :]
    g1 = norm_g[1, 1][None, :]
    wr = ret_w_qkvg[0].astype(BF16)
    wo = ret_w_o[0].astype(BF16)
    gn = ret_gn_w[0][None, :]
    gf = final_norm_g[None, :]
    lg = jnp.broadcast_to(ret_decay_logit[0].T[:, :, None, None],
                          (RET_HEADS, 2, V7X_SUBLANES, V7X_LANES))
    hw = RET_HEADS * RET_DK

    q, k, v, gate = _ret_in(xp, mod[1], ctx_mod, g0, wr, None)
    yp, new_state = _retention(q.reshape(bp, sp, hw), k.reshape(bp, sp, hw), v.reshape(bp, sp, hw),
                               gate.reshape(bp, sp, hw), lg, gn, None, RET_HEADS, sp)
    y_prompt = _post(yp.reshape(bp * sp, hw), xp, mod[1], ctx_mod, g1, wo, w1, w2, 1, gf)

    q, k, v, gate = _ret_in(xs, mod[1], lat_mod, g0, wr, _rope_tables(ss, RET_DK, False))
    ys = _retention(q.reshape(bs, ss, hw), k.reshape(bs, ss, hw), v.reshape(bs, ss, hw),
                    gate.reshape(bs, ss, hw), lg, gn, state_ret, 1, RET_Q_TILE)
    y_sample = _post(ys.reshape(bs * ss, hw), xs, mod[1], lat_mod, g1, wo, w1, w2, 1, gf)

    return (y_prompt.reshape(bp, sp, d),
            y_sample.reshape(bs, ss, d),
            kp.reshape(bp, 1, sp, N_KV_HEADS, HEAD_DIM),
            vp.reshape(bp, 1, sp, N_KV_HEADS, HEAD_DIM),
            new_state)
```

```python
import functools

import numpy as np
import jax
import jax.numpy as jnp
from jax import lax
from jax.experimental import pallas as pl
from jax.experimental.pallas import tpu as pltpu

F32 = jnp.float32
BF16 = jnp.bfloat16

D_MODEL = 1024
GRID_W = 64
N_HEADS = 8
N_KV_HEADS = 2
HEAD_DIM = 128
GROUP = N_HEADS // N_KV_HEADS
ROPE_BASE = 10000.0
RET_HEADS = 4
RET_DK = 256
RET_DV = 256
D_FF = 4 * D_MODEL
Q_SCALE = HEAD_DIM ** -0.5 * float(np.log2(np.e))
EPS = 1e-6
QKV_W = (N_HEADS + 2 * N_KV_HEADS) * HEAD_DIM
RET_W = 2 * RET_HEADS * RET_DK + 2 * RET_HEADS * RET_DV

V7X_LANES = 128
V7X_SUBLANES = 8
MOD_ROWS = 16
ROW_TILE = 512
ATTN_Q_TILE = 256
RET_Q_TILE = 1024
RET_SUB_TILE = 256
VMEM_LIMIT = 56 * 1024 * 1024


def _params(sem, vmem=VMEM_LIMIT):
    return pltpu.CompilerParams(dimension_semantics=sem, vmem_limit_bytes=vmem)


def _const_spec(shape):
    nd = len(shape)
    return pl.BlockSpec(shape, lambda *_: (0,) * nd, pipeline_mode=pl.Buffered(1))


def _sigmoid(x):
    return 1.0 / (1.0 + jnp.exp(-x))


def _cast_plumbing(casts, steps):
    in_specs, out_specs, out_shapes = [], [], []
    for arr, layer in casts:
        _, r, c = arr.shape
        slab = r // steps
        assert slab * steps == r and slab % 16 == 0
        in_specs.append(pl.BlockSpec((1, slab, c), functools.partial(lambda i, l: (l, i, 0), l=layer)))
        out_specs.append(pl.BlockSpec((slab, c), lambda i: (i, 0)))
        out_shapes.append(jax.ShapeDtypeStruct((r, c), BF16))
    return in_specs, out_specs, out_shapes


def _with_casts(body, n_in, n_out, n_cast):
    def kernel(*refs):
        ins = refs[:n_in]
        cast_in = refs[n_in:n_in + n_cast]
        outs = refs[n_in + n_cast:n_in + n_cast + n_out]
        cast_out = refs[n_in + n_cast + n_out:n_in + 2 * n_cast + n_out]
        scratch = refs[n_in + 2 * n_cast + n_out:]
        for src, dst in zip(cast_in, cast_out):
            dst[...] = src[0].astype(dst.dtype)
        body(*ins, *outs, *scratch)
    return kernel


def _mod_kernel(cond_ref, w_ref, b_ref, o_ref):
    s = cond_ref[...]
    s = (s * _sigmoid(s)).astype(BF16)
    o_ref[0] = jnp.dot(s, w_ref[0].astype(BF16), preferred_element_type=F32) + b_ref[0]


def _modulation(cond, w_mod, b_mod):
    depth = w_mod.shape[0]
    tn = 1024
    return pl.pallas_call(
        _mod_kernel,
        out_shape=jax.ShapeDtypeStruct((depth, MOD_ROWS, 6 * D_MODEL), F32),
        grid=(depth, 6 * D_MODEL // tn),
        in_specs=[
            pl.BlockSpec((MOD_ROWS, D_MODEL), lambda i, j: (0, 0)),
            pl.BlockSpec((1, D_MODEL, tn), lambda i, j: (i, 0, j)),
            pl.BlockSpec((1, 1, tn), lambda i, j: (i, 0, j)),
        ],
        out_specs=pl.BlockSpec((1, MOD_ROWS, tn), lambda i, j: (i, 0, j)),
        compiler_params=_params(("parallel", "parallel")),
        name="modulation",
    )(cond, w_mod, b_mod.reshape(depth, 1, 6 * D_MODEL))


def _norm_modulate(x, g, mod_ref, which):
    y = x * lax.rsqrt(jnp.mean(x * x, axis=-1, keepdims=True) + EPS) * g
    shift = mod_ref[0, :, (3 * which) * D_MODEL:(3 * which + 1) * D_MODEL]
    scale = mod_ref[0, :, (3 * which + 1) * D_MODEL:(3 * which + 2) * D_MODEL]
    return y * (1.0 + scale) + shift


def _mod_spec(rows_per_mod_row, first_row):
    tiles = rows_per_mod_row // ROW_TILE if rows_per_mod_row else 0
    if tiles:
        return pl.BlockSpec((1, 1, 6 * D_MODEL), lambda i: (first_row + i // tiles, 0, 0))
    return pl.BlockSpec((1, 1, 6 * D_MODEL), lambda i: (first_row, 0, 0))


def _rope_tables(n, dim, paired):
    quarter = dim // 4
    rows = np.repeat(np.arange(n // GRID_W), GRID_W).astype(np.float64)
    cols = np.tile(np.arange(GRID_W), n // GRID_W).astype(np.float64)
    freqs = ROPE_BASE ** (-np.arange(quarter, dtype=np.float64) / quarter)
    ar = rows[:, None] * freqs
    ac = cols[:, None] * freqs
    if paired:
        cos = np.concatenate([np.cos(ar), np.cos(ac), np.cos(ar), np.cos(ac)], axis=1)
        sin = np.concatenate([-np.sin(ar), -np.sin(ac), np.sin(ar), np.sin(ac)], axis=1)
    else:
        cos = np.concatenate([np.cos(ar), np.cos(ar), np.cos(ac), np.cos(ac)], axis=1)
        sin = np.concatenate([-np.sin(ar), np.sin(ar), -np.sin(ac), np.sin(ac)], axis=1)
    return jnp.asarray(cos, F32), jnp.asarray(sin, F32)


_Q = HEAD_DIM // 4
_PAIRED_ORDER = np.concatenate([np.arange(0, _Q), np.arange(2 * _Q, 3 * _Q),
                                np.arange(_Q, 2 * _Q), np.arange(3 * _Q, 4 * _Q)])


def _swap_halves(t):
    return pltpu.roll(t, V7X_LANES // 2, axis=1)


def _attn_in_kernel(x_ref, mod_ref, g_ref, w_ref, qn_ref, kn_ref, *rest, use_rope):
    nq = N_HEADS * HEAD_DIM
    nk = N_KV_HEADS * HEAD_DIM
    if use_rope:
        cos_ref, sin_ref, reorder_ref, q_ref, k_ref, v_ref, wqk_ref = rest

        @pl.when(pl.program_id(0) == 0)
        def _():
            for head in range(N_HEADS + N_KV_HEADS):
                cols = slice(head * HEAD_DIM, (head + 1) * HEAD_DIM)
                wqk_ref[:, cols] = jnp.dot(w_ref[:, cols], reorder_ref[...],
                                           preferred_element_type=F32).astype(BF16)
    else:
        q_ref, k_ref, v_ref = rest
        wqk_ref = w_ref
    h = _norm_modulate(x_ref[...], g_ref[...], mod_ref, 0).astype(BF16)
    tm = x_ref.shape[0]
    pair_w = 2 * HEAD_DIM

    def put_kv(ref, kh, val):
        if use_rope:
            ref[:, kh * HEAD_DIM:(kh + 1) * HEAD_DIM] = val.astype(ref.dtype)
        else:
            ref[pl.ds(kh, tm, stride=N_KV_HEADS), :] = val.astype(ref.dtype)

    qw = qn_ref[...] * Q_SCALE
    kw = kn_ref[...]
    if use_rope:
        q_tabs = (qw * cos_ref[...], _swap_halves(qw) * sin_ref[...])
        k_tabs = (kw * cos_ref[...], _swap_halves(kw) * sin_ref[...])
        ra = lax.broadcasted_iota(jnp.int32, (pair_w, pair_w), 0) // HEAD_DIM
        rb = lax.broadcasted_iota(jnp.int32, (pair_w, pair_w), 1) // HEAD_DIM
        head_sum = jnp.where(ra == rb, 1.0, 0.0).astype(BF16)
    n_pairs = (N_HEADS + N_KV_HEADS) // 2
    t2_next = jnp.dot(h, wqk_ref[:, 0:pair_w], preferred_element_type=F32)
    for pair in range(n_pairs):
        t2 = t2_next
        if pair + 1 < n_pairs:
            t2_next = jnp.dot(h, wqk_ref[:, (pair + 1) * pair_w:(pair + 2) * pair_w], preferred_element_type=F32)
        else:
            t2_next = jnp.dot(h, w_ref[:, nq + nk:], preferred_element_type=F32)
        if use_rope:
            ss2 = jnp.dot((t2 * t2).astype(BF16), head_sum, preferred_element_type=F32)
            r2 = lax.rsqrt(ss2 * (1.0 / HEAD_DIM) + EPS)
        for j in range(2):
            head = 2 * pair + j
            t = t2[:, j * HEAD_DIM:(j + 1) * HEAD_DIM]
            if use_rope:
                r = r2[:, j * HEAD_DIM:(j + 1) * HEAD_DIM]
            else:
                r = lax.rsqrt(jnp.mean(t * t, axis=-1, keepdims=True) + EPS)
            if use_rope:
                wc, ws = q_tabs if head < N_HEADS else k_tabs
                t = (t * wc + _swap_halves(t) * ws) * r
            else:
                t = t * r * (qw if head < N_HEADS else kw)
            if head < N_HEADS:
                q_ref[:, head * HEAD_DIM:(head + 1) * HEAD_DIM] = t.astype(q_ref.dtype)
            else:
                put_kv(k_ref, head - N_HEADS, t)
    vv = t2_next
    for kh in range(N_KV_HEADS):
        put_kv(v_ref, kh, vv[:, kh * HEAD_DIM:(kh + 1) * HEAD_DIM])


def _attn_in(x, mod, mod_spec, g, w, qn, kn, rope, casts=()):
    rows = x.shape[0]
    nk = N_KV_HEADS * HEAD_DIM
    if rope is None:
        kv_shape = jax.ShapeDtypeStruct((rows * N_KV_HEADS, HEAD_DIM), F32)
        kv_spec = pl.BlockSpec((ROW_TILE * N_KV_HEADS, HEAD_DIM), lambda i: (i, 0))
    else:
        kv_shape = jax.ShapeDtypeStruct((rows, nk), BF16)
        kv_spec = pl.BlockSpec((ROW_TILE, nk), lambda i: (i, 0))
    in_specs = [
        pl.BlockSpec((ROW_TILE, D_MODEL), lambda i: (i, 0)),
        mod_spec,
        _const_spec((1, D_MODEL)),
        _const_spec((D_MODEL, QKV_W)),
        _const_spec((1, HEAD_DIM)),
        _const_spec((1, HEAD_DIM)),
    ]
    args = [x, mod, g, w, qn, kn]
    scratch = []
    if rope is not None:
        cos, sin, reorder = rope
        seq_tiles = cos.shape[0] // ROW_TILE
        in_specs += [pl.BlockSpec((ROW_TILE, HEAD_DIM), lambda i: (i % seq_tiles, 0))] * 2
        in_specs.append(_const_spec((HEAD_DIM, HEAD_DIM)))
        args += [cos, sin, reorder]
        scratch = [pltpu.VMEM((D_MODEL, (N_HEADS + N_KV_HEADS) * HEAD_DIM), BF16)]
    steps = rows // ROW_TILE
    c_in, c_out, c_shapes = _cast_plumbing(casts, steps)
    out_specs = [pl.BlockSpec((ROW_TILE, N_HEADS * HEAD_DIM), lambda i: (i, 0)), kv_spec, kv_spec]
    out_shapes = [jax.ShapeDtypeStruct((rows, N_HEADS * HEAD_DIM), BF16), kv_shape, kv_shape]
    body = functools.partial(_attn_in_kernel, use_rope=rope is not None)
    outs = pl.pallas_call(
        _with_casts(body, len(args), len(out_shapes), len(casts)),
        out_shape=tuple(out_shapes + c_shapes),
        grid=(steps,),
        in_specs=in_specs + c_in,
        out_specs=tuple(out_specs + c_out),
        scratch_shapes=scratch,
        compiler_params=_params(("arbitrary",)),
        name="attn_in_rope" if rope is not None else "attn_in",
    )(*args, *[arr for arr, _ in casts])
    return outs[:3], list(outs[3:])


def _kv_head(ref, kv):
    if ref.dtype == BF16:
        return ref[0, :, kv * HEAD_DIM:(kv + 1) * HEAD_DIM]
    if len(ref.shape) == 5:
        return ref[0, 0, :, kv, :].astype(BF16)
    tokens = ref.shape[1] // N_KV_HEADS
    return ref[0, pl.ds(kv, tokens, stride=N_KV_HEADS), :].astype(BF16)


def _attn_kernel(q_ref, k_ref, v_ref, *rest, cached):
    if cached:
        ck_ref, cv_ref, reorder_ref, o_ref, ck_scr, cv_scr = rest

        @pl.when(pl.program_id(1) == 0)
        def _():
            for kv in range(N_KV_HEADS):
                ck_scr[kv] = jnp.dot(_kv_head(ck_ref, kv), reorder_ref[...],
                                     preferred_element_type=F32).astype(BF16)
                cv_scr[kv] = _kv_head(cv_ref, kv)
    else:
        (o_ref,) = rest
    nt = (((1,), (1,)), ((), ()))
    ks = [_kv_head(k_ref, kv) for kv in range(N_KV_HEADS)]
    vs = [_kv_head(v_ref, kv) for kv in range(N_KV_HEADS)]

    def scores(head):
        kv = head // GROUP
        q = q_ref[0, :, head * HEAD_DIM:(head + 1) * HEAD_DIM]
        s = lax.dot_general(q, ks[kv], nt, preferred_element_type=F32)
        sc = lax.dot_general(q, ck_scr[kv], nt, preferred_element_type=F32) if cached else None
        return s, sc

    nxt = scores(0)
    for head in range(N_HEADS):
        kv = head // GROUP
        s, sc = nxt
        if head + 1 < N_HEADS:
            nxt = scores(head + 1)
        m = jnp.max(s, axis=-1, keepdims=True)
        if cached:
            m = jnp.maximum(m, jnp.max(sc, axis=-1, keepdims=True))
        p = jnp.exp2(s - m)
        l = jnp.sum(p, axis=-1, keepdims=True)
        o = jnp.dot(p.astype(BF16), vs[kv], preferred_element_type=F32)
        if cached:
            pc = jnp.exp2(sc - m)
            l = l + jnp.sum(pc, axis=-1, keepdims=True)
            o = o + jnp.dot(pc.astype(BF16), cv_scr[kv], preferred_element_type=F32)
        o_ref[0, :, head * HEAD_DIM:(head + 1) * HEAD_DIM] = (o / l).astype(o_ref.dtype)


def _attention(q, k, v, cache=None):
    b, n, width = q.shape
    tq = min(ATTN_Q_TILE, n)
    in_specs = [pl.BlockSpec((1, tq, width), lambda bi, qi: (bi, qi, 0))]
    in_specs += [pl.BlockSpec((1,) + k.shape[1:], lambda bi, qi: (bi, 0, 0))] * 2
    args = [q, k, v]
    scratch = []
    if cache is not None:
        in_specs += [pl.BlockSpec((1,) + cache[0].shape[1:], lambda bi, qi: (bi, 0, 0, 0, 0))] * 2
        in_specs.append(pl.BlockSpec((HEAD_DIM, HEAD_DIM), lambda bi, qi: (0, 0)))
        args += list(cache)
        scratch = [pltpu.VMEM((N_KV_HEADS, cache[0].shape[2], HEAD_DIM), BF16)] * 2
    return pl.pallas_call(
        functools.partial(_attn_kernel, cached=cache is not None),
        out_shape=jax.ShapeDtypeStruct((b, n, width), BF16),
        grid=(b, n // tq),
        in_specs=in_specs,
        out_specs=pl.BlockSpec((1, tq, width), lambda bi, qi: (bi, qi, 0)),
        scratch_shapes=scratch,
        compiler_params=_params(("arbitrary", "arbitrary")),
        name="attention_cached" if cache is not None else "attention",
    )(*args)


def _post_kernel(a_ref, x_ref, mod_ref, g_ref, wo_ref, w1_ref, w2_ref, *rest, final):
    if final:
        gf_ref, o_ref = rest
    else:
        (o_ref,) = rest
    gate1 = mod_ref[0, :, 2 * D_MODEL:3 * D_MODEL]
    gate2 = mod_ref[0, :, 5 * D_MODEL:6 * D_MODEL]
    x = x_ref[...] + gate1 * jnp.dot(a_ref[...], wo_ref[...], preferred_element_type=F32)
    h = _norm_modulate(x, g_ref[...], mod_ref, 1).astype(BF16)
    ff_tile = D_FF // 4
    y = None
    for f in range(D_FF // ff_tile):
        u = jnp.dot(h, w1_ref[:, f * ff_tile:(f + 1) * ff_tile], preferred_element_type=F32)
        u = jnp.square(jnp.maximum(u, 0.0)).astype(BF16)
        part = jnp.dot(u, w2_ref[f * ff_tile:(f + 1) * ff_tile, :], preferred_element_type=F32)
        y = part if y is None else y + part
    x = x + gate2 * y
    if final:
        x = x * lax.rsqrt(jnp.mean(x * x, axis=-1, keepdims=True) + EPS) * gf_ref[...]
    o_ref[...] = x


def _post(a, x, mod, mod_spec, g, wo, w1, w2, gf=None, casts=()):
    rows = x.shape[0]
    in_specs = [
        pl.BlockSpec((ROW_TILE, D_MODEL), lambda i: (i, 0)),
        pl.BlockSpec((ROW_TILE, D_MODEL), lambda i: (i, 0)),
        mod_spec,
        _const_spec((1, D_MODEL)),
        _const_spec((D_MODEL, D_MODEL)),
        _const_spec((D_MODEL, D_FF)),
        _const_spec((D_FF, D_MODEL)),
    ]
    args = [a, x, mod, g, wo, w1, w2]
    if gf is not None:
        in_specs.append(_const_spec((1, D_MODEL)))
        args.append(gf)
    steps = rows // ROW_TILE
    c_in, c_out, c_shapes = _cast_plumbing(casts, steps)
    body = functools.partial(_post_kernel, final=gf is not None)
    outs = pl.pallas_call(
        _with_casts(body, len(args), 1, len(casts)),
        out_shape=tuple([jax.ShapeDtypeStruct((rows, D_MODEL), F32)] + c_shapes),
        grid=(steps,),
        in_specs=in_specs + c_in,
        out_specs=tuple([pl.BlockSpec((ROW_TILE, D_MODEL), lambda i: (i, 0))] + c_out),
        compiler_params=_params(("parallel",)),
        name="post_final" if gf is not None else "post",
    )(*args, *[arr for arr, _ in casts])
    return outs[0], list(outs[1:])


def _ret_in_kernel(x_ref, mod_ref, g_ref, w_ref, *rest, use_rope):
    if use_rope:
        cos_ref, sin_ref, q_ref, k_ref, v_ref, gate_ref = rest
    else:
        q_ref, k_ref, v_ref, gate_ref = rest
    h = _norm_modulate(x_ref[...], g_ref[...], mod_ref, 0).astype(BF16)
    width = RET_HEADS * RET_DK
    for part, ref in enumerate((q_ref, k_ref, v_ref, gate_ref)):
        t = jnp.dot(h, w_ref[:, part * width:(part + 1) * width], preferred_element_type=F32)
        if part < 2 and use_rope:
            for c in range(width // V7X_LANES):
                half = (c % (RET_DK // V7X_LANES)) * V7X_LANES
                tc = t[:, c * V7X_LANES:(c + 1) * V7X_LANES]
                tc = (tc * cos_ref[:, half:half + V7X_LANES]
                      + _swap_halves(tc) * sin_ref[:, half:half + V7X_LANES])
                if part == 0:
                    tc = tc * RET_DK ** -0.5
                ref[:, c * V7X_LANES:(c + 1) * V7X_LANES] = tc.astype(ref.dtype)
        else:
            if part == 0:
                t = t * RET_DK ** -0.5
            ref[...] = t.astype(ref.dtype)


def _ret_in(x, mod, mod_spec, g, w, rope):
    rows = x.shape[0]
    width = RET_HEADS * RET_DK
    in_specs = [
        pl.BlockSpec((ROW_TILE, D_MODEL), lambda i: (i, 0)),
        mod_spec,
        _const_spec((1, D_MODEL)),
        _const_spec((D_MODEL, RET_W)),
    ]
    args = [x, mod, g, w]
    if rope is not None:
        seq_tiles = rope[0].shape[0] // ROW_TILE
        in_specs += [pl.BlockSpec((ROW_TILE, RET_DK), lambda i: (i % seq_tiles, 0))] * 2
        args += list(rope)
    out_block = pl.BlockSpec((ROW_TILE, width), lambda i: (i, 0))
    return pl.pallas_call(
        functools.partial(_ret_in_kernel, use_rope=rope is not None),
        out_shape=(jax.ShapeDtypeStruct((rows, width), BF16),
                   jax.ShapeDtypeStruct((rows, width), BF16),
                   jax.ShapeDtypeStruct((rows, width), BF16),
                   jax.ShapeDtypeStruct((rows, width), F32)),
        grid=(rows // ROW_TILE,),
        in_specs=in_specs,
        out_specs=(out_block, out_block, out_block, out_block),
        compiler_params=_params(("parallel",)),
        name="ret_in_rope" if rope is not None else "ret_in",
    )(*args)


def _log_sigmoid(x):
    return jnp.minimum(x, 0.0) - jnp.log(1.0 + jnp.exp(-jnp.abs(x)))


def _ret_kernel(q_ref, k_ref, v_ref, gate_ref, lg_ref, gn_ref, *rest, n, tq, heads, has_state):
    if has_state:
        s0_ref, y_ref, d_ref = rest
    else:
        y_ref, st_ref, d_ref = rest
    b = pl.program_id(1)
    qi = pl.program_id(2)
    r0 = qi * tq
    nt = (((1,), (1,)), ((), ()))
    sub = min(tq, RET_SUB_TILE)
    lgs = [(_log_sigmoid(lg_ref[hh, 0, 0:1, 0:1]), _log_sigmoid(lg_ref[hh, 1, 0:1, 0:1]))
           for hh in range(heads)]

    @pl.when(b == 0)
    def _():
        ii = r0 + lax.broadcasted_iota(jnp.int32, (tq, n), 0)
        jj = lax.broadcasted_iota(jnp.int32, (tq, n), 1)
        diff = (ii - jj).astype(F32)
        for hh in range(heads):
            lg_f, lg_b = lgs[hh]
            d_ref[hh, qi] = jnp.exp(jnp.where(diff >= 0.0, lg_f, -lg_b) * diff)

    def head_cols(hh):
        return slice(hh * RET_DK, (hh + 1) * RET_DK)

    def scores(unit):
        hh, si = unit
        return lax.dot_general(q_ref[0, si * sub:(si + 1) * sub, head_cols(hh)], k_ref[0, :, head_cols(hh)],
                               nt, preferred_element_type=F32)

    units = [(hh, si) for hh in range(heads) for si in range(tq // sub)]
    nxt = scores(units[0])
    for ui, (hh, si) in enumerate(units):
        s = nxt
        if ui + 1 < len(units):
            nxt = scores(units[ui + 1])
        lg_f, lg_b = lgs[hh]
        cols = head_cols(hh)
        rows = slice(si * sub, (si + 1) * sub)
        v = v_ref[0, :, cols]
        p = (s * d_ref[hh, qi, rows, :]).astype(BF16)
        o = jnp.dot(p, v, preferred_element_type=F32)
        if has_state:
            q = q_ref[0, rows, cols]
            i_col = (r0 + si * sub + lax.broadcasted_iota(jnp.int32, (sub, 1), 0)).astype(F32)
            o = o + (jnp.dot(q, s0_ref[0, 0, 0, hh].astype(BF16), preferred_element_type=F32)
                     * jnp.exp(lg_f * (i_col + 1.0))
                     + jnp.dot(q, s0_ref[0, 0, 1, hh].astype(BF16), preferred_element_type=F32)
                     * jnp.exp(lg_b * (n - i_col)))
        elif si == 0:
            eye = jnp.where(lax.broadcasted_iota(jnp.int32, (RET_DK, RET_DK), 0)
                            == lax.broadcasted_iota(jnp.int32, (RET_DK, RET_DK), 1), 1.0, 0.0).astype(BF16)
            kt = lax.dot_general(eye, k_ref[0, :, cols], nt, preferred_element_type=F32)
            j_row = lax.broadcasted_iota(jnp.int32, (1, n), 1).astype(F32)
            st_ref[0, 0, 0, hh] = jnp.dot((kt * jnp.exp(lg_f * (n - 1.0 - j_row))).astype(BF16), v,
                                          preferred_element_type=F32)
            st_ref[0, 0, 1, hh] = jnp.dot((kt * jnp.exp(lg_b * j_row)).astype(BF16), v,
                                          preferred_element_type=F32)
        mu = jnp.mean(o, axis=-1, keepdims=True)
        oc = o - mu
        var = jnp.mean(oc * oc, axis=-1, keepdims=True)
        on = oc * lax.rsqrt(var + EPS) * gn_ref[:, cols]
        gt = gate_ref[0, rows, cols]
        y_ref[0, rows, cols] = (gt * _sigmoid(gt) * on).astype(y_ref.dtype)


def _retention(q, k, v, gate, lg, gn, state0, heads, tq):
    b, n, _ = q.shape
    assert n % tq == 0 and RET_HEADS % heads == 0 and (state0 is not None or tq == n)
    width = heads * RET_DK
    q_tile = pl.BlockSpec((1, tq, width), lambda hg, bi, qi: (bi, qi, hg))
    seq = pl.BlockSpec((1, n, width), lambda hg, bi, qi: (bi, 0, hg))
    st = pl.BlockSpec((1, 1, 2, heads, RET_DK, RET_DV), lambda hg, bi, qi: (bi, 0, 0, hg, 0, 0))
    in_specs = [q_tile, seq, seq, q_tile,
                pl.BlockSpec((heads, 2, V7X_SUBLANES, V7X_LANES), lambda hg, bi, qi: (hg, 0, 0, 0)),
                pl.BlockSpec((1, width), lambda hg, bi, qi: (0, hg))]
    args = [q, k, v, gate, lg, gn]
    y_shape = jax.ShapeDtypeStruct((b, n, RET_HEADS * RET_DV), BF16)
    if state0 is not None:
        in_specs.append(st)
        args.append(state0)
        out_shape, out_specs = y_shape, q_tile
    else:
        out_shape = (y_shape, jax.ShapeDtypeStruct((b, 1, 2, RET_HEADS, RET_DK, RET_DV), F32))
        out_specs = (q_tile, st)
    return pl.pallas_call(
        functools.partial(_ret_kernel, n=n, tq=tq, heads=heads, has_state=state0 is not None),
        out_shape=out_shape,
        grid=(RET_HEADS // heads, b, n // tq),
        in_specs=in_specs,
        out_specs=out_specs,
        scratch_shapes=[pltpu.VMEM((heads, n // tq, tq, n), F32)],
        compiler_params=_params(("arbitrary", "arbitrary", "arbitrary")),
        name="retention_state" if state0 is not None else "retention",
    )(*args)


def kernel(x_prompt, x_sample, cache_k, cache_v, state_ret, c, c_ctx, w_mod, b_mod, norm_g,
           attn_w_qkv, attn_q_norm, attn_k_norm, attn_w_o, ret_w_qkvg, ret_decay_logit, ret_gn_w,
           ret_w_o, mlp_w1, mlp_w2, final_norm_g):
    bp, sp, d = x_prompt.shape
    bs, ss, _ = x_sample.shape
    depth = w_mod.shape[0]
    assert d == D_MODEL and depth == 2 and 1 + bs <= MOD_ROWS
    assert (bp * sp) % ROW_TILE == 0 and ss % ROW_TILE == 0

    cond = jnp.concatenate([c_ctx[None, :], c, jnp.zeros((MOD_ROWS - 1 - bs, d), F32)], axis=0)
    mod = _modulation(cond, w_mod, b_mod)
    mod = mod.reshape(depth, MOD_ROWS, 1, 6 * d)
    ctx_mod = _mod_spec(0, 0)
    lat_mod = _mod_spec(ss, 1)

    xp = x_prompt.reshape(bp * sp, d)
    xs = x_sample.reshape(bs * ss, d)
    nkv = N_KV_HEADS * HEAD_DIM

    g0 = norm_g[0, 0][None, :]
    g1 = norm_g[0, 1][None, :]
    wqkv = attn_w_qkv[0].astype(BF16)
    qn = attn_q_norm[0][None, :]
    kn = attn_k_norm[0][None, :]

    (qp, kp, vp), (w1,) = _attn_in(xp, mod[0], ctx_mod, g0, wqkv, qn, kn, None, casts=[(mlp_w1, 0)])
    reorder = np.zeros((HEAD_DIM, HEAD_DIM), np.float32)
    reorder[_PAIRED_ORDER, np.arange(HEAD_DIM)] = 1.0
    reorder = jnp.asarray(reorder, BF16)
    (qs, ks, vs), (w2, wo) = _attn_in(xs, mod[0], lat_mod, g0, wqkv, qn[:, _PAIRED_ORDER], kn[:, _PAIRED_ORDER],
                                      _rope_tables(ss, HEAD_DIM, True) + (reorder,),
                                      casts=[(mlp_w2, 0), (attn_w_o, 0)])

    ap = _attention(qp.reshape(bp, sp, -1), kp.reshape(bp, sp * N_KV_HEADS, HEAD_DIM),
                    vp.reshape(bp, sp * N_KV_HEADS, HEAD_DIM))
    xp, (wr,) = _post(ap.reshape(bp * sp, -1), xp, mod[0], ctx_mod, g1, wo, w1, w2, casts=[(ret_w_qkvg, 0)])
    cache = (cache_k, cache_v, reorder)
    as_ = _attention(qs.reshape(bs, ss, -1), ks.reshape(bs, ss, nkv), vs.reshape(bs, ss, nkv), cache)
    xs, (w1, w2, wo) = _post(as_.reshape(bs * ss, -1), xs, mod[0], lat_mod, g1, wo, w1, w2,
                             casts=[(mlp_w1, 1), (mlp_w2, 1), (ret_w_o, 0)])

    g0 = norm_g[1, 0][None, :]
    g1 = norm_g[1, 1][None, :]
    gn = ret_gn_w[0][None, :]
    gf = final_norm_g[None, :]
    lg = jnp.broadcast_to(ret_decay_logit[0].T[:, :, None, None],
                          (RET_HEADS, 2, V7X_SUBLANES, V7X_LANES))
    hw = RET_HEADS * RET_DK

    q, k, v, gate = _ret_in(xp, mod[1], ctx_mod, g0, wr, None)
    yp, new_state = _retention(q.reshape(bp, sp, hw), k.reshape(bp, sp, hw), v.reshape(bp, sp, hw),
                               gate.reshape(bp, sp, hw), lg, gn, None, RET_HEADS, sp)
    y_prompt, _ = _post(yp.reshape(bp * sp, hw), xp, mod[1], ctx_mod, g1, wo, w1, w2, gf)

    q, k, v, gate = _ret_in(xs, mod[1], lat_mod, g0, wr, _rope_tables(ss, RET_DK, False))
    ys = _retention(q.reshape(bs, ss, hw), k.reshape(bs, ss, hw), v.reshape(bs, ss, hw),
                    gate.reshape(bs, ss, hw), lg, gn, state_ret, 1, RET_Q_TILE)
    y_sample, _ = _post(ys.reshape(bs * ss, hw), xs, mod[1], lat_mod, g1, wo, w1, w2, gf)

    return (y_prompt.reshape(bp, sp, d),
            y_sample.reshape(bs, ss, d),
            kp.reshape(bp, 1, sp, N_KV_HEADS, HEAD_DIM),
            vp.reshape(bp, 1, sp, N_KV_HEADS, HEAD_DIM),
            new_state)
```

```python
import functools

import numpy as np
import jax
import jax.numpy as jnp
from jax import lax
from jax.experimental import pallas as pl
from jax.experimental.pallas import tpu as pltpu

F32 = jnp.float32
BF16 = jnp.bfloat16

D_MODEL = 1024
GRID_W = 64
N_HEADS = 8
N_KV_HEADS = 2
HEAD_DIM = 128
GROUP = N_HEADS // N_KV_HEADS
ROPE_BASE = 10000.0
RET_HEADS = 4
RET_DK = 256
RET_DV = 256
D_FF = 4 * D_MODEL
Q_SCALE = HEAD_DIM ** -0.5 * float(np.log2(np.e))
EPS = 1e-6
QKV_W = (N_HEADS + 2 * N_KV_HEADS) * HEAD_DIM
RET_W = 2 * RET_HEADS * RET_DK + 2 * RET_HEADS * RET_DV

V7X_LANES = 128
V7X_SUBLANES = 8
MOD_ROWS = 16
ROW_TILE = 512
ATTN_Q_TILE = 512
RET_Q_TILE = 1024
RET_SUB_TILE = 256
RET_STATE_HEADS = 2
VMEM_LIMIT = 56 * 1024 * 1024


def _params(sem, vmem=VMEM_LIMIT):
    return pltpu.CompilerParams(dimension_semantics=sem, vmem_limit_bytes=vmem)


def _const_spec(shape):
    nd = len(shape)
    return pl.BlockSpec(shape, lambda *_: (0,) * nd, pipeline_mode=pl.Buffered(1))


def _sigmoid(x):
    return 1.0 / (1.0 + jnp.exp(-x))


def _cast_plumbing(casts, steps):
    in_specs, out_specs, out_shapes = [], [], []
    for arr, layer in casts:
        _, r, c = arr.shape
        slab = r // steps
        assert slab * steps == r and slab % 16 == 0
        in_specs.append(pl.BlockSpec((1, slab, c), functools.partial(lambda i, l: (l, i, 0), l=layer)))
        out_specs.append(pl.BlockSpec((slab, c), lambda i: (i, 0)))
        out_shapes.append(jax.ShapeDtypeStruct((r, c), BF16))
    return in_specs, out_specs, out_shapes


def _with_casts(body, n_in, n_out, n_cast):
    def kernel(*refs):
        ins = refs[:n_in]
        cast_in = refs[n_in:n_in + n_cast]
        outs = refs[n_in + n_cast:n_in + n_cast + n_out]
        cast_out = refs[n_in + n_cast + n_out:n_in + 2 * n_cast + n_out]
        scratch = refs[n_in + 2 * n_cast + n_out:]
        for src, dst in zip(cast_in, cast_out):
            dst[...] = src[0].astype(dst.dtype)
        body(*ins, *outs, *scratch)
    return kernel


def _mod_kernel(cond_ref, w_ref, b_ref, o_ref):
    s = cond_ref[...]
    s = (s * _sigmoid(s)).astype(BF16)
    o_ref[0] = jnp.dot(s, w_ref[0].astype(BF16), preferred_element_type=F32) + b_ref[0]


def _modulation(cond, w_mod, b_mod):
    depth = w_mod.shape[0]
    tn = 1024
    return pl.pallas_call(
        _mod_kernel,
        out_shape=jax.ShapeDtypeStruct((depth, MOD_ROWS, 6 * D_MODEL), F32),
        grid=(depth, 6 * D_MODEL // tn),
        in_specs=[
            pl.BlockSpec((MOD_ROWS, D_MODEL), lambda i, j: (0, 0)),
            pl.BlockSpec((1, D_MODEL, tn), lambda i, j: (i, 0, j)),
            pl.BlockSpec((1, 1, tn), lambda i, j: (i, 0, j)),
        ],
        out_specs=pl.BlockSpec((1, MOD_ROWS, tn), lambda i, j: (i, 0, j)),
        compiler_params=_params(("parallel", "parallel")),
        name="modulation",
    )(cond, w_mod, b_mod.reshape(depth, 1, 6 * D_MODEL))


def _norm_modulate(x, g, mod_ref, which):
    y = x * lax.rsqrt(jnp.mean(x * x, axis=-1, keepdims=True) + EPS) * g
    shift = mod_ref[0, :, (3 * which) * D_MODEL:(3 * which + 1) * D_MODEL]
    scale = mod_ref[0, :, (3 * which + 1) * D_MODEL:(3 * which + 2) * D_MODEL]
    return y * (1.0 + scale) + shift


def _mod_spec(rows_per_mod_row, first_row):
    tiles = rows_per_mod_row // ROW_TILE if rows_per_mod_row else 0
    if tiles:
        return pl.BlockSpec((1, 1, 6 * D_MODEL), lambda i: (first_row + i // tiles, 0, 0))
    return pl.BlockSpec((1, 1, 6 * D_MODEL), lambda i: (first_row, 0, 0))


def _rope_tables(n, dim, paired):
    quarter = dim // 4
    rows = np.repeat(np.arange(n // GRID_W), GRID_W).astype(np.float64)
    cols = np.tile(np.arange(GRID_W), n // GRID_W).astype(np.float64)
    freqs = ROPE_BASE ** (-np.arange(quarter, dtype=np.float64) / quarter)
    ar = rows[:, None] * freqs
    ac = cols[:, None] * freqs
    if paired:
        cos = np.concatenate([np.cos(ar), np.cos(ac), np.cos(ar), np.cos(ac)], axis=1)
        sin = np.concatenate([-np.sin(ar), -np.sin(ac), np.sin(ar), np.sin(ac)], axis=1)
    else:
        cos = np.concatenate([np.cos(ar), np.cos(ar), np.cos(ac), np.cos(ac)], axis=1)
        sin = np.concatenate([-np.sin(ar), np.sin(ar), -np.sin(ac), np.sin(ac)], axis=1)
    return jnp.asarray(cos, F32), jnp.asarray(sin, F32)


_Q = HEAD_DIM // 4
_PAIRED_ORDER = np.concatenate([np.arange(0, _Q), np.arange(2 * _Q, 3 * _Q),
                                np.arange(_Q, 2 * _Q), np.arange(3 * _Q, 4 * _Q)])


def _swap_halves(t):
    return pltpu.roll(t, V7X_LANES // 2, axis=1)


def _attn_in_kernel(x_ref, mod_ref, g_ref, w_ref, qn_ref, kn_ref, *rest, use_rope):
    nq = N_HEADS * HEAD_DIM
    nk = N_KV_HEADS * HEAD_DIM
    if use_rope:
        cos_ref, sin_ref, reorder_ref, q_ref, k_ref, v_ref, wqk_ref = rest

        @pl.when(pl.program_id(0) == 0)
        def _():
            for head in range(N_HEADS + N_KV_HEADS):
                cols = slice(head * HEAD_DIM, (head + 1) * HEAD_DIM)
                wqk_ref[:, cols] = jnp.dot(w_ref[:, cols], reorder_ref[...],
                                           preferred_element_type=F32).astype(BF16)
    else:
        q_ref, k_ref, v_ref = rest
        wqk_ref = w_ref
    h = _norm_modulate(x_ref[...], g_ref[...], mod_ref, 0).astype(BF16)
    tm = x_ref.shape[0]
    pair_w = 2 * HEAD_DIM

    def put_kv(ref, kh, val):
        if use_rope:
            ref[:, kh * HEAD_DIM:(kh + 1) * HEAD_DIM] = val.astype(ref.dtype)
        else:
            ref[pl.ds(kh, tm, stride=N_KV_HEADS), :] = val.astype(ref.dtype)

    qw = qn_ref[...] * Q_SCALE
    kw = kn_ref[...]
    if use_rope:
        q_tabs = (qw * cos_ref[...], _swap_halves(qw) * sin_ref[...])
        k_tabs = (kw * cos_ref[...], _swap_halves(kw) * sin_ref[...])
        ra = lax.broadcasted_iota(jnp.int32, (pair_w, pair_w), 0) // HEAD_DIM
        rb = lax.broadcasted_iota(jnp.int32, (pair_w, pair_w), 1) // HEAD_DIM
        head_sum = jnp.where(ra == rb, 1.0, 0.0).astype(BF16)
    n_pairs = (N_HEADS + N_KV_HEADS) // 2
    t2_next = jnp.dot(h, wqk_ref[:, 0:pair_w], preferred_element_type=F32)
    for pair in range(n_pairs):
        t2 = t2_next
        if pair + 1 < n_pairs:
            t2_next = jnp.dot(h, wqk_ref[:, (pair + 1) * pair_w:(pair + 2) * pair_w], preferred_element_type=F32)
        else:
            t2_next = jnp.dot(h, w_ref[:, nq + nk:], preferred_element_type=F32)
        if use_rope:
            ss2 = jnp.dot((t2 * t2).astype(BF16), head_sum, preferred_element_type=F32)
            r2 = lax.rsqrt(ss2 * (1.0 / HEAD_DIM) + EPS)
        for j in range(2):
            head = 2 * pair + j
            t = t2[:, j * HEAD_DIM:(j + 1) * HEAD_DIM]
            if use_rope:
                r = r2[:, j * HEAD_DIM:(j + 1) * HEAD_DIM]
            else:
                r = lax.rsqrt(jnp.mean(t * t, axis=-1, keepdims=True) + EPS)
            if use_rope:
                wc, ws = q_tabs if head < N_HEADS else k_tabs
                t = (t * wc + _swap_halves(t) * ws) * r
            else:
                t = t * r * (qw if head < N_HEADS else kw)
            if head < N_HEADS:
                q_ref[:, head * HEAD_DIM:(head + 1) * HEAD_DIM] = t.astype(q_ref.dtype)
            else:
                put_kv(k_ref, head - N_HEADS, t)
    vv = t2_next
    for kh in range(N_KV_HEADS):
        put_kv(v_ref, kh, vv[:, kh * HEAD_DIM:(kh + 1) * HEAD_DIM])


def _attn_in(x, mod, mod_spec, g, w, qn, kn, rope, casts=()):
    rows = x.shape[0]
    nk = N_KV_HEADS * HEAD_DIM
    if rope is None:
        kv_shape = jax.ShapeDtypeStruct((rows * N_KV_HEADS, HEAD_DIM), F32)
        kv_spec = pl.BlockSpec((ROW_TILE * N_KV_HEADS, HEAD_DIM), lambda i: (i, 0))
    else:
        kv_shape = jax.ShapeDtypeStruct((rows, nk), BF16)
        kv_spec = pl.BlockSpec((ROW_TILE, nk), lambda i: (i, 0))
    in_specs = [
        pl.BlockSpec((ROW_TILE, D_MODEL), lambda i: (i, 0)),
        mod_spec,
        _const_spec((1, D_MODEL)),
        _const_spec((D_MODEL, QKV_W)),
        _const_spec((1, HEAD_DIM)),
        _const_spec((1, HEAD_DIM)),
    ]
    args = [x, mod, g, w, qn, kn]
    scratch = []
    if rope is not None:
        cos, sin, reorder = rope
        seq_tiles = cos.shape[0] // ROW_TILE
        in_specs += [pl.BlockSpec((ROW_TILE, HEAD_DIM), lambda i: (i % seq_tiles, 0))] * 2
        in_specs.append(_const_spec((HEAD_DIM, HEAD_DIM)))
        args += [cos, sin, reorder]
        scratch = [pltpu.VMEM((D_MODEL, (N_HEADS + N_KV_HEADS) * HEAD_DIM), BF16)]
    steps = rows // ROW_TILE
    c_in, c_out, c_shapes = _cast_plumbing(casts, steps)
    out_specs = [pl.BlockSpec((ROW_TILE, N_HEADS * HEAD_DIM), lambda i: (i, 0)), kv_spec, kv_spec]
    out_shapes = [jax.ShapeDtypeStruct((rows, N_HEADS * HEAD_DIM), BF16), kv_shape, kv_shape]
    body = functools.partial(_attn_in_kernel, use_rope=rope is not None)
    outs = pl.pallas_call(
        _with_casts(body, len(args), len(out_shapes), len(casts)),
        out_shape=tuple(out_shapes + c_shapes),
        grid=(steps,),
        in_specs=in_specs + c_in,
        out_specs=tuple(out_specs + c_out),
        scratch_shapes=scratch,
        compiler_params=_params(("arbitrary",)),
        name="attn_in_rope" if rope is not None else "attn_in",
    )(*args, *[arr for arr, _ in casts])
    return outs[:3], list(outs[3:])


def _kv_head(ref, kv):
    if ref.dtype == BF16:
        return ref[0, :, kv * HEAD_DIM:(kv + 1) * HEAD_DIM]
    if len(ref.shape) == 5:
        return ref[0, 0, :, kv, :].astype(BF16)
    tokens = ref.shape[1] // N_KV_HEADS
    return ref[0, pl.ds(kv, tokens, stride=N_KV_HEADS), :].astype(BF16)


def _attn_kernel(q_ref, k_ref, v_ref, *rest, cached):
    if cached:
        ck_ref, cv_ref, reorder_ref, o_ref, ck_scr, cv_scr = rest

        @pl.when(pl.program_id(1) == 0)
        def _():
            for kv in range(N_KV_HEADS):
                ck_scr[kv] = jnp.dot(_kv_head(ck_ref, kv), reorder_ref[...],
                                     preferred_element_type=F32).astype(BF16)
                cv_scr[kv] = _kv_head(cv_ref, kv)
    else:
        (o_ref,) = rest
    nt = (((1,), (1,)), ((), ()))
    ks = [_kv_head(k_ref, kv) for kv in range(N_KV_HEADS)]
    vs = [_kv_head(v_ref, kv) for kv in range(N_KV_HEADS)]

    def scores(head):
        kv = head // GROUP
        q = q_ref[0, :, head * HEAD_DIM:(head + 1) * HEAD_DIM]
        s = lax.dot_general(q, ks[kv], nt, preferred_element_type=F32)
        sc = lax.dot_general(q, ck_scr[kv], nt, preferred_element_type=F32) if cached else None
        return s, sc

    nxt = scores(0)
    for head in range(N_HEADS):
        kv = head // GROUP
        s, sc = nxt
        if head + 1 < N_HEADS:
            nxt = scores(head + 1)
        m = jnp.max(s, axis=-1, keepdims=True)
        if cached:
            m = jnp.maximum(m, jnp.max(sc, axis=-1, keepdims=True))
        p = jnp.exp2(s - m)
        l = jnp.sum(p, axis=-1, keepdims=True)
        o = jnp.dot(p.astype(BF16), vs[kv], preferred_element_type=F32)
        if cached:
            pc = jnp.exp2(sc - m)
            l = l + jnp.sum(pc, axis=-1, keepdims=True)
            o = o + jnp.dot(pc.astype(BF16), cv_scr[kv], preferred_element_type=F32)
        o_ref[0, :, head * HEAD_DIM:(head + 1) * HEAD_DIM] = (o / l).astype(o_ref.dtype)


def _attention(q, k, v, cache=None):
    b, n, width = q.shape
    tq = min(ATTN_Q_TILE, n)
    in_specs = [pl.BlockSpec((1, tq, width), lambda bi, qi: (bi, qi, 0))]
    in_specs += [pl.BlockSpec((1,) + k.shape[1:], lambda bi, qi: (bi, 0, 0))] * 2
    args = [q, k, v]
    scratch = []
    if cache is not None:
        in_specs += [pl.BlockSpec((1,) + cache[0].shape[1:], lambda bi, qi: (bi, 0, 0, 0, 0))] * 2
        in_specs.append(pl.BlockSpec((HEAD_DIM, HEAD_DIM), lambda bi, qi: (0, 0)))
        args += list(cache)
        scratch = [pltpu.VMEM((N_KV_HEADS, cache[0].shape[2], HEAD_DIM), BF16)] * 2
    return pl.pallas_call(
        functools.partial(_attn_kernel, cached=cache is not None),
        out_shape=jax.ShapeDtypeStruct((b, n, width), BF16),
        grid=(b, n // tq),
        in_specs=in_specs,
        out_specs=pl.BlockSpec((1, tq, width), lambda bi, qi: (bi, qi, 0)),
        scratch_shapes=scratch,
        compiler_params=_params(("arbitrary", "arbitrary")),
        name="attention_cached" if cache is not None else "attention",
    )(*args)


def _post_kernel(a_ref, x_ref, mod_ref, g_ref, wo_ref, w1_ref, w2_ref, *rest, final):
    if final:
        gf_ref, o_ref = rest
    else:
        (o_ref,) = rest
    gate1 = mod_ref[0, :, 2 * D_MODEL:3 * D_MODEL]
    gate2 = mod_ref[0, :, 5 * D_MODEL:6 * D_MODEL]
    x = x_ref[...] + gate1 * jnp.dot(a_ref[...], wo_ref[...], preferred_element_type=F32)
    h = _norm_modulate(x, g_ref[...], mod_ref, 1).astype(BF16)
    ff_tile = D_FF // 4
    y = None
    for f in range(D_FF // ff_tile):
        u = jnp.dot(h, w1_ref[:, f * ff_tile:(f + 1) * ff_tile], preferred_element_type=F32)
        u = jnp.square(jnp.maximum(u, 0.0)).astype(BF16)
        part = jnp.dot(u, w2_ref[f * ff_tile:(f + 1) * ff_tile, :], preferred_element_type=F32)
        y = part if y is None else y + part
    x = x + gate2 * y
    if final:
        x = x * lax.rsqrt(jnp.mean(x * x, axis=-1, keepdims=True) + EPS) * gf_ref[...]
    o_ref[...] = x


def _post(a, x, mod, mod_spec, g, wo, w1, w2, gf=None, casts=()):
    rows = x.shape[0]
    in_specs = [
        pl.BlockSpec((ROW_TILE, D_MODEL), lambda i: (i, 0)),
        pl.BlockSpec((ROW_TILE, D_MODEL), lambda i: (i, 0)),
        mod_spec,
        _const_spec((1, D_MODEL)),
        _const_spec((D_MODEL, D_MODEL)),
        _const_spec((D_MODEL, D_FF)),
        _const_spec((D_FF, D_MODEL)),
    ]
    args = [a, x, mod, g, wo, w1, w2]
    if gf is not None:
        in_specs.append(_const_spec((1, D_MODEL)))
        args.append(gf)
    steps = rows // ROW_TILE
    c_in, c_out, c_shapes = _cast_plumbing(casts, steps)
    body = functools.partial(_post_kernel, final=gf is not None)
    outs = pl.pallas_call(
        _with_casts(body, len(args), 1, len(casts)),
        out_shape=tuple([jax.ShapeDtypeStruct((rows, D_MODEL), F32)] + c_shapes),
        grid=(steps,),
        in_specs=in_specs + c_in,
        out_specs=tuple([pl.BlockSpec((ROW_TILE, D_MODEL), lambda i: (i, 0))] + c_out),
        compiler_params=_params(("parallel",)),
        name="post_final" if gf is not None else "post",
    )(*args, *[arr for arr, _ in casts])
    return outs[0], list(outs[1:])


def _ret_in_kernel(x_ref, mod_ref, g_ref, w_ref, *rest, use_rope):
    if use_rope:
        cos_ref, sin_ref, q_ref, k_ref, v_ref, gate_ref = rest
    else:
        q_ref, k_ref, v_ref, gate_ref = rest
    h = _norm_modulate(x_ref[...], g_ref[...], mod_ref, 0).astype(BF16)
    width = RET_HEADS * RET_DK
    for part, ref in enumerate((q_ref, k_ref, v_ref, gate_ref)):
        t = jnp.dot(h, w_ref[:, part * width:(part + 1) * width], preferred_element_type=F32)
        if part < 2 and use_rope:
            for c in range(width // V7X_LANES):
                half = (c % (RET_DK // V7X_LANES)) * V7X_LANES
                tc = t[:, c * V7X_LANES:(c + 1) * V7X_LANES]
                tc = (tc * cos_ref[:, half:half + V7X_LANES]
                      + _swap_halves(tc) * sin_ref[:, half:half + V7X_LANES])
                if part == 0:
                    tc = tc * RET_DK ** -0.5
                ref[:, c * V7X_LANES:(c + 1) * V7X_LANES] = tc.astype(ref.dtype)
        else:
            if part == 0:
                t = t * RET_DK ** -0.5
            ref[...] = t.astype(ref.dtype)


def _ret_in(x, mod, mod_spec, g, w, rope):
    rows = x.shape[0]
    width = RET_HEADS * RET_DK
    in_specs = [
        pl.BlockSpec((ROW_TILE, D_MODEL), lambda i: (i, 0)),
        mod_spec,
        _const_spec((1, D_MODEL)),
        _const_spec((D_MODEL, RET_W)),
    ]
    args = [x, mod, g, w]
    if rope is not None:
        seq_tiles = rope[0].shape[0] // ROW_TILE
        in_specs += [pl.BlockSpec((ROW_TILE, RET_DK), lambda i: (i % seq_tiles, 0))] * 2
        args += list(rope)
    out_block = pl.BlockSpec((ROW_TILE, width), lambda i: (i, 0))
    return pl.pallas_call(
        functools.partial(_ret_in_kernel, use_rope=rope is not None),
        out_shape=(jax.ShapeDtypeStruct((rows, width), BF16),
                   jax.ShapeDtypeStruct((rows, width), BF16),
                   jax.ShapeDtypeStruct((rows, width), BF16),
                   jax.ShapeDtypeStruct((rows, width), F32)),
        grid=(rows // ROW_TILE,),
        in_specs=in_specs,
        out_specs=(out_block, out_block, out_block, out_block),
        compiler_params=_params(("parallel",)),
        name="ret_in_rope" if rope is not None else "ret_in",
    )(*args)


def _log_sigmoid(x):
    return jnp.minimum(x, 0.0) - jnp.log(1.0 + jnp.exp(-jnp.abs(x)))


def _ret_kernel(q_ref, k_ref, v_ref, gate_ref, lg_ref, gn_ref, *rest, n, tq, heads, has_state):
    if has_state:
        s0_ref, y_ref, d_ref = rest
    else:
        y_ref, st_ref, d_ref = rest
    b = pl.program_id(1)
    qi = pl.program_id(2)
    r0 = qi * tq
    nt = (((1,), (1,)), ((), ()))
    sub = min(tq, RET_SUB_TILE)
    lgs = [(_log_sigmoid(lg_ref[hh, 0, 0:1, 0:1]), _log_sigmoid(lg_ref[hh, 1, 0:1, 0:1]))
           for hh in range(heads)]

    @pl.when(b == 0)
    def _():
        ii = r0 + lax.broadcasted_iota(jnp.int32, (tq, n), 0)
        jj = lax.broadcasted_iota(jnp.int32, (tq, n), 1)
        diff = (ii - jj).astype(F32)
        for hh in range(heads):
            lg_f, lg_b = lgs[hh]
            d_ref[hh, qi] = jnp.exp(jnp.where(diff >= 0.0, lg_f, -lg_b) * diff)

    def head_cols(hh):
        return slice(hh * RET_DK, (hh + 1) * RET_DK)

    def scores(unit):
        hh, si = unit
        return lax.dot_general(q_ref[0, si * sub:(si + 1) * sub, head_cols(hh)], k_ref[0, :, head_cols(hh)],
                               nt, preferred_element_type=F32)

    units = [(hh, si) for hh in range(heads) for si in range(tq // sub)]
    nxt = scores(units[0])
    for ui, (hh, si) in enumerate(units):
        s = nxt
        if ui + 1 < len(units):
            nxt = scores(units[ui + 1])
        lg_f, lg_b = lgs[hh]
        cols = head_cols(hh)
        rows = slice(si * sub, (si + 1) * sub)
        v = v_ref[0, :, cols]
        p = (s * d_ref[hh, qi, rows, :]).astype(BF16)
        o = jnp.dot(p, v, preferred_element_type=F32)
        if has_state:
            q = q_ref[0, rows, cols]
            i_col = (r0 + si * sub + lax.broadcasted_iota(jnp.int32, (sub, 1), 0)).astype(F32)
            o = o + (jnp.dot(q, s0_ref[0, 0, 0, hh].astype(BF16), preferred_element_type=F32)
                     * jnp.exp(lg_f * (i_col + 1.0))
                     + jnp.dot(q, s0_ref[0, 0, 1, hh].astype(BF16), preferred_element_type=F32)
                     * jnp.exp(lg_b * (n - i_col)))
        elif si == 0:
            eye = jnp.where(lax.broadcasted_iota(jnp.int32, (RET_DK, RET_DK), 0)
                            == lax.broadcasted_iota(jnp.int32, (RET_DK, RET_DK), 1), 1.0, 0.0).astype(BF16)
            kt = lax.dot_general(eye, k_ref[0, :, cols], nt, preferred_element_type=F32)
            j_row = lax.broadcasted_iota(jnp.int32, (1, n), 1).astype(F32)
            st_ref[0, 0, 0, hh] = jnp.dot((kt * jnp.exp(lg_f * (n - 1.0 - j_row))).astype(BF16), v,
                                          preferred_element_type=F32)
            st_ref[0, 0, 1, hh] = jnp.dot((kt * jnp.exp(lg_b * j_row)).astype(BF16), v,
                                          preferred_element_type=F32)
        mu = jnp.mean(o, axis=-1, keepdims=True)
        oc = o - mu
        var = jnp.mean(oc * oc, axis=-1, keepdims=True)
        on = oc * lax.rsqrt(var + EPS) * gn_ref[:, cols]
        gt = gate_ref[0, rows, cols]
        y_ref[0, rows, cols] = (gt * _sigmoid(gt) * on).astype(y_ref.dtype)


def _retention(q, k, v, gate, lg, gn, state0, heads, tq):
    b, n, _ = q.shape
    assert n % tq == 0 and RET_HEADS % heads == 0 and (state0 is not None or tq == n)
    width = heads * RET_DK
    q_tile = pl.BlockSpec((1, tq, width), lambda hg, bi, qi: (bi, qi, hg))
    seq = pl.BlockSpec((1, n, width), lambda hg, bi, qi: (bi, 0, hg))
    st = pl.BlockSpec((1, 1, 2, heads, RET_DK, RET_DV), lambda hg, bi, qi: (bi, 0, 0, hg, 0, 0))
    in_specs = [q_tile, seq, seq, q_tile,
                pl.BlockSpec((heads, 2, V7X_SUBLANES, V7X_LANES), lambda hg, bi, qi: (hg, 0, 0, 0)),
                pl.BlockSpec((1, width), lambda hg, bi, qi: (0, hg))]
    args = [q, k, v, gate, lg, gn]
    y_shape = jax.ShapeDtypeStruct((b, n, RET_HEADS * RET_DV), BF16)
    if state0 is not None:
        in_specs.append(st)
        args.append(state0)
        out_shape, out_specs = y_shape, q_tile
    else:
        out_shape = (y_shape, jax.ShapeDtypeStruct((b, 1, 2, RET_HEADS, RET_DK, RET_DV), F32))
        out_specs = (q_tile, st)
    return pl.pallas_call(
        functools.partial(_ret_kernel, n=n, tq=tq, heads=heads, has_state=state0 is not None),
        out_shape=out_shape,
        grid=(RET_HEADS // heads, b, n // tq),
        in_specs=in_specs,
        out_specs=out_specs,
        scratch_shapes=[pltpu.VMEM((heads, n // tq, tq, n), F32)],
        compiler_params=_params(("arbitrary", "arbitrary", "arbitrary")),
        name="retention_state" if state0 is not None else "retention",
    )(*args)


def kernel(x_prompt, x_sample, cache_k, cache_v, state_ret, c, c_ctx, w_mod, b_mod, norm_g,
           attn_w_qkv, attn_q_norm, attn_k_norm, attn_w_o, ret_w_qkvg, ret_decay_logit, ret_gn_w,
           ret_w_o, mlp_w1, mlp_w2, final_norm_g):
    bp, sp, d = x_prompt.shape
    bs, ss, _ = x_sample.shape
    depth = w_mod.shape[0]
    assert d == D_MODEL and depth == 2 and 1 + bs <= MOD_ROWS
    assert (bp * sp) % ROW_TILE == 0 and ss % ROW_TILE == 0

    cond = jnp.concatenate([c_ctx[None, :], c, jnp.zeros((MOD_ROWS - 1 - bs, d), F32)], axis=0)
    mod = _modulation(cond, w_mod, b_mod)
    mod = mod.reshape(depth, MOD_ROWS, 1, 6 * d)
    ctx_mod = _mod_spec(0, 0)
    lat_mod = _mod_spec(ss, 1)

    xp = x_prompt.reshape(bp * sp, d)
    xs = x_sample.reshape(bs * ss, d)
    nkv = N_KV_HEADS * HEAD_DIM

    g0 = norm_g[0, 0][None, :]
    g1 = norm_g[0, 1][None, :]
    wqkv = attn_w_qkv[0].astype(BF16)
    qn = attn_q_norm[0][None, :]
    kn = attn_k_norm[0][None, :]

    (qp, kp, vp), (w1,) = _attn_in(xp, mod[0], ctx_mod, g0, wqkv, qn, kn, None, casts=[(mlp_w1, 0)])
    reorder = np.zeros((HEAD_DIM, HEAD_DIM), np.float32)
    reorder[_PAIRED_ORDER, np.arange(HEAD_DIM)] = 1.0
    reorder = jnp.asarray(reorder, BF16)
    (qs, ks, vs), (w2, wo) = _attn_in(xs, mod[0], lat_mod, g0, wqkv, qn[:, _PAIRED_ORDER], kn[:, _PAIRED_ORDER],
                                      _rope_tables(ss, HEAD_DIM, True) + (reorder,),
                                      casts=[(mlp_w2, 0), (attn_w_o, 0)])

    ap = _attention(qp.reshape(bp, sp, -1), kp.reshape(bp, sp * N_KV_HEADS, HEAD_DIM),
                    vp.reshape(bp, sp * N_KV_HEADS, HEAD_DIM))
    xp, (wr,) = _post(ap.reshape(bp * sp, -1), xp, mod[0], ctx_mod, g1, wo, w1, w2, casts=[(ret_w_qkvg, 0)])
    cache = (cache_k, cache_v, reorder)
    as_ = _attention(qs.reshape(bs, ss, -1), ks.reshape(bs, ss, nkv), vs.reshape(bs, ss, nkv), cache)
    xs, (w1, w2, wo) = _post(as_.reshape(bs * ss, -1), xs, mod[0], lat_mod, g1, wo, w1, w2,
                             casts=[(mlp_w1, 1), (mlp_w2, 1), (ret_w_o, 0)])

    g0 = norm_g[1, 0][None, :]
    g1 = norm_g[1, 1][None, :]
    gn = ret_gn_w[0][None, :]
    gf = final_norm_g[None, :]
    lg = jnp.broadcast_to(ret_decay_logit[0].T[:, :, None, None],
                          (RET_HEADS, 2, V7X_SUBLANES, V7X_LANES))
    hw = RET_HEADS * RET_DK

    q, k, v, gate = _ret_in(xp, mod[1], ctx_mod, g0, wr, None)
    yp, new_state = _retention(q.reshape(bp, sp, hw), k.reshape(bp, sp, hw), v.reshape(bp, sp, hw),
                               gate.reshape(bp, sp, hw), lg, gn, None, RET_HEADS, sp)
    y_prompt, _ = _post(yp.reshape(bp * sp, hw), xp, mod[1], ctx_mod, g1, wo, w1, w2, gf)

    q, k, v, gate = _ret_in(xs, mod[1], lat_mod, g0, wr, _rope_tables(ss, RET_DK, False))
    ys = _retention(q.reshape(bs, ss, hw), k.reshape(bs, ss, hw), v.reshape(bs, ss, hw),
                    gate.reshape(bs, ss, hw), lg, gn, state_ret, RET_STATE_HEADS, RET_Q_TILE)
    y_sample, _ = _post(ys.reshape(bs * ss, hw), xs, mod[1], lat_mod, g1, wo, w1, w2, gf)

    return (y_prompt.reshape(bp, sp, d),
            y_sample.reshape(bs, ss, d),
            kp.reshape(bp, 1, sp, N_KV_HEADS, HEAD_DIM),
            vp.reshape(bp, 1, sp, N_KV_HEADS, HEAD_DIM),
            new_state)
```

```python
import functools

import numpy as np
import jax
import jax.numpy as jnp
from jax import lax
from jax.experimental import pallas as pl
from jax.experimental.pallas import tpu as pltpu

F32 = jnp.float32
BF16 = jnp.bfloat16

D_MODEL = 1024
GRID_W = 64
N_HEADS = 8
N_KV_HEADS = 2
HEAD_DIM = 128
GROUP = N_HEADS // N_KV_HEADS
ROPE_BASE = 10000.0
RET_HEADS = 4
RET_DK = 256
RET_DV = 256
D_FF = 4 * D_MODEL
Q_SCALE = HEAD_DIM ** -0.5 * float(np.log2(np.e))
EPS = 1e-6
QKV_W = (N_HEADS + 2 * N_KV_HEADS) * HEAD_DIM
RET_W = 2 * RET_HEADS * RET_DK + 2 * RET_HEADS * RET_DV

V7X_LANES = 128
V7X_SUBLANES = 8
MOD_ROWS = 16
ROW_TILE = 512
POST_ROW_TILE = 1024
POST_SUB_TILE = 512
ATTN_Q_TILE = 512
RET_Q_TILE = 1024
RET_SUB_TILE = 256
RET_STATE_HEADS = 2
VMEM_LIMIT = 56 * 1024 * 1024


def _params(sem, vmem=VMEM_LIMIT):
    return pltpu.CompilerParams(dimension_semantics=sem, vmem_limit_bytes=vmem)


def _const_spec(shape):
    nd = len(shape)
    return pl.BlockSpec(shape, lambda *_: (0,) * nd, pipeline_mode=pl.Buffered(1))


def _sigmoid(x):
    return 1.0 / (1.0 + jnp.exp(-x))


def _cast_plumbing(casts, steps):
    in_specs, out_specs, out_shapes = [], [], []
    for arr, layer in casts:
        _, r, c = arr.shape
        slab = r // steps
        assert slab * steps == r and slab % 16 == 0
        in_specs.append(pl.BlockSpec((1, slab, c), functools.partial(lambda *idx, l: (l, idx[0], 0), l=layer)))
        out_specs.append(pl.BlockSpec((slab, c), lambda *idx: (idx[0], 0)))
        out_shapes.append(jax.ShapeDtypeStruct((r, c), BF16))
    return in_specs, out_specs, out_shapes


def _with_casts(body, n_in, n_out, n_cast):
    def kernel(*refs):
        ins = refs[:n_in]
        cast_in = refs[n_in:n_in + n_cast]
        outs = refs[n_in + n_cast:n_in + n_cast + n_out]
        cast_out = refs[n_in + n_cast + n_out:n_in + 2 * n_cast + n_out]
        scratch = refs[n_in + 2 * n_cast + n_out:]
        for src, dst in zip(cast_in, cast_out):
            dst[...] = src[0].astype(dst.dtype)
        body(*ins, *outs, *scratch)
    return kernel


def _mod_kernel(cond_ref, w_ref, b_ref, o_ref):
    s = cond_ref[...]
    s = (s * _sigmoid(s)).astype(BF16)
    o_ref[0] = jnp.dot(s, w_ref[0].astype(BF16), preferred_element_type=F32) + b_ref[0]


def _modulation(cond, w_mod, b_mod):
    depth = w_mod.shape[0]
    tn = 1024
    return pl.pallas_call(
        _mod_kernel,
        out_shape=jax.ShapeDtypeStruct((depth, MOD_ROWS, 6 * D_MODEL), F32),
        grid=(depth, 6 * D_MODEL // tn),
        in_specs=[
            pl.BlockSpec((MOD_ROWS, D_MODEL), lambda i, j: (0, 0)),
            pl.BlockSpec((1, D_MODEL, tn), lambda i, j: (i, 0, j)),
            pl.BlockSpec((1, 1, tn), lambda i, j: (i, 0, j)),
        ],
        out_specs=pl.BlockSpec((1, MOD_ROWS, tn), lambda i, j: (i, 0, j)),
        compiler_params=_params(("parallel", "parallel")),
        name="modulation",
    )(cond, w_mod, b_mod.reshape(depth, 1, 6 * D_MODEL))


def _norm_modulate(x, g, mod_ref, which):
    y = x * lax.rsqrt(jnp.mean(x * x, axis=-1, keepdims=True) + EPS) * g
    shift = mod_ref[0, :, (3 * which) * D_MODEL:(3 * which + 1) * D_MODEL]
    scale = mod_ref[0, :, (3 * which + 1) * D_MODEL:(3 * which + 2) * D_MODEL]
    return y * (1.0 + scale) + shift


def _mod_spec(mod_rows, tile):
    rows_per_mod_row, first_row = mod_rows
    if rows_per_mod_row:
        assert rows_per_mod_row % tile == 0
        tiles = rows_per_mod_row // tile
        return pl.BlockSpec((1, 1, 6 * D_MODEL), lambda i: (first_row + i // tiles, 0, 0))
    return pl.BlockSpec((1, 1, 6 * D_MODEL), lambda i: (first_row, 0, 0))


def _rope_tables(n, dim, paired):
    quarter = dim // 4
    rows = np.repeat(np.arange(n // GRID_W), GRID_W).astype(np.float64)
    cols = np.tile(np.arange(GRID_W), n // GRID_W).astype(np.float64)
    freqs = ROPE_BASE ** (-np.arange(quarter, dtype=np.float64) / quarter)
    ar = rows[:, None] * freqs
    ac = cols[:, None] * freqs
    if paired:
        cos = np.concatenate([np.cos(ar), np.cos(ac), np.cos(ar), np.cos(ac)], axis=1)
        sin = np.concatenate([-np.sin(ar), -np.sin(ac), np.sin(ar), np.sin(ac)], axis=1)
    else:
        cos = np.concatenate([np.cos(ar), np.cos(ar), np.cos(ac), np.cos(ac)], axis=1)
        sin = np.concatenate([-np.sin(ar), np.sin(ar), -np.sin(ac), np.sin(ac)], axis=1)
    return jnp.asarray(cos, F32), jnp.asarray(sin, F32)


_Q = HEAD_DIM // 4
_PAIRED_ORDER = np.concatenate([np.arange(0, _Q), np.arange(2 * _Q, 3 * _Q),
                                np.arange(_Q, 2 * _Q), np.arange(3 * _Q, 4 * _Q)])


def _swap_halves(t):
    return pltpu.roll(t, V7X_LANES // 2, axis=1)


def _attn_in_kernel(x_ref, mod_ref, g_ref, w_ref, qn_ref, kn_ref, *rest, use_rope):
    nq = N_HEADS * HEAD_DIM
    nk = N_KV_HEADS * HEAD_DIM
    if use_rope:
        cos_ref, sin_ref, reorder_ref, q_ref, k_ref, v_ref, wqk_ref = rest

        @pl.when(pl.program_id(0) == 0)
        def _():
            for head in range(N_HEADS + N_KV_HEADS):
                cols = slice(head * HEAD_DIM, (head + 1) * HEAD_DIM)
                wqk_ref[:, cols] = jnp.dot(w_ref[:, cols], reorder_ref[...],
                                           preferred_element_type=F32).astype(BF16)
    else:
        q_ref, k_ref, v_ref = rest
        wqk_ref = w_ref
    h = _norm_modulate(x_ref[...], g_ref[...], mod_ref, 0).astype(BF16)
    tm = x_ref.shape[0]
    pair_w = 2 * HEAD_DIM

    def put_kv(ref, kh, val):
        if use_rope:
            ref[:, kh * HEAD_DIM:(kh + 1) * HEAD_DIM] = val.astype(ref.dtype)
        else:
            ref[pl.ds(kh, tm, stride=N_KV_HEADS), :] = val.astype(ref.dtype)

    qw = qn_ref[...] * Q_SCALE
    kw = kn_ref[...]
    if use_rope:
        q_tabs = (qw * cos_ref[...], _swap_halves(qw) * sin_ref[...])
        k_tabs = (kw * cos_ref[...], _swap_halves(kw) * sin_ref[...])
        ra = lax.broadcasted_iota(jnp.int32, (pair_w, pair_w), 0) // HEAD_DIM
        rb = lax.broadcasted_iota(jnp.int32, (pair_w, pair_w), 1) // HEAD_DIM
        head_sum = jnp.where(ra == rb, 1.0, 0.0).astype(BF16)
    n_pairs = (N_HEADS + N_KV_HEADS) // 2
    t2_next = jnp.dot(h, wqk_ref[:, 0:pair_w], preferred_element_type=F32)
    for pair in range(n_pairs):
        t2 = t2_next
        if pair + 1 < n_pairs:
            t2_next = jnp.dot(h, wqk_ref[:, (pair + 1) * pair_w:(pair + 2) * pair_w], preferred_element_type=F32)
        else:
            t2_next = jnp.dot(h, w_ref[:, nq + nk:], preferred_element_type=F32)
        if use_rope:
            ss2 = jnp.dot((t2 * t2).astype(BF16), head_sum, preferred_element_type=F32)
            r2 = lax.rsqrt(ss2 * (1.0 / HEAD_DIM) + EPS)
        for j in range(2):
            head = 2 * pair + j
            t = t2[:, j * HEAD_DIM:(j + 1) * HEAD_DIM]
            if use_rope:
                r = r2[:, j * HEAD_DIM:(j + 1) * HEAD_DIM]
            else:
                r = lax.rsqrt(jnp.mean(t * t, axis=-1, keepdims=True) + EPS)
            if use_rope:
                wc, ws = q_tabs if head < N_HEADS else k_tabs
                t = (t * wc + _swap_halves(t) * ws) * r
            else:
                t = t * r * (qw if head < N_HEADS else kw)
            if head < N_HEADS:
                q_ref[:, head * HEAD_DIM:(head + 1) * HEAD_DIM] = t.astype(q_ref.dtype)
            else:
                put_kv(k_ref, head - N_HEADS, t)
    vv = t2_next
    for kh in range(N_KV_HEADS):
        put_kv(v_ref, kh, vv[:, kh * HEAD_DIM:(kh + 1) * HEAD_DIM])


def _attn_in(x, mod, mod_rows, g, w, qn, kn, rope, casts=()):
    rows = x.shape[0]
    nk = N_KV_HEADS * HEAD_DIM
    if rope is None:
        kv_shape = jax.ShapeDtypeStruct((rows * N_KV_HEADS, HEAD_DIM), F32)
        kv_spec = pl.BlockSpec((ROW_TILE * N_KV_HEADS, HEAD_DIM), lambda i: (i, 0))
    else:
        kv_shape = jax.ShapeDtypeStruct((rows, nk), BF16)
        kv_spec = pl.BlockSpec((ROW_TILE, nk), lambda i: (i, 0))
    in_specs = [
        pl.BlockSpec((ROW_TILE, D_MODEL), lambda i: (i, 0)),
        _mod_spec(mod_rows, ROW_TILE),
        _const_spec((1, D_MODEL)),
        _const_spec((D_MODEL, QKV_W)),
        _const_spec((1, HEAD_DIM)),
        _const_spec((1, HEAD_DIM)),
    ]
    args = [x, mod, g, w, qn, kn]
    scratch = []
    if rope is not None:
        cos, sin, reorder = rope
        seq_tiles = cos.shape[0] // ROW_TILE
        in_specs += [pl.BlockSpec((ROW_TILE, HEAD_DIM), lambda i: (i % seq_tiles, 0))] * 2
        in_specs.append(_const_spec((HEAD_DIM, HEAD_DIM)))
        args += [cos, sin, reorder]
        scratch = [pltpu.VMEM((D_MODEL, (N_HEADS + N_KV_HEADS) * HEAD_DIM), BF16)]
    steps = rows // ROW_TILE
    c_in, c_out, c_shapes = _cast_plumbing(casts, steps)
    out_specs = [pl.BlockSpec((ROW_TILE, N_HEADS * HEAD_DIM), lambda i: (i, 0)), kv_spec, kv_spec]
    out_shapes = [jax.ShapeDtypeStruct((rows, N_HEADS * HEAD_DIM), BF16), kv_shape, kv_shape]
    body = functools.partial(_attn_in_kernel, use_rope=rope is not None)
    outs = pl.pallas_call(
        _with_casts(body, len(args), len(out_shapes), len(casts)),
        out_shape=tuple(out_shapes + c_shapes),
        grid=(steps,),
        in_specs=in_specs + c_in,
        out_specs=tuple(out_specs + c_out),
        scratch_shapes=scratch,
        compiler_params=_params(("arbitrary",)),
        name="attn_in_rope" if rope is not None else "attn_in",
    )(*args, *[arr for arr, _ in casts])
    return outs[:3], list(outs[3:])


def _kv_head(ref, kv):
    if ref.dtype == BF16:
        return ref[0, :, kv * HEAD_DIM:(kv + 1) * HEAD_DIM]
    if len(ref.shape) == 5:
        return ref[0, 0, :, kv, :].astype(BF16)
    tokens = ref.shape[1] // N_KV_HEADS
    return ref[0, pl.ds(kv, tokens, stride=N_KV_HEADS), :].astype(BF16)


def _attn_kernel(q_ref, k_ref, v_ref, *rest, cached):
    if cached:
        ck_ref, cv_ref, reorder_ref, o_ref, ck_scr, cv_scr = rest

        @pl.when(pl.program_id(1) == 0)
        def _():
            for kv in range(N_KV_HEADS):
                ck_scr[kv] = jnp.dot(_kv_head(ck_ref, kv), reorder_ref[...],
                                     preferred_element_type=F32).astype(BF16)
                cv_scr[kv] = _kv_head(cv_ref, kv)
    else:
        (o_ref,) = rest
    nt = (((1,), (1,)), ((), ()))
    ks = [_kv_head(k_ref, kv) for kv in range(N_KV_HEADS)]
    vs = [_kv_head(v_ref, kv) for kv in range(N_KV_HEADS)]

    def scores(head):
        kv = head // GROUP
        q = q_ref[0, :, head * HEAD_DIM:(head + 1) * HEAD_DIM]
        s = lax.dot_general(q, ks[kv], nt, preferred_element_type=F32)
        sc = lax.dot_general(q, ck_scr[kv], nt, preferred_element_type=F32) if cached else None
        return s, sc

    nxt = scores(0)
    for head in range(N_HEADS):
        kv = head // GROUP
        s, sc = nxt
        if head + 1 < N_HEADS:
            nxt = scores(head + 1)
        m = jnp.max(s, axis=-1, keepdims=True)
        if cached:
            m = jnp.maximum(m, jnp.max(sc, axis=-1, keepdims=True))
        p = jnp.exp2(s - m)
        l = jnp.sum(p, axis=-1, keepdims=True)
        o = jnp.dot(p.astype(BF16), vs[kv], preferred_element_type=F32)
        if cached:
            pc = jnp.exp2(sc - m)
            l = l + jnp.sum(pc, axis=-1, keepdims=True)
            o = o + jnp.dot(pc.astype(BF16), cv_scr[kv], preferred_element_type=F32)
        o_ref[0, :, head * HEAD_DIM:(head + 1) * HEAD_DIM] = (o / l).astype(o_ref.dtype)


def _attention(q, k, v, cache=None, casts=()):
    b, n, width = q.shape
    tq = min(ATTN_Q_TILE, n)
    assert not casts or n == tq
    in_specs = [pl.BlockSpec((1, tq, width), lambda bi, qi: (bi, qi, 0))]
    in_specs += [pl.BlockSpec((1,) + k.shape[1:], lambda bi, qi: (bi, 0, 0))] * 2
    args = [q, k, v]
    scratch = []
    if cache is not None:
        in_specs += [pl.BlockSpec((1,) + cache[0].shape[1:], lambda bi, qi: (bi, 0, 0, 0, 0))] * 2
        in_specs.append(pl.BlockSpec((HEAD_DIM, HEAD_DIM), lambda bi, qi: (0, 0)))
        args += list(cache)
        scratch = [pltpu.VMEM((N_KV_HEADS, cache[0].shape[2], HEAD_DIM), BF16)] * 2
    c_in, c_out, c_shapes = _cast_plumbing(casts, b)
    body = functools.partial(_attn_kernel, cached=cache is not None)
    outs = pl.pallas_call(
        _with_casts(body, len(args), 1, len(casts)),
        out_shape=tuple([jax.ShapeDtypeStruct((b, n, width), BF16)] + c_shapes),
        grid=(b, n // tq),
        in_specs=in_specs + c_in,
        out_specs=tuple([pl.BlockSpec((1, tq, width), lambda bi, qi: (bi, qi, 0))] + c_out),
        scratch_shapes=scratch,
        compiler_params=_params(("arbitrary", "arbitrary")),
        name="attention_cached" if cache is not None else "attention",
    )(*args, *[arr for arr, _ in casts])
    return outs[0], list(outs[1:])


def _post_kernel(a_ref, x_ref, mod_ref, g_ref, wo_ref, w1_ref, w2_ref, *rest, final):
    if final:
        gf_ref, o_ref = rest
    else:
        (o_ref,) = rest
    gate1 = mod_ref[0, :, 2 * D_MODEL:3 * D_MODEL]
    gate2 = mod_ref[0, :, 5 * D_MODEL:6 * D_MODEL]
    ff_tile = D_FF // 4

    def attn_residual(t):
        rows = slice(t * POST_SUB_TILE, (t + 1) * POST_SUB_TILE)
        x = x_ref[rows, :] + gate1 * jnp.dot(a_ref[rows, :], wo_ref[...], preferred_element_type=F32)
        return x, _norm_modulate(x, g_ref[...], mod_ref, 1).astype(BF16)

    n_sub = x_ref.shape[0] // POST_SUB_TILE
    nxt = attn_residual(0)
    for t in range(n_sub):
        x, h = nxt
        if t + 1 < n_sub:
            nxt = attn_residual(t + 1)
        y = None
        for f in range(D_FF // ff_tile):
            u = jnp.dot(h, w1_ref[:, f * ff_tile:(f + 1) * ff_tile], preferred_element_type=F32)
            u = jnp.square(jnp.maximum(u, 0.0)).astype(BF16)
            part = jnp.dot(u, w2_ref[f * ff_tile:(f + 1) * ff_tile, :], preferred_element_type=F32)
            y = part if y is None else y + part
        x = x + gate2 * y
        if final:
            x = x * lax.rsqrt(jnp.mean(x * x, axis=-1, keepdims=True) + EPS) * gf_ref[...]
        o_ref[t * POST_SUB_TILE:(t + 1) * POST_SUB_TILE, :] = x


def _post(a, x, mod, mod_rows, g, wo, w1, w2, gf=None, casts=()):
    rows = x.shape[0]
    row_block = pl.BlockSpec((POST_ROW_TILE, D_MODEL), lambda i: (i, 0))
    in_specs = [
        row_block,
        row_block,
        _mod_spec(mod_rows, POST_ROW_TILE),
        _const_spec((1, D_MODEL)),
        _const_spec((D_MODEL, D_MODEL)),
        _const_spec((D_MODEL, D_FF)),
        _const_spec((D_FF, D_MODEL)),
    ]
    args = [a, x, mod, g, wo, w1, w2]
    if gf is not None:
        in_specs.append(_const_spec((1, D_MODEL)))
        args.append(gf)
    steps = rows // POST_ROW_TILE
    c_in, c_out, c_shapes = _cast_plumbing(casts, steps)
    body = functools.partial(_post_kernel, final=gf is not None)
    outs = pl.pallas_call(
        _with_casts(body, len(args), 1, len(casts)),
        out_shape=tuple([jax.ShapeDtypeStruct((rows, D_MODEL), F32)] + c_shapes),
        grid=(steps,),
        in_specs=in_specs + c_in,
        out_specs=tuple([row_block] + c_out),
        compiler_params=_params(("parallel",)),
        name="post_final" if gf is not None else "post",
    )(*args, *[arr for arr, _ in casts])
    return outs[0], list(outs[1:])


def _ret_in_kernel(x_ref, mod_ref, g_ref, w_ref, *rest, use_rope):
    if use_rope:
        cos_ref, sin_ref, q_ref, k_ref, v_ref, gate_ref = rest
    else:
        q_ref, k_ref, v_ref, gate_ref = rest
    h = _norm_modulate(x_ref[...], g_ref[...], mod_ref, 0).astype(BF16)
    width = RET_HEADS * RET_DK
    for part, ref in enumerate((q_ref, k_ref, v_ref, gate_ref)):
        t = jnp.dot(h, w_ref[:, part * width:(part + 1) * width], preferred_element_type=F32)
        if part < 2 and use_rope:
            for c in range(width // V7X_LANES):
                half = (c % (RET_DK // V7X_LANES)) * V7X_LANES
                tc = t[:, c * V7X_LANES:(c + 1) * V7X_LANES]
                tc = (tc * cos_ref[:, half:half + V7X_LANES]
                      + _swap_halves(tc) * sin_ref[:, half:half + V7X_LANES])
                if part == 0:
                    tc = tc * RET_DK ** -0.5
                ref[:, c * V7X_LANES:(c + 1) * V7X_LANES] = tc.astype(ref.dtype)
        else:
            if part == 0:
                t = t * RET_DK ** -0.5
            ref[...] = t.astype(ref.dtype)


def _ret_in(x, mod, mod_rows, g, w, rope, casts=()):
    rows = x.shape[0]
    width = RET_HEADS * RET_DK
    in_specs = [
        pl.BlockSpec((ROW_TILE, D_MODEL), lambda i: (i, 0)),
        _mod_spec(mod_rows, ROW_TILE),
        _const_spec((1, D_MODEL)),
        _const_spec((D_MODEL, RET_W)),
    ]
    args = [x, mod, g, w]
    if rope is not None:
        seq_tiles = rope[0].shape[0] // ROW_TILE
        in_specs += [pl.BlockSpec((ROW_TILE, RET_DK), lambda i: (i % seq_tiles, 0))] * 2
        args += list(rope)
    out_block = pl.BlockSpec((ROW_TILE, width), lambda i: (i, 0))
    steps = rows // ROW_TILE
    c_in, c_out, c_shapes = _cast_plumbing(casts, steps)
    body = functools.partial(_ret_in_kernel, use_rope=rope is not None)
    outs = pl.pallas_call(
        _with_casts(body, len(args), 4, len(casts)),
        out_shape=tuple([jax.ShapeDtypeStruct((rows, width), BF16)] * 3
                        + [jax.ShapeDtypeStruct((rows, width), F32)] + c_shapes),
        grid=(steps,),
        in_specs=in_specs + c_in,
        out_specs=tuple([out_block] * 4 + c_out),
        compiler_params=_params(("parallel",)),
        name="ret_in_rope" if rope is not None else "ret_in",
    )(*args, *[arr for arr, _ in casts])
    return outs[:4], list(outs[4:])


def _log_sigmoid(x):
    return jnp.minimum(x, 0.0) - jnp.log(1.0 + jnp.exp(-jnp.abs(x)))


def _ret_kernel(q_ref, k_ref, v_ref, gate_ref, lg_ref, gn_ref, *rest, n, tq, heads, has_state):
    if has_state:
        s0_ref, y_ref, d_ref = rest
    else:
        y_ref, st_ref, d_ref = rest
    b = pl.program_id(1)
    qi = pl.program_id(2)
    r0 = qi * tq
    nt = (((1,), (1,)), ((), ()))
    sub = min(tq, RET_SUB_TILE)
    lgs = [(_log_sigmoid(lg_ref[hh, 0, 0:1, 0:1]), _log_sigmoid(lg_ref[hh, 1, 0:1, 0:1]))
           for hh in range(heads)]

    @pl.when(b == 0)
    def _():
        ii = r0 + lax.broadcasted_iota(jnp.int32, (tq, n), 0)
        jj = lax.broadcasted_iota(jnp.int32, (tq, n), 1)
        diff = (ii - jj).astype(F32)
        for hh in range(heads):
            lg_f, lg_b = lgs[hh]
            d_ref[hh, qi] = jnp.exp(jnp.where(diff >= 0.0, lg_f, -lg_b) * diff)

    def head_cols(hh):
        return slice(hh * RET_DK, (hh + 1) * RET_DK)

    def scores(unit):
        hh, si = unit
        return lax.dot_general(q_ref[0, si * sub:(si + 1) * sub, head_cols(hh)], k_ref[0, :, head_cols(hh)],
                               nt, preferred_element_type=F32)

    units = [(hh, si) for hh in range(heads) for si in range(tq // sub)]
    nxt = scores(units[0])
    for ui, (hh, si) in enumerate(units):
        s = nxt
        if ui + 1 < len(units):
            nxt = scores(units[ui + 1])
        lg_f, lg_b = lgs[hh]
        cols = head_cols(hh)
        rows = slice(si * sub, (si + 1) * sub)
        v = v_ref[0, :, cols]
        p = (s * d_ref[hh, qi, rows, :]).astype(BF16)
        o = jnp.dot(p, v, preferred_element_type=F32)
        if has_state:
            q = q_ref[0, rows, cols]
            i_col = (r0 + si * sub + lax.broadcasted_iota(jnp.int32, (sub, 1), 0)).astype(F32)
            o = o + (jnp.dot(q, s0_ref[0, 0, 0, hh].astype(BF16), preferred_element_type=F32)
                     * jnp.exp(lg_f * (i_col + 1.0))
                     + jnp.dot(q, s0_ref[0, 0, 1, hh].astype(BF16), preferred_element_type=F32)
                     * jnp.exp(lg_b * (n - i_col)))
        elif si == 0:
            eye = jnp.where(lax.broadcasted_iota(jnp.int32, (RET_DK, RET_DK), 0)
                            == lax.broadcasted_iota(jnp.int32, (RET_DK, RET_DK), 1), 1.0, 0.0).astype(BF16)
            kt = lax.dot_general(eye, k_ref[0, :, cols], nt, preferred_element_type=F32)
            j_row = lax.broadcasted_iota(jnp.int32, (1, n), 1).astype(F32)
            st_ref[0, 0, 0, hh] = jnp.dot((kt * jnp.exp(lg_f * (n - 1.0 - j_row))).astype(BF16), v,
                                          preferred_element_type=F32)
            st_ref[0, 0, 1, hh] = jnp.dot((kt * jnp.exp(lg_b * j_row)).astype(BF16), v,
                                          preferred_element_type=F32)
        mu = jnp.mean(o, axis=-1, keepdims=True)
        oc = o - mu
        var = jnp.mean(oc * oc, axis=-1, keepdims=True)
        on = oc * lax.rsqrt(var + EPS) * gn_ref[:, cols]
        gt = gate_ref[0, rows, cols]
        y_ref[0, rows, cols] = (gt * _sigmoid(gt) * on).astype(y_ref.dtype)


def _retention(q, k, v, gate, lg, gn, state0, heads, tq):
    b, n, _ = q.shape
    assert n % tq == 0 and RET_HEADS % heads == 0 and (state0 is not None or tq == n)
    width = heads * RET_DK
    q_tile = pl.BlockSpec((1, tq, width), lambda hg, bi, qi: (bi, qi, hg))
    seq = pl.BlockSpec((1, n, width), lambda hg, bi, qi: (bi, 0, hg))
    st = pl.BlockSpec((1, 1, 2, heads, RET_DK, RET_DV), lambda hg, bi, qi: (bi, 0, 0, hg, 0, 0))
    in_specs = [q_tile, seq, seq, q_tile,
                pl.BlockSpec((heads, 2, V7X_SUBLANES, V7X_LANES), lambda hg, bi, qi: (hg, 0, 0, 0)),
                pl.BlockSpec((1, width), lambda hg, bi, qi: (0, hg))]
    args = [q, k, v, gate, lg, gn]
    y_shape = jax.ShapeDtypeStruct((b, n, RET_HEADS * RET_DV), BF16)
    if state0 is not None:
        in_specs.append(st)
        args.append(state0)
        out_shape, out_specs = y_shape, q_tile
    else:
        out_shape = (y_shape, jax.ShapeDtypeStruct((b, 1, 2, RET_HEADS, RET_DK, RET_DV), F32))
        out_specs = (q_tile, st)
    return pl.pallas_call(
        functools.partial(_ret_kernel, n=n, tq=tq, heads=heads, has_state=state0 is not None),
        out_shape=out_shape,
        grid=(RET_HEADS // heads, b, n // tq),
        in_specs=in_specs,
        out_specs=out_specs,
        scratch_shapes=[pltpu.VMEM((heads, n // tq, tq, n), F32)],
        compiler_params=_params(("arbitrary", "arbitrary", "arbitrary")),
        name="retention_state" if state0 is not None else "retention",
    )(*args)


def kernel(x_prompt, x_sample, cache_k, cache_v, state_ret, c, c_ctx, w_mod, b_mod, norm_g,
           attn_w_qkv, attn_q_norm, attn_k_norm, attn_w_o, ret_w_qkvg, ret_decay_logit, ret_gn_w,
           ret_w_o, mlp_w1, mlp_w2, final_norm_g):
    bp, sp, d = x_prompt.shape
    bs, ss, _ = x_sample.shape
    depth = w_mod.shape[0]
    assert d == D_MODEL and depth == 2 and 1 + bs <= MOD_ROWS
    assert (bp * sp) % POST_ROW_TILE == 0 and ss % POST_ROW_TILE == 0 and POST_ROW_TILE % ROW_TILE == 0

    cond = jnp.concatenate([c_ctx[None, :], c, jnp.zeros((MOD_ROWS - 1 - bs, d), F32)], axis=0)
    mod = _modulation(cond, w_mod, b_mod)
    mod = mod.reshape(depth, MOD_ROWS, 1, 6 * d)
    ctx_mod = (0, 0)
    lat_mod = (ss, 1)

    xp = x_prompt.reshape(bp * sp, d)
    xs = x_sample.reshape(bs * ss, d)
    nkv = N_KV_HEADS * HEAD_DIM

    g0 = norm_g[0, 0][None, :]
    g1 = norm_g[0, 1][None, :]
    wqkv = attn_w_qkv[0].astype(BF16)
    qn = attn_q_norm[0][None, :]
    kn = attn_k_norm[0][None, :]

    (qp, kp, vp), (w1,) = _attn_in(xp, mod[0], ctx_mod, g0, wqkv, qn, kn, None, casts=[(mlp_w1, 0)])
    reorder = np.zeros((HEAD_DIM, HEAD_DIM), np.float32)
    reorder[_PAIRED_ORDER, np.arange(HEAD_DIM)] = 1.0
    reorder = jnp.asarray(reorder, BF16)
    (qs, ks, vs), (w2, wo) = _attn_in(xs, mod[0], lat_mod, g0, wqkv, qn[:, _PAIRED_ORDER], kn[:, _PAIRED_ORDER],
                                      _rope_tables(ss, HEAD_DIM, True) + (reorder,),
                                      casts=[(mlp_w2, 0), (attn_w_o, 0)])

    ap, (wr,) = _attention(qp.reshape(bp, sp, -1), kp.reshape(bp, sp * N_KV_HEADS, HEAD_DIM),
                           vp.reshape(bp, sp * N_KV_HEADS, HEAD_DIM), casts=[(ret_w_qkvg, 0)])
    xp, _ = _post(ap.reshape(bp * sp, -1), xp, mod[0], ctx_mod, g1, wo, w1, w2)
    cache = (cache_k, cache_v, reorder)
    as_, _ = _attention(qs.reshape(bs, ss, -1), ks.reshape(bs, ss, nkv), vs.reshape(bs, ss, nkv), cache)
    xs, _ = _post(as_.reshape(bs * ss, -1), xs, mod[0], lat_mod, g1, wo, w1, w2)

    g0 = norm_g[1, 0][None, :]
    g1 = norm_g[1, 1][None, :]
    gn = ret_gn_w[0][None, :]
    gf = final_norm_g[None, :]
    lg = jnp.broadcast_to(ret_decay_logit[0].T[:, :, None, None],
                          (RET_HEADS, 2, V7X_SUBLANES, V7X_LANES))
    hw = RET_HEADS * RET_DK

    (q, k, v, gate), (w1,) = _ret_in(xp, mod[1], ctx_mod, g0, wr, None, casts=[(mlp_w1, 1)])
    yp, new_state = _retention(q.reshape(bp, sp, hw), k.reshape(bp, sp, hw), v.reshape(bp, sp, hw),
                               gate.reshape(bp, sp, hw), lg, gn, None, RET_HEADS, sp)
    (q, k, v, gate), (w2, wo) = _ret_in(xs, mod[1], lat_mod, g0, wr, _rope_tables(ss, RET_DK, False),
                                        casts=[(mlp_w2, 1), (ret_w_o, 0)])
    ys = _retention(q.reshape(bs, ss, hw), k.reshape(bs, ss, hw), v.reshape(bs, ss, hw),
                    gate.reshape(bs, ss, hw), lg, gn, state_ret, RET_STATE_HEADS, RET_Q_TILE)
    y_prompt, _ = _post(yp.reshape(bp * sp, hw), xp, mod[1], ctx_mod, g1, wo, w1, w2, gf)
    y_sample, _ = _post(ys.reshape(bs * ss, hw), xs, mod[1], lat_mod, g1, wo, w1, w2, gf)

    return (y_prompt.reshape(bp, sp, d),
            y_sample.reshape(bs, ss, d),
            kp.reshape(bp, 1, sp, N_KV_HEADS, HEAD_DIM),
            vp.reshape(bp, 1, sp, N_KV_HEADS, HEAD_DIM),
            new_state)
```

```python
import functools

import numpy as np
import jax
import jax.numpy as jnp
from jax import lax
from jax.experimental import pallas as pl
from jax.experimental.pallas import tpu as pltpu

F32 = jnp.float32
BF16 = jnp.bfloat16

D_MODEL = 1024
GRID_W = 64
N_HEADS = 8
N_KV_HEADS = 2
HEAD_DIM = 128
GROUP = N_HEADS // N_KV_HEADS
ROPE_BASE = 10000.0
RET_HEADS = 4
RET_DK = 256
RET_DV = 256
D_FF = 4 * D_MODEL
Q_SCALE = HEAD_DIM ** -0.5 * float(np.log2(np.e))
EPS = 1e-6
QKV_W = (N_HEADS + 2 * N_KV_HEADS) * HEAD_DIM
RET_W = 2 * RET_HEADS * RET_DK + 2 * RET_HEADS * RET_DV

V7X_LANES = 128
V7X_SUBLANES = 8
MOD_ROWS = 16
ROW_TILE = 1024
PROJ_SUB_TILE = 512
POST_ROW_TILE = 1024
POST_SUB_TILE = 512
ATTN_Q_TILE = 512
RET_Q_TILE = 1024
RET_SUB_TILE = 256
RET_STATE_HEADS = 2
VMEM_LIMIT = 56 * 1024 * 1024


def _params(sem, vmem=VMEM_LIMIT):
    return pltpu.CompilerParams(dimension_semantics=sem, vmem_limit_bytes=vmem)


def _const_spec(shape):
    nd = len(shape)
    return pl.BlockSpec(shape, lambda *_: (0,) * nd, pipeline_mode=pl.Buffered(1))


def _sigmoid(x):
    return 1.0 / (1.0 + jnp.exp(-x))


def _cast_plumbing(casts, steps):
    in_specs, out_specs, out_shapes = [], [], []
    for arr, layer in casts:
        _, r, c = arr.shape
        slab = r // steps
        assert slab * steps == r and slab % 16 == 0
        in_specs.append(pl.BlockSpec((1, slab, c), functools.partial(lambda *idx, l: (l, idx[0], 0), l=layer)))
        out_specs.append(pl.BlockSpec((slab, c), lambda *idx: (idx[0], 0)))
        out_shapes.append(jax.ShapeDtypeStruct((r, c), BF16))
    return in_specs, out_specs, out_shapes


def _with_casts(body, n_in, n_out, n_cast):
    def kernel(*refs):
        ins = refs[:n_in]
        cast_in = refs[n_in:n_in + n_cast]
        outs = refs[n_in + n_cast:n_in + n_cast + n_out]
        cast_out = refs[n_in + n_cast + n_out:n_in + 2 * n_cast + n_out]
        scratch = refs[n_in + 2 * n_cast + n_out:]
        for src, dst in zip(cast_in, cast_out):
            dst[...] = src[0].astype(dst.dtype)
        body(*ins, *outs, *scratch)
    return kernel


def _mod_kernel(cond_ref, w_ref, b_ref, o_ref):
    s = cond_ref[...]
    s = (s * _sigmoid(s)).astype(BF16)
    o_ref[0] = jnp.dot(s, w_ref[0].astype(BF16), preferred_element_type=F32) + b_ref[0]


def _modulation(cond, w_mod, b_mod):
    depth = w_mod.shape[0]
    tn = 1024
    return pl.pallas_call(
        _mod_kernel,
        out_shape=jax.ShapeDtypeStruct((depth, MOD_ROWS, 6 * D_MODEL), F32),
        grid=(depth, 6 * D_MODEL // tn),
        in_specs=[
            pl.BlockSpec((MOD_ROWS, D_MODEL), lambda i, j: (0, 0)),
            pl.BlockSpec((1, D_MODEL, tn), lambda i, j: (i, 0, j)),
            pl.BlockSpec((1, 1, tn), lambda i, j: (i, 0, j)),
        ],
        out_specs=pl.BlockSpec((1, MOD_ROWS, tn), lambda i, j: (i, 0, j)),
        compiler_params=_params(("parallel", "parallel")),
        name="modulation",
    )(cond, w_mod, b_mod.reshape(depth, 1, 6 * D_MODEL))


def _norm_modulate(x, g, mod_ref, which):
    y = x * lax.rsqrt(jnp.mean(x * x, axis=-1, keepdims=True) + EPS) * g
    shift = mod_ref[0, :, (3 * which) * D_MODEL:(3 * which + 1) * D_MODEL]
    scale = mod_ref[0, :, (3 * which + 1) * D_MODEL:(3 * which + 2) * D_MODEL]
    return y * (1.0 + scale) + shift


def _mod_spec(mod_rows, tile):
    rows_per_mod_row, first_row = mod_rows
    if rows_per_mod_row:
        assert rows_per_mod_row % tile == 0
        tiles = rows_per_mod_row // tile
        return pl.BlockSpec((1, 1, 6 * D_MODEL), lambda i: (first_row + i // tiles, 0, 0))
    return pl.BlockSpec((1, 1, 6 * D_MODEL), lambda i: (first_row, 0, 0))


def _rope_tables(n, dim, paired):
    quarter = dim // 4
    rows = np.repeat(np.arange(n // GRID_W), GRID_W).astype(np.float64)
    cols = np.tile(np.arange(GRID_W), n // GRID_W).astype(np.float64)
    freqs = ROPE_BASE ** (-np.arange(quarter, dtype=np.float64) / quarter)
    ar = rows[:, None] * freqs
    ac = cols[:, None] * freqs
    if paired:
        cos = np.concatenate([np.cos(ar), np.cos(ac), np.cos(ar), np.cos(ac)], axis=1)
        sin = np.concatenate([-np.sin(ar), -np.sin(ac), np.sin(ar), np.sin(ac)], axis=1)
    else:
        cos = np.concatenate([np.cos(ar), np.cos(ar), np.cos(ac), np.cos(ac)], axis=1)
        sin = np.concatenate([-np.sin(ar), np.sin(ar), -np.sin(ac), np.sin(ac)], axis=1)
    return jnp.asarray(cos, F32), jnp.asarray(sin, F32)


_Q = HEAD_DIM // 4
_PAIRED_ORDER = np.concatenate([np.arange(0, _Q), np.arange(2 * _Q, 3 * _Q),
                                np.arange(_Q, 2 * _Q), np.arange(3 * _Q, 4 * _Q)])


def _swap_halves(t):
    return pltpu.roll(t, V7X_LANES // 2, axis=1)


def _attn_in_kernel(x_ref, mod_ref, g_ref, w_ref, qn_ref, kn_ref, *rest, use_rope):
    nq = N_HEADS * HEAD_DIM
    nk = N_KV_HEADS * HEAD_DIM
    if use_rope:
        cos_ref, sin_ref, reorder_ref, q_ref, k_ref, v_ref, wqk_ref = rest

        @pl.when(pl.program_id(0) == 0)
        def _():
            for head in range(N_HEADS + N_KV_HEADS):
                cols = slice(head * HEAD_DIM, (head + 1) * HEAD_DIM)
                wqk_ref[:, cols] = jnp.dot(w_ref[:, cols], reorder_ref[...],
                                           preferred_element_type=F32).astype(BF16)
    else:
        q_ref, k_ref, v_ref = rest
        wqk_ref = w_ref
    tm = PROJ_SUB_TILE
    n_sub = x_ref.shape[0] // tm
    pair_w = 2 * HEAD_DIM
    n_pairs = (N_HEADS + N_KV_HEADS) // 2
    qw = qn_ref[...] * Q_SCALE
    kw = kn_ref[...]
    if use_rope:
        ra = lax.broadcasted_iota(jnp.int32, (pair_w, pair_w), 0) // HEAD_DIM
        rb = lax.broadcasted_iota(jnp.int32, (pair_w, pair_w), 1) // HEAD_DIM
        head_sum = jnp.where(ra == rb, 1.0, 0.0).astype(BF16)

    def normed(sub):
        return _norm_modulate(x_ref[sub * tm:(sub + 1) * tm, :], g_ref[...], mod_ref, 0).astype(BF16)

    def put_kv(ref, sub, kh, val):
        if use_rope:
            ref[sub * tm:(sub + 1) * tm, kh * HEAD_DIM:(kh + 1) * HEAD_DIM] = val.astype(ref.dtype)
        else:
            ref[pl.ds(sub * tm * N_KV_HEADS + kh, tm, stride=N_KV_HEADS), :] = val.astype(ref.dtype)

    h_next = normed(0)
    for sub in range(n_sub):
        h = h_next
        if sub + 1 < n_sub:
            h_next = normed(sub + 1)
        rows = slice(sub * tm, (sub + 1) * tm)
        if use_rope:
            q_tabs = (qw * cos_ref[rows, :], _swap_halves(qw) * sin_ref[rows, :])
            k_tabs = (kw * cos_ref[rows, :], _swap_halves(kw) * sin_ref[rows, :])
        t2_next = jnp.dot(h, wqk_ref[:, 0:pair_w], preferred_element_type=F32)
        for pair in range(n_pairs):
            t2 = t2_next
            if pair + 1 < n_pairs:
                t2_next = jnp.dot(h, wqk_ref[:, (pair + 1) * pair_w:(pair + 2) * pair_w],
                                  preferred_element_type=F32)
            else:
                t2_next = jnp.dot(h, w_ref[:, nq + nk:], preferred_element_type=F32)
            if use_rope:
                ss2 = jnp.dot((t2 * t2).astype(BF16), head_sum, preferred_element_type=F32)
                r2 = lax.rsqrt(ss2 * (1.0 / HEAD_DIM) + EPS)
            for j in range(2):
                head = 2 * pair + j
                t = t2[:, j * HEAD_DIM:(j + 1) * HEAD_DIM]
                if use_rope:
                    r = r2[:, j * HEAD_DIM:(j + 1) * HEAD_DIM]
                    wc, ws = q_tabs if head < N_HEADS else k_tabs
                    t = (t * wc + _swap_halves(t) * ws) * r
                else:
                    r = lax.rsqrt(jnp.mean(t * t, axis=-1, keepdims=True) + EPS)
                    t = t * r * (qw if head < N_HEADS else kw)
                if head < N_HEADS:
                    q_ref[rows, head * HEAD_DIM:(head + 1) * HEAD_DIM] = t.astype(q_ref.dtype)
                else:
                    put_kv(k_ref, sub, head - N_HEADS, t)
        vv = t2_next
        for kh in range(N_KV_HEADS):
            put_kv(v_ref, sub, kh, vv[:, kh * HEAD_DIM:(kh + 1) * HEAD_DIM])


def _attn_in(x, mod, mod_rows, g, w, qn, kn, rope, casts=()):
    rows = x.shape[0]
    nk = N_KV_HEADS * HEAD_DIM
    if rope is None:
        kv_shape = jax.ShapeDtypeStruct((rows * N_KV_HEADS, HEAD_DIM), F32)
        kv_spec = pl.BlockSpec((ROW_TILE * N_KV_HEADS, HEAD_DIM), lambda i: (i, 0))
    else:
        kv_shape = jax.ShapeDtypeStruct((rows, nk), BF16)
        kv_spec = pl.BlockSpec((ROW_TILE, nk), lambda i: (i, 0))
    in_specs = [
        pl.BlockSpec((ROW_TILE, D_MODEL), lambda i: (i, 0)),
        _mod_spec(mod_rows, ROW_TILE),
        _const_spec((1, D_MODEL)),
        _const_spec((D_MODEL, QKV_W)),
        _const_spec((1, HEAD_DIM)),
        _const_spec((1, HEAD_DIM)),
    ]
    args = [x, mod, g, w, qn, kn]
    scratch = []
    if rope is not None:
        cos, sin, reorder = rope
        seq_tiles = cos.shape[0] // ROW_TILE
        in_specs += [pl.BlockSpec((ROW_TILE, HEAD_DIM), lambda i: (i % seq_tiles, 0))] * 2
        in_specs.append(_const_spec((HEAD_DIM, HEAD_DIM)))
        args += [cos, sin, reorder]
        scratch = [pltpu.VMEM((D_MODEL, (N_HEADS + N_KV_HEADS) * HEAD_DIM), BF16)]
    steps = rows // ROW_TILE
    c_in, c_out, c_shapes = _cast_plumbing(casts, steps)
    out_specs = [pl.BlockSpec((ROW_TILE, N_HEADS * HEAD_DIM), lambda i: (i, 0)), kv_spec, kv_spec]
    out_shapes = [jax.ShapeDtypeStruct((rows, N_HEADS * HEAD_DIM), BF16), kv_shape, kv_shape]
    body = functools.partial(_attn_in_kernel, use_rope=rope is not None)
    outs = pl.pallas_call(
        _with_casts(body, len(args), len(out_shapes), len(casts)),
        out_shape=tuple(out_shapes + c_shapes),
        grid=(steps,),
        in_specs=in_specs + c_in,
        out_specs=tuple(out_specs + c_out),
        scratch_shapes=scratch,
        compiler_params=_params(("arbitrary",)),
        name="attn_in_rope" if rope is not None else "attn_in",
    )(*args, *[arr for arr, _ in casts])
    return outs[:3], list(outs[3:])


def _kv_head(ref, kv):
    if ref.dtype == BF16:
        return ref[0, :, kv * HEAD_DIM:(kv + 1) * HEAD_DIM]
    if len(ref.shape) == 5:
        return ref[0, 0, :, kv, :].astype(BF16)
    tokens = ref.shape[1] // N_KV_HEADS
    return ref[0, pl.ds(kv, tokens, stride=N_KV_HEADS), :].astype(BF16)


def _attn_kernel(q_ref, k_ref, v_ref, *rest, cached):
    if cached:
        ck_ref, cv_ref, reorder_ref, o_ref, ck_scr, cv_scr = rest

        @pl.when(pl.program_id(1) == 0)
        def _():
            for kv in range(N_KV_HEADS):
                ck_scr[kv] = jnp.dot(_kv_head(ck_ref, kv), reorder_ref[...],
                                     preferred_element_type=F32).astype(BF16)
                cv_scr[kv] = _kv_head(cv_ref, kv)
    else:
        (o_ref,) = rest
    nt = (((1,), (1,)), ((), ()))
    ks = [_kv_head(k_ref, kv) for kv in range(N_KV_HEADS)]
    vs = [_kv_head(v_ref, kv) for kv in range(N_KV_HEADS)]

    def scores(head):
        kv = head // GROUP
        q = q_ref[0, :, head * HEAD_DIM:(head + 1) * HEAD_DIM]
        s = lax.dot_general(q, ks[kv], nt, preferred_element_type=F32)
        sc = lax.dot_general(q, ck_scr[kv], nt, preferred_element_type=F32) if cached else None
        return s, sc

    nxt = scores(0)
    for head in range(N_HEADS):
        kv = head // GROUP
        s, sc = nxt
        if head + 1 < N_HEADS:
            nxt = scores(head + 1)
        m = jnp.max(s, axis=-1, keepdims=True)
        if cached:
            m = jnp.maximum(m, jnp.max(sc, axis=-1, keepdims=True))
        p = jnp.exp2(s - m)
        l = jnp.sum(p, axis=-1, keepdims=True)
        o = jnp.dot(p.astype(BF16), vs[kv], preferred_element_type=F32)
        if cached:
            pc = jnp.exp2(sc - m)
            l = l + jnp.sum(pc, axis=-1, keepdims=True)
            o = o + jnp.dot(pc.astype(BF16), cv_scr[kv], preferred_element_type=F32)
        o_ref[0, :, head * HEAD_DIM:(head + 1) * HEAD_DIM] = (o / l).astype(o_ref.dtype)


def _attention(q, k, v, cache=None, casts=()):
    b, n, width = q.shape
    tq = min(ATTN_Q_TILE, n)
    assert not casts or n == tq
    in_specs = [pl.BlockSpec((1, tq, width), lambda bi, qi: (bi, qi, 0))]
    in_specs += [pl.BlockSpec((1,) + k.shape[1:], lambda bi, qi: (bi, 0, 0))] * 2
    args = [q, k, v]
    scratch = []
    if cache is not None:
        in_specs += [pl.BlockSpec((1,) + cache[0].shape[1:], lambda bi, qi: (bi, 0, 0, 0, 0))] * 2
        in_specs.append(pl.BlockSpec((HEAD_DIM, HEAD_DIM), lambda bi, qi: (0, 0)))
        args += list(cache)
        scratch = [pltpu.VMEM((N_KV_HEADS, cache[0].shape[2], HEAD_DIM), BF16)] * 2
    c_in, c_out, c_shapes = _cast_plumbing(casts, b)
    body = functools.partial(_attn_kernel, cached=cache is not None)
    outs = pl.pallas_call(
        _with_casts(body, len(args), 1, len(casts)),
        out_shape=tuple([jax.ShapeDtypeStruct((b, n, width), BF16)] + c_shapes),
        grid=(b, n // tq),
        in_specs=in_specs + c_in,
        out_specs=tuple([pl.BlockSpec((1, tq, width), lambda bi, qi: (bi, qi, 0))] + c_out),
        scratch_shapes=scratch,
        compiler_params=_params(("arbitrary", "arbitrary")),
        name="attention_cached" if cache is not None else "attention",
    )(*args, *[arr for arr, _ in casts])
    return outs[0], list(outs[1:])


def _post_kernel(a_ref, x_ref, mod_ref, g_ref, wo_ref, w1_ref, w2_ref, *rest, final):
    if final:
        gf_ref, o_ref = rest
    else:
        (o_ref,) = rest
    gate1 = mod_ref[0, :, 2 * D_MODEL:3 * D_MODEL]
    gate2 = mod_ref[0, :, 5 * D_MODEL:6 * D_MODEL]
    ff_tile = D_FF // 4

    def attn_residual(t):
        rows = slice(t * POST_SUB_TILE, (t + 1) * POST_SUB_TILE)
        x = x_ref[rows, :] + gate1 * jnp.dot(a_ref[rows, :], wo_ref[...], preferred_element_type=F32)
        return x, _norm_modulate(x, g_ref[...], mod_ref, 1).astype(BF16)

    n_sub = x_ref.shape[0] // POST_SUB_TILE
    nxt = attn_residual(0)
    for t in range(n_sub):
        x, h = nxt
        if t + 1 < n_sub:
            nxt = attn_residual(t + 1)
        y = None
        for f in range(D_FF // ff_tile):
            u = jnp.dot(h, w1_ref[:, f * ff_tile:(f + 1) * ff_tile], preferred_element_type=F32)
            u = jnp.square(jnp.maximum(u, 0.0)).astype(BF16)
            part = jnp.dot(u, w2_ref[f * ff_tile:(f + 1) * ff_tile, :], preferred_element_type=F32)
            y = part if y is None else y + part
        x = x + gate2 * y
        if final:
            x = x * lax.rsqrt(jnp.mean(x * x, axis=-1, keepdims=True) + EPS) * gf_ref[...]
        o_ref[t * POST_SUB_TILE:(t + 1) * POST_SUB_TILE, :] = x


def _post(a, x, mod, mod_rows, g, wo, w1, w2, gf=None, casts=()):
    rows = x.shape[0]
    row_block = pl.BlockSpec((POST_ROW_TILE, D_MODEL), lambda i: (i, 0))
    in_specs = [
        row_block,
        row_block,
        _mod_spec(mod_rows, POST_ROW_TILE),
        _const_spec((1, D_MODEL)),
        _const_spec((D_MODEL, D_MODEL)),
        _const_spec((D_MODEL, D_FF)),
        _const_spec((D_FF, D_MODEL)),
    ]
    args = [a, x, mod, g, wo, w1, w2]
    if gf is not None:
        in_specs.append(_const_spec((1, D_MODEL)))
        args.append(gf)
    steps = rows // POST_ROW_TILE
    c_in, c_out, c_shapes = _cast_plumbing(casts, steps)
    body = functools.partial(_post_kernel, final=gf is not None)
    outs = pl.pallas_call(
        _with_casts(body, len(args), 1, len(casts)),
        out_shape=tuple([jax.ShapeDtypeStruct((rows, D_MODEL), F32)] + c_shapes),
        grid=(steps,),
        in_specs=in_specs + c_in,
        out_specs=tuple([row_block] + c_out),
        compiler_params=_params(("parallel",)),
        name="post_final" if gf is not None else "post",
    )(*args, *[arr for arr, _ in casts])
    return outs[0], list(outs[1:])


def _ret_in_kernel(x_ref, mod_ref, g_ref, w_ref, *rest, use_rope):
    if use_rope:
        cos_ref, sin_ref, q_ref, k_ref, v_ref, gate_ref = rest
    else:
        q_ref, k_ref, v_ref, gate_ref = rest
    width = RET_HEADS * RET_DK
    tm = PROJ_SUB_TILE
    n_sub = x_ref.shape[0] // tm

    def normed(sub):
        return _norm_modulate(x_ref[sub * tm:(sub + 1) * tm, :], g_ref[...], mod_ref, 0).astype(BF16)

    h_next = normed(0)
    for sub in range(n_sub):
        h = h_next
        if sub + 1 < n_sub:
            h_next = normed(sub + 1)
        rows = slice(sub * tm, (sub + 1) * tm)
        for part, ref in enumerate((q_ref, k_ref, v_ref, gate_ref)):
            t = jnp.dot(h, w_ref[:, part * width:(part + 1) * width], preferred_element_type=F32)
            if part < 2 and use_rope:
                for c in range(width // V7X_LANES):
                    half = (c % (RET_DK // V7X_LANES)) * V7X_LANES
                    tc = t[:, c * V7X_LANES:(c + 1) * V7X_LANES]
                    tc = (tc * cos_ref[rows, half:half + V7X_LANES]
                          + _swap_halves(tc) * sin_ref[rows, half:half + V7X_LANES])
                    if part == 0:
                        tc = tc * RET_DK ** -0.5
                    ref[rows, c * V7X_LANES:(c + 1) * V7X_LANES] = tc.astype(ref.dtype)
            else:
                if part == 0:
                    t = t * RET_DK ** -0.5
                ref[rows, :] = t.astype(ref.dtype)


def _ret_in(x, mod, mod_rows, g, w, rope, casts=()):
    rows = x.shape[0]
    width = RET_HEADS * RET_DK
    in_specs = [
        pl.BlockSpec((ROW_TILE, D_MODEL), lambda i: (i, 0)),
        _mod_spec(mod_rows, ROW_TILE),
        _const_spec((1, D_MODEL)),
        _const_spec((D_MODEL, RET_W)),
    ]
    args = [x, mod, g, w]
    if rope is not None:
        seq_tiles = rope[0].shape[0] // ROW_TILE
        in_specs += [pl.BlockSpec((ROW_TILE, RET_DK), lambda i: (i % seq_tiles, 0))] * 2
        args += list(rope)
    out_block = pl.BlockSpec((ROW_TILE, width), lambda i: (i, 0))
    steps = rows // ROW_TILE
    c_in, c_out, c_shapes = _cast_plumbing(casts, steps)
    body = functools.partial(_ret_in_kernel, use_rope=rope is not None)
    outs = pl.pallas_call(
        _with_casts(body, len(args), 4, len(casts)),
        out_shape=tuple([jax.ShapeDtypeStruct((rows, width), BF16)] * 3
                        + [jax.ShapeDtypeStruct((rows, width), F32)] + c_shapes),
        grid=(steps,),
        in_specs=in_specs + c_in,
        out_specs=tuple([out_block] * 4 + c_out),
        compiler_params=_params(("parallel",)),
        name="ret_in_rope" if rope is not None else "ret_in",
    )(*args, *[arr for arr, _ in casts])
    return outs[:4], list(outs[4:])


def _log_sigmoid(x):
    return jnp.minimum(x, 0.0) - jnp.log(1.0 + jnp.exp(-jnp.abs(x)))


def _ret_kernel(q_ref, k_ref, v_ref, gate_ref, lg_ref, gn_ref, *rest, n, tq, heads, has_state):
    if has_state:
        s0_ref, y_ref, d_ref = rest
    else:
        y_ref, st_ref, d_ref = rest
    b = pl.program_id(1)
    qi = pl.program_id(2)
    r0 = qi * tq
    nt = (((1,), (1,)), ((), ()))
    sub = min(tq, RET_SUB_TILE)
    lgs = [(_log_sigmoid(lg_ref[hh, 0, 0:1, 0:1]), _log_sigmoid(lg_ref[hh, 1, 0:1, 0:1]))
           for hh in range(heads)]

    @pl.when(b == 0)
    def _():
        ii = r0 + lax.broadcasted_iota(jnp.int32, (tq, n), 0)
        jj = lax.broadcasted_iota(jnp.int32, (tq, n), 1)
        diff = (ii - jj).astype(F32)
        for hh in range(heads):
            lg_f, lg_b = lgs[hh]
            d_ref[hh, qi] = jnp.exp(jnp.where(diff >= 0.0, lg_f, -lg_b) * diff)

    def head_cols(hh):
        return slice(hh * RET_DK, (hh + 1) * RET_DK)

    def scores(unit):
        hh, si = unit
        return lax.dot_general(q_ref[0, si * sub:(si + 1) * sub, head_cols(hh)], k_ref[0, :, head_cols(hh)],
                               nt, preferred_element_type=F32)

    units = [(hh, si) for hh in range(heads) for si in range(tq // sub)]
    nxt = scores(units[0])
    for ui, (hh, si) in enumerate(units):
        s = nxt
        if ui + 1 < len(units):
            nxt = scores(units[ui + 1])
        lg_f, lg_b = lgs[hh]
        cols = head_cols(hh)
        rows = slice(si * sub, (si + 1) * sub)
        v = v_ref[0, :, cols]
        p = (s * d_ref[hh, qi, rows, :]).astype(BF16)
        o = jnp.dot(p, v, preferred_element_type=F32)
        if has_state:
            q = q_ref[0, rows, cols]
            i_col = (r0 + si * sub + lax.broadcasted_iota(jnp.int32, (sub, 1), 0)).astype(F32)
            o = o + (jnp.dot(q, s0_ref[0, 0, 0, hh].astype(BF16), preferred_element_type=F32)
                     * jnp.exp(lg_f * (i_col + 1.0))
                     + jnp.dot(q, s0_ref[0, 0, 1, hh].astype(BF16), preferred_element_type=F32)
                     * jnp.exp(lg_b * (n - i_col)))
        elif si == 0:
            eye = jnp.where(lax.broadcasted_iota(jnp.int32, (RET_DK, RET_DK), 0)
                            == lax.broadcasted_iota(jnp.int32, (RET_DK, RET_DK), 1), 1.0, 0.0).astype(BF16)
            kt = lax.dot_general(eye, k_ref[0, :, cols], nt, preferred_element_type=F32)
            j_row = lax.broadcasted_iota(jnp.int32, (1, n), 1).astype(F32)
            st_ref[0, 0, 0, hh] = jnp.dot((kt * jnp.exp(lg_f * (n - 1.0 - j_row))).astype(BF16), v,
                                          preferred_element_type=F32)
            st_ref[0, 0, 1, hh] = jnp.dot((kt * jnp.exp(lg_b * j_row)).astype(BF16), v,
                                          preferred_element_type=F32)
        mu = jnp.mean(o, axis=-1, keepdims=True)
        oc = o - mu
        var = jnp.mean(oc * oc, axis=-1, keepdims=True)
        on = oc * lax.rsqrt(var + EPS) * gn_ref[:, cols]
        gt = gate_ref[0, rows, cols]
        y_ref[0, rows, cols] = (gt * _sigmoid(gt) * on).astype(y_ref.dtype)


def _retention(q, k, v, gate, lg, gn, state0, heads, tq):
    b, n, _ = q.shape
    assert n % tq == 0 and RET_HEADS % heads == 0 and (state0 is not None or tq == n)
    width = heads * RET_DK
    q_tile = pl.BlockSpec((1, tq, width), lambda hg, bi, qi: (bi, qi, hg))
    seq = pl.BlockSpec((1, n, width), lambda hg, bi, qi: (bi, 0, hg))
    st = pl.BlockSpec((1, 1, 2, heads, RET_DK, RET_DV), lambda hg, bi, qi: (bi, 0, 0, hg, 0, 0))
    in_specs = [q_tile, seq, seq, q_tile,
                pl.BlockSpec((heads, 2, V7X_SUBLANES, V7X_LANES), lambda hg, bi, qi: (hg, 0, 0, 0)),
                pl.BlockSpec((1, width), lambda hg, bi, qi: (0, hg))]
    args = [q, k, v, gate, lg, gn]
    y_shape = jax.ShapeDtypeStruct((b, n, RET_HEADS * RET_DV), BF16)
    if state0 is not None:
        in_specs.append(st)
        args.append(state0)
        out_shape, out_specs = y_shape, q_tile
    else:
        out_shape = (y_shape, jax.ShapeDtypeStruct((b, 1, 2, RET_HEADS, RET_DK, RET_DV), F32))
        out_specs = (q_tile, st)
    return pl.pallas_call(
        functools.partial(_ret_kernel, n=n, tq=tq, heads=heads, has_state=state0 is not None),
        out_shape=out_shape,
        grid=(RET_HEADS // heads, b, n // tq),
        in_specs=in_specs,
        out_specs=out_specs,
        scratch_shapes=[pltpu.VMEM((heads, n // tq, tq, n), F32)],
        compiler_params=_params(("arbitrary", "arbitrary", "arbitrary")),
        name="retention_state" if state0 is not None else "retention",
    )(*args)


def kernel(x_prompt, x_sample, cache_k, cache_v, state_ret, c, c_ctx, w_mod, b_mod, norm_g,
           attn_w_qkv, attn_q_norm, attn_k_norm, attn_w_o, ret_w_qkvg, ret_decay_logit, ret_gn_w,
           ret_w_o, mlp_w1, mlp_w2, final_norm_g):
    bp, sp, d = x_prompt.shape
    bs, ss, _ = x_sample.shape
    depth = w_mod.shape[0]
    assert d == D_MODEL and depth == 2 and 1 + bs <= MOD_ROWS
    assert (bp * sp) % POST_ROW_TILE == 0 and ss % POST_ROW_TILE == 0 and POST_ROW_TILE % ROW_TILE == 0

    cond = jnp.concatenate([c_ctx[None, :], c, jnp.zeros((MOD_ROWS - 1 - bs, d), F32)], axis=0)
    mod = _modulation(cond, w_mod, b_mod)
    mod = mod.reshape(depth, MOD_ROWS, 1, 6 * d)
    ctx_mod = (0, 0)
    lat_mod = (ss, 1)

    xp = x_prompt.reshape(bp * sp, d)
    xs = x_sample.reshape(bs * ss, d)
    nkv = N_KV_HEADS * HEAD_DIM

    g0 = norm_g[0, 0][None, :]
    g1 = norm_g[0, 1][None, :]
    wqkv = attn_w_qkv[0].astype(BF16)
    qn = attn_q_norm[0][None, :]
    kn = attn_k_norm[0][None, :]

    (qp, kp, vp), (w1,) = _attn_in(xp, mod[0], ctx_mod, g0, wqkv, qn, kn, None, casts=[(mlp_w1, 0)])
    reorder = np.zeros((HEAD_DIM, HEAD_DIM), np.float32)
    reorder[_PAIRED_ORDER, np.arange(HEAD_DIM)] = 1.0
    reorder = jnp.asarray(reorder, BF16)
    (qs, ks, vs), (w2, wo) = _attn_in(xs, mod[0], lat_mod, g0, wqkv, qn[:, _PAIRED_ORDER], kn[:, _PAIRED_ORDER],
                                      _rope_tables(ss, HEAD_DIM, True) + (reorder,),
                                      casts=[(mlp_w2, 0), (attn_w_o, 0)])

    ap, (wr,) = _attention(qp.reshape(bp, sp, -1), kp.reshape(bp, sp * N_KV_HEADS, HEAD_DIM),
                           vp.reshape(bp, sp * N_KV_HEADS, HEAD_DIM), casts=[(ret_w_qkvg, 0)])
    xp, _ = _post(ap.reshape(bp * sp, -1), xp, mod[0], ctx_mod, g1, wo, w1, w2)
    cache = (cache_k, cache_v, reorder)
    as_, _ = _attention(qs.reshape(bs, ss, -1), ks.reshape(bs, ss, nkv), vs.reshape(bs, ss, nkv), cache)
    xs, _ = _post(as_.reshape(bs * ss, -1), xs, mod[0], lat_mod, g1, wo, w1, w2)

    g0 = norm_g[1, 0][None, :]
    g1 = norm_g[1, 1][None, :]
    gn = ret_gn_w[0][None, :]
    gf = final_norm_g[None, :]
    lg = jnp.broadcast_to(ret_decay_logit[0].T[:, :, None, None],
                          (RET_HEADS, 2, V7X_SUBLANES, V7X_LANES))
    hw = RET_HEADS * RET_DK

    (q, k, v, gate), (w1,) = _ret_in(xp, mod[1], ctx_mod, g0, wr, None, casts=[(mlp_w1, 1)])
    yp, new_state = _retention(q.reshape(bp, sp, hw), k.reshape(bp, sp, hw), v.reshape(bp, sp, hw),
                               gate.reshape(bp, sp, hw), lg, gn, None, RET_HEADS, sp)
    (q, k, v, gate), (w2, wo) = _ret_in(xs, mod[1], lat_mod, g0, wr, _rope_tables(ss, RET_DK, False),
                                        casts=[(mlp_w2, 1), (ret_w_o, 0)])
    ys = _retention(q.reshape(bs, ss, hw), k.reshape(bs, ss, hw), v.reshape(bs, ss, hw),
                    gate.reshape(bs, ss, hw), lg, gn, state_ret, RET_STATE_HEADS, RET_Q_TILE)
    y_prompt, _ = _post(yp.reshape(bp * sp, hw), xp, mod[1], ctx_mod, g1, wo, w1, w2, gf)
    y_sample, _ = _post(ys.reshape(bs * ss, hw), xs, mod[1], lat_mod, g1, wo, w1, w2, gf)

    return (y_prompt.reshape(bp, sp, d),
            y_sample.reshape(bs, ss, d),
            kp.reshape(bp, 1, sp, N_KV_HEADS, HEAD_DIM),
            vp.reshape(bp, 1, sp, N_KV_HEADS, HEAD_DIM),
            new_state)
```

```python
import functools

import numpy as np
import jax
import jax.numpy as jnp
from jax import lax
from jax.experimental import pallas as pl
from jax.experimental.pallas import tpu as pltpu

F32 = jnp.float32
BF16 = jnp.bfloat16

D_MODEL = 1024
GRID_W = 64
N_HEADS = 8
N_KV_HEADS = 2
HEAD_DIM = 128
GROUP = N_HEADS // N_KV_HEADS
ROPE_BASE = 10000.0
RET_HEADS = 4
RET_DK = 256
RET_DV = 256
D_FF = 4 * D_MODEL
Q_SCALE = HEAD_DIM ** -0.5 * float(np.log2(np.e))
EPS = 1e-6
QKV_W = (N_HEADS + 2 * N_KV_HEADS) * HEAD_DIM
RET_W = 2 * RET_HEADS * RET_DK + 2 * RET_HEADS * RET_DV

V7X_LANES = 128
V7X_SUBLANES = 8
MOD_ROWS = 16
ROW_TILE = 1024
PROJ_SUB_TILE = 512
POST_ROW_TILE = 1024
POST_SUB_TILE = 512
ATTN_Q_TILE = 512
RET_Q_TILE = 1024
RET_SUB_TILE = 256
RET_STATE_HEADS = 2
VMEM_LIMIT = 56 * 1024 * 1024


def _params(sem, vmem=VMEM_LIMIT):
    return pltpu.CompilerParams(dimension_semantics=sem, vmem_limit_bytes=vmem)


def _const_spec(shape):
    nd = len(shape)
    return pl.BlockSpec(shape, lambda *_: (0,) * nd, pipeline_mode=pl.Buffered(1))


def _sigmoid(x):
    return 1.0 / (1.0 + jnp.exp(-x))


def _cast_plumbing(casts, steps):
    in_specs, out_specs, out_shapes = [], [], []
    for arr, layer in casts:
        _, r, c = arr.shape
        slab = r // steps
        assert slab * steps == r and slab % 16 == 0
        in_specs.append(pl.BlockSpec((1, slab, c), functools.partial(lambda *idx, l: (l, idx[0], 0), l=layer)))
        out_specs.append(pl.BlockSpec((slab, c), lambda *idx: (idx[0], 0)))
        out_shapes.append(jax.ShapeDtypeStruct((r, c), BF16))
    return in_specs, out_specs, out_shapes


def _with_casts(body, n_in, n_out, n_cast):
    def kernel(*refs):
        ins = refs[:n_in]
        cast_in = refs[n_in:n_in + n_cast]
        outs = refs[n_in + n_cast:n_in + n_cast + n_out]
        cast_out = refs[n_in + n_cast + n_out:n_in + 2 * n_cast + n_out]
        scratch = refs[n_in + 2 * n_cast + n_out:]
        for src, dst in zip(cast_in, cast_out):
            dst[...] = src[0].astype(dst.dtype)
        body(*ins, *outs, *scratch)
    return kernel


def _mod_kernel(cond_ref, w_ref, b_ref, o_ref):
    s = cond_ref[...]
    s = (s * _sigmoid(s)).astype(BF16)
    o_ref[0] = jnp.dot(s, w_ref[0].astype(BF16), preferred_element_type=F32) + b_ref[0]


def _modulation(cond, w_mod, b_mod):
    depth = w_mod.shape[0]
    tn = 1024
    return pl.pallas_call(
        _mod_kernel,
        out_shape=jax.ShapeDtypeStruct((depth, MOD_ROWS, 6 * D_MODEL), F32),
        grid=(depth, 6 * D_MODEL // tn),
        in_specs=[
            pl.BlockSpec((MOD_ROWS, D_MODEL), lambda i, j: (0, 0)),
            pl.BlockSpec((1, D_MODEL, tn), lambda i, j: (i, 0, j)),
            pl.BlockSpec((1, 1, tn), lambda i, j: (i, 0, j)),
        ],
        out_specs=pl.BlockSpec((1, MOD_ROWS, tn), lambda i, j: (i, 0, j)),
        compiler_params=_params(("parallel", "parallel")),
        name="modulation",
    )(cond, w_mod, b_mod.reshape(depth, 1, 6 * D_MODEL))


def _norm_modulate(x, g, mod_ref, which):
    y = x * lax.rsqrt(jnp.mean(x * x, axis=-1, keepdims=True) + EPS) * g
    shift = mod_ref[0, :, (3 * which) * D_MODEL:(3 * which + 1) * D_MODEL]
    scale = mod_ref[0, :, (3 * which + 1) * D_MODEL:(3 * which + 2) * D_MODEL]
    return y * (1.0 + scale) + shift


def _mod_spec(mod_rows, tile):
    rows_per_mod_row, first_row = mod_rows
    if rows_per_mod_row:
        assert rows_per_mod_row % tile == 0
        tiles = rows_per_mod_row // tile
        return pl.BlockSpec((1, 1, 6 * D_MODEL), lambda i: (first_row + i // tiles, 0, 0))
    return pl.BlockSpec((1, 1, 6 * D_MODEL), lambda i: (first_row, 0, 0))


def _rope_tables(n, dim, paired):
    quarter = dim // 4
    rows = np.repeat(np.arange(n // GRID_W), GRID_W).astype(np.float64)
    cols = np.tile(np.arange(GRID_W), n // GRID_W).astype(np.float64)
    freqs = ROPE_BASE ** (-np.arange(quarter, dtype=np.float64) / quarter)
    ar = rows[:, None] * freqs
    ac = cols[:, None] * freqs
    if paired:
        cos = np.concatenate([np.cos(ar), np.cos(ac), np.cos(ar), np.cos(ac)], axis=1)
        sin = np.concatenate([-np.sin(ar), -np.sin(ac), np.sin(ar), np.sin(ac)], axis=1)
    else:
        cos = np.concatenate([np.cos(ar), np.cos(ar), np.cos(ac), np.cos(ac)], axis=1)
        sin = np.concatenate([-np.sin(ar), np.sin(ar), -np.sin(ac), np.sin(ac)], axis=1)
    return jnp.asarray(cos, F32), jnp.asarray(sin, F32)


_Q = HEAD_DIM // 4
_PAIRED_ORDER = np.concatenate([np.arange(0, _Q), np.arange(2 * _Q, 3 * _Q),
                                np.arange(_Q, 2 * _Q), np.arange(3 * _Q, 4 * _Q)])


def _swap_halves(t):
    return pltpu.roll(t, V7X_LANES // 2, axis=1)


def _attn_in_kernel(x_ref, mod_ref, g_ref, w_ref, qn_ref, kn_ref, *rest, use_rope):
    nq = N_HEADS * HEAD_DIM
    nk = N_KV_HEADS * HEAD_DIM
    if use_rope:
        cos_ref, sin_ref, reorder_ref, q_ref, k_ref, v_ref, wqk_ref = rest

        @pl.when(pl.program_id(0) == 0)
        def _():
            for head in range(N_HEADS + N_KV_HEADS):
                cols = slice(head * HEAD_DIM, (head + 1) * HEAD_DIM)
                wqk_ref[:, cols] = jnp.dot(w_ref[:, cols], reorder_ref[...],
                                           preferred_element_type=F32).astype(BF16)
    else:
        q_ref, k_ref, v_ref = rest
        wqk_ref = w_ref
    tm = PROJ_SUB_TILE
    n_sub = x_ref.shape[0] // tm
    pair_w = 2 * HEAD_DIM
    n_pairs = (N_HEADS + N_KV_HEADS) // 2
    qw = qn_ref[...] * Q_SCALE
    kw = kn_ref[...]
    if use_rope:
        ra = lax.broadcasted_iota(jnp.int32, (pair_w, pair_w), 0) // HEAD_DIM
        rb = lax.broadcasted_iota(jnp.int32, (pair_w, pair_w), 1) // HEAD_DIM
        head_sum = jnp.where(ra == rb, 1.0, 0.0).astype(BF16)

    def normed(sub):
        return _norm_modulate(x_ref[sub * tm:(sub + 1) * tm, :], g_ref[...], mod_ref, 0).astype(BF16)

    def put_kv(ref, sub, kh, val):
        if use_rope:
            ref[sub * tm:(sub + 1) * tm, kh * HEAD_DIM:(kh + 1) * HEAD_DIM] = val.astype(ref.dtype)
        else:
            ref[pl.ds(sub * tm * N_KV_HEADS + kh, tm, stride=N_KV_HEADS), :] = val.astype(ref.dtype)

    h_next = normed(0)
    for sub in range(n_sub):
        h = h_next
        if sub + 1 < n_sub:
            h_next = normed(sub + 1)
        rows = slice(sub * tm, (sub + 1) * tm)
        if use_rope:
            q_tabs = (qw * cos_ref[rows, :], _swap_halves(qw) * sin_ref[rows, :])
            k_tabs = (kw * cos_ref[rows, :], _swap_halves(kw) * sin_ref[rows, :])
        t2_next = jnp.dot(h, wqk_ref[:, 0:pair_w], preferred_element_type=F32)
        for pair in range(n_pairs):
            t2 = t2_next
            if pair + 1 < n_pairs:
                t2_next = jnp.dot(h, wqk_ref[:, (pair + 1) * pair_w:(pair + 2) * pair_w],
                                  preferred_element_type=F32)
            else:
                t2_next = jnp.dot(h, w_ref[:, nq + nk:], preferred_element_type=F32)
            if use_rope:
                ss2 = jnp.dot((t2 * t2).astype(BF16), head_sum, preferred_element_type=F32)
                r2 = lax.rsqrt(ss2 * (1.0 / HEAD_DIM) + EPS)
            for j in range(2):
                head = 2 * pair + j
                t = t2[:, j * HEAD_DIM:(j + 1) * HEAD_DIM]
                if use_rope:
                    r = r2[:, j * HEAD_DIM:(j + 1) * HEAD_DIM]
                    wc, ws = q_tabs if head < N_HEADS else k_tabs
                    t = (t * wc + _swap_halves(t) * ws) * r
                else:
                    r = lax.rsqrt(jnp.mean(t * t, axis=-1, keepdims=True) + EPS)
                    t = t * r * (qw if head < N_HEADS else kw)
                if head < N_HEADS:
                    q_ref[rows, head * HEAD_DIM:(head + 1) * HEAD_DIM] = t.astype(q_ref.dtype)
                else:
                    put_kv(k_ref, sub, head - N_HEADS, t)
        vv = t2_next
        for kh in range(N_KV_HEADS):
            put_kv(v_ref, sub, kh, vv[:, kh * HEAD_DIM:(kh + 1) * HEAD_DIM])


def _attn_in(x, mod, mod_rows, g, w, qn, kn, rope, tile, casts=()):
    rows = x.shape[0]
    nk = N_KV_HEADS * HEAD_DIM
    if rope is None:
        kv_shape = jax.ShapeDtypeStruct((rows * N_KV_HEADS, HEAD_DIM), F32)
        kv_spec = pl.BlockSpec((tile * N_KV_HEADS, HEAD_DIM), lambda i: (i, 0))
    else:
        kv_shape = jax.ShapeDtypeStruct((rows, nk), BF16)
        kv_spec = pl.BlockSpec((tile, nk), lambda i: (i, 0))
    in_specs = [
        pl.BlockSpec((tile, D_MODEL), lambda i: (i, 0)),
        _mod_spec(mod_rows, tile),
        _const_spec((1, D_MODEL)),
        _const_spec((D_MODEL, QKV_W)),
        _const_spec((1, HEAD_DIM)),
        _const_spec((1, HEAD_DIM)),
    ]
    args = [x, mod, g, w, qn, kn]
    scratch = []
    if rope is not None:
        cos, sin, reorder = rope
        seq_tiles = cos.shape[0] // tile
        in_specs += [pl.BlockSpec((tile, HEAD_DIM), lambda i: (i % seq_tiles, 0))] * 2
        in_specs.append(_const_spec((HEAD_DIM, HEAD_DIM)))
        args += [cos, sin, reorder]
        scratch = [pltpu.VMEM((D_MODEL, (N_HEADS + N_KV_HEADS) * HEAD_DIM), BF16)]
    steps = rows // tile
    c_in, c_out, c_shapes = _cast_plumbing(casts, steps)
    out_specs = [pl.BlockSpec((tile, N_HEADS * HEAD_DIM), lambda i: (i, 0)), kv_spec, kv_spec]
    out_shapes = [jax.ShapeDtypeStruct((rows, N_HEADS * HEAD_DIM), BF16), kv_shape, kv_shape]
    body = functools.partial(_attn_in_kernel, use_rope=rope is not None)
    outs = pl.pallas_call(
        _with_casts(body, len(args), len(out_shapes), len(casts)),
        out_shape=tuple(out_shapes + c_shapes),
        grid=(steps,),
        in_specs=in_specs + c_in,
        out_specs=tuple(out_specs + c_out),
        scratch_shapes=scratch,
        compiler_params=_params(("arbitrary",)),
        name="attn_in_rope" if rope is not None else "attn_in",
    )(*args, *[arr for arr, _ in casts])
    return outs[:3], list(outs[3:])


def _kv_head(ref, kv):
    if ref.dtype == BF16:
        return ref[0, :, kv * HEAD_DIM:(kv + 1) * HEAD_DIM]
    if len(ref.shape) == 5:
        return ref[0, 0, :, kv, :].astype(BF16)
    tokens = ref.shape[1] // N_KV_HEADS
    return ref[0, pl.ds(kv, tokens, stride=N_KV_HEADS), :].astype(BF16)


def _attn_kernel(q_ref, k_ref, v_ref, *rest, cached):
    if cached:
        ck_ref, cv_ref, reorder_ref, o_ref, ck_scr, cv_scr = rest

        @pl.when(pl.program_id(1) == 0)
        def _():
            for kv in range(N_KV_HEADS):
                ck_scr[kv] = jnp.dot(_kv_head(ck_ref, kv), reorder_ref[...],
                                     preferred_element_type=F32).astype(BF16)
                cv_scr[kv] = _kv_head(cv_ref, kv)
    else:
        (o_ref,) = rest
    nt = (((1,), (1,)), ((), ()))
    ks = [_kv_head(k_ref, kv) for kv in range(N_KV_HEADS)]
    vs = [_kv_head(v_ref, kv) for kv in range(N_KV_HEADS)]

    def scores(head):
        kv = head // GROUP
        q = q_ref[0, :, head * HEAD_DIM:(head + 1) * HEAD_DIM]
        s = lax.dot_general(q, ks[kv], nt, preferred_element_type=F32)
        sc = lax.dot_general(q, ck_scr[kv], nt, preferred_element_type=F32) if cached else None
        return s, sc

    nxt = scores(0)
    for head in range(N_HEADS):
        kv = head // GROUP
        s, sc = nxt
        if head + 1 < N_HEADS:
            nxt = scores(head + 1)
        m = jnp.max(s, axis=-1, keepdims=True)
        if cached:
            m = jnp.maximum(m, jnp.max(sc, axis=-1, keepdims=True))
        p = jnp.exp2(s - m)
        l = jnp.sum(p, axis=-1, keepdims=True)
        o = jnp.dot(p.astype(BF16), vs[kv], preferred_element_type=F32)
        if cached:
            pc = jnp.exp2(sc - m)
            l = l + jnp.sum(pc, axis=-1, keepdims=True)
            o = o + jnp.dot(pc.astype(BF16), cv_scr[kv], preferred_element_type=F32)
        o_ref[0, :, head * HEAD_DIM:(head + 1) * HEAD_DIM] = (o / l).astype(o_ref.dtype)


def _attention(q, k, v, cache=None, casts=()):
    b, n, width = q.shape
    tq = min(ATTN_Q_TILE, n)
    assert not casts or n == tq
    in_specs = [pl.BlockSpec((1, tq, width), lambda bi, qi: (bi, qi, 0))]
    in_specs += [pl.BlockSpec((1,) + k.shape[1:], lambda bi, qi: (bi, 0, 0))] * 2
    args = [q, k, v]
    scratch = []
    if cache is not None:
        in_specs += [pl.BlockSpec((1,) + cache[0].shape[1:], lambda bi, qi: (bi, 0, 0, 0, 0))] * 2
        in_specs.append(pl.BlockSpec((HEAD_DIM, HEAD_DIM), lambda bi, qi: (0, 0)))
        args += list(cache)
        scratch = [pltpu.VMEM((N_KV_HEADS, cache[0].shape[2], HEAD_DIM), BF16)] * 2
    c_in, c_out, c_shapes = _cast_plumbing(casts, b)
    body = functools.partial(_attn_kernel, cached=cache is not None)
    outs = pl.pallas_call(
        _with_casts(body, len(args), 1, len(casts)),
        out_shape=tuple([jax.ShapeDtypeStruct((b, n, width), BF16)] + c_shapes),
        grid=(b, n // tq),
        in_specs=in_specs + c_in,
        out_specs=tuple([pl.BlockSpec((1, tq, width), lambda bi, qi: (bi, qi, 0))] + c_out),
        scratch_shapes=scratch,
        compiler_params=_params(("arbitrary", "arbitrary")),
        name="attention_cached" if cache is not None else "attention",
    )(*args, *[arr for arr, _ in casts])
    return outs[0], list(outs[1:])


def _post_kernel(a_ref, x_ref, mod_ref, g_ref, wo_ref, w1_ref, w2_ref, *rest, final):
    if final:
        gf_ref, o_ref = rest
    else:
        (o_ref,) = rest
    gate1 = mod_ref[0, :, 2 * D_MODEL:3 * D_MODEL]
    gate2 = mod_ref[0, :, 5 * D_MODEL:6 * D_MODEL]
    ff_tile = D_FF // 4

    def attn_residual(t):
        rows = slice(t * POST_SUB_TILE, (t + 1) * POST_SUB_TILE)
        x = x_ref[rows, :] + gate1 * jnp.dot(a_ref[rows, :], wo_ref[...], preferred_element_type=F32)
        return x, _norm_modulate(x, g_ref[...], mod_ref, 1).astype(BF16)

    n_sub = x_ref.shape[0] // POST_SUB_TILE
    nxt = attn_residual(0)
    for t in range(n_sub):
        x, h = nxt
        if t + 1 < n_sub:
            nxt = attn_residual(t + 1)
        y = None
        for f in range(D_FF // ff_tile):
            u = jnp.dot(h, w1_ref[:, f * ff_tile:(f + 1) * ff_tile], preferred_element_type=F32)
            u = jnp.square(jnp.maximum(u, 0.0)).astype(BF16)
            part = jnp.dot(u, w2_ref[f * ff_tile:(f + 1) * ff_tile, :], preferred_element_type=F32)
            y = part if y is None else y + part
        x = x + gate2 * y
        if final:
            x = x * lax.rsqrt(jnp.mean(x * x, axis=-1, keepdims=True) + EPS) * gf_ref[...]
        o_ref[t * POST_SUB_TILE:(t + 1) * POST_SUB_TILE, :] = x


def _post(a, x, mod, mod_rows, g, wo, w1, w2, gf=None, casts=()):
    rows = x.shape[0]
    row_block = pl.BlockSpec((POST_ROW_TILE, D_MODEL), lambda i: (i, 0))
    in_specs = [
        row_block,
        row_block,
        _mod_spec(mod_rows, POST_ROW_TILE),
        _const_spec((1, D_MODEL)),
        _const_spec((D_MODEL, D_MODEL)),
        _const_spec((D_MODEL, D_FF)),
        _const_spec((D_FF, D_MODEL)),
    ]
    args = [a, x, mod, g, wo, w1, w2]
    if gf is not None:
        in_specs.append(_const_spec((1, D_MODEL)))
        args.append(gf)
    steps = rows // POST_ROW_TILE
    c_in, c_out, c_shapes = _cast_plumbing(casts, steps)
    body = functools.partial(_post_kernel, final=gf is not None)
    outs = pl.pallas_call(
        _with_casts(body, len(args), 1, len(casts)),
        out_shape=tuple([jax.ShapeDtypeStruct((rows, D_MODEL), F32)] + c_shapes),
        grid=(steps,),
        in_specs=in_specs + c_in,
        out_specs=tuple([row_block] + c_out),
        compiler_params=_params(("parallel",)),
        name="post_final" if gf is not None else "post",
    )(*args, *[arr for arr, _ in casts])
    return outs[0], list(outs[1:])


def _ret_in_kernel(x_ref, mod_ref, g_ref, w_ref, *rest, use_rope):
    if use_rope:
        cos_ref, sin_ref, q_ref, k_ref, v_ref, gate_ref = rest
    else:
        q_ref, k_ref, v_ref, gate_ref = rest
    width = RET_HEADS * RET_DK
    tm = PROJ_SUB_TILE
    n_sub = x_ref.shape[0] // tm

    def normed(sub):
        return _norm_modulate(x_ref[sub * tm:(sub + 1) * tm, :], g_ref[...], mod_ref, 0).astype(BF16)

    h_next = normed(0)
    for sub in range(n_sub):
        h = h_next
        if sub + 1 < n_sub:
            h_next = normed(sub + 1)
        rows = slice(sub * tm, (sub + 1) * tm)
        for part, ref in enumerate((q_ref, k_ref, v_ref, gate_ref)):
            t = jnp.dot(h, w_ref[:, part * width:(part + 1) * width], preferred_element_type=F32)
            if part < 2 and use_rope:
                for c in range(width // V7X_LANES):
                    half = (c % (RET_DK // V7X_LANES)) * V7X_LANES
                    tc = t[:, c * V7X_LANES:(c + 1) * V7X_LANES]
                    tc = (tc * cos_ref[rows, half:half + V7X_LANES]
                          + _swap_halves(tc) * sin_ref[rows, half:half + V7X_LANES])
                    if part == 0:
                        tc = tc * RET_DK ** -0.5
                    ref[rows, c * V7X_LANES:(c + 1) * V7X_LANES] = tc.astype(ref.dtype)
            else:
                if part == 0:
                    t = t * RET_DK ** -0.5
                ref[rows, :] = t.astype(ref.dtype)


def _ret_in(x, mod, mod_rows, g, w, rope, casts=()):
    rows = x.shape[0]
    width = RET_HEADS * RET_DK
    in_specs = [
        pl.BlockSpec((ROW_TILE, D_MODEL), lambda i: (i, 0)),
        _mod_spec(mod_rows, ROW_TILE),
        _const_spec((1, D_MODEL)),
        _const_spec((D_MODEL, RET_W)),
    ]
    args = [x, mod, g, w]
    if rope is not None:
        seq_tiles = rope[0].shape[0] // ROW_TILE
        in_specs += [pl.BlockSpec((ROW_TILE, RET_DK), lambda i: (i % seq_tiles, 0))] * 2
        args += list(rope)
    out_block = pl.BlockSpec((ROW_TILE, width), lambda i: (i, 0))
    steps = rows // ROW_TILE
    c_in, c_out, c_shapes = _cast_plumbing(casts, steps)
    body = functools.partial(_ret_in_kernel, use_rope=rope is not None)
    outs = pl.pallas_call(
        _with_casts(body, len(args), 4, len(casts)),
        out_shape=tuple([jax.ShapeDtypeStruct((rows, width), BF16)] * 3
                        + [jax.ShapeDtypeStruct((rows, width), F32)] + c_shapes),
        grid=(steps,),
        in_specs=in_specs + c_in,
        out_specs=tuple([out_block] * 4 + c_out),
        compiler_params=_params(("parallel",)),
        name="ret_in_rope" if rope is not None else "ret_in",
    )(*args, *[arr for arr, _ in casts])
    return outs[:4], list(outs[4:])


def _log_sigmoid(x):
    return jnp.minimum(x, 0.0) - jnp.log(1.0 + jnp.exp(-jnp.abs(x)))


def _ret_kernel(q_ref, k_ref, v_ref, gate_ref, lg_ref, gn_ref, *rest, n, tq, heads, has_state):
    if has_state:
        s0_ref, y_ref, d_ref = rest
    else:
        y_ref, st_ref, d_ref = rest
    b = pl.program_id(1)
    qi = pl.program_id(2)
    r0 = qi * tq
    nt = (((1,), (1,)), ((), ()))
    sub = min(tq, RET_SUB_TILE)
    lgs = [(_log_sigmoid(lg_ref[hh, 0, 0:1, 0:1]), _log_sigmoid(lg_ref[hh, 1, 0:1, 0:1]))
           for hh in range(heads)]

    @pl.when(b == 0)
    def _():
        ii = r0 + lax.broadcasted_iota(jnp.int32, (tq, n), 0)
        jj = lax.broadcasted_iota(jnp.int32, (tq, n), 1)
        diff = (ii - jj).astype(F32)
        for hh in range(heads):
            lg_f, lg_b = lgs[hh]
            d_ref[hh, qi] = jnp.exp(jnp.where(diff >= 0.0, lg_f, -lg_b) * diff)

    def head_cols(hh):
        return slice(hh * RET_DK, (hh + 1) * RET_DK)

    def scores(unit):
        hh, si = unit
        return lax.dot_general(q_ref[0, si * sub:(si + 1) * sub, head_cols(hh)], k_ref[0, :, head_cols(hh)],
                               nt, preferred_element_type=F32)

    units = [(hh, si) for hh in range(heads) for si in range(tq // sub)]
    nxt = scores(units[0])
    for ui, (hh, si) in enumerate(units):
        s = nxt
        if ui + 1 < len(units):
            nxt = scores(units[ui + 1])
        lg_f, lg_b = lgs[hh]
        cols = head_cols(hh)
        rows = slice(si * sub, (si + 1) * sub)
        v = v_ref[0, :, cols]
        p = (s * d_ref[hh, qi, rows, :]).astype(BF16)
        o = jnp.dot(p, v, preferred_element_type=F32)
        if has_state:
            q = q_ref[0, rows, cols]
            i_col = (r0 + si * sub + lax.broadcasted_iota(jnp.int32, (sub, 1), 0)).astype(F32)
            o = o + (jnp.dot(q, s0_ref[0, 0, 0, hh].astype(BF16), preferred_element_type=F32)
                     * jnp.exp(lg_f * (i_col + 1.0))
                     + jnp.dot(q, s0_ref[0, 0, 1, hh].astype(BF16), preferred_element_type=F32)
                     * jnp.exp(lg_b * (n - i_col)))
        elif si == 0:
            eye = jnp.where(lax.broadcasted_iota(jnp.int32, (RET_DK, RET_DK), 0)
                            == lax.broadcasted_iota(jnp.int32, (RET_DK, RET_DK), 1), 1.0, 0.0).astype(BF16)
            kt = lax.dot_general(eye, k_ref[0, :, cols], nt, preferred_element_type=F32)
            j_row = lax.broadcasted_iota(jnp.int32, (1, n), 1).astype(F32)
            st_ref[0, 0, 0, hh] = jnp.dot((kt * jnp.exp(lg_f * (n - 1.0 - j_row))).astype(BF16), v,
                                          preferred_element_type=F32)
            st_ref[0, 0, 1, hh] = jnp.dot((kt * jnp.exp(lg_b * j_row)).astype(BF16), v,
                                          preferred_element_type=F32)
        mu = jnp.mean(o, axis=-1, keepdims=True)
        oc = o - mu
        var = jnp.mean(oc * oc, axis=-1, keepdims=True)
        on = oc * lax.rsqrt(var + EPS) * gn_ref[:, cols]
        gt = gate_ref[0, rows, cols]
        y_ref[0, rows, cols] = (gt * _sigmoid(gt) * on).astype(y_ref.dtype)


def _retention(q, k, v, gate, lg, gn, state0, heads, tq):
    b, n, _ = q.shape
    assert n % tq == 0 and RET_HEADS % heads == 0 and (state0 is not None or tq == n)
    width = heads * RET_DK
    q_tile = pl.BlockSpec((1, tq, width), lambda hg, bi, qi: (bi, qi, hg))
    seq = pl.BlockSpec((1, n, width), lambda hg, bi, qi: (bi, 0, hg))
    st = pl.BlockSpec((1, 1, 2, heads, RET_DK, RET_DV), lambda hg, bi, qi: (bi, 0, 0, hg, 0, 0))
    in_specs = [q_tile, seq, seq, q_tile,
                pl.BlockSpec((heads, 2, V7X_SUBLANES, V7X_LANES), lambda hg, bi, qi: (hg, 0, 0, 0)),
                pl.BlockSpec((1, width), lambda hg, bi, qi: (0, hg))]
    args = [q, k, v, gate, lg, gn]
    y_shape = jax.ShapeDtypeStruct((b, n, RET_HEADS * RET_DV), BF16)
    if state0 is not None:
        in_specs.append(st)
        args.append(state0)
        out_shape, out_specs = y_shape, q_tile
    else:
        out_shape = (y_shape, jax.ShapeDtypeStruct((b, 1, 2, RET_HEADS, RET_DK, RET_DV), F32))
        out_specs = (q_tile, st)
    return pl.pallas_call(
        functools.partial(_ret_kernel, n=n, tq=tq, heads=heads, has_state=state0 is not None),
        out_shape=out_shape,
        grid=(RET_HEADS // heads, b, n // tq),
        in_specs=in_specs,
        out_specs=out_specs,
        scratch_shapes=[pltpu.VMEM((heads, n // tq, tq, n), F32)],
        compiler_params=_params(("arbitrary", "arbitrary", "arbitrary")),
        name="retention_state" if state0 is not None else "retention",
    )(*args)


def kernel(x_prompt, x_sample, cache_k, cache_v, state_ret, c, c_ctx, w_mod, b_mod, norm_g,
           attn_w_qkv, attn_q_norm, attn_k_norm, attn_w_o, ret_w_qkvg, ret_decay_logit, ret_gn_w,
           ret_w_o, mlp_w1, mlp_w2, final_norm_g):
    bp, sp, d = x_prompt.shape
    bs, ss, _ = x_sample.shape
    depth = w_mod.shape[0]
    assert d == D_MODEL and depth == 2 and 1 + bs <= MOD_ROWS
    assert (bp * sp) % POST_ROW_TILE == 0 and ss % POST_ROW_TILE == 0 and POST_ROW_TILE % ROW_TILE == 0

    cond = jnp.concatenate([c_ctx[None, :], c, jnp.zeros((MOD_ROWS - 1 - bs, d), F32)], axis=0)
    mod = _modulation(cond, w_mod, b_mod)
    mod = mod.reshape(depth, MOD_ROWS, 1, 6 * d)
    ctx_mod = (0, 0)
    lat_mod = (ss, 1)

    xp = x_prompt.reshape(bp * sp, d)
    xs = x_sample.reshape(bs * ss, d)
    nkv = N_KV_HEADS * HEAD_DIM

    g0 = norm_g[0, 0][None, :]
    g1 = norm_g[0, 1][None, :]
    wqkv = attn_w_qkv[0].astype(BF16)
    qn = attn_q_norm[0][None, :]
    kn = attn_k_norm[0][None, :]

    (qp, kp, vp), (w1,) = _attn_in(xp, mod[0], ctx_mod, g0, wqkv, qn, kn, None, PROJ_SUB_TILE,
                                   casts=[(mlp_w1, 0)])
    reorder = np.zeros((HEAD_DIM, HEAD_DIM), np.float32)
    reorder[_PAIRED_ORDER, np.arange(HEAD_DIM)] = 1.0
    reorder = jnp.asarray(reorder, BF16)
    (qs, ks, vs), (w2, wo) = _attn_in(xs, mod[0], lat_mod, g0, wqkv, qn[:, _PAIRED_ORDER], kn[:, _PAIRED_ORDER],
                                      _rope_tables(ss, HEAD_DIM, True) + (reorder,), ROW_TILE,
                                      casts=[(mlp_w2, 0), (attn_w_o, 0)])

    ap, (wr,) = _attention(qp.reshape(bp, sp, -1), kp.reshape(bp, sp * N_KV_HEADS, HEAD_DIM),
                           vp.reshape(bp, sp * N_KV_HEADS, HEAD_DIM), casts=[(ret_w_qkvg, 0)])
    xp, _ = _post(ap.reshape(bp * sp, -1), xp, mod[0], ctx_mod, g1, wo, w1, w2)
    cache = (cache_k, cache_v, reorder)
    as_, _ = _attention(qs.reshape(bs, ss, -1), ks.reshape(bs, ss, nkv), vs.reshape(bs, ss, nkv), cache)
    xs, _ = _post(as_.reshape(bs * ss, -1), xs, mod[0], lat_mod, g1, wo, w1, w2)

    g0 = norm_g[1, 0][None, :]
    g1 = norm_g[1, 1][None, :]
    gn = ret_gn_w[0][None, :]
    gf = final_norm_g[None, :]
    lg = jnp.broadcast_to(ret_decay_logit[0].T[:, :, None, None],
                          (RET_HEADS, 2, V7X_SUBLANES, V7X_LANES))
    hw = RET_HEADS * RET_DK

    (q, k, v, gate), (w1,) = _ret_in(xp, mod[1], ctx_mod, g0, wr, None, casts=[(mlp_w1, 1)])
    yp, new_state = _retention(q.reshape(bp, sp, hw), k.reshape(bp, sp, hw), v.reshape(bp, sp, hw),
                               gate.reshape(bp, sp, hw), lg, gn, None, RET_HEADS, sp)
    (q, k, v, gate), (w2, wo) = _ret_in(xs, mod[1], lat_mod, g0, wr, _rope_tables(ss, RET_DK, False),
                                        casts=[(mlp_w2, 1), (ret_w_o, 0)])
    ys = _retention(q.reshape(bs, ss, hw), k.reshape(bs, ss, hw), v.reshape(bs, ss, hw),
                    gate.reshape(bs, ss, hw), lg, gn, state_ret, RET_STATE_HEADS, RET_Q_TILE)
    y_prompt, _ = _post(yp.reshape(bp * sp, hw), xp, mod[1], ctx_mod, g1, wo, w1, w2, gf)
    y_sample, _ = _post(ys.reshape(bs * ss, hw), xs, mod[1], lat_mod, g1, wo, w1, w2, gf)

    return (y_prompt.reshape(bp, sp, d),
            y_sample.reshape(bs, ss, d),
            kp.reshape(bp, 1, sp, N_KV_HEADS, HEAD_DIM),
            vp.reshape(bp, 1, sp, N_KV_HEADS, HEAD_DIM),
            new_state)
```

```python
import functools

import numpy as np
import jax
import jax.numpy as jnp
from jax import lax
from jax.experimental import pallas as pl
from jax.experimental.pallas import tpu as pltpu

F32 = jnp.float32
BF16 = jnp.bfloat16

D_MODEL = 1024
GRID_W = 64
N_HEADS = 8
N_KV_HEADS = 2
HEAD_DIM = 128
GROUP = N_HEADS // N_KV_HEADS
ROPE_BASE = 10000.0
RET_HEADS = 4
RET_DK = 256
RET_DV = 256
D_FF = 4 * D_MODEL
Q_SCALE = HEAD_DIM ** -0.5 * float(np.log2(np.e))
EPS = 1e-6
QKV_W = (N_HEADS + 2 * N_KV_HEADS) * HEAD_DIM
RET_W = 2 * RET_HEADS * RET_DK + 2 * RET_HEADS * RET_DV

V7X_LANES = 128
V7X_SUBLANES = 8
MOD_ROWS = 16
ROW_TILE = 1024
PROJ_SUB_TILE = 512
POST_ROW_TILE = 1024
POST_SUB_TILE = 512
ATTN_Q_TILE = 512
RET_Q_TILE = 1024
RET_SUB_TILE = 256
RET_STATE_HEADS = 2
VMEM_LIMIT = 56 * 1024 * 1024


def _params(sem, vmem=VMEM_LIMIT):
    return pltpu.CompilerParams(dimension_semantics=sem, vmem_limit_bytes=vmem)


def _const_spec(shape):
    nd = len(shape)
    return pl.BlockSpec(shape, lambda *_: (0,) * nd, pipeline_mode=pl.Buffered(1))


def _sigmoid(x):
    return 1.0 / (1.0 + jnp.exp(-x))


def _cast_plumbing(casts, steps):
    in_specs, out_specs, out_shapes = [], [], []
    for arr, layer in casts:
        _, r, c = arr.shape
        slab = r // steps
        assert slab * steps == r and slab % 16 == 0
        in_specs.append(pl.BlockSpec((1, slab, c), functools.partial(lambda *idx, l: (l, idx[0], 0), l=layer)))
        out_specs.append(pl.BlockSpec((slab, c), lambda *idx: (idx[0], 0)))
        out_shapes.append(jax.ShapeDtypeStruct((r, c), BF16))
    return in_specs, out_specs, out_shapes


def _with_casts(body, n_in, n_out, n_cast):
    def kernel(*refs):
        ins = refs[:n_in]
        cast_in = refs[n_in:n_in + n_cast]
        outs = refs[n_in + n_cast:n_in + n_cast + n_out]
        cast_out = refs[n_in + n_cast + n_out:n_in + 2 * n_cast + n_out]
        scratch = refs[n_in + 2 * n_cast + n_out:]
        for src, dst in zip(cast_in, cast_out):
            dst[...] = src[0].astype(dst.dtype)
        body(*ins, *outs, *scratch)
    return kernel


def _mod_kernel(cond_ref, w_ref, b_ref, o_ref):
    s = cond_ref[...]
    s = (s * _sigmoid(s)).astype(BF16)
    o_ref[0] = jnp.dot(s, w_ref[0].astype(BF16), preferred_element_type=F32) + b_ref[0]


def _modulation(cond, w_mod, b_mod):
    depth = w_mod.shape[0]
    tn = 1024
    return pl.pallas_call(
        _mod_kernel,
        out_shape=jax.ShapeDtypeStruct((depth, MOD_ROWS, 6 * D_MODEL), F32),
        grid=(depth, 6 * D_MODEL // tn),
        in_specs=[
            pl.BlockSpec((MOD_ROWS, D_MODEL), lambda i, j: (0, 0)),
            pl.BlockSpec((1, D_MODEL, tn), lambda i, j: (i, 0, j)),
            pl.BlockSpec((1, 1, tn), lambda i, j: (i, 0, j)),
        ],
        out_specs=pl.BlockSpec((1, MOD_ROWS, tn), lambda i, j: (i, 0, j)),
        compiler_params=_params(("parallel", "parallel")),
        name="modulation",
    )(cond, w_mod, b_mod.reshape(depth, 1, 6 * D_MODEL))


def _norm_modulate(x, g, mod_ref, which):
    y = x * lax.rsqrt(jnp.mean(x * x, axis=-1, keepdims=True) + EPS) * g
    shift = mod_ref[0, :, (3 * which) * D_MODEL:(3 * which + 1) * D_MODEL]
    scale = mod_ref[0, :, (3 * which + 1) * D_MODEL:(3 * which + 2) * D_MODEL]
    return y * (1.0 + scale) + shift


def _mod_spec(mod_rows, tile):
    rows_per_mod_row, first_row = mod_rows
    if rows_per_mod_row:
        assert rows_per_mod_row % tile == 0
        tiles = rows_per_mod_row // tile
        return pl.BlockSpec((1, 1, 6 * D_MODEL), lambda i: (first_row + i // tiles, 0, 0))
    return pl.BlockSpec((1, 1, 6 * D_MODEL), lambda i: (first_row, 0, 0))


def _rope_tables(n, dim, paired):
    quarter = dim // 4
    rows = np.repeat(np.arange(n // GRID_W), GRID_W).astype(np.float64)
    cols = np.tile(np.arange(GRID_W), n // GRID_W).astype(np.float64)
    freqs = ROPE_BASE ** (-np.arange(quarter, dtype=np.float64) / quarter)
    ar = rows[:, None] * freqs
    ac = cols[:, None] * freqs
    if paired:
        cos = np.concatenate([np.cos(ar), np.cos(ac), np.cos(ar), np.cos(ac)], axis=1)
        sin = np.concatenate([-np.sin(ar), -np.sin(ac), np.sin(ar), np.sin(ac)], axis=1)
    else:
        cos = np.concatenate([np.cos(ar), np.cos(ar), np.cos(ac), np.cos(ac)], axis=1)
        sin = np.concatenate([-np.sin(ar), np.sin(ar), -np.sin(ac), np.sin(ac)], axis=1)
    return jnp.asarray(cos, F32), jnp.asarray(sin, F32)


_Q = HEAD_DIM // 4
_PAIRED_ORDER = np.concatenate([np.arange(0, _Q), np.arange(2 * _Q, 3 * _Q),
                                np.arange(_Q, 2 * _Q), np.arange(3 * _Q, 4 * _Q)])


def _swap_halves(t):
    return pltpu.roll(t, V7X_LANES // 2, axis=1)


def _attn_in_kernel(x_ref, mod_ref, g_ref, w_ref, qn_ref, kn_ref, *rest, use_rope):
    nq = N_HEADS * HEAD_DIM
    nk = N_KV_HEADS * HEAD_DIM
    if use_rope:
        cos_ref, sin_ref, reorder_ref, q_ref, k_ref, v_ref, wqk_ref = rest

        @pl.when(pl.program_id(0) == 0)
        def _():
            for head in range(N_HEADS + N_KV_HEADS):
                cols = slice(head * HEAD_DIM, (head + 1) * HEAD_DIM)
                wqk_ref[:, cols] = jnp.dot(w_ref[:, cols], reorder_ref[...],
                                           preferred_element_type=F32).astype(BF16)
    else:
        q_ref, k_ref, v_ref = rest
        wqk_ref = w_ref
    tm = PROJ_SUB_TILE
    n_sub = x_ref.shape[0] // tm
    pair_w = 2 * HEAD_DIM
    n_pairs = (N_HEADS + N_KV_HEADS) // 2
    qw = qn_ref[...] * Q_SCALE
    kw = kn_ref[...]
    if use_rope:
        ra = lax.broadcasted_iota(jnp.int32, (pair_w, pair_w), 0) // HEAD_DIM
        rb = lax.broadcasted_iota(jnp.int32, (pair_w, pair_w), 1) // HEAD_DIM
        head_sum = jnp.where(ra == rb, 1.0, 0.0).astype(BF16)

    def normed(sub):
        return _norm_modulate(x_ref[sub * tm:(sub + 1) * tm, :], g_ref[...], mod_ref, 0).astype(BF16)

    def put_kv(ref, sub, kh, val):
        if use_rope:
            ref[sub * tm:(sub + 1) * tm, kh * HEAD_DIM:(kh + 1) * HEAD_DIM] = val.astype(ref.dtype)
        else:
            ref[pl.ds(sub * tm * N_KV_HEADS + kh, tm, stride=N_KV_HEADS), :] = val.astype(ref.dtype)

    h_next = normed(0)
    for sub in range(n_sub):
        h = h_next
        if sub + 1 < n_sub:
            h_next = normed(sub + 1)
        rows = slice(sub * tm, (sub + 1) * tm)
        if use_rope:
            q_tabs = (qw * cos_ref[rows, :], _swap_halves(qw) * sin_ref[rows, :])
            k_tabs = (kw * cos_ref[rows, :], _swap_halves(kw) * sin_ref[rows, :])
        t2_next = jnp.dot(h, wqk_ref[:, 0:pair_w], preferred_element_type=F32)
        for pair in range(n_pairs):
            t2 = t2_next
            if pair + 1 < n_pairs:
                t2_next = jnp.dot(h, wqk_ref[:, (pair + 1) * pair_w:(pair + 2) * pair_w],
                                  preferred_element_type=F32)
            else:
                t2_next = jnp.dot(h, w_ref[:, nq + nk:], preferred_element_type=F32)
            if use_rope:
                ss2 = jnp.dot((t2 * t2).astype(BF16), head_sum, preferred_element_type=F32)
                r2 = lax.rsqrt(ss2 * (1.0 / HEAD_DIM) + EPS)
            for j in range(2):
                head = 2 * pair + j
                t = t2[:, j * HEAD_DIM:(j + 1) * HEAD_DIM]
                if use_rope:
                    r = r2[:, j * HEAD_DIM:(j + 1) * HEAD_DIM]
                    wc, ws = q_tabs if head < N_HEADS else k_tabs
                    t = (t * wc + _swap_halves(t) * ws) * r
                else:
                    r = lax.rsqrt(jnp.mean(t * t, axis=-1, keepdims=True) + EPS)
                    t = t * r * (qw if head < N_HEADS else kw)
                if head < N_HEADS:
                    q_ref[rows, head * HEAD_DIM:(head + 1) * HEAD_DIM] = t.astype(q_ref.dtype)
                else:
                    put_kv(k_ref, sub, head - N_HEADS, t)
        vv = t2_next
        for kh in range(N_KV_HEADS):
            put_kv(v_ref, sub, kh, vv[:, kh * HEAD_DIM:(kh + 1) * HEAD_DIM])


def _attn_in(x, mod, mod_rows, g, w, qn, kn, rope, tile, casts=()):
    rows = x.shape[0]
    nk = N_KV_HEADS * HEAD_DIM
    if rope is None:
        kv_shape = jax.ShapeDtypeStruct((rows * N_KV_HEADS, HEAD_DIM), F32)
        kv_spec = pl.BlockSpec((tile * N_KV_HEADS, HEAD_DIM), lambda i: (i, 0))
    else:
        kv_shape = jax.ShapeDtypeStruct((rows, nk), BF16)
        kv_spec = pl.BlockSpec((tile, nk), lambda i: (i, 0))
    in_specs = [
        pl.BlockSpec((tile, D_MODEL), lambda i: (i, 0)),
        _mod_spec(mod_rows, tile),
        _const_spec((1, D_MODEL)),
        _const_spec((D_MODEL, QKV_W)),
        _const_spec((1, HEAD_DIM)),
        _const_spec((1, HEAD_DIM)),
    ]
    args = [x, mod, g, w, qn, kn]
    scratch = []
    if rope is not None:
        cos, sin, reorder = rope
        seq_tiles = cos.shape[0] // tile
        in_specs += [pl.BlockSpec((tile, HEAD_DIM), lambda i: (i % seq_tiles, 0))] * 2
        in_specs.append(_const_spec((HEAD_DIM, HEAD_DIM)))
        args += [cos, sin, reorder]
        scratch = [pltpu.VMEM((D_MODEL, (N_HEADS + N_KV_HEADS) * HEAD_DIM), BF16)]
    steps = rows // tile
    c_in, c_out, c_shapes = _cast_plumbing(casts, steps)
    out_specs = [pl.BlockSpec((tile, N_HEADS * HEAD_DIM), lambda i: (i, 0)), kv_spec, kv_spec]
    out_shapes = [jax.ShapeDtypeStruct((rows, N_HEADS * HEAD_DIM), BF16), kv_shape, kv_shape]
    body = functools.partial(_attn_in_kernel, use_rope=rope is not None)
    outs = pl.pallas_call(
        _with_casts(body, len(args), len(out_shapes), len(casts)),
        out_shape=tuple(out_shapes + c_shapes),
        grid=(steps,),
        in_specs=in_specs + c_in,
        out_specs=tuple(out_specs + c_out),
        scratch_shapes=scratch,
        compiler_params=_params(("arbitrary",)),
        name="attn_in_rope" if rope is not None else "attn_in",
    )(*args, *[arr for arr, _ in casts])
    return outs[:3], list(outs[3:])


def _kv_head(ref, kv):
    if ref.dtype == BF16:
        return ref[0, :, kv * HEAD_DIM:(kv + 1) * HEAD_DIM]
    if len(ref.shape) == 5:
        return ref[0, 0, :, kv, :].astype(BF16)
    tokens = ref.shape[1] // N_KV_HEADS
    return ref[0, pl.ds(kv, tokens, stride=N_KV_HEADS), :].astype(BF16)


def _attn_kernel(q_ref, k_ref, v_ref, *rest, cached):
    if cached:
        ck_ref, cv_ref, reorder_ref, o_ref, ck_scr, cv_scr = rest

        @pl.when(pl.program_id(1) == 0)
        def _():
            for kv in range(N_KV_HEADS):
                ck_scr[kv] = jnp.dot(_kv_head(ck_ref, kv), reorder_ref[...],
                                     preferred_element_type=F32).astype(BF16)
                cv_scr[kv] = _kv_head(cv_ref, kv)
    else:
        (o_ref,) = rest
    nt = (((1,), (1,)), ((), ()))
    ks = [_kv_head(k_ref, kv) for kv in range(N_KV_HEADS)]
    vs = [_kv_head(v_ref, kv) for kv in range(N_KV_HEADS)]

    def scores(head):
        kv = head // GROUP
        q = q_ref[0, :, head * HEAD_DIM:(head + 1) * HEAD_DIM]
        s = lax.dot_general(q, ks[kv], nt, preferred_element_type=F32)
        sc = lax.dot_general(q, ck_scr[kv], nt, preferred_element_type=F32) if cached else None
        return s, sc

    nxt = scores(0)
    for head in range(N_HEADS):
        kv = head // GROUP
        s, sc = nxt
        if head + 1 < N_HEADS:
            nxt = scores(head + 1)
        m = jnp.max(s, axis=-1, keepdims=True)
        if cached:
            m = jnp.maximum(m, jnp.max(sc, axis=-1, keepdims=True))
        p = jnp.exp2(s - m)
        l = jnp.sum(p, axis=-1, keepdims=True)
        o = jnp.dot(p.astype(BF16), vs[kv], preferred_element_type=F32)
        if cached:
            pc = jnp.exp2(sc - m)
            l = l + jnp.sum(pc, axis=-1, keepdims=True)
            o = o + jnp.dot(pc.astype(BF16), cv_scr[kv], preferred_element_type=F32)
        o_ref[0, :, head * HEAD_DIM:(head + 1) * HEAD_DIM] = (o / l).astype(o_ref.dtype)


def _attention(q, k, v, cache=None, casts=()):
    b, n, width = q.shape
    tq = min(ATTN_Q_TILE, n)
    in_specs = [pl.BlockSpec((1, tq, width), lambda bi, qi: (bi, qi, 0))]
    in_specs += [pl.BlockSpec((1,) + k.shape[1:], lambda bi, qi: (bi, 0, 0))] * 2
    args = [q, k, v]
    scratch = []
    if cache is not None:
        in_specs += [pl.BlockSpec((1,) + cache[0].shape[1:], lambda bi, qi: (bi, 0, 0, 0, 0))] * 2
        in_specs.append(pl.BlockSpec((HEAD_DIM, HEAD_DIM), lambda bi, qi: (0, 0)))
        args += list(cache)
        scratch = [pltpu.VMEM((N_KV_HEADS, cache[0].shape[2], HEAD_DIM), BF16)] * 2
    c_in, c_out, c_shapes = _cast_plumbing(casts, b)
    body = functools.partial(_attn_kernel, cached=cache is not None)
    outs = pl.pallas_call(
        _with_casts(body, len(args), 1, len(casts)),
        out_shape=tuple([jax.ShapeDtypeStruct((b, n, width), BF16)] + c_shapes),
        grid=(b, n // tq),
        in_specs=in_specs + c_in,
        out_specs=tuple([pl.BlockSpec((1, tq, width), lambda bi, qi: (bi, qi, 0))] + c_out),
        scratch_shapes=scratch,
        compiler_params=_params(("arbitrary", "arbitrary")),
        name="attention_cached" if cache is not None else "attention",
    )(*args, *[arr for arr, _ in casts])
    return outs[0], list(outs[1:])


def _post_kernel(a_ref, x_ref, mod_ref, g_ref, wo_ref, w1_ref, w2_ref, *rest, final):
    if final:
        gf_ref, o_ref = rest
    else:
        (o_ref,) = rest
    gate1 = mod_ref[0, :, 2 * D_MODEL:3 * D_MODEL]
    gate2 = mod_ref[0, :, 5 * D_MODEL:6 * D_MODEL]
    ff_tile = D_FF // 4

    def attn_residual(t):
        rows = slice(t * POST_SUB_TILE, (t + 1) * POST_SUB_TILE)
        x = x_ref[rows, :] + gate1 * jnp.dot(a_ref[rows, :], wo_ref[...], preferred_element_type=F32)
        return x, _norm_modulate(x, g_ref[...], mod_ref, 1).astype(BF16)

    n_sub = x_ref.shape[0] // POST_SUB_TILE
    nxt = attn_residual(0)
    for t in range(n_sub):
        x, h = nxt
        if t + 1 < n_sub:
            nxt = attn_residual(t + 1)
        y = None
        for f in range(D_FF // ff_tile):
            u = jnp.dot(h, w1_ref[:, f * ff_tile:(f + 1) * ff_tile], preferred_element_type=F32)
            u = jnp.square(jnp.maximum(u, 0.0)).astype(BF16)
            part = jnp.dot(u, w2_ref[f * ff_tile:(f + 1) * ff_tile, :], preferred_element_type=F32)
            y = part if y is None else y + part
        x = x + gate2 * y
        if final:
            x = x * lax.rsqrt(jnp.mean(x * x, axis=-1, keepdims=True) + EPS) * gf_ref[...]
        o_ref[t * POST_SUB_TILE:(t + 1) * POST_SUB_TILE, :] = x


def _post(a, x, mod, mod_rows, g, wo, w1, w2, gf=None, casts=()):
    rows = x.shape[0]
    row_block = pl.BlockSpec((POST_ROW_TILE, D_MODEL), lambda i: (i, 0))
    in_specs = [
        row_block,
        row_block,
        _mod_spec(mod_rows, POST_ROW_TILE),
        _const_spec((1, D_MODEL)),
        _const_spec((D_MODEL, D_MODEL)),
        _const_spec((D_MODEL, D_FF)),
        _const_spec((D_FF, D_MODEL)),
    ]
    args = [a, x, mod, g, wo, w1, w2]
    if gf is not None:
        in_specs.append(_const_spec((1, D_MODEL)))
        args.append(gf)
    steps = rows // POST_ROW_TILE
    c_in, c_out, c_shapes = _cast_plumbing(casts, steps)
    body = functools.partial(_post_kernel, final=gf is not None)
    outs = pl.pallas_call(
        _with_casts(body, len(args), 1, len(casts)),
        out_shape=tuple([jax.ShapeDtypeStruct((rows, D_MODEL), F32)] + c_shapes),
        grid=(steps,),
        in_specs=in_specs + c_in,
        out_specs=tuple([row_block] + c_out),
        compiler_params=_params(("parallel",)),
        name="post_final" if gf is not None else "post",
    )(*args, *[arr for arr, _ in casts])
    return outs[0], list(outs[1:])


def _ret_in_kernel(x_ref, mod_ref, g_ref, w_ref, *rest, use_rope):
    if use_rope:
        cos_ref, sin_ref, q_ref, k_ref, v_ref, gate_ref = rest
    else:
        q_ref, k_ref, v_ref, gate_ref = rest
    width = RET_HEADS * RET_DK
    tm = PROJ_SUB_TILE
    n_sub = x_ref.shape[0] // tm

    def normed(sub):
        return _norm_modulate(x_ref[sub * tm:(sub + 1) * tm, :], g_ref[...], mod_ref, 0).astype(BF16)

    h_next = normed(0)
    for sub in range(n_sub):
        h = h_next
        if sub + 1 < n_sub:
            h_next = normed(sub + 1)
        rows = slice(sub * tm, (sub + 1) * tm)
        for part, ref in enumerate((q_ref, k_ref, v_ref, gate_ref)):
            t = jnp.dot(h, w_ref[:, part * width:(part + 1) * width], preferred_element_type=F32)
            if part < 2 and use_rope:
                for c in range(width // V7X_LANES):
                    half = (c % (RET_DK // V7X_LANES)) * V7X_LANES
                    tc = t[:, c * V7X_LANES:(c + 1) * V7X_LANES]
                    tc = (tc * cos_ref[rows, half:half + V7X_LANES]
                          + _swap_halves(tc) * sin_ref[rows, half:half + V7X_LANES])
                    if part == 0:
                        tc = tc * RET_DK ** -0.5
                    ref[rows, c * V7X_LANES:(c + 1) * V7X_LANES] = tc.astype(ref.dtype)
            else:
                if part == 0:
                    t = t * RET_DK ** -0.5
                ref[rows, :] = t.astype(ref.dtype)


def _ret_in(x, mod, mod_rows, g, w, rope, tile, casts=()):
    rows = x.shape[0]
    width = RET_HEADS * RET_DK
    in_specs = [
        pl.BlockSpec((tile, D_MODEL), lambda i: (i, 0)),
        _mod_spec(mod_rows, tile),
        _const_spec((1, D_MODEL)),
        _const_spec((D_MODEL, RET_W)),
    ]
    args = [x, mod, g, w]
    if rope is not None:
        seq_tiles = rope[0].shape[0] // tile
        in_specs += [pl.BlockSpec((tile, RET_DK), lambda i: (i % seq_tiles, 0))] * 2
        args += list(rope)
    out_block = pl.BlockSpec((tile, width), lambda i: (i, 0))
    steps = rows // tile
    c_in, c_out, c_shapes = _cast_plumbing(casts, steps)
    body = functools.partial(_ret_in_kernel, use_rope=rope is not None)
    outs = pl.pallas_call(
        _with_casts(body, len(args), 4, len(casts)),
        out_shape=tuple([jax.ShapeDtypeStruct((rows, width), BF16)] * 3
                        + [jax.ShapeDtypeStruct((rows, width), F32)] + c_shapes),
        grid=(steps,),
        in_specs=in_specs + c_in,
        out_specs=tuple([out_block] * 4 + c_out),
        compiler_params=_params(("parallel",)),
        name="ret_in_rope" if rope is not None else "ret_in",
    )(*args, *[arr for arr, _ in casts])
    return outs[:4], list(outs[4:])


def _log_sigmoid(x):
    return jnp.minimum(x, 0.0) - jnp.log(1.0 + jnp.exp(-jnp.abs(x)))


def _ret_kernel(q_ref, k_ref, v_ref, gate_ref, lg_ref, gn_ref, *rest, n, tq, heads, has_state):
    if has_state:
        s0_ref, y_ref, d_ref = rest
    else:
        y_ref, st_ref, d_ref = rest
    b = pl.program_id(1)
    qi = pl.program_id(2)
    r0 = qi * tq
    nt = (((1,), (1,)), ((), ()))
    sub = min(tq, RET_SUB_TILE)
    lgs = [(_log_sigmoid(lg_ref[hh, 0, 0:1, 0:1]), _log_sigmoid(lg_ref[hh, 1, 0:1, 0:1]))
           for hh in range(heads)]

    @pl.when(b == 0)
    def _():
        ii = r0 + lax.broadcasted_iota(jnp.int32, (tq, n), 0)
        jj = lax.broadcasted_iota(jnp.int32, (tq, n), 1)
        diff = (ii - jj).astype(F32)
        for hh in range(heads):
            lg_f, lg_b = lgs[hh]
            d_ref[hh, qi] = jnp.exp(jnp.where(diff >= 0.0, lg_f, -lg_b) * diff)

    def head_cols(hh):
        return slice(hh * RET_DK, (hh + 1) * RET_DK)

    def scores(unit):
        hh, si = unit
        return lax.dot_general(q_ref[0, si * sub:(si + 1) * sub, head_cols(hh)], k_ref[0, :, head_cols(hh)],
                               nt, preferred_element_type=F32)

    units = [(hh, si) for hh in range(heads) for si in range(tq // sub)]
    nxt = scores(units[0])
    for ui, (hh, si) in enumerate(units):
        s = nxt
        if ui + 1 < len(units):
            nxt = scores(units[ui + 1])
        lg_f, lg_b = lgs[hh]
        cols = head_cols(hh)
        rows = slice(si * sub, (si + 1) * sub)
        v = v_ref[0, :, cols]
        p = (s * d_ref[hh, qi, rows, :]).astype(BF16)
        o = jnp.dot(p, v, preferred_element_type=F32)
        if has_state:
            q = q_ref[0, rows, cols]
            i_col = (r0 + si * sub + lax.broadcasted_iota(jnp.int32, (sub, 1), 0)).astype(F32)
            o = o + (jnp.dot(q, s0_ref[0, 0, 0, hh].astype(BF16), preferred_element_type=F32)
                     * jnp.exp(lg_f * (i_col + 1.0))
                     + jnp.dot(q, s0_ref[0, 0, 1, hh].astype(BF16), preferred_element_type=F32)
                     * jnp.exp(lg_b * (n - i_col)))
        elif si == 0:
            eye = jnp.where(lax.broadcasted_iota(jnp.int32, (RET_DK, RET_DK), 0)
                            == lax.broadcasted_iota(jnp.int32, (RET_DK, RET_DK), 1), 1.0, 0.0).astype(BF16)
            kt = lax.dot_general(eye, k_ref[0, :, cols], nt, preferred_element_type=F32)
            j_row = lax.broadcasted_iota(jnp.int32, (1, n), 1).astype(F32)
            st_ref[0, 0, 0, hh] = jnp.dot((kt * jnp.exp(lg_f * (n - 1.0 - j_row))).astype(BF16), v,
                                          preferred_element_type=F32)
            st_ref[0, 0, 1, hh] = jnp.dot((kt * jnp.exp(lg_b * j_row)).astype(BF16), v,
                                          preferred_element_type=F32)
        mu = jnp.mean(o, axis=-1, keepdims=True)
        oc = o - mu
        var = jnp.mean(oc * oc, axis=-1, keepdims=True)
        on = oc * lax.rsqrt(var + EPS) * gn_ref[:, cols]
        gt = gate_ref[0, rows, cols]
        y_ref[0, rows, cols] = (gt * _sigmoid(gt) * on).astype(y_ref.dtype)


def _retention(q, k, v, gate, lg, gn, state0, heads, tq):
    b, n, _ = q.shape
    assert n % tq == 0 and RET_HEADS % heads == 0 and (state0 is not None or tq == n)
    width = heads * RET_DK
    q_tile = pl.BlockSpec((1, tq, width), lambda hg, bi, qi: (bi, qi, hg))
    seq = pl.BlockSpec((1, n, width), lambda hg, bi, qi: (bi, 0, hg))
    st = pl.BlockSpec((1, 1, 2, heads, RET_DK, RET_DV), lambda hg, bi, qi: (bi, 0, 0, hg, 0, 0))
    in_specs = [q_tile, seq, seq, q_tile,
                pl.BlockSpec((heads, 2, V7X_SUBLANES, V7X_LANES), lambda hg, bi, qi: (hg, 0, 0, 0)),
                pl.BlockSpec((1, width), lambda hg, bi, qi: (0, hg))]
    args = [q, k, v, gate, lg, gn]
    y_shape = jax.ShapeDtypeStruct((b, n, RET_HEADS * RET_DV), BF16)
    if state0 is not None:
        in_specs.append(st)
        args.append(state0)
        out_shape, out_specs = y_shape, q_tile
    else:
        out_shape = (y_shape, jax.ShapeDtypeStruct((b, 1, 2, RET_HEADS, RET_DK, RET_DV), F32))
        out_specs = (q_tile, st)
    return pl.pallas_call(
        functools.partial(_ret_kernel, n=n, tq=tq, heads=heads, has_state=state0 is not None),
        out_shape=out_shape,
        grid=(RET_HEADS // heads, b, n // tq),
        in_specs=in_specs,
        out_specs=out_specs,
        scratch_shapes=[pltpu.VMEM((heads, n // tq, tq, n), F32)],
        compiler_params=_params(("arbitrary", "arbitrary", "arbitrary")),
        name="retention_state" if state0 is not None else "retention",
    )(*args)


def kernel(x_prompt, x_sample, cache_k, cache_v, state_ret, c, c_ctx, w_mod, b_mod, norm_g,
           attn_w_qkv, attn_q_norm, attn_k_norm, attn_w_o, ret_w_qkvg, ret_decay_logit, ret_gn_w,
           ret_w_o, mlp_w1, mlp_w2, final_norm_g):
    bp, sp, d = x_prompt.shape
    bs, ss, _ = x_sample.shape
    depth = w_mod.shape[0]
    assert d == D_MODEL and depth == 2 and 1 + bs <= MOD_ROWS
    assert (bp * sp) % POST_ROW_TILE == 0 and ss % POST_ROW_TILE == 0 and POST_ROW_TILE % ROW_TILE == 0

    cond = jnp.concatenate([c_ctx[None, :], c, jnp.zeros((MOD_ROWS - 1 - bs, d), F32)], axis=0)
    mod = _modulation(cond, w_mod, b_mod)
    mod = mod.reshape(depth, MOD_ROWS, 1, 6 * d)
    ctx_mod = (0, 0)
    lat_mod = (ss, 1)

    xp = x_prompt.reshape(bp * sp, d)
    xs = x_sample.reshape(bs * ss, d)
    nkv = N_KV_HEADS * HEAD_DIM

    g0 = norm_g[0, 0][None, :]
    g1 = norm_g[0, 1][None, :]
    wqkv = attn_w_qkv[0].astype(BF16)
    qn = attn_q_norm[0][None, :]
    kn = attn_k_norm[0][None, :]

    (qp, kp, vp), (w1,) = _attn_in(xp, mod[0], ctx_mod, g0, wqkv, qn, kn, None, PROJ_SUB_TILE,
                                   casts=[(mlp_w1, 0)])
    reorder = np.zeros((HEAD_DIM, HEAD_DIM), np.float32)
    reorder[_PAIRED_ORDER, np.arange(HEAD_DIM)] = 1.0
    reorder = jnp.asarray(reorder, BF16)
    (qs, ks, vs), (w2, wo) = _attn_in(xs, mod[0], lat_mod, g0, wqkv, qn[:, _PAIRED_ORDER], kn[:, _PAIRED_ORDER],
                                      _rope_tables(ss, HEAD_DIM, True) + (reorder,), ROW_TILE,
                                      casts=[(mlp_w2, 0), (attn_w_o, 0)])

    ap, _ = _attention(qp.reshape(bp, sp, -1), kp.reshape(bp, sp * N_KV_HEADS, HEAD_DIM),
                       vp.reshape(bp, sp * N_KV_HEADS, HEAD_DIM))
    xp, _ = _post(ap.reshape(bp * sp, -1), xp, mod[0], ctx_mod, g1, wo, w1, w2)
    cache = (cache_k, cache_v, reorder)
    as_, (wr,) = _attention(qs.reshape(bs, ss, -1), ks.reshape(bs, ss, nkv), vs.reshape(bs, ss, nkv), cache,
                            casts=[(ret_w_qkvg, 0)])
    xs, _ = _post(as_.reshape(bs * ss, -1), xs, mod[0], lat_mod, g1, wo, w1, w2)

    g0 = norm_g[1, 0][None, :]
    g1 = norm_g[1, 1][None, :]
    gn = ret_gn_w[0][None, :]
    gf = final_norm_g[None, :]
    lg = jnp.broadcast_to(ret_decay_logit[0].T[:, :, None, None],
                          (RET_HEADS, 2, V7X_SUBLANES, V7X_LANES))
    hw = RET_HEADS * RET_DK

    (q, k, v, gate), (w1,) = _ret_in(xp, mod[1], ctx_mod, g0, wr, None, PROJ_SUB_TILE, casts=[(mlp_w1, 1)])
    yp, new_state = _retention(q.reshape(bp, sp, hw), k.reshape(bp, sp, hw), v.reshape(bp, sp, hw),
                               gate.reshape(bp, sp, hw), lg, gn, None, RET_HEADS, sp)
    (q, k, v, gate), (w2, wo) = _ret_in(xs, mod[1], lat_mod, g0, wr, _rope_tables(ss, RET_DK, False),
                                        ROW_TILE, casts=[(mlp_w2, 1), (ret_w_o, 0)])
    ys = _retention(q.reshape(bs, ss, hw), k.reshape(bs, ss, hw), v.reshape(bs, ss, hw),
                    gate.reshape(bs, ss, hw), lg, gn, state_ret, RET_STATE_HEADS, RET_Q_TILE)
    y_prompt, _ = _post(yp.reshape(bp * sp, hw), xp, mod[1], ctx_mod, g1, wo, w1, w2, gf)
    y_sample, _ = _post(ys.reshape(bs * ss, hw), xs, mod[1], lat_mod, g1, wo, w1, w2, gf)

    return (y_prompt.reshape(bp, sp, d),
            y_sample.reshape(bs, ss, d),
            kp.reshape(bp, 1, sp, N_KV_HEADS, HEAD_DIM),
            vp.reshape(bp, 1, sp, N_KV_HEADS, HEAD_DIM),
            new_state)
```

```python
import functools

import numpy as np
import jax
import jax.numpy as jnp
from jax import lax
from jax.experimental import pallas as pl
from jax.experimental.pallas import tpu as pltpu

F32 = jnp.float32
BF16 = jnp.bfloat16

D_MODEL = 1024
GRID_W = 64
N_HEADS = 8
N_KV_HEADS = 2
HEAD_DIM = 128
GROUP = N_HEADS // N_KV_HEADS
ROPE_BASE = 10000.0
RET_HEADS = 4
RET_DK = 256
RET_DV = 256
D_FF = 4 * D_MODEL
Q_SCALE = HEAD_DIM ** -0.5 * float(np.log2(np.e))
EPS = 1e-6
QKV_W = (N_HEADS + 2 * N_KV_HEADS) * HEAD_DIM
RET_W = 2 * RET_HEADS * RET_DK + 2 * RET_HEADS * RET_DV

V7X_LANES = 128
V7X_SUBLANES = 8
MOD_ROWS = 16
ROW_TILE = 1024
PROJ_SUB_TILE = 512
POST_ROW_TILE = 1024
POST_SUB_TILE = 512
ATTN_Q_TILE = 512
RET_Q_TILE = 1024
RET_SUB_TILE = 256
RET_STATE_HEADS = 2
VMEM_LIMIT = 56 * 1024 * 1024


def _params(sem, vmem=VMEM_LIMIT):
    return pltpu.CompilerParams(dimension_semantics=sem, vmem_limit_bytes=vmem)


def _const_spec(shape):
    nd = len(shape)
    return pl.BlockSpec(shape, lambda *_: (0,) * nd, pipeline_mode=pl.Buffered(1))


def _sigmoid(x):
    return 1.0 / (1.0 + jnp.exp(-x))


def _cast_plumbing(casts, steps):
    in_specs, out_specs, out_shapes = [], [], []
    for arr, layer in casts:
        _, r, c = arr.shape
        slab = r // steps
        assert slab * steps == r and slab % 16 == 0
        in_specs.append(pl.BlockSpec((1, slab, c), functools.partial(lambda *idx, l: (l, idx[0], 0), l=layer)))
        out_specs.append(pl.BlockSpec((slab, c), lambda *idx: (idx[0], 0)))
        out_shapes.append(jax.ShapeDtypeStruct((r, c), BF16))
    return in_specs, out_specs, out_shapes


def _with_casts(body, n_in, n_out, n_cast):
    def kernel(*refs):
        ins = refs[:n_in]
        cast_in = refs[n_in:n_in + n_cast]
        outs = refs[n_in + n_cast:n_in + n_cast + n_out]
        cast_out = refs[n_in + n_cast + n_out:n_in + 2 * n_cast + n_out]
        scratch = refs[n_in + 2 * n_cast + n_out:]
        for src, dst in zip(cast_in, cast_out):
            dst[...] = src[0].astype(dst.dtype)
        body(*ins, *outs, *scratch)
    return kernel


def _mod_kernel(cond_ref, w_ref, b_ref, o_ref):
    s = cond_ref[...]
    s = (s * _sigmoid(s)).astype(BF16)
    o_ref[0] = jnp.dot(s, w_ref[0].astype(BF16), preferred_element_type=F32) + b_ref[0]


def _modulation(cond, w_mod, b_mod):
    depth = w_mod.shape[0]
    tn = 1024
    return pl.pallas_call(
        _mod_kernel,
        out_shape=jax.ShapeDtypeStruct((depth, MOD_ROWS, 6 * D_MODEL), F32),
        grid=(depth, 6 * D_MODEL // tn),
        in_specs=[
            pl.BlockSpec((MOD_ROWS, D_MODEL), lambda i, j: (0, 0)),
            pl.BlockSpec((1, D_MODEL, tn), lambda i, j: (i, 0, j)),
            pl.BlockSpec((1, 1, tn), lambda i, j: (i, 0, j)),
        ],
        out_specs=pl.BlockSpec((1, MOD_ROWS, tn), lambda i, j: (i, 0, j)),
        compiler_params=_params(("parallel", "parallel")),
        name="modulation",
    )(cond, w_mod, b_mod.reshape(depth, 1, 6 * D_MODEL))


def _norm_modulate(x, g, mod_ref, which):
    y = x * lax.rsqrt(jnp.mean(x * x, axis=-1, keepdims=True) + EPS) * g
    shift = mod_ref[0, :, (3 * which) * D_MODEL:(3 * which + 1) * D_MODEL]
    scale = mod_ref[0, :, (3 * which + 1) * D_MODEL:(3 * which + 2) * D_MODEL]
    return y * (1.0 + scale) + shift


def _mod_spec(mod_rows, tile):
    rows_per_mod_row, first_row = mod_rows
    if rows_per_mod_row:
        assert rows_per_mod_row % tile == 0
        tiles = rows_per_mod_row // tile
        return pl.BlockSpec((1, 1, 6 * D_MODEL), lambda i: (first_row + i // tiles, 0, 0))
    return pl.BlockSpec((1, 1, 6 * D_MODEL), lambda i: (first_row, 0, 0))


def _rope_tables(n, dim, paired):
    quarter = dim // 4
    rows = np.repeat(np.arange(n // GRID_W), GRID_W).astype(np.float64)
    cols = np.tile(np.arange(GRID_W), n // GRID_W).astype(np.float64)
    freqs = ROPE_BASE ** (-np.arange(quarter, dtype=np.float64) / quarter)
    ar = rows[:, None] * freqs
    ac = cols[:, None] * freqs
    if paired:
        cos = np.concatenate([np.cos(ar), np.cos(ac), np.cos(ar), np.cos(ac)], axis=1)
        sin = np.concatenate([-np.sin(ar), -np.sin(ac), np.sin(ar), np.sin(ac)], axis=1)
    else:
        cos = np.concatenate([np.cos(ar), np.cos(ar), np.cos(ac), np.cos(ac)], axis=1)
        sin = np.concatenate([-np.sin(ar), np.sin(ar), -np.sin(ac), np.sin(ac)], axis=1)
    return jnp.asarray(cos, F32), jnp.asarray(sin, F32)


_Q = HEAD_DIM // 4
_PAIRED_ORDER = np.concatenate([np.arange(0, _Q), np.arange(2 * _Q, 3 * _Q),
                                np.arange(_Q, 2 * _Q), np.arange(3 * _Q, 4 * _Q)])


def _swap_halves(t):
    return pltpu.roll(t, V7X_LANES // 2, axis=1)


def _attn_in_kernel(x_ref, mod_ref, g_ref, w_ref, qn_ref, kn_ref, *rest, use_rope):
    nq = N_HEADS * HEAD_DIM
    nk = N_KV_HEADS * HEAD_DIM
    if use_rope:
        cos_ref, sin_ref, reorder_ref, q_ref, k_ref, v_ref, wqk_ref = rest

        @pl.when(pl.program_id(0) == 0)
        def _():
            for head in range(N_HEADS + N_KV_HEADS):
                cols = slice(head * HEAD_DIM, (head + 1) * HEAD_DIM)
                wqk_ref[:, cols] = jnp.dot(w_ref[:, cols], reorder_ref[...],
                                           preferred_element_type=F32).astype(BF16)
    else:
        q_ref, k_ref, v_ref = rest
        wqk_ref = w_ref
    tm = PROJ_SUB_TILE
    n_sub = x_ref.shape[0] // tm
    pair_w = 2 * HEAD_DIM
    n_pairs = (N_HEADS + N_KV_HEADS) // 2
    qw = qn_ref[...] * Q_SCALE
    kw = kn_ref[...]
    if use_rope:
        ra = lax.broadcasted_iota(jnp.int32, (pair_w, pair_w), 0) // HEAD_DIM
        rb = lax.broadcasted_iota(jnp.int32, (pair_w, pair_w), 1) // HEAD_DIM
        head_sum = jnp.where(ra == rb, 1.0, 0.0).astype(BF16)

    def normed(sub):
        return _norm_modulate(x_ref[sub * tm:(sub + 1) * tm, :], g_ref[...], mod_ref, 0).astype(BF16)

    def put_kv(ref, sub, kh, val):
        if use_rope:
            ref[sub * tm:(sub + 1) * tm, kh * HEAD_DIM:(kh + 1) * HEAD_DIM] = val.astype(ref.dtype)
        else:
            ref[pl.ds(sub * tm * N_KV_HEADS + kh, tm, stride=N_KV_HEADS), :] = val.astype(ref.dtype)

    h_next = normed(0)
    for sub in range(n_sub):
        h = h_next
        if sub + 1 < n_sub:
            h_next = normed(sub + 1)
        rows = slice(sub * tm, (sub + 1) * tm)
        if use_rope:
            q_tabs = (qw * cos_ref[rows, :], _swap_halves(qw) * sin_ref[rows, :])
            k_tabs = (kw * cos_ref[rows, :], _swap_halves(kw) * sin_ref[rows, :])
        t2_next = jnp.dot(h, wqk_ref[:, 0:pair_w], preferred_element_type=F32)
        for pair in range(n_pairs):
            t2 = t2_next
            if pair + 1 < n_pairs:
                t2_next = jnp.dot(h, wqk_ref[:, (pair + 1) * pair_w:(pair + 2) * pair_w],
                                  preferred_element_type=F32)
            else:
                t2_next = jnp.dot(h, w_ref[:, nq + nk:], preferred_element_type=F32)
            if use_rope:
                ss2 = jnp.dot((t2 * t2).astype(BF16), head_sum, preferred_element_type=F32)
                r2 = lax.rsqrt(ss2 * (1.0 / HEAD_DIM) + EPS)
            for j in range(2):
                head = 2 * pair + j
                t = t2[:, j * HEAD_DIM:(j + 1) * HEAD_DIM]
                if use_rope:
                    r = r2[:, j * HEAD_DIM:(j + 1) * HEAD_DIM]
                    wc, ws = q_tabs if head < N_HEADS else k_tabs
                    t = (t * wc + _swap_halves(t) * ws) * r
                else:
                    r = lax.rsqrt(jnp.mean(t * t, axis=-1, keepdims=True) + EPS)
                    t = t * r * (qw if head < N_HEADS else kw)
                if head < N_HEADS:
                    q_ref[rows, head * HEAD_DIM:(head + 1) * HEAD_DIM] = t.astype(q_ref.dtype)
                else:
                    put_kv(k_ref, sub, head - N_HEADS, t)
        vv = t2_next
        for kh in range(N_KV_HEADS):
            put_kv(v_ref, sub, kh, vv[:, kh * HEAD_DIM:(kh + 1) * HEAD_DIM])


def _attn_in(x, mod, mod_rows, g, w, qn, kn, rope, tile, casts=()):
    rows = x.shape[0]
    nk = N_KV_HEADS * HEAD_DIM
    if rope is None:
        kv_shape = jax.ShapeDtypeStruct((rows * N_KV_HEADS, HEAD_DIM), F32)
        kv_spec = pl.BlockSpec((tile * N_KV_HEADS, HEAD_DIM), lambda i: (i, 0))
    else:
        kv_shape = jax.ShapeDtypeStruct((rows, nk), BF16)
        kv_spec = pl.BlockSpec((tile, nk), lambda i: (i, 0))
    in_specs = [
        pl.BlockSpec((tile, D_MODEL), lambda i: (i, 0)),
        _mod_spec(mod_rows, tile),
        _const_spec((1, D_MODEL)),
        _const_spec((D_MODEL, QKV_W)),
        _const_spec((1, HEAD_DIM)),
        _const_spec((1, HEAD_DIM)),
    ]
    args = [x, mod, g, w, qn, kn]
    scratch = []
    if rope is not None:
        cos, sin, reorder = rope
        seq_tiles = cos.shape[0] // tile
        in_specs += [pl.BlockSpec((tile, HEAD_DIM), lambda i: (i % seq_tiles, 0))] * 2
        in_specs.append(_const_spec((HEAD_DIM, HEAD_DIM)))
        args += [cos, sin, reorder]
        scratch = [pltpu.VMEM((D_MODEL, (N_HEADS + N_KV_HEADS) * HEAD_DIM), BF16)]
    steps = rows // tile
    c_in, c_out, c_shapes = _cast_plumbing(casts, steps)
    out_specs = [pl.BlockSpec((tile, N_HEADS * HEAD_DIM), lambda i: (i, 0)), kv_spec, kv_spec]
    out_shapes = [jax.ShapeDtypeStruct((rows, N_HEADS * HEAD_DIM), BF16), kv_shape, kv_shape]
    body = functools.partial(_attn_in_kernel, use_rope=rope is not None)
    outs = pl.pallas_call(
        _with_casts(body, len(args), len(out_shapes), len(casts)),
        out_shape=tuple(out_shapes + c_shapes),
        grid=(steps,),
        in_specs=in_specs + c_in,
        out_specs=tuple(out_specs + c_out),
        scratch_shapes=scratch,
        compiler_params=_params(("arbitrary",)),
        name="attn_in_rope" if rope is not None else "attn_in",
    )(*args, *[arr for arr, _ in casts])
    return outs[:3], list(outs[3:])


def _kv_head(ref, kv):
    if ref.dtype == BF16:
        return ref[0, :, kv * HEAD_DIM:(kv + 1) * HEAD_DIM]
    if len(ref.shape) == 5:
        return ref[0, 0, :, kv, :].astype(BF16)
    tokens = ref.shape[1] // N_KV_HEADS
    return ref[0, pl.ds(kv, tokens, stride=N_KV_HEADS), :].astype(BF16)


def _attn_kernel(q_ref, k_ref, v_ref, *rest, cached):
    if cached:
        ck_hbm, cv_hbm, reorder_ref, o_ref, ck_scr, cv_scr, stage, sems = rest
        b = pl.program_id(0)
        n_b = pl.num_programs(0)

        def head_copies(bi, slot):
            return [pltpu.make_async_copy(src.at[bi, 0, :, kv, :], stage.at[slot, t, kv], sems.at[slot, t, kv])
                    for t, src in enumerate((ck_hbm, cv_hbm)) for kv in range(N_KV_HEADS)]

        @pl.when(pl.program_id(1) == 0)
        def _():
            slot = b % 2

            @pl.when(b == 0)
            def _():
                for cp in head_copies(0, 0):
                    cp.start()

            @pl.when(b + 1 < n_b)
            def _():
                for cp in head_copies(b + 1, 1 - slot):
                    cp.start()

            for cp in head_copies(b, slot):
                cp.wait()
            for kv in range(N_KV_HEADS):
                ck_scr[kv] = jnp.dot(stage[slot, 0, kv].astype(BF16), reorder_ref[...],
                                     preferred_element_type=F32).astype(BF16)
                cv_scr[kv] = stage[slot, 1, kv].astype(BF16)
    else:
        (o_ref,) = rest
    nt = (((1,), (1,)), ((), ()))
    ks = [_kv_head(k_ref, kv) for kv in range(N_KV_HEADS)]
    vs = [_kv_head(v_ref, kv) for kv in range(N_KV_HEADS)]

    def scores(head):
        kv = head // GROUP
        q = q_ref[0, :, head * HEAD_DIM:(head + 1) * HEAD_DIM]
        s = lax.dot_general(q, ks[kv], nt, preferred_element_type=F32)
        sc = lax.dot_general(q, ck_scr[kv], nt, preferred_element_type=F32) if cached else None
        return s, sc

    nxt = scores(0)
    for head in range(N_HEADS):
        kv = head // GROUP
        s, sc = nxt
        if head + 1 < N_HEADS:
            nxt = scores(head + 1)
        m = jnp.max(s, axis=-1, keepdims=True)
        if cached:
            m = jnp.maximum(m, jnp.max(sc, axis=-1, keepdims=True))
        p = jnp.exp2(s - m)
        l = jnp.sum(p, axis=-1, keepdims=True)
        o = jnp.dot(p.astype(BF16), vs[kv], preferred_element_type=F32)
        if cached:
            pc = jnp.exp2(sc - m)
            l = l + jnp.sum(pc, axis=-1, keepdims=True)
            o = o + jnp.dot(pc.astype(BF16), cv_scr[kv], preferred_element_type=F32)
        o_ref[0, :, head * HEAD_DIM:(head + 1) * HEAD_DIM] = (o / l).astype(o_ref.dtype)


def _attention(q, k, v, cache=None, casts=()):
    b, n, width = q.shape
    tq = min(ATTN_Q_TILE, n)
    in_specs = [pl.BlockSpec((1, tq, width), lambda bi, qi: (bi, qi, 0))]
    in_specs += [pl.BlockSpec((1,) + k.shape[1:], lambda bi, qi: (bi, 0, 0))] * 2
    args = [q, k, v]
    scratch = []
    if cache is not None:
        past = cache[0].shape[2]
        in_specs += [pl.BlockSpec(memory_space=pl.ANY)] * 2
        in_specs.append(pl.BlockSpec((HEAD_DIM, HEAD_DIM), lambda bi, qi: (0, 0)))
        args += list(cache)
        scratch = [pltpu.VMEM((N_KV_HEADS, past, HEAD_DIM), BF16)] * 2
        scratch += [pltpu.VMEM((2, 2, N_KV_HEADS, past, HEAD_DIM), F32),
                    pltpu.SemaphoreType.DMA((2, 2, N_KV_HEADS))]
    c_in, c_out, c_shapes = _cast_plumbing(casts, b)
    body = functools.partial(_attn_kernel, cached=cache is not None)
    outs = pl.pallas_call(
        _with_casts(body, len(args), 1, len(casts)),
        out_shape=tuple([jax.ShapeDtypeStruct((b, n, width), BF16)] + c_shapes),
        grid=(b, n // tq),
        in_specs=in_specs + c_in,
        out_specs=tuple([pl.BlockSpec((1, tq, width), lambda bi, qi: (bi, qi, 0))] + c_out),
        scratch_shapes=scratch,
        compiler_params=_params(("arbitrary", "arbitrary")),
        name="attention_cached" if cache is not None else "attention",
    )(*args, *[arr for arr, _ in casts])
    return outs[0], list(outs[1:])


def _post_kernel(a_ref, x_ref, mod_ref, g_ref, wo_ref, w1_ref, w2_ref, *rest, final):
    if final:
        gf_ref, o_ref = rest
    else:
        (o_ref,) = rest
    gate1 = mod_ref[0, :, 2 * D_MODEL:3 * D_MODEL]
    gate2 = mod_ref[0, :, 5 * D_MODEL:6 * D_MODEL]
    ff_tile = D_FF // 4

    def attn_residual(t):
        rows = slice(t * POST_SUB_TILE, (t + 1) * POST_SUB_TILE)
        x = x_ref[rows, :] + gate1 * jnp.dot(a_ref[rows, :], wo_ref[...], preferred_element_type=F32)
        return x, _norm_modulate(x, g_ref[...], mod_ref, 1).astype(BF16)

    n_sub = x_ref.shape[0] // POST_SUB_TILE
    nxt = attn_residual(0)
    for t in range(n_sub):
        x, h = nxt
        if t + 1 < n_sub:
            nxt = attn_residual(t + 1)
        y = None
        for f in range(D_FF // ff_tile):
            u = jnp.dot(h, w1_ref[:, f * ff_tile:(f + 1) * ff_tile], preferred_element_type=F32)
            u = jnp.square(jnp.maximum(u, 0.0)).astype(BF16)
            part = jnp.dot(u, w2_ref[f * ff_tile:(f + 1) * ff_tile, :], preferred_element_type=F32)
            y = part if y is None else y + part
        x = x + gate2 * y
        if final:
            x = x * lax.rsqrt(jnp.mean(x * x, axis=-1, keepdims=True) + EPS) * gf_ref[...]
        o_ref[t * POST_SUB_TILE:(t + 1) * POST_SUB_TILE, :] = x


def _post(a, x, mod, mod_rows, g, wo, w1, w2, gf=None, casts=()):
    rows = x.shape[0]
    row_block = pl.BlockSpec((POST_ROW_TILE, D_MODEL), lambda i: (i, 0))
    in_specs = [
        row_block,
        row_block,
        _mod_spec(mod_rows, POST_ROW_TILE),
        _const_spec((1, D_MODEL)),
        _const_spec((D_MODEL, D_MODEL)),
        _const_spec((D_MODEL, D_FF)),
        _const_spec((D_FF, D_MODEL)),
    ]
    args = [a, x, mod, g, wo, w1, w2]
    if gf is not None:
        in_specs.append(_const_spec((1, D_MODEL)))
        args.append(gf)
    steps = rows // POST_ROW_TILE
    c_in, c_out, c_shapes = _cast_plumbing(casts, steps)
    body = functools.partial(_post_kernel, final=gf is not None)
    outs = pl.pallas_call(
        _with_casts(body, len(args), 1, len(casts)),
        out_shape=tuple([jax.ShapeDtypeStruct((rows, D_MODEL), F32)] + c_shapes),
        grid=(steps,),
        in_specs=in_specs + c_in,
        out_specs=tuple([row_block] + c_out),
        compiler_params=_params(("parallel",)),
        name="post_final" if gf is not None else "post",
    )(*args, *[arr for arr, _ in casts])
    return outs[0], list(outs[1:])


def _ret_in_kernel(x_ref, mod_ref, g_ref, w_ref, *rest, use_rope):
    if use_rope:
        cos_ref, sin_ref, q_ref, k_ref, v_ref, gate_ref = rest
    else:
        q_ref, k_ref, v_ref, gate_ref = rest
    width = RET_HEADS * RET_DK
    tm = PROJ_SUB_TILE
    n_sub = x_ref.shape[0] // tm

    def normed(sub):
        return _norm_modulate(x_ref[sub * tm:(sub + 1) * tm, :], g_ref[...], mod_ref, 0).astype(BF16)

    h_next = normed(0)
    for sub in range(n_sub):
        h = h_next
        if sub + 1 < n_sub:
            h_next = normed(sub + 1)
        rows = slice(sub * tm, (sub + 1) * tm)
        for part, ref in enumerate((q_ref, k_ref, v_ref, gate_ref)):
            t = jnp.dot(h, w_ref[:, part * width:(part + 1) * width], preferred_element_type=F32)
            if part < 2 and use_rope:
                for c in range(width // V7X_LANES):
                    half = (c % (RET_DK // V7X_LANES)) * V7X_LANES
                    tc = t[:, c * V7X_LANES:(c + 1) * V7X_LANES]
                    tc = (tc * cos_ref[rows, half:half + V7X_LANES]
                          + _swap_halves(tc) * sin_ref[rows, half:half + V7X_LANES])
                    if part == 0:
                        tc = tc * RET_DK ** -0.5
                    ref[rows, c * V7X_LANES:(c + 1) * V7X_LANES] = tc.astype(ref.dtype)
            else:
                if part == 0:
                    t = t * RET_DK ** -0.5
                ref[rows, :] = t.astype(ref.dtype)


def _ret_in(x, mod, mod_rows, g, w, rope, tile, casts=()):
    rows = x.shape[0]
    width = RET_HEADS * RET_DK
    in_specs = [
        pl.BlockSpec((tile, D_MODEL), lambda i: (i, 0)),
        _mod_spec(mod_rows, tile),
        _const_spec((1, D_MODEL)),
        _const_spec((D_MODEL, RET_W)),
    ]
    args = [x, mod, g, w]
    if rope is not None:
        seq_tiles = rope[0].shape[0] // tile
        in_specs += [pl.BlockSpec((tile, RET_DK), lambda i: (i % seq_tiles, 0))] * 2
        args += list(rope)
    out_block = pl.BlockSpec((tile, width), lambda i: (i, 0))
    steps = rows // tile
    c_in, c_out, c_shapes = _cast_plumbing(casts, steps)
    body = functools.partial(_ret_in_kernel, use_rope=rope is not None)
    outs = pl.pallas_call(
        _with_casts(body, len(args), 4, len(casts)),
        out_shape=tuple([jax.ShapeDtypeStruct((rows, width), BF16)] * 3
                        + [jax.ShapeDtypeStruct((rows, width), F32)] + c_shapes),
        grid=(steps,),
        in_specs=in_specs + c_in,
        out_specs=tuple([out_block] * 4 + c_out),
        compiler_params=_params(("parallel",)),
        name="ret_in_rope" if rope is not None else "ret_in",
    )(*args, *[arr for arr, _ in casts])
    return outs[:4], list(outs[4:])


def _log_sigmoid(x):
    return jnp.minimum(x, 0.0) - jnp.log(1.0 + jnp.exp(-jnp.abs(x)))


def _ret_kernel(q_ref, k_ref, v_ref, gate_ref, lg_ref, gn_ref, *rest, n, tq, heads, has_state):
    if has_state:
        s0_ref, y_ref, d_ref = rest
    else:
        y_ref, st_ref, d_ref = rest
    b = pl.program_id(1)
    qi = pl.program_id(2)
    r0 = qi * tq
    nt = (((1,), (1,)), ((), ()))
    sub = min(tq, RET_SUB_TILE)
    lgs = [(_log_sigmoid(lg_ref[hh, 0, 0:1, 0:1]), _log_sigmoid(lg_ref[hh, 1, 0:1, 0:1]))
           for hh in range(heads)]

    @pl.when(b == 0)
    def _():
        ii = r0 + lax.broadcasted_iota(jnp.int32, (tq, n), 0)
        jj = lax.broadcasted_iota(jnp.int32, (tq, n), 1)
        diff = (ii - jj).astype(F32)
        for hh in range(heads):
            lg_f, lg_b = lgs[hh]
            d_ref[hh, qi] = jnp.exp(jnp.where(diff >= 0.0, lg_f, -lg_b) * diff)

    def head_cols(hh):
        return slice(hh * RET_DK, (hh + 1) * RET_DK)

    def scores(unit):
        hh, si = unit
        return lax.dot_general(q_ref[0, si * sub:(si + 1) * sub, head_cols(hh)], k_ref[0, :, head_cols(hh)],
                               nt, preferred_element_type=F32)

    units = [(hh, si) for hh in range(heads) for si in range(tq // sub)]
    nxt = scores(units[0])
    for ui, (hh, si) in enumerate(units):
        s = nxt
        if ui + 1 < len(units):
            nxt = scores(units[ui + 1])
        lg_f, lg_b = lgs[hh]
        cols = head_cols(hh)
        rows = slice(si * sub, (si + 1) * sub)
        v = v_ref[0, :, cols]
        p = (s * d_ref[hh, qi, rows, :]).astype(BF16)
        o = jnp.dot(p, v, preferred_element_type=F32)
        if has_state:
            q = q_ref[0, rows, cols]
            i_col = (r0 + si * sub + lax.broadcasted_iota(jnp.int32, (sub, 1), 0)).astype(F32)
            o = o + (jnp.dot(q, s0_ref[0, 0, 0, hh].astype(BF16), preferred_element_type=F32)
                     * jnp.exp(lg_f * (i_col + 1.0))
                     + jnp.dot(q, s0_ref[0, 0, 1, hh].astype(BF16), preferred_element_type=F32)
                     * jnp.exp(lg_b * (n - i_col)))
        elif si == 0:
            eye = jnp.where(lax.broadcasted_iota(jnp.int32, (RET_DK, RET_DK), 0)
                            == lax.broadcasted_iota(jnp.int32, (RET_DK, RET_DK), 1), 1.0, 0.0).astype(BF16)
            kt = lax.dot_general(eye, k_ref[0, :, cols], nt, preferred_element_type=F32)
            j_row = lax.broadcasted_iota(jnp.int32, (1, n), 1).astype(F32)
            st_ref[0, 0, 0, hh] = jnp.dot((kt * jnp.exp(lg_f * (n - 1.0 - j_row))).astype(BF16), v,
                                          preferred_element_type=F32)
            st_ref[0, 0, 1, hh] = jnp.dot((kt * jnp.exp(lg_b * j_row)).astype(BF16), v,
                                          preferred_element_type=F32)
        mu = jnp.mean(o, axis=-1, keepdims=True)
        oc = o - mu
        var = jnp.mean(oc * oc, axis=-1, keepdims=True)
        on = oc * lax.rsqrt(var + EPS) * gn_ref[:, cols]
        gt = gate_ref[0, rows, cols]
        y_ref[0, rows, cols] = (gt * _sigmoid(gt) * on).astype(y_ref.dtype)


def _retention(q, k, v, gate, lg, gn, state0, heads, tq):
    b, n, _ = q.shape
    assert n % tq == 0 and RET_HEADS % heads == 0 and (state0 is not None or tq == n)
    width = heads * RET_DK
    q_tile = pl.BlockSpec((1, tq, width), lambda hg, bi, qi: (bi, qi, hg))
    seq = pl.BlockSpec((1, n, width), lambda hg, bi, qi: (bi, 0, hg))
    st = pl.BlockSpec((1, 1, 2, heads, RET_DK, RET_DV), lambda hg, bi, qi: (bi, 0, 0, hg, 0, 0))
    in_specs = [q_tile, seq, seq, q_tile,
                pl.BlockSpec((heads, 2, V7X_SUBLANES, V7X_LANES), lambda hg, bi, qi: (hg, 0, 0, 0)),
                pl.BlockSpec((1, width), lambda hg, bi, qi: (0, hg))]
    args = [q, k, v, gate, lg, gn]
    y_shape = jax.ShapeDtypeStruct((b, n, RET_HEADS * RET_DV), BF16)
    if state0 is not None:
        in_specs.append(st)
        args.append(state0)
        out_shape, out_specs = y_shape, q_tile
    else:
        out_shape = (y_shape, jax.ShapeDtypeStruct((b, 1, 2, RET_HEADS, RET_DK, RET_DV), F32))
        out_specs = (q_tile, st)
    return pl.pallas_call(
        functools.partial(_ret_kernel, n=n, tq=tq, heads=heads, has_state=state0 is not None),
        out_shape=out_shape,
        grid=(RET_HEADS // heads, b, n // tq),
        in_specs=in_specs,
        out_specs=out_specs,
        scratch_shapes=[pltpu.VMEM((heads, n // tq, tq, n), F32)],
        compiler_params=_params(("arbitrary", "arbitrary", "arbitrary")),
        name="retention_state" if state0 is not None else "retention",
    )(*args)


def kernel(x_prompt, x_sample, cache_k, cache_v, state_ret, c, c_ctx, w_mod, b_mod, norm_g,
           attn_w_qkv, attn_q_norm, attn_k_norm, attn_w_o, ret_w_qkvg, ret_decay_logit, ret_gn_w,
           ret_w_o, mlp_w1, mlp_w2, final_norm_g):
    bp, sp, d = x_prompt.shape
    bs, ss, _ = x_sample.shape
    depth = w_mod.shape[0]
    assert d == D_MODEL and depth == 2 and 1 + bs <= MOD_ROWS
    assert (bp * sp) % POST_ROW_TILE == 0 and ss % POST_ROW_TILE == 0 and POST_ROW_TILE % ROW_TILE == 0

    cond = jnp.concatenate([c_ctx[None, :], c, jnp.zeros((MOD_ROWS - 1 - bs, d), F32)], axis=0)
    mod = _modulation(cond, w_mod, b_mod)
    mod = mod.reshape(depth, MOD_ROWS, 1, 6 * d)
    ctx_mod = (0, 0)
    lat_mod = (ss, 1)

    xp = x_prompt.reshape(bp * sp, d)
    xs = x_sample.reshape(bs * ss, d)
    nkv = N_KV_HEADS * HEAD_DIM

    g0 = norm_g[0, 0][None, :]
    g1 = norm_g[0, 1][None, :]
    wqkv = attn_w_qkv[0].astype(BF16)
    qn = attn_q_norm[0][None, :]
    kn = attn_k_norm[0][None, :]

    (qp, kp, vp), (w1,) = _attn_in(xp, mod[0], ctx_mod, g0, wqkv, qn, kn, None, PROJ_SUB_TILE,
                                   casts=[(mlp_w1, 0)])
    reorder = np.zeros((HEAD_DIM, HEAD_DIM), np.float32)
    reorder[_PAIRED_ORDER, np.arange(HEAD_DIM)] = 1.0
    reorder = jnp.asarray(reorder, BF16)
    (qs, ks, vs), (w2, wo) = _attn_in(xs, mod[0], lat_mod, g0, wqkv, qn[:, _PAIRED_ORDER], kn[:, _PAIRED_ORDER],
                                      _rope_tables(ss, HEAD_DIM, True) + (reorder,), ROW_TILE,
                                      casts=[(mlp_w2, 0), (attn_w_o, 0)])

    ap, _ = _attention(qp.reshape(bp, sp, -1), kp.reshape(bp, sp * N_KV_HEADS, HEAD_DIM),
                       vp.reshape(bp, sp * N_KV_HEADS, HEAD_DIM))
    xp, _ = _post(ap.reshape(bp * sp, -1), xp, mod[0], ctx_mod, g1, wo, w1, w2)
    cache = (cache_k, cache_v, reorder)
    as_, (wr,) = _attention(qs.reshape(bs, ss, -1), ks.reshape(bs, ss, nkv), vs.reshape(bs, ss, nkv), cache,
                            casts=[(ret_w_qkvg, 0)])
    xs, _ = _post(as_.reshape(bs * ss, -1), xs, mod[0], lat_mod, g1, wo, w1, w2)

    g0 = norm_g[1, 0][None, :]
    g1 = norm_g[1, 1][None, :]
    gn = ret_gn_w[0][None, :]
    gf = final_norm_g[None, :]
    lg = jnp.broadcast_to(ret_decay_logit[0].T[:, :, None, None],
                          (RET_HEADS, 2, V7X_SUBLANES, V7X_LANES))
    hw = RET_HEADS * RET_DK

    (q, k, v, gate), (w1,) = _ret_in(xp, mod[1], ctx_mod, g0, wr, None, PROJ_SUB_TILE, casts=[(mlp_w1, 1)])
    yp, new_state = _retention(q.reshape(bp, sp, hw), k.reshape(bp, sp, hw), v.reshape(bp, sp, hw),
                               gate.reshape(bp, sp, hw), lg, gn, None, RET_HEADS, sp)
    (q, k, v, gate), (w2, wo) = _ret_in(xs, mod[1], lat_mod, g0, wr, _rope_tables(ss, RET_DK, False),
                                        ROW_TILE, casts=[(mlp_w2, 1), (ret_w_o, 0)])
    ys = _retention(q.reshape(bs, ss, hw), k.reshape(bs, ss, hw), v.reshape(bs, ss, hw),
                    gate.reshape(bs, ss, hw), lg, gn, state_ret, RET_STATE_HEADS, RET_Q_TILE)
    y_prompt, _ = _post(yp.reshape(bp * sp, hw), xp, mod[1], ctx_mod, g1, wo, w1, w2, gf)
    y_sample, _ = _post(ys.reshape(bs * ss, hw), xs, mod[1], lat_mod, g1, wo, w1, w2, gf)

    return (y_prompt.reshape(bp, sp, d),
            y_sample.reshape(bs, ss, d),
            kp.reshape(bp, 1, sp, N_KV_HEADS, HEAD_DIM),
            vp.reshape(bp, 1, sp, N_KV_HEADS, HEAD_DIM),
            new_state)
```

```python
import functools

import numpy as np
import jax
import jax.numpy as jnp
from jax import lax
from jax.experimental import pallas as pl
from jax.experimental.pallas import tpu as pltpu

F32 = jnp.float32
BF16 = jnp.bfloat16

D_MODEL = 1024
GRID_W = 64
N_HEADS = 8
N_KV_HEADS = 2
HEAD_DIM = 128
GROUP = N_HEADS // N_KV_HEADS
ROPE_BASE = 10000.0
RET_HEADS = 4
RET_DK = 256
RET_DV = 256
D_FF = 4 * D_MODEL
Q_SCALE = HEAD_DIM ** -0.5 * float(np.log2(np.e))
EPS = 1e-6
QKV_W = (N_HEADS + 2 * N_KV_HEADS) * HEAD_DIM
RET_W = 2 * RET_HEADS * RET_DK + 2 * RET_HEADS * RET_DV

V7X_LANES = 128
V7X_SUBLANES = 8
MOD_ROWS = 16
ROW_TILE = 1024
PROJ_SUB_TILE = 512
POST_ROW_TILE = 1024
POST_SUB_TILE = 512
POST_FF_TILE = 1024
ATTN_Q_TILE = 512
RET_Q_TILE = 1024
RET_SUB_TILE = 256
RET_STATE_HEADS = 2
VMEM_LIMIT = 56 * 1024 * 1024


def _params(sem, vmem=VMEM_LIMIT):
    return pltpu.CompilerParams(dimension_semantics=sem, vmem_limit_bytes=vmem)


def _const_spec(shape):
    nd = len(shape)
    return pl.BlockSpec(shape, lambda *_: (0,) * nd, pipeline_mode=pl.Buffered(1))


def _sigmoid(x):
    return 1.0 / (1.0 + jnp.exp(-x))


def _cast_plumbing(casts, steps):
    in_specs, out_specs, out_shapes = [], [], []
    for arr, layer in casts:
        _, r, c = arr.shape
        slab = r // steps
        assert slab * steps == r and slab % 16 == 0
        in_specs.append(pl.BlockSpec((1, slab, c), functools.partial(lambda *idx, l: (l, idx[0], 0), l=layer)))
        out_specs.append(pl.BlockSpec((slab, c), lambda *idx: (idx[0], 0)))
        out_shapes.append(jax.ShapeDtypeStruct((r, c), BF16))
    return in_specs, out_specs, out_shapes


def _with_casts(body, n_in, n_out, n_cast):
    def kernel(*refs):
        ins = refs[:n_in]
        cast_in = refs[n_in:n_in + n_cast]
        outs = refs[n_in + n_cast:n_in + n_cast + n_out]
        cast_out = refs[n_in + n_cast + n_out:n_in + 2 * n_cast + n_out]
        scratch = refs[n_in + 2 * n_cast + n_out:]
        for src, dst in zip(cast_in, cast_out):
            dst[...] = src[0].astype(dst.dtype)
        body(*ins, *outs, *scratch)
    return kernel


def _mod_kernel(cond_ref, w_ref, b_ref, o_ref):
    s = cond_ref[...]
    s = (s * _sigmoid(s)).astype(BF16)
    o_ref[0] = jnp.dot(s, w_ref[0].astype(BF16), preferred_element_type=F32) + b_ref[0]


def _modulation(cond, w_mod, b_mod):
    depth = w_mod.shape[0]
    tn = 1024
    return pl.pallas_call(
        _mod_kernel,
        out_shape=jax.ShapeDtypeStruct((depth, MOD_ROWS, 6 * D_MODEL), F32),
        grid=(depth, 6 * D_MODEL // tn),
        in_specs=[
            pl.BlockSpec((MOD_ROWS, D_MODEL), lambda i, j: (0, 0)),
            pl.BlockSpec((1, D_MODEL, tn), lambda i, j: (i, 0, j)),
            pl.BlockSpec((1, 1, tn), lambda i, j: (i, 0, j)),
        ],
        out_specs=pl.BlockSpec((1, MOD_ROWS, tn), lambda i, j: (i, 0, j)),
        compiler_params=_params(("parallel", "parallel")),
        name="modulation",
    )(cond, w_mod, b_mod.reshape(depth, 1, 6 * D_MODEL))


def _norm_modulate(x, g, mod_ref, which):
    y = x * lax.rsqrt(jnp.mean(x * x, axis=-1, keepdims=True) + EPS) * g
    shift = mod_ref[0, :, (3 * which) * D_MODEL:(3 * which + 1) * D_MODEL]
    scale = mod_ref[0, :, (3 * which + 1) * D_MODEL:(3 * which + 2) * D_MODEL]
    return y * (1.0 + scale) + shift


def _mod_spec(mod_rows, tile):
    rows_per_mod_row, first_row = mod_rows
    if rows_per_mod_row:
        assert rows_per_mod_row % tile == 0
        tiles = rows_per_mod_row // tile
        return pl.BlockSpec((1, 1, 6 * D_MODEL), lambda i: (first_row + i // tiles, 0, 0))
    return pl.BlockSpec((1, 1, 6 * D_MODEL), lambda i: (first_row, 0, 0))


def _rope_tables(n, dim, paired):
    quarter = dim // 4
    rows = np.repeat(np.arange(n // GRID_W), GRID_W).astype(np.float64)
    cols = np.tile(np.arange(GRID_W), n // GRID_W).astype(np.float64)
    freqs = ROPE_BASE ** (-np.arange(quarter, dtype=np.float64) / quarter)
    ar = rows[:, None] * freqs
    ac = cols[:, None] * freqs
    if paired:
        cos = np.concatenate([np.cos(ar), np.cos(ac), np.cos(ar), np.cos(ac)], axis=1)
        sin = np.concatenate([-np.sin(ar), -np.sin(ac), np.sin(ar), np.sin(ac)], axis=1)
    else:
        cos = np.concatenate([np.cos(ar), np.cos(ar), np.cos(ac), np.cos(ac)], axis=1)
        sin = np.concatenate([-np.sin(ar), np.sin(ar), -np.sin(ac), np.sin(ac)], axis=1)
    return jnp.asarray(cos, F32), jnp.asarray(sin, F32)


_Q = HEAD_DIM // 4
_PAIRED_ORDER = np.concatenate([np.arange(0, _Q), np.arange(2 * _Q, 3 * _Q),
                                np.arange(_Q, 2 * _Q), np.arange(3 * _Q, 4 * _Q)])


def _swap_halves(t):
    return pltpu.roll(t, V7X_LANES // 2, axis=1)


def _attn_in_kernel(x_ref, mod_ref, g_ref, w_ref, qn_ref, kn_ref, *rest, use_rope):
    nq = N_HEADS * HEAD_DIM
    nk = N_KV_HEADS * HEAD_DIM
    if use_rope:
        cos_ref, sin_ref, reorder_ref, q_ref, k_ref, v_ref, wqk_ref = rest

        @pl.when(pl.program_id(0) == 0)
        def _():
            for head in range(N_HEADS + N_KV_HEADS):
                cols = slice(head * HEAD_DIM, (head + 1) * HEAD_DIM)
                wqk_ref[:, cols] = jnp.dot(w_ref[:, cols], reorder_ref[...],
                                           preferred_element_type=F32).astype(BF16)
    else:
        q_ref, k_ref, v_ref = rest
        wqk_ref = w_ref
    tm = PROJ_SUB_TILE
    n_sub = x_ref.shape[0] // tm
    pair_w = 2 * HEAD_DIM
    n_pairs = (N_HEADS + N_KV_HEADS) // 2
    qw = qn_ref[...] * Q_SCALE
    kw = kn_ref[...]
    if use_rope:
        ra = lax.broadcasted_iota(jnp.int32, (pair_w, pair_w), 0) // HEAD_DIM
        rb = lax.broadcasted_iota(jnp.int32, (pair_w, pair_w), 1) // HEAD_DIM
        head_sum = jnp.where(ra == rb, 1.0, 0.0).astype(BF16)

    def normed(sub):
        return _norm_modulate(x_ref[sub * tm:(sub + 1) * tm, :], g_ref[...], mod_ref, 0).astype(BF16)

    def put_kv(ref, sub, kh, val):
        if use_rope:
            ref[sub * tm:(sub + 1) * tm, kh * HEAD_DIM:(kh + 1) * HEAD_DIM] = val.astype(ref.dtype)
        else:
            ref[pl.ds(sub * tm * N_KV_HEADS + kh, tm, stride=N_KV_HEADS), :] = val.astype(ref.dtype)

    h_next = normed(0)
    for sub in range(n_sub):
        h = h_next
        if sub + 1 < n_sub:
            h_next = normed(sub + 1)
        rows = slice(sub * tm, (sub + 1) * tm)
        if use_rope:
            q_tabs = (qw * cos_ref[rows, :], _swap_halves(qw) * sin_ref[rows, :])
            k_tabs = (kw * cos_ref[rows, :], _swap_halves(kw) * sin_ref[rows, :])
        t2_next = jnp.dot(h, wqk_ref[:, 0:pair_w], preferred_element_type=F32)
        for pair in range(n_pairs):
            t2 = t2_next
            if pair + 1 < n_pairs:
                t2_next = jnp.dot(h, wqk_ref[:, (pair + 1) * pair_w:(pair + 2) * pair_w],
                                  preferred_element_type=F32)
            else:
                t2_next = jnp.dot(h, w_ref[:, nq + nk:], preferred_element_type=F32)
            if use_rope:
                ss2 = jnp.dot((t2 * t2).astype(BF16), head_sum, preferred_element_type=F32)
                r2 = lax.rsqrt(ss2 * (1.0 / HEAD_DIM) + EPS)
            for j in range(2):
                head = 2 * pair + j
                t = t2[:, j * HEAD_DIM:(j + 1) * HEAD_DIM]
                if use_rope:
                    r = r2[:, j * HEAD_DIM:(j + 1) * HEAD_DIM]
                    wc, ws = q_tabs if head < N_HEADS else k_tabs
                    t = (t * wc + _swap_halves(t) * ws) * r
                else:
                    r = lax.rsqrt(jnp.mean(t * t, axis=-1, keepdims=True) + EPS)
                    t = t * r * (qw if head < N_HEADS else kw)
                if head < N_HEADS:
                    q_ref[rows, head * HEAD_DIM:(head + 1) * HEAD_DIM] = t.astype(q_ref.dtype)
                else:
                    put_kv(k_ref, sub, head - N_HEADS, t)
        vv = t2_next
        for kh in range(N_KV_HEADS):
            put_kv(v_ref, sub, kh, vv[:, kh * HEAD_DIM:(kh + 1) * HEAD_DIM])


def _attn_in(x, mod, mod_rows, g, w, qn, kn, rope, tile, casts=()):
    rows = x.shape[0]
    nk = N_KV_HEADS * HEAD_DIM
    if rope is None:
        kv_shape = jax.ShapeDtypeStruct((rows * N_KV_HEADS, HEAD_DIM), F32)
        kv_spec = pl.BlockSpec((tile * N_KV_HEADS, HEAD_DIM), lambda i: (i, 0))
    else:
        kv_shape = jax.ShapeDtypeStruct((rows, nk), BF16)
        kv_spec = pl.BlockSpec((tile, nk), lambda i: (i, 0))
    in_specs = [
        pl.BlockSpec((tile, D_MODEL), lambda i: (i, 0)),
        _mod_spec(mod_rows, tile),
        _const_spec((1, D_MODEL)),
        _const_spec((D_MODEL, QKV_W)),
        _const_spec((1, HEAD_DIM)),
        _const_spec((1, HEAD_DIM)),
    ]
    args = [x, mod, g, w, qn, kn]
    scratch = []
    if rope is not None:
        cos, sin, reorder = rope
        seq_tiles = cos.shape[0] // tile
        in_specs += [pl.BlockSpec((tile, HEAD_DIM), lambda i: (i % seq_tiles, 0))] * 2
        in_specs.append(_const_spec((HEAD_DIM, HEAD_DIM)))
        args += [cos, sin, reorder]
        scratch = [pltpu.VMEM((D_MODEL, (N_HEADS + N_KV_HEADS) * HEAD_DIM), BF16)]
    steps = rows // tile
    c_in, c_out, c_shapes = _cast_plumbing(casts, steps)
    out_specs = [pl.BlockSpec((tile, N_HEADS * HEAD_DIM), lambda i: (i, 0)), kv_spec, kv_spec]
    out_shapes = [jax.ShapeDtypeStruct((rows, N_HEADS * HEAD_DIM), BF16), kv_shape, kv_shape]
    body = functools.partial(_attn_in_kernel, use_rope=rope is not None)
    outs = pl.pallas_call(
        _with_casts(body, len(args), len(out_shapes), len(casts)),
        out_shape=tuple(out_shapes + c_shapes),
        grid=(steps,),
        in_specs=in_specs + c_in,
        out_specs=tuple(out_specs + c_out),
        scratch_shapes=scratch,
        compiler_params=_params(("arbitrary",)),
        name="attn_in_rope" if rope is not None else "attn_in",
    )(*args, *[arr for arr, _ in casts])
    return outs[:3], list(outs[3:])


def _kv_head(ref, kv):
    if ref.dtype == BF16:
        return ref[0, :, kv * HEAD_DIM:(kv + 1) * HEAD_DIM]
    if len(ref.shape) == 5:
        return ref[0, 0, :, kv, :].astype(BF16)
    tokens = ref.shape[1] // N_KV_HEADS
    return ref[0, pl.ds(kv, tokens, stride=N_KV_HEADS), :].astype(BF16)


def _attn_kernel(q_ref, k_ref, v_ref, *rest, cached):
    if cached:
        ck_hbm, cv_hbm, reorder_ref, o_ref, ck_scr, cv_scr, stage, sems = rest
        b = pl.program_id(0)
        n_b = pl.num_programs(0)

        def head_copies(bi, slot):
            return [pltpu.make_async_copy(src.at[bi, 0, :, kv, :], stage.at[slot, t, kv], sems.at[slot, t, kv])
                    for t, src in enumerate((ck_hbm, cv_hbm)) for kv in range(N_KV_HEADS)]

        @pl.when(pl.program_id(1) == 0)
        def _():
            slot = b % 2

            @pl.when(b == 0)
            def _():
                for cp in head_copies(0, 0):
                    cp.start()

            @pl.when(b + 1 < n_b)
            def _():
                for cp in head_copies(b + 1, 1 - slot):
                    cp.start()

            for cp in head_copies(b, slot):
                cp.wait()
            for kv in range(N_KV_HEADS):
                ck_scr[kv] = jnp.dot(stage[slot, 0, kv].astype(BF16), reorder_ref[...],
                                     preferred_element_type=F32).astype(BF16)
                cv_scr[kv] = stage[slot, 1, kv].astype(BF16)
    else:
        (o_ref,) = rest
    nt = (((1,), (1,)), ((), ()))
    ks = [_kv_head(k_ref, kv) for kv in range(N_KV_HEADS)]
    vs = [_kv_head(v_ref, kv) for kv in range(N_KV_HEADS)]

    def scores(head):
        kv = head // GROUP
        q = q_ref[0, :, head * HEAD_DIM:(head + 1) * HEAD_DIM]
        s = lax.dot_general(q, ks[kv], nt, preferred_element_type=F32)
        sc = lax.dot_general(q, ck_scr[kv], nt, preferred_element_type=F32) if cached else None
        return s, sc

    nxt = scores(0)
    for head in range(N_HEADS):
        kv = head // GROUP
        s, sc = nxt
        if head + 1 < N_HEADS:
            nxt = scores(head + 1)
        m = jnp.max(s, axis=-1, keepdims=True)
        if cached:
            m = jnp.maximum(m, jnp.max(sc, axis=-1, keepdims=True))
        p = jnp.exp2(s - m)
        l = jnp.sum(p, axis=-1, keepdims=True)
        o = jnp.dot(p.astype(BF16), vs[kv], preferred_element_type=F32)
        if cached:
            pc = jnp.exp2(sc - m)
            l = l + jnp.sum(pc, axis=-1, keepdims=True)
            o = o + jnp.dot(pc.astype(BF16), cv_scr[kv], preferred_element_type=F32)
        o_ref[0, :, head * HEAD_DIM:(head + 1) * HEAD_DIM] = (o / l).astype(o_ref.dtype)


def _attention(q, k, v, cache=None, casts=()):
    b, n, width = q.shape
    tq = min(ATTN_Q_TILE, n)
    in_specs = [pl.BlockSpec((1, tq, width), lambda bi, qi: (bi, qi, 0))]
    in_specs += [pl.BlockSpec((1,) + k.shape[1:], lambda bi, qi: (bi, 0, 0))] * 2
    args = [q, k, v]
    scratch = []
    if cache is not None:
        past = cache[0].shape[2]
        in_specs += [pl.BlockSpec(memory_space=pl.ANY)] * 2
        in_specs.append(pl.BlockSpec((HEAD_DIM, HEAD_DIM), lambda bi, qi: (0, 0)))
        args += list(cache)
        scratch = [pltpu.VMEM((N_KV_HEADS, past, HEAD_DIM), BF16)] * 2
        scratch += [pltpu.VMEM((2, 2, N_KV_HEADS, past, HEAD_DIM), F32),
                    pltpu.SemaphoreType.DMA((2, 2, N_KV_HEADS))]
    c_in, c_out, c_shapes = _cast_plumbing(casts, b)
    body = functools.partial(_attn_kernel, cached=cache is not None)
    outs = pl.pallas_call(
        _with_casts(body, len(args), 1, len(casts)),
        out_shape=tuple([jax.ShapeDtypeStruct((b, n, width), BF16)] + c_shapes),
        grid=(b, n // tq),
        in_specs=in_specs + c_in,
        out_specs=tuple([pl.BlockSpec((1, tq, width), lambda bi, qi: (bi, qi, 0))] + c_out),
        scratch_shapes=scratch,
        compiler_params=_params(("arbitrary", "arbitrary")),
        name="attention_cached" if cache is not None else "attention",
    )(*args, *[arr for arr, _ in casts])
    return outs[0], list(outs[1:])


def _post_kernel(a_ref, x_ref, mod_ref, g_ref, wo_hbm, w1_hbm, w2_hbm, *rest, final):
    if final:
        gf_ref, o_ref, wo_ref, w1_ref, w2_ref, sems = rest
    else:
        o_ref, wo_ref, w1_ref, w2_ref, sems = rest
    gate1 = mod_ref[0, :, 2 * D_MODEL:3 * D_MODEL]
    gate2 = mod_ref[0, :, 5 * D_MODEL:6 * D_MODEL]
    n_ff = D_FF // POST_FF_TILE

    def ff(f):
        return pl.ds(f * POST_FF_TILE, POST_FF_TILE)

    def weight_copies():
        cps = [pltpu.make_async_copy(wo_hbm, wo_ref, sems.at[0])]
        for f in range(n_ff):
            cps.append(pltpu.make_async_copy(w1_hbm.at[:, ff(f)], w1_ref.at[:, ff(f)], sems.at[1 + 2 * f]))
            cps.append(pltpu.make_async_copy(w2_hbm.at[ff(f), :], w2_ref.at[ff(f), :], sems.at[2 + 2 * f]))
        return cps

    def attn_residual(t):
        rows = slice(t * POST_SUB_TILE, (t + 1) * POST_SUB_TILE)
        x = x_ref[rows, :] + gate1 * jnp.dot(a_ref[rows, :], wo_ref[...], preferred_element_type=F32)
        return x, _norm_modulate(x, g_ref[...], mod_ref, 1).astype(BF16)

    def compute(copies):
        n_sub = x_ref.shape[0] // POST_SUB_TILE
        if copies:
            copies[0].wait()
        nxt = attn_residual(0)
        for t in range(n_sub):
            x, h = nxt
            if t + 1 < n_sub:
                nxt = attn_residual(t + 1)
            y = None
            for f in range(n_ff):
                if copies and t == 0:
                    copies[1 + 2 * f].wait()
                    copies[2 + 2 * f].wait()
                u = jnp.dot(h, w1_ref[:, f * POST_FF_TILE:(f + 1) * POST_FF_TILE], preferred_element_type=F32)
                u = jnp.square(jnp.maximum(u, 0.0)).astype(BF16)
                part = jnp.dot(u, w2_ref[f * POST_FF_TILE:(f + 1) * POST_FF_TILE, :],
                               preferred_element_type=F32)
                y = part if y is None else y + part
            x = x + gate2 * y
            if final:
                x = x * lax.rsqrt(jnp.mean(x * x, axis=-1, keepdims=True) + EPS) * gf_ref[...]
            o_ref[t * POST_SUB_TILE:(t + 1) * POST_SUB_TILE, :] = x

    @pl.when(pl.program_id(0) == 0)
    def _():
        copies = weight_copies()
        for cp in copies:
            cp.start()
        compute(copies)

    @pl.when(pl.program_id(0) != 0)
    def _():
        compute(None)


def _post(a, x, mod, mod_rows, g, wo, w1, w2, gf=None, casts=()):
    rows = x.shape[0]
    row_block = pl.BlockSpec((POST_ROW_TILE, D_MODEL), lambda i: (i, 0))
    in_specs = [
        row_block,
        row_block,
        _mod_spec(mod_rows, POST_ROW_TILE),
        _const_spec((1, D_MODEL)),
        pl.BlockSpec(memory_space=pl.ANY),
        pl.BlockSpec(memory_space=pl.ANY),
        pl.BlockSpec(memory_space=pl.ANY),
    ]
    args = [a, x, mod, g, wo, w1, w2]
    if gf is not None:
        in_specs.append(_const_spec((1, D_MODEL)))
        args.append(gf)
    steps = rows // POST_ROW_TILE
    c_in, c_out, c_shapes = _cast_plumbing(casts, steps)
    body = functools.partial(_post_kernel, final=gf is not None)
    outs = pl.pallas_call(
        _with_casts(body, len(args), 1, len(casts)),
        out_shape=tuple([jax.ShapeDtypeStruct((rows, D_MODEL), F32)] + c_shapes),
        grid=(steps,),
        in_specs=in_specs + c_in,
        out_specs=tuple([row_block] + c_out),
        scratch_shapes=[pltpu.VMEM((D_MODEL, D_MODEL), BF16),
                        pltpu.VMEM((D_MODEL, D_FF), BF16),
                        pltpu.VMEM((D_FF, D_MODEL), BF16),
                        pltpu.SemaphoreType.DMA((1 + 2 * (D_FF // POST_FF_TILE),))],
        compiler_params=_params(("arbitrary",)),
        name="post_final" if gf is not None else "post",
    )(*args, *[arr for arr, _ in casts])
    return outs[0], list(outs[1:])


def _ret_in_kernel(x_ref, mod_ref, g_ref, w_ref, *rest, use_rope):
    if use_rope:
        cos_ref, sin_ref, q_ref, k_ref, v_ref, gate_ref = rest
    else:
        q_ref, k_ref, v_ref, gate_ref = rest
    width = RET_HEADS * RET_DK
    tm = PROJ_SUB_TILE
    n_sub = x_ref.shape[0] // tm

    def normed(sub):
        return _norm_modulate(x_ref[sub * tm:(sub + 1) * tm, :], g_ref[...], mod_ref, 0).astype(BF16)

    h_next = normed(0)
    for sub in range(n_sub):
        h = h_next
        if sub + 1 < n_sub:
            h_next = normed(sub + 1)
        rows = slice(sub * tm, (sub + 1) * tm)
        for part, ref in enumerate((q_ref, k_ref, v_ref, gate_ref)):
            t = jnp.dot(h, w_ref[:, part * width:(part + 1) * width], preferred_element_type=F32)
            if part < 2 and use_rope:
                for c in range(width // V7X_LANES):
                    half = (c % (RET_DK // V7X_LANES)) * V7X_LANES
                    tc = t[:, c * V7X_LANES:(c + 1) * V7X_LANES]
                    tc = (tc * cos_ref[rows, half:half + V7X_LANES]
                          + _swap_halves(tc) * sin_ref[rows, half:half + V7X_LANES])
                    if part == 0:
                        tc = tc * RET_DK ** -0.5
                    ref[rows, c * V7X_LANES:(c + 1) * V7X_LANES] = tc.astype(ref.dtype)
            else:
                if part == 0:
                    t = t * RET_DK ** -0.5
                ref[rows, :] = t.astype(ref.dtype)


def _ret_in(x, mod, mod_rows, g, w, rope, tile, casts=()):
    rows = x.shape[0]
    width = RET_HEADS * RET_DK
    in_specs = [
        pl.BlockSpec((tile, D_MODEL), lambda i: (i, 0)),
        _mod_spec(mod_rows, tile),
        _const_spec((1, D_MODEL)),
        _const_spec((D_MODEL, RET_W)),
    ]
    args = [x, mod, g, w]
    if rope is not None:
        seq_tiles = rope[0].shape[0] // tile
        in_specs += [pl.BlockSpec((tile, RET_DK), lambda i: (i % seq_tiles, 0))] * 2
        args += list(rope)
    out_block = pl.BlockSpec((tile, width), lambda i: (i, 0))
    steps = rows // tile
    c_in, c_out, c_shapes = _cast_plumbing(casts, steps)
    body = functools.partial(_ret_in_kernel, use_rope=rope is not None)
    outs = pl.pallas_call(
        _with_casts(body, len(args), 4, len(casts)),
        out_shape=tuple([jax.ShapeDtypeStruct((rows, width), BF16)] * 3
                        + [jax.ShapeDtypeStruct((rows, width), F32)] + c_shapes),
        grid=(steps,),
        in_specs=in_specs + c_in,
        out_specs=tuple([out_block] * 4 + c_out),
        compiler_params=_params(("parallel",)),
        name="ret_in_rope" if rope is not None else "ret_in",
    )(*args, *[arr for arr, _ in casts])
    return outs[:4], list(outs[4:])


def _log_sigmoid(x):
    return jnp.minimum(x, 0.0) - jnp.log(1.0 + jnp.exp(-jnp.abs(x)))


def _ret_kernel(q_ref, k_ref, v_ref, gate_ref, lg_ref, gn_ref, *rest, n, tq, heads, has_state):
    if has_state:
        s0_ref, y_ref, d_ref = rest
    else:
        y_ref, st_ref, d_ref = rest
    b = pl.program_id(1)
    qi = pl.program_id(2)
    r0 = qi * tq
    nt = (((1,), (1,)), ((), ()))
    sub = min(tq, RET_SUB_TILE)
    lgs = [(_log_sigmoid(lg_ref[hh, 0, 0:1, 0:1]), _log_sigmoid(lg_ref[hh, 1, 0:1, 0:1]))
           for hh in range(heads)]

    @pl.when(b == 0)
    def _():
        ii = r0 + lax.broadcasted_iota(jnp.int32, (tq, n), 0)
        jj = lax.broadcasted_iota(jnp.int32, (tq, n), 1)
        diff = (ii - jj).astype(F32)
        for hh in range(heads):
            lg_f, lg_b = lgs[hh]
            d_ref[hh, qi] = jnp.exp(jnp.where(diff >= 0.0, lg_f, -lg_b) * diff)

    def head_cols(hh):
        return slice(hh * RET_DK, (hh + 1) * RET_DK)

    def scores(unit):
        hh, si = unit
        return lax.dot_general(q_ref[0, si * sub:(si + 1) * sub, head_cols(hh)], k_ref[0, :, head_cols(hh)],
                               nt, preferred_element_type=F32)

    units = [(hh, si) for hh in range(heads) for si in range(tq // sub)]
    nxt = scores(units[0])
    for ui, (hh, si) in enumerate(units):
        s = nxt
        if ui + 1 < len(units):
            nxt = scores(units[ui + 1])
        lg_f, lg_b = lgs[hh]
        cols = head_cols(hh)
        rows = slice(si * sub, (si + 1) * sub)
        v = v_ref[0, :, cols]
        p = (s * d_ref[hh, qi, rows, :]).astype(BF16)
        o = jnp.dot(p, v, preferred_element_type=F32)
        if has_state:
            q = q_ref[0, rows, cols]
            i_col = (r0 + si * sub + lax.broadcasted_iota(jnp.int32, (sub, 1), 0)).astype(F32)
            o = o + (jnp.dot(q, s0_ref[0, 0, 0, hh].astype(BF16), preferred_element_type=F32)
                     * jnp.exp(lg_f * (i_col + 1.0))
                     + jnp.dot(q, s0_ref[0, 0, 1, hh].astype(BF16), preferred_element_type=F32)
                     * jnp.exp(lg_b * (n - i_col)))
        elif si == 0:
            eye = jnp.where(lax.broadcasted_iota(jnp.int32, (RET_DK, RET_DK), 0)
                            == lax.broadcasted_iota(jnp.int32, (RET_DK, RET_DK), 1), 1.0, 0.0).astype(BF16)
            kt = lax.dot_general(eye, k_ref[0, :, cols], nt, preferred_element_type=F32)
            j_row = lax.broadcasted_iota(jnp.int32, (1, n), 1).astype(F32)
            st_ref[0, 0, 0, hh] = jnp.dot((kt * jnp.exp(lg_f * (n - 1.0 - j_row))).astype(BF16), v,
                                          preferred_element_type=F32)
            st_ref[0, 0, 1, hh] = jnp.dot((kt * jnp.exp(lg_b * j_row)).astype(BF16), v,
                                          preferred_element_type=F32)
        mu = jnp.mean(o, axis=-1, keepdims=True)
        oc = o - mu
        var = jnp.mean(oc * oc, axis=-1, keepdims=True)
        on = oc * lax.rsqrt(var + EPS) * gn_ref[:, cols]
        gt = gate_ref[0, rows, cols]
        y_ref[0, rows, cols] = (gt * _sigmoid(gt) * on).astype(y_ref.dtype)


def _retention(q, k, v, gate, lg, gn, state0, heads, tq):
    b, n, _ = q.shape
    assert n % tq == 0 and RET_HEADS % heads == 0 and (state0 is not None or tq == n)
    width = heads * RET_DK
    q_tile = pl.BlockSpec((1, tq, width), lambda hg, bi, qi: (bi, qi, hg))
    seq = pl.BlockSpec((1, n, width), lambda hg, bi, qi: (bi, 0, hg))
    st = pl.BlockSpec((1, 1, 2, heads, RET_DK, RET_DV), lambda hg, bi, qi: (bi, 0, 0, hg, 0, 0))
    in_specs = [q_tile, seq, seq, q_tile,
                pl.BlockSpec((heads, 2, V7X_SUBLANES, V7X_LANES), lambda hg, bi, qi: (hg, 0, 0, 0)),
                pl.BlockSpec((1, width), lambda hg, bi, qi: (0, hg))]
    args = [q, k, v, gate, lg, gn]
    y_shape = jax.ShapeDtypeStruct((b, n, RET_HEADS * RET_DV), BF16)
    if state0 is not None:
        in_specs.append(st)
        args.append(state0)
        out_shape, out_specs = y_shape, q_tile
    else:
        out_shape = (y_shape, jax.ShapeDtypeStruct((b, 1, 2, RET_HEADS, RET_DK, RET_DV), F32))
        out_specs = (q_tile, st)
    return pl.pallas_call(
        functools.partial(_ret_kernel, n=n, tq=tq, heads=heads, has_state=state0 is not None),
        out_shape=out_shape,
        grid=(RET_HEADS // heads, b, n // tq),
        in_specs=in_specs,
        out_specs=out_specs,
        scratch_shapes=[pltpu.VMEM((heads, n // tq, tq, n), F32)],
        compiler_params=_params(("arbitrary", "arbitrary", "arbitrary")),
        name="retention_state" if state0 is not None else "retention",
    )(*args)


def kernel(x_prompt, x_sample, cache_k, cache_v, state_ret, c, c_ctx, w_mod, b_mod, norm_g,
           attn_w_qkv, attn_q_norm, attn_k_norm, attn_w_o, ret_w_qkvg, ret_decay_logit, ret_gn_w,
           ret_w_o, mlp_w1, mlp_w2, final_norm_g):
    bp, sp, d = x_prompt.shape
    bs, ss, _ = x_sample.shape
    depth = w_mod.shape[0]
    assert d == D_MODEL and depth == 2 and 1 + bs <= MOD_ROWS
    assert (bp * sp) % POST_ROW_TILE == 0 and ss % POST_ROW_TILE == 0 and POST_ROW_TILE % ROW_TILE == 0

    cond = jnp.concatenate([c_ctx[None, :], c, jnp.zeros((MOD_ROWS - 1 - bs, d), F32)], axis=0)
    mod = _modulation(cond, w_mod, b_mod)
    mod = mod.reshape(depth, MOD_ROWS, 1, 6 * d)
    ctx_mod = (0, 0)
    lat_mod = (ss, 1)

    xp = x_prompt.reshape(bp * sp, d)
    xs = x_sample.reshape(bs * ss, d)
    nkv = N_KV_HEADS * HEAD_DIM

    g0 = norm_g[0, 0][None, :]
    g1 = norm_g[0, 1][None, :]
    wqkv = attn_w_qkv[0].astype(BF16)
    qn = attn_q_norm[0][None, :]
    kn = attn_k_norm[0][None, :]

    (qp, kp, vp), (w1,) = _attn_in(xp, mod[0], ctx_mod, g0, wqkv, qn, kn, None, PROJ_SUB_TILE,
                                   casts=[(mlp_w1, 0)])
    reorder = np.zeros((HEAD_DIM, HEAD_DIM), np.float32)
    reorder[_PAIRED_ORDER, np.arange(HEAD_DIM)] = 1.0
    reorder = jnp.asarray(reorder, BF16)
    (qs, ks, vs), (w2, wo) = _attn_in(xs, mod[0], lat_mod, g0, wqkv, qn[:, _PAIRED_ORDER], kn[:, _PAIRED_ORDER],
                                      _rope_tables(ss, HEAD_DIM, True) + (reorder,), ROW_TILE,
                                      casts=[(mlp_w2, 0), (attn_w_o, 0)])

    ap, _ = _attention(qp.reshape(bp, sp, -1), kp.reshape(bp, sp * N_KV_HEADS, HEAD_DIM),
                       vp.reshape(bp, sp * N_KV_HEADS, HEAD_DIM))
    xp, _ = _post(ap.reshape(bp * sp, -1), xp, mod[0], ctx_mod, g1, wo, w1, w2)
    cache = (cache_k, cache_v, reorder)
    as_, (wr,) = _attention(qs.reshape(bs, ss, -1), ks.reshape(bs, ss, nkv), vs.reshape(bs, ss, nkv), cache,
                            casts=[(ret_w_qkvg, 0)])
    xs, _ = _post(as_.reshape(bs * ss, -1), xs, mod[0], lat_mod, g1, wo, w1, w2)

    g0 = norm_g[1, 0][None, :]
    g1 = norm_g[1, 1][None, :]
    gn = ret_gn_w[0][None, :]
    gf = final_norm_g[None, :]
    lg = jnp.broadcast_to(ret_decay_logit[0].T[:, :, None, None],
                          (RET_HEADS, 2, V7X_SUBLANES, V7X_LANES))
    hw = RET_HEADS * RET_DK

    (q, k, v, gate), (w1,) = _ret_in(xp, mod[1], ctx_mod, g0, wr, None, PROJ_SUB_TILE, casts=[(mlp_w1, 1)])
    yp, new_state = _retention(q.reshape(bp, sp, hw), k.reshape(bp, sp, hw), v.reshape(bp, sp, hw),
                               gate.reshape(bp, sp, hw), lg, gn, None, RET_HEADS, sp)
    (q, k, v, gate), (w2, wo) = _ret_in(xs, mod[1], lat_mod, g0, wr, _rope_tables(ss, RET_DK, False),
                                        ROW_TILE, casts=[(mlp_w2, 1), (ret_w_o, 0)])
    ys = _retention(q.reshape(bs, ss, hw), k.reshape(bs, ss, hw), v.reshape(bs, ss, hw),
                    gate.reshape(bs, ss, hw), lg, gn, state_ret, RET_STATE_HEADS, RET_Q_TILE)
    y_prompt, _ = _post(yp.reshape(bp * sp, hw), xp, mod[1], ctx_mod, g1, wo, w1, w2, gf)
    y_sample, _ = _post(ys.reshape(bs * ss, hw), xs, mod[1], lat_mod, g1, wo, w1, w2, gf)

    return (y_prompt.reshape(bp, sp, d),
            y_sample.reshape(bs, ss, d),
            kp.reshape(bp, 1, sp, N_KV_HEADS, HEAD_DIM),
            vp.reshape(bp, 1, sp, N_KV_HEADS, HEAD_DIM),
            new_state)
```

```python
import functools

import numpy as np
import jax
import jax.numpy as jnp
from jax import lax
from jax.experimental import pallas as pl
from jax.experimental.pallas import tpu as pltpu

F32 = jnp.float32
BF16 = jnp.bfloat16

D_MODEL = 1024
GRID_W = 64
N_HEADS = 8
N_KV_HEADS = 2
HEAD_DIM = 128
GROUP = N_HEADS // N_KV_HEADS
ROPE_BASE = 10000.0
RET_HEADS = 4
RET_DK = 256
RET_DV = 256
D_FF = 4 * D_MODEL
Q_SCALE = HEAD_DIM ** -0.5 * float(np.log2(np.e))
EPS = 1e-6
QKV_W = (N_HEADS + 2 * N_KV_HEADS) * HEAD_DIM
RET_W = 2 * RET_HEADS * RET_DK + 2 * RET_HEADS * RET_DV

V7X_LANES = 128
V7X_SUBLANES = 8
ROW_TILE = 1024
PROJ_SUB_TILE = 512
POST_ROW_TILE = 1024
POST_SUB_TILE = 512
POST_FF_TILE = 1024
ATTN_Q_TILE = 512
RET_Q_TILE = 1024
RET_SUB_TILE = 256
RET_STATE_HEADS = 2
VMEM_LIMIT = 56 * 1024 * 1024


def _params(sem, vmem=VMEM_LIMIT):
    return pltpu.CompilerParams(dimension_semantics=sem, vmem_limit_bytes=vmem)


def _const_spec(shape):
    nd = len(shape)
    return pl.BlockSpec(shape, lambda *_: (0,) * nd, pipeline_mode=pl.Buffered(1))


def _sigmoid(x):
    return 1.0 / (1.0 + jnp.exp(-x))


def _cast_plumbing(casts, steps):
    in_specs, out_specs, out_shapes = [], [], []
    for arr, layer in casts:
        _, r, c = arr.shape
        slab = r // steps
        assert slab * steps == r and slab % 16 == 0
        in_specs.append(pl.BlockSpec((1, slab, c), functools.partial(lambda *idx, l: (l, idx[0], 0), l=layer)))
        out_specs.append(pl.BlockSpec((slab, c), lambda *idx: (idx[0], 0)))
        out_shapes.append(jax.ShapeDtypeStruct((r, c), BF16))
    return in_specs, out_specs, out_shapes


def _with_casts(body, n_in, n_out, n_cast):
    def kernel(*refs):
        ins = refs[:n_in]
        cast_in = refs[n_in:n_in + n_cast]
        outs = refs[n_in + n_cast:n_in + n_cast + n_out]
        cast_out = refs[n_in + n_cast + n_out:n_in + 2 * n_cast + n_out]
        scratch = refs[n_in + 2 * n_cast + n_out:]
        for src, dst in zip(cast_in, cast_out):
            dst[...] = src[0].astype(dst.dtype)
        body(*ins, *outs, *scratch)
    return kernel


def _mod_kernel(c_ref, cctx_ref, w_ref, b_ref, o_ref):
    cond = jnp.concatenate([c_ref[...], jnp.broadcast_to(cctx_ref[...], (V7X_SUBLANES, D_MODEL))], axis=0)
    s = (cond * _sigmoid(cond)).astype(BF16)
    bias = b_ref[pl.ds(pl.program_id(0), 1), :]
    o_ref[0, :, 0, :] = jnp.dot(s, w_ref[0].astype(BF16), preferred_element_type=F32) + bias


def _modulation(c, c_ctx, w_mod, b_mod, casts=()):
    depth = w_mod.shape[0]
    rows = c.shape[0] + V7X_SUBLANES
    tn = 1024
    c_in, c_out, c_shapes = _cast_plumbing(casts, depth)
    outs = pl.pallas_call(
        _with_casts(_mod_kernel, 4, 1, len(casts)),
        out_shape=tuple([jax.ShapeDtypeStruct((depth, rows, 1, 6 * D_MODEL), F32)] + c_shapes),
        grid=(depth, 6 * D_MODEL // tn),
        in_specs=[
            pl.BlockSpec(c.shape, lambda i, j: (0, 0)),
            pl.BlockSpec((1, D_MODEL), lambda i, j: (0, 0)),
            pl.BlockSpec((1, D_MODEL, tn), lambda i, j: (i, 0, j)),
            pl.BlockSpec((depth, tn), lambda i, j: (0, j)),
        ] + c_in,
        out_specs=tuple([pl.BlockSpec((1, rows, 1, tn), lambda i, j: (i, 0, 0, j))] + c_out),
        compiler_params=_params(("arbitrary", "arbitrary")),
        name="modulation",
    )(c, c_ctx.reshape(1, D_MODEL), w_mod, b_mod, *[arr for arr, _ in casts])
    return outs[0], list(outs[1:])


def _norm_modulate(x, g, mod_ref, which):
    y = x * lax.rsqrt(jnp.mean(x * x, axis=-1, keepdims=True) + EPS) * g
    shift = mod_ref[0, 0, :, (3 * which) * D_MODEL:(3 * which + 1) * D_MODEL]
    scale = mod_ref[0, 0, :, (3 * which + 1) * D_MODEL:(3 * which + 2) * D_MODEL]
    return y * (1.0 + scale) + shift


def _mod_spec(mod_rows, tile):
    layer, rows_per_mod_row, first_row = mod_rows
    if rows_per_mod_row:
        assert rows_per_mod_row % tile == 0
        tiles = rows_per_mod_row // tile
        return pl.BlockSpec((1, 1, 1, 6 * D_MODEL), lambda i: (layer, first_row + i // tiles, 0, 0))
    return pl.BlockSpec((1, 1, 1, 6 * D_MODEL), lambda i: (layer, first_row, 0, 0))


def _rope_tables(n, dim, paired):
    quarter = dim // 4
    rows = np.repeat(np.arange(n // GRID_W), GRID_W).astype(np.float64)
    cols = np.tile(np.arange(GRID_W), n // GRID_W).astype(np.float64)
    freqs = ROPE_BASE ** (-np.arange(quarter, dtype=np.float64) / quarter)
    ar = rows[:, None] * freqs
    ac = cols[:, None] * freqs
    if paired:
        cos = np.concatenate([np.cos(ar), np.cos(ac), np.cos(ar), np.cos(ac)], axis=1)
        sin = np.concatenate([-np.sin(ar), -np.sin(ac), np.sin(ar), np.sin(ac)], axis=1)
    else:
        cos = np.concatenate([np.cos(ar), np.cos(ar), np.cos(ac), np.cos(ac)], axis=1)
        sin = np.concatenate([-np.sin(ar), np.sin(ar), -np.sin(ac), np.sin(ac)], axis=1)
    return jnp.asarray(cos, F32), jnp.asarray(sin, F32)


_Q = HEAD_DIM // 4
_PAIRED_ORDER = np.concatenate([np.arange(0, _Q), np.arange(2 * _Q, 3 * _Q),
                                np.arange(_Q, 2 * _Q), np.arange(3 * _Q, 4 * _Q)])


def _paired_lanes(w):
    lane = lax.broadcasted_iota(jnp.int32, w.shape, 1)
    from_c = pltpu.roll(w, V7X_LANES - _Q, axis=1)
    from_b = pltpu.roll(w, _Q, axis=1)
    return jnp.where((lane >= _Q) & (lane < 2 * _Q), from_c,
                     jnp.where((lane >= 2 * _Q) & (lane < 3 * _Q), from_b, w))


def _swap_halves(t):
    return pltpu.roll(t, V7X_LANES // 2, axis=1)


def _attn_in_kernel(x_ref, mod_ref, g_ref, w_ref, qn_ref, kn_ref, *rest, use_rope):
    nq = N_HEADS * HEAD_DIM
    nk = N_KV_HEADS * HEAD_DIM
    if use_rope:
        cos_ref, sin_ref, reorder_ref, q_ref, k_ref, v_ref, wqk_ref = rest

        @pl.when(pl.program_id(0) == 0)
        def _():
            for head in range(N_HEADS + N_KV_HEADS):
                cols = slice(head * HEAD_DIM, (head + 1) * HEAD_DIM)
                wqk_ref[:, cols] = jnp.dot(w_ref[:, cols], reorder_ref[...],
                                           preferred_element_type=F32).astype(BF16)
    else:
        q_ref, k_ref, v_ref = rest
        wqk_ref = w_ref
    tm = PROJ_SUB_TILE
    n_sub = x_ref.shape[0] // tm
    pair_w = 2 * HEAD_DIM
    n_pairs = (N_HEADS + N_KV_HEADS) // 2
    qw = qn_ref[...] * Q_SCALE
    kw = kn_ref[...]
    if use_rope:
        qw = _paired_lanes(qw)
        kw = _paired_lanes(kw)
        ra = lax.broadcasted_iota(jnp.int32, (pair_w, pair_w), 0) // HEAD_DIM
        rb = lax.broadcasted_iota(jnp.int32, (pair_w, pair_w), 1) // HEAD_DIM
        head_sum = jnp.where(ra == rb, 1.0, 0.0).astype(BF16)

    def normed(sub):
        return _norm_modulate(x_ref[sub * tm:(sub + 1) * tm, :], g_ref[...], mod_ref, 0).astype(BF16)

    def put_kv(ref, sub, kh, val):
        if use_rope:
            ref[sub * tm:(sub + 1) * tm, kh * HEAD_DIM:(kh + 1) * HEAD_DIM] = val.astype(ref.dtype)
        else:
            ref[pl.ds(sub * tm * N_KV_HEADS + kh, tm, stride=N_KV_HEADS), :] = val.astype(ref.dtype)

    h_next = normed(0)
    for sub in range(n_sub):
        h = h_next
        if sub + 1 < n_sub:
            h_next = normed(sub + 1)
        rows = slice(sub * tm, (sub + 1) * tm)
        if use_rope:
            q_tabs = (qw * cos_ref[rows, :], _swap_halves(qw) * sin_ref[rows, :])
            k_tabs = (kw * cos_ref[rows, :], _swap_halves(kw) * sin_ref[rows, :])
        t2_next = jnp.dot(h, wqk_ref[:, 0:pair_w], preferred_element_type=F32)
        for pair in range(n_pairs):
            t2 = t2_next
            if pair + 1 < n_pairs:
                t2_next = jnp.dot(h, wqk_ref[:, (pair + 1) * pair_w:(pair + 2) * pair_w],
                                  preferred_element_type=F32)
            else:
                t2_next = jnp.dot(h, w_ref[:, nq + nk:], preferred_element_type=F32)
            if use_rope:
                ss2 = jnp.dot((t2 * t2).astype(BF16), head_sum, preferred_element_type=F32)
                r2 = lax.rsqrt(ss2 * (1.0 / HEAD_DIM) + EPS)
            for j in range(2):
                head = 2 * pair + j
                t = t2[:, j * HEAD_DIM:(j + 1) * HEAD_DIM]
                if use_rope:
                    r = r2[:, j * HEAD_DIM:(j + 1) * HEAD_DIM]
                    wc, ws = q_tabs if head < N_HEADS else k_tabs
                    t = (t * wc + _swap_halves(t) * ws) * r
                else:
                    r = lax.rsqrt(jnp.mean(t * t, axis=-1, keepdims=True) + EPS)
                    t = t * r * (qw if head < N_HEADS else kw)
                if head < N_HEADS:
                    q_ref[rows, head * HEAD_DIM:(head + 1) * HEAD_DIM] = t.astype(q_ref.dtype)
                else:
                    put_kv(k_ref, sub, head - N_HEADS, t)
        vv = t2_next
        for kh in range(N_KV_HEADS):
            put_kv(v_ref, sub, kh, vv[:, kh * HEAD_DIM:(kh + 1) * HEAD_DIM])


def _attn_in(x, mod, mod_rows, g, w, qn, kn, rope, tile, casts=()):
    rows = x.shape[0]
    nk = N_KV_HEADS * HEAD_DIM
    if rope is None:
        kv_shape = jax.ShapeDtypeStruct((rows * N_KV_HEADS, HEAD_DIM), F32)
        kv_spec = pl.BlockSpec((tile * N_KV_HEADS, HEAD_DIM), lambda i: (i, 0))
    else:
        kv_shape = jax.ShapeDtypeStruct((rows, nk), BF16)
        kv_spec = pl.BlockSpec((tile, nk), lambda i: (i, 0))
    in_specs = [
        pl.BlockSpec((tile, D_MODEL), lambda i: (i, 0)),
        _mod_spec(mod_rows, tile),
        _const_spec((1, D_MODEL)),
        _const_spec((D_MODEL, QKV_W)),
        _const_spec((1, HEAD_DIM)),
        _const_spec((1, HEAD_DIM)),
    ]
    args = [x, mod, g, w, qn, kn]
    scratch = []
    if rope is not None:
        cos, sin, reorder = rope
        seq_tiles = cos.shape[0] // tile
        in_specs += [pl.BlockSpec((tile, HEAD_DIM), lambda i: (i % seq_tiles, 0))] * 2
        in_specs.append(_const_spec((HEAD_DIM, HEAD_DIM)))
        args += [cos, sin, reorder]
        scratch = [pltpu.VMEM((D_MODEL, (N_HEADS + N_KV_HEADS) * HEAD_DIM), BF16)]
    steps = rows // tile
    c_in, c_out, c_shapes = _cast_plumbing(casts, steps)
    out_specs = [pl.BlockSpec((tile, N_HEADS * HEAD_DIM), lambda i: (i, 0)), kv_spec, kv_spec]
    out_shapes = [jax.ShapeDtypeStruct((rows, N_HEADS * HEAD_DIM), BF16), kv_shape, kv_shape]
    body = functools.partial(_attn_in_kernel, use_rope=rope is not None)
    outs = pl.pallas_call(
        _with_casts(body, len(args), len(out_shapes), len(casts)),
        out_shape=tuple(out_shapes + c_shapes),
        grid=(steps,),
        in_specs=in_specs + c_in,
        out_specs=tuple(out_specs + c_out),
        scratch_shapes=scratch,
        compiler_params=_params(("arbitrary",)),
        name="attn_in_rope" if rope is not None else "attn_in",
    )(*args, *[arr for arr, _ in casts])
    return outs[:3], list(outs[3:])


def _kv_head(ref, kv):
    if ref.dtype == BF16:
        return ref[0, :, kv * HEAD_DIM:(kv + 1) * HEAD_DIM]
    if len(ref.shape) == 5:
        return ref[0, 0, :, kv, :].astype(BF16)
    tokens = ref.shape[1] // N_KV_HEADS
    return ref[0, pl.ds(kv, tokens, stride=N_KV_HEADS), :].astype(BF16)


def _attn_kernel(q_ref, k_ref, v_ref, *rest, cached):
    if cached:
        ck_hbm, cv_hbm, reorder_ref, o_ref, ck_scr, cv_scr, stage, sems = rest
        b = pl.program_id(0)
        n_b = pl.num_programs(0)

        def head_copies(bi, slot):
            return [pltpu.make_async_copy(src.at[bi, 0, :, kv, :], stage.at[slot, t, kv], sems.at[slot, t, kv])
                    for t, src in enumerate((ck_hbm, cv_hbm)) for kv in range(N_KV_HEADS)]

        @pl.when(pl.program_id(1) == 0)
        def _():
            slot = b % 2

            @pl.when(b == 0)
            def _():
                for cp in head_copies(0, 0):
                    cp.start()

            @pl.when(b + 1 < n_b)
            def _():
                for cp in head_copies(b + 1, 1 - slot):
                    cp.start()

            for cp in head_copies(b, slot):
                cp.wait()
            for kv in range(N_KV_HEADS):
                ck_scr[kv] = jnp.dot(stage[slot, 0, kv].astype(BF16), reorder_ref[...],
                                     preferred_element_type=F32).astype(BF16)
                cv_scr[kv] = stage[slot, 1, kv].astype(BF16)
    else:
        (o_ref,) = rest
    nt = (((1,), (1,)), ((), ()))
    ks = [_kv_head(k_ref, kv) for kv in range(N_KV_HEADS)]
    vs = [_kv_head(v_ref, kv) for kv in range(N_KV_HEADS)]

    def scores(head):
        kv = head // GROUP
        q = q_ref[0, :, head * HEAD_DIM:(head + 1) * HEAD_DIM]
        s = lax.dot_general(q, ks[kv], nt, preferred_element_type=F32)
        sc = lax.dot_general(q, ck_scr[kv], nt, preferred_element_type=F32) if cached else None
        return s, sc

    nxt = scores(0)
    for head in range(N_HEADS):
        kv = head // GROUP
        s, sc = nxt
        if head + 1 < N_HEADS:
            nxt = scores(head + 1)
        m = jnp.max(s, axis=-1, keepdims=True)
        if cached:
            m = jnp.maximum(m, jnp.max(sc, axis=-1, keepdims=True))
        p = jnp.exp2(s - m)
        l = jnp.sum(p, axis=-1, keepdims=True)
        o = jnp.dot(p.astype(BF16), vs[kv], preferred_element_type=F32)
        if cached:
            pc = jnp.exp2(sc - m)
            l = l + jnp.sum(pc, axis=-1, keepdims=True)
            o = o + jnp.dot(pc.astype(BF16), cv_scr[kv], preferred_element_type=F32)
        o_ref[0, :, head * HEAD_DIM:(head + 1) * HEAD_DIM] = (o / l).astype(o_ref.dtype)


def _attention(q, k, v, cache=None, casts=()):
    b, n, width = q.shape
    tq = min(ATTN_Q_TILE, n)
    in_specs = [pl.BlockSpec((1, tq, width), lambda bi, qi: (bi, qi, 0))]
    in_specs += [pl.BlockSpec((1,) + k.shape[1:], lambda bi, qi: (bi, 0, 0))] * 2
    args = [q, k, v]
    scratch = []
    if cache is not None:
        past = cache[0].shape[2]
        in_specs += [pl.BlockSpec(memory_space=pl.ANY)] * 2
        in_specs.append(pl.BlockSpec((HEAD_DIM, HEAD_DIM), lambda bi, qi: (0, 0)))
        args += list(cache)
        scratch = [pltpu.VMEM((N_KV_HEADS, past, HEAD_DIM), BF16)] * 2
        scratch += [pltpu.VMEM((2, 2, N_KV_HEADS, past, HEAD_DIM), F32),
                    pltpu.SemaphoreType.DMA((2, 2, N_KV_HEADS))]
    c_in, c_out, c_shapes = _cast_plumbing(casts, b)
    body = functools.partial(_attn_kernel, cached=cache is not None)
    outs = pl.pallas_call(
        _with_casts(body, len(args), 1, len(casts)),
        out_shape=tuple([jax.ShapeDtypeStruct((b, n, width), BF16)] + c_shapes),
        grid=(b, n // tq),
        in_specs=in_specs + c_in,
        out_specs=tuple([pl.BlockSpec((1, tq, width), lambda bi, qi: (bi, qi, 0))] + c_out),
        scratch_shapes=scratch,
        compiler_params=_params(("arbitrary", "arbitrary")),
        name="attention_cached" if cache is not None else "attention",
    )(*args, *[arr for arr, _ in casts])
    return outs[0], list(outs[1:])


def _post_kernel(a_ref, x_ref, mod_ref, g_ref, wo_ref, w1_ref, w2_ref, *rest, final):
    if final:
        gf_ref, o_ref = rest
    else:
        (o_ref,) = rest
    gate1 = mod_ref[0, 0, :, 2 * D_MODEL:3 * D_MODEL]
    gate2 = mod_ref[0, 0, :, 5 * D_MODEL:6 * D_MODEL]
    n_ff = D_FF // POST_FF_TILE

    def attn_residual(t):
        rows = slice(t * POST_SUB_TILE, (t + 1) * POST_SUB_TILE)
        x = x_ref[rows, :] + gate1 * jnp.dot(a_ref[rows, :], wo_ref[...], preferred_element_type=F32)
        return x, _norm_modulate(x, g_ref[...], mod_ref, 1).astype(BF16)

    n_sub = x_ref.shape[0] // POST_SUB_TILE
    nxt = attn_residual(0)
    for t in range(n_sub):
        x, h = nxt
        if t + 1 < n_sub:
            nxt = attn_residual(t + 1)
        y = None
        for f in range(n_ff):
            u = jnp.dot(h, w1_ref[:, f * POST_FF_TILE:(f + 1) * POST_FF_TILE], preferred_element_type=F32)
            u = jnp.square(jnp.maximum(u, 0.0)).astype(BF16)
            part = jnp.dot(u, w2_ref[f * POST_FF_TILE:(f + 1) * POST_FF_TILE, :], preferred_element_type=F32)
            y = part if y is None else y + part
        x = x + gate2 * y
        if final:
            x = x * lax.rsqrt(jnp.mean(x * x, axis=-1, keepdims=True) + EPS) * gf_ref[...]
        o_ref[t * POST_SUB_TILE:(t + 1) * POST_SUB_TILE, :] = x


def _post(a, x, mod, mod_rows, g, wo, w1, w2, gf=None, casts=()):
    rows = x.shape[0]
    row_block = pl.BlockSpec((POST_ROW_TILE, D_MODEL), lambda i: (i, 0))
    in_specs = [
        row_block,
        row_block,
        _mod_spec(mod_rows, POST_ROW_TILE),
        _const_spec((1, D_MODEL)),
        _const_spec((D_MODEL, D_MODEL)),
        _const_spec((D_MODEL, D_FF)),
        _const_spec((D_FF, D_MODEL)),
    ]
    args = [a, x, mod, g, wo, w1, w2]
    if gf is not None:
        in_specs.append(_const_spec((1, D_MODEL)))
        args.append(gf)
    steps = rows // POST_ROW_TILE
    c_in, c_out, c_shapes = _cast_plumbing(casts, steps)
    body = functools.partial(_post_kernel, final=gf is not None)
    outs = pl.pallas_call(
        _with_casts(body, len(args), 1, len(casts)),
        out_shape=tuple([jax.ShapeDtypeStruct((rows, D_MODEL), F32)] + c_shapes),
        grid=(steps,),
        in_specs=in_specs + c_in,
        out_specs=tuple([row_block] + c_out),
        compiler_params=_params(("parallel",)),
        name="post_final" if gf is not None else "post",
    )(*args, *[arr for arr, _ in casts])
    return outs[0], list(outs[1:])


def _ret_in_kernel(x_ref, mod_ref, g_ref, w_ref, *rest, use_rope):
    if use_rope:
        cos_ref, sin_ref, q_ref, k_ref, v_ref, gate_ref = rest
    else:
        q_ref, k_ref, v_ref, gate_ref = rest
    width = RET_HEADS * RET_DK
    tm = PROJ_SUB_TILE
    n_sub = x_ref.shape[0] // tm

    def normed(sub):
        return _norm_modulate(x_ref[sub * tm:(sub + 1) * tm, :], g_ref[...], mod_ref, 0).astype(BF16)

    h_next = normed(0)
    for sub in range(n_sub):
        h = h_next
        if sub + 1 < n_sub:
            h_next = normed(sub + 1)
        rows = slice(sub * tm, (sub + 1) * tm)
        for part, ref in enumerate((q_ref, k_ref, v_ref, gate_ref)):
            t = jnp.dot(h, w_ref[:, part * width:(part + 1) * width], preferred_element_type=F32)
            if part < 2 and use_rope:
                for c in range(width // V7X_LANES):
                    half = (c % (RET_DK // V7X_LANES)) * V7X_LANES
                    tc = t[:, c * V7X_LANES:(c + 1) * V7X_LANES]
                    tc = (tc * cos_ref[rows, half:half + V7X_LANES]
                          + _swap_halves(tc) * sin_ref[rows, half:half + V7X_LANES])
                    if part == 0:
                        tc = tc * RET_DK ** -0.5
                    ref[rows, c * V7X_LANES:(c + 1) * V7X_LANES] = tc.astype(ref.dtype)
            else:
                if part == 0:
                    t = t * RET_DK ** -0.5
                ref[rows, :] = t.astype(ref.dtype)


def _ret_in(x, mod, mod_rows, g, w, rope, tile, casts=()):
    rows = x.shape[0]
    width = RET_HEADS * RET_DK
    in_specs = [
        pl.BlockSpec((tile, D_MODEL), lambda i: (i, 0)),
        _mod_spec(mod_rows, tile),
        _const_spec((1, D_MODEL)),
        _const_spec((D_MODEL, RET_W)),
    ]
    args = [x, mod, g, w]
    if rope is not None:
        seq_tiles = rope[0].shape[0] // tile
        in_specs += [pl.BlockSpec((tile, RET_DK), lambda i: (i % seq_tiles, 0))] * 2
        args += list(rope)
    out_block = pl.BlockSpec((tile, width), lambda i: (i, 0))
    steps = rows // tile
    c_in, c_out, c_shapes = _cast_plumbing(casts, steps)
    body = functools.partial(_ret_in_kernel, use_rope=rope is not None)
    outs = pl.pallas_call(
        _with_casts(body, len(args), 4, len(casts)),
        out_shape=tuple([jax.ShapeDtypeStruct((rows, width), BF16)] * 3
                        + [jax.ShapeDtypeStruct((rows, width), F32)] + c_shapes),
        grid=(steps,),
        in_specs=in_specs + c_in,
        out_specs=tuple([out_block] * 4 + c_out),
        compiler_params=_params(("parallel",)),
        name="ret_in_rope" if rope is not None else "ret_in",
    )(*args, *[arr for arr, _ in casts])
    return outs[:4], list(outs[4:])


def _log_sigmoid(x):
    return jnp.minimum(x, 0.0) - jnp.log(1.0 + jnp.exp(-jnp.abs(x)))


def _ret_kernel(q_ref, k_ref, v_ref, gate_ref, lg_ref, gn_ref, *rest, n, tq, heads, has_state):
    if has_state:
        s0_ref, y_ref, d_ref = rest
    else:
        y_ref, st_ref, d_ref = rest
    b = pl.program_id(1)
    qi = pl.program_id(2)
    r0 = qi * tq
    nt = (((1,), (1,)), ((), ()))
    sub = min(tq, RET_SUB_TILE)
    lgs = [(_log_sigmoid(lg_ref[hh, 0, 0:1, 0:1]), _log_sigmoid(lg_ref[hh, 1, 0:1, 0:1]))
           for hh in range(heads)]

    @pl.when(b == 0)
    def _():
        ii = r0 + lax.broadcasted_iota(jnp.int32, (tq, n), 0)
        jj = lax.broadcasted_iota(jnp.int32, (tq, n), 1)
        diff = (ii - jj).astype(F32)
        for hh in range(heads):
            lg_f, lg_b = lgs[hh]
            d_ref[hh, qi] = jnp.exp(jnp.where(diff >= 0.0, lg_f, -lg_b) * diff)

    def head_cols(hh):
        return slice(hh * RET_DK, (hh + 1) * RET_DK)

    def scores(unit):
        hh, si = unit
        return lax.dot_general(q_ref[0, si * sub:(si + 1) * sub, head_cols(hh)], k_ref[0, :, head_cols(hh)],
                               nt, preferred_element_type=F32)

    units = [(hh, si) for hh in range(heads) for si in range(tq // sub)]
    nxt = scores(units[0])
    for ui, (hh, si) in enumerate(units):
        s = nxt
        if ui + 1 < len(units):
            nxt = scores(units[ui + 1])
        lg_f, lg_b = lgs[hh]
        cols = head_cols(hh)
        rows = slice(si * sub, (si + 1) * sub)
        v = v_ref[0, :, cols]
        p = (s * d_ref[hh, qi, rows, :]).astype(BF16)
        o = jnp.dot(p, v, preferred_element_type=F32)
        if has_state:
            q = q_ref[0, rows, cols]
            i_col = (r0 + si * sub + lax.broadcasted_iota(jnp.int32, (sub, 1), 0)).astype(F32)
            o = o + (jnp.dot(q, s0_ref[0, 0, 0, hh].astype(BF16), preferred_element_type=F32)
                     * jnp.exp(lg_f * (i_col + 1.0))
                     + jnp.dot(q, s0_ref[0, 0, 1, hh].astype(BF16), preferred_element_type=F32)
                     * jnp.exp(lg_b * (n - i_col)))
        elif si == 0:
            eye = jnp.where(lax.broadcasted_iota(jnp.int32, (RET_DK, RET_DK), 0)
                            == lax.broadcasted_iota(jnp.int32, (RET_DK, RET_DK), 1), 1.0, 0.0).astype(BF16)
            kt = lax.dot_general(eye, k_ref[0, :, cols], nt, preferred_element_type=F32)
            j_row = lax.broadcasted_iota(jnp.int32, (1, n), 1).astype(F32)
            st_ref[0, 0, 0, hh] = jnp.dot((kt * jnp.exp(lg_f * (n - 1.0 - j_row))).astype(BF16), v,
                                          preferred_element_type=F32)
            st_ref[0, 0, 1, hh] = jnp.dot((kt * jnp.exp(lg_b * j_row)).astype(BF16), v,
                                          preferred_element_type=F32)
        mu = jnp.mean(o, axis=-1, keepdims=True)
        oc = o - mu
        var = jnp.mean(oc * oc, axis=-1, keepdims=True)
        on = oc * lax.rsqrt(var + EPS) * gn_ref[:, cols]
        gt = gate_ref[0, rows, cols]
        y_ref[0, rows, cols] = (gt * _sigmoid(gt) * on).astype(y_ref.dtype)


def _retention(q, k, v, gate, lg, gn, state0, heads, tq):
    b, n, _ = q.shape
    assert n % tq == 0 and RET_HEADS % heads == 0 and (state0 is not None or tq == n)
    width = heads * RET_DK
    q_tile = pl.BlockSpec((1, tq, width), lambda hg, bi, qi: (bi, qi, hg))
    seq = pl.BlockSpec((1, n, width), lambda hg, bi, qi: (bi, 0, hg))
    st = pl.BlockSpec((1, 1, 2, heads, RET_DK, RET_DV), lambda hg, bi, qi: (bi, 0, 0, hg, 0, 0))
    in_specs = [q_tile, seq, seq, q_tile,
                pl.BlockSpec((heads, 2, V7X_SUBLANES, V7X_LANES), lambda hg, bi, qi: (hg, 0, 0, 0)),
                pl.BlockSpec((1, width), lambda hg, bi, qi: (0, hg))]
    args = [q, k, v, gate, lg, gn]
    y_shape = jax.ShapeDtypeStruct((b, n, RET_HEADS * RET_DV), BF16)
    if state0 is not None:
        in_specs.append(st)
        args.append(state0)
        out_shape, out_specs = y_shape, q_tile
    else:
        out_shape = (y_shape, jax.ShapeDtypeStruct((b, 1, 2, RET_HEADS, RET_DK, RET_DV), F32))
        out_specs = (q_tile, st)
    return pl.pallas_call(
        functools.partial(_ret_kernel, n=n, tq=tq, heads=heads, has_state=state0 is not None),
        out_shape=out_shape,
        grid=(RET_HEADS // heads, b, n // tq),
        in_specs=in_specs,
        out_specs=out_specs,
        scratch_shapes=[pltpu.VMEM((heads, n // tq, tq, n), F32)],
        compiler_params=_params(("arbitrary", "arbitrary", "arbitrary")),
        name="retention_state" if state0 is not None else "retention",
    )(*args)


def kernel(x_prompt, x_sample, cache_k, cache_v, state_ret, c, c_ctx, w_mod, b_mod, norm_g,
           attn_w_qkv, attn_q_norm, attn_k_norm, attn_w_o, ret_w_qkvg, ret_decay_logit, ret_gn_w,
           ret_w_o, mlp_w1, mlp_w2, final_norm_g):
    bp, sp, d = x_prompt.shape
    bs, ss, _ = x_sample.shape
    depth = w_mod.shape[0]
    assert d == D_MODEL and depth == 2 and bs % V7X_SUBLANES == 0
    assert (bp * sp) % POST_ROW_TILE == 0 and ss % POST_ROW_TILE == 0 and POST_ROW_TILE % ROW_TILE == 0

    mod, (wqkv,) = _modulation(c, c_ctx, w_mod, b_mod, casts=[(attn_w_qkv, 0)])
    ctx_mod = (0, bs)
    lat_mod = (ss, 0)

    xp = x_prompt.reshape(bp * sp, d)
    xs = x_sample.reshape(bs * ss, d)
    nkv = N_KV_HEADS * HEAD_DIM

    g0 = norm_g[0, 0][None, :]
    g1 = norm_g[0, 1][None, :]
    qn = attn_q_norm[0][None, :]
    kn = attn_k_norm[0][None, :]

    (qp, kp, vp), (w1,) = _attn_in(xp, mod, (0,) + ctx_mod, g0, wqkv, qn, kn, None, PROJ_SUB_TILE,
                                   casts=[(mlp_w1, 0)])
    reorder = np.zeros((HEAD_DIM, HEAD_DIM), np.float32)
    reorder[_PAIRED_ORDER, np.arange(HEAD_DIM)] = 1.0
    reorder = jnp.asarray(reorder, BF16)
    (qs, ks, vs), (w2, wo) = _attn_in(xs, mod, (0,) + lat_mod, g0, wqkv, qn, kn,
                                      _rope_tables(ss, HEAD_DIM, True) + (reorder,), ROW_TILE,
                                      casts=[(mlp_w2, 0), (attn_w_o, 0)])

    ap, _ = _attention(qp.reshape(bp, sp, -1), kp.reshape(bp, sp * N_KV_HEADS, HEAD_DIM),
                       vp.reshape(bp, sp * N_KV_HEADS, HEAD_DIM))
    xp, _ = _post(ap.reshape(bp * sp, -1), xp, mod, (0,) + ctx_mod, g1, wo, w1, w2)
    cache = (cache_k, cache_v, reorder)
    as_, (wr,) = _attention(qs.reshape(bs, ss, -1), ks.reshape(bs, ss, nkv), vs.reshape(bs, ss, nkv), cache,
                            casts=[(ret_w_qkvg, 0)])
    xs, _ = _post(as_.reshape(bs * ss, -1), xs, mod, (0,) + lat_mod, g1, wo, w1, w2)

    g0 = norm_g[1, 0][None, :]
    g1 = norm_g[1, 1][None, :]
    gn = ret_gn_w[0][None, :]
    gf = final_norm_g[None, :]
    lg = jnp.broadcast_to(ret_decay_logit[0].T[:, :, None, None],
                          (RET_HEADS, 2, V7X_SUBLANES, V7X_LANES))
    hw = RET_HEADS * RET_DK

    (q, k, v, gate), (w1,) = _ret_in(xp, mod, (1,) + ctx_mod, g0, wr, None, PROJ_SUB_TILE, casts=[(mlp_w1, 1)])
    yp, new_state = _retention(q.reshape(bp, sp, hw), k.reshape(bp, sp, hw), v.reshape(bp, sp, hw),
                               gate.reshape(bp, sp, hw), lg, gn, None, RET_HEADS, sp)
    (q, k, v, gate), (w2, wo) = _ret_in(xs, mod, (1,) + lat_mod, g0, wr, _rope_tables(ss, RET_DK, False),
                                        ROW_TILE, casts=[(mlp_w2, 1), (ret_w_o, 0)])
    ys = _retention(q.reshape(bs, ss, hw), k.reshape(bs, ss, hw), v.reshape(bs, ss, hw),
                    gate.reshape(bs, ss, hw), lg, gn, state_ret, RET_STATE_HEADS, RET_Q_TILE)
    y_prompt, _ = _post(yp.reshape(bp * sp, hw), xp, mod, (1,) + ctx_mod, g1, wo, w1, w2, gf)
    y_sample, _ = _post(ys.reshape(bs * ss, hw), xs, mod, (1,) + lat_mod, g1, wo, w1, w2, gf)

    return (y_prompt.reshape(bp, sp, d),
            y_sample.reshape(bs, ss, d),
            kp.reshape(bp, 1, sp, N_KV_HEADS, HEAD_DIM),
            vp.reshape(bp, 1, sp, N_KV_HEADS, HEAD_DIM),
            new_state)
```

```python
import functools

import numpy as np
import jax
import jax.numpy as jnp
from jax import lax
from jax.experimental import pallas as pl
from jax.experimental.pallas import tpu as pltpu

F32 = jnp.float32
BF16 = jnp.bfloat16

D_MODEL = 1024
GRID_W = 64
N_HEADS = 8
N_KV_HEADS = 2
HEAD_DIM = 128
GROUP = N_HEADS // N_KV_HEADS
ROPE_BASE = 10000.0
RET_HEADS = 4
RET_DK = 256
RET_DV = 256
D_FF = 4 * D_MODEL
Q_SCALE = HEAD_DIM ** -0.5 * float(np.log2(np.e))
EPS = 1e-6
QKV_W = (N_HEADS + 2 * N_KV_HEADS) * HEAD_DIM
RET_W = 2 * RET_HEADS * RET_DK + 2 * RET_HEADS * RET_DV

V7X_LANES = 128
V7X_SUBLANES = 8
MOD_COL_TILE = 2048
ROW_TILE = 1024
PROJ_SUB_TILE = 512
POST_ROW_TILE = 1024
POST_SUB_TILE = 512
POST_FF_TILE = 1024
ATTN_Q_TILE = 512
RET_Q_TILE = 1024
RET_SUB_TILE = 256
RET_STATE_HEADS = 2
VMEM_LIMIT = 56 * 1024 * 1024


def _params(sem, vmem=VMEM_LIMIT):
    return pltpu.CompilerParams(dimension_semantics=sem, vmem_limit_bytes=vmem)


def _const_spec(shape):
    nd = len(shape)
    return pl.BlockSpec(shape, lambda *_: (0,) * nd, pipeline_mode=pl.Buffered(1))


def _sigmoid(x):
    return 1.0 / (1.0 + jnp.exp(-x))


def _cast_plumbing(casts, steps):
    in_specs, out_specs, out_shapes = [], [], []
    for arr, layer in casts:
        _, r, c = arr.shape
        slab = r // steps
        assert slab * steps == r and slab % 16 == 0
        in_specs.append(pl.BlockSpec((1, slab, c), functools.partial(lambda *idx, l: (l, idx[0], 0), l=layer)))
        out_specs.append(pl.BlockSpec((slab, c), lambda *idx: (idx[0], 0)))
        out_shapes.append(jax.ShapeDtypeStruct((r, c), BF16))
    return in_specs, out_specs, out_shapes


def _with_casts(body, n_in, n_out, n_cast):
    def kernel(*refs):
        ins = refs[:n_in]
        cast_in = refs[n_in:n_in + n_cast]
        outs = refs[n_in + n_cast:n_in + n_cast + n_out]
        cast_out = refs[n_in + n_cast + n_out:n_in + 2 * n_cast + n_out]
        scratch = refs[n_in + 2 * n_cast + n_out:]
        for src, dst in zip(cast_in, cast_out):
            dst[...] = src[0].astype(dst.dtype)
        body(*ins, *outs, *scratch)
    return kernel


def _mod_kernel(c_ref, cctx_ref, w_ref, b_ref, o_ref):
    cond = jnp.concatenate([c_ref[...], jnp.broadcast_to(cctx_ref[...], (V7X_SUBLANES, D_MODEL))], axis=0)
    s = (cond * _sigmoid(cond)).astype(BF16)
    bias = b_ref[pl.ds(pl.program_id(0), 1), :]
    o_ref[0, :, 0, :] = jnp.dot(s, w_ref[0].astype(BF16), preferred_element_type=F32) + bias


def _modulation(c, c_ctx, w_mod, b_mod, casts=()):
    depth = w_mod.shape[0]
    rows = c.shape[0] + V7X_SUBLANES
    tn = MOD_COL_TILE
    c_in, c_out, c_shapes = _cast_plumbing(casts, depth)
    outs = pl.pallas_call(
        _with_casts(_mod_kernel, 4, 1, len(casts)),
        out_shape=tuple([jax.ShapeDtypeStruct((depth, rows, 1, 6 * D_MODEL), F32)] + c_shapes),
        grid=(depth, 6 * D_MODEL // tn),
        in_specs=[
            pl.BlockSpec(c.shape, lambda i, j: (0, 0)),
            pl.BlockSpec((1, D_MODEL), lambda i, j: (0, 0)),
            pl.BlockSpec((1, D_MODEL, tn), lambda i, j: (i, 0, j)),
            pl.BlockSpec((depth, tn), lambda i, j: (0, j)),
        ] + c_in,
        out_specs=tuple([pl.BlockSpec((1, rows, 1, tn), lambda i, j: (i, 0, 0, j))] + c_out),
        compiler_params=_params(("arbitrary", "arbitrary")),
        name="modulation",
    )(c, c_ctx.reshape(1, D_MODEL), w_mod, b_mod, *[arr for arr, _ in casts])
    return outs[0], list(outs[1:])


def _norm_modulate(x, g, mod_ref, which):
    y = x * lax.rsqrt(jnp.mean(x * x, axis=-1, keepdims=True) + EPS) * g
    shift = mod_ref[0, 0, :, (3 * which) * D_MODEL:(3 * which + 1) * D_MODEL]
    scale = mod_ref[0, 0, :, (3 * which + 1) * D_MODEL:(3 * which + 2) * D_MODEL]
    return y * (1.0 + scale) + shift


def _mod_spec(mod_rows, tile):
    layer, rows_per_mod_row, first_row = mod_rows
    if rows_per_mod_row:
        assert rows_per_mod_row % tile == 0
        tiles = rows_per_mod_row // tile
        return pl.BlockSpec((1, 1, 1, 6 * D_MODEL), lambda i: (layer, first_row + i // tiles, 0, 0))
    return pl.BlockSpec((1, 1, 1, 6 * D_MODEL), lambda i: (layer, first_row, 0, 0))


def _rope_tables(n, dim, paired):
    quarter = dim // 4
    rows = np.repeat(np.arange(n // GRID_W), GRID_W).astype(np.float64)
    cols = np.tile(np.arange(GRID_W), n // GRID_W).astype(np.float64)
    freqs = ROPE_BASE ** (-np.arange(quarter, dtype=np.float64) / quarter)
    ar = rows[:, None] * freqs
    ac = cols[:, None] * freqs
    if paired:
        cos = np.concatenate([np.cos(ar), np.cos(ac), np.cos(ar), np.cos(ac)], axis=1)
        sin = np.concatenate([-np.sin(ar), -np.sin(ac), np.sin(ar), np.sin(ac)], axis=1)
    else:
        cos = np.concatenate([np.cos(ar), np.cos(ar), np.cos(ac), np.cos(ac)], axis=1)
        sin = np.concatenate([-np.sin(ar), np.sin(ar), -np.sin(ac), np.sin(ac)], axis=1)
    return jnp.asarray(cos, F32), jnp.asarray(sin, F32)


_Q = HEAD_DIM // 4
_PAIRED_ORDER = np.concatenate([np.arange(0, _Q), np.arange(2 * _Q, 3 * _Q),
                                np.arange(_Q, 2 * _Q), np.arange(3 * _Q, 4 * _Q)])


def _paired_lanes(w):
    lane = lax.broadcasted_iota(jnp.int32, w.shape, 1)
    from_c = pltpu.roll(w, V7X_LANES - _Q, axis=1)
    from_b = pltpu.roll(w, _Q, axis=1)
    return jnp.where((lane >= _Q) & (lane < 2 * _Q), from_c,
                     jnp.where((lane >= 2 * _Q) & (lane < 3 * _Q), from_b, w))


def _swap_halves(t):
    return pltpu.roll(t, V7X_LANES // 2, axis=1)


def _attn_in_kernel(x_ref, mod_ref, g_ref, w_ref, qn_ref, kn_ref, *rest, use_rope):
    nq = N_HEADS * HEAD_DIM
    nk = N_KV_HEADS * HEAD_DIM
    if use_rope:
        cos_ref, sin_ref, reorder_ref, q_ref, k_ref, v_ref, wqk_ref = rest

        @pl.when(pl.program_id(0) == 0)
        def _():
            for head in range(N_HEADS + N_KV_HEADS):
                cols = slice(head * HEAD_DIM, (head + 1) * HEAD_DIM)
                wqk_ref[:, cols] = jnp.dot(w_ref[:, cols], reorder_ref[...],
                                           preferred_element_type=F32).astype(BF16)
    else:
        q_ref, k_ref, v_ref = rest
        wqk_ref = w_ref
    tm = PROJ_SUB_TILE
    n_sub = x_ref.shape[0] // tm
    pair_w = 2 * HEAD_DIM
    n_pairs = (N_HEADS + N_KV_HEADS) // 2
    qw = qn_ref[...] * Q_SCALE
    kw = kn_ref[...]
    if use_rope:
        qw = _paired_lanes(qw)
        kw = _paired_lanes(kw)
        ra = lax.broadcasted_iota(jnp.int32, (pair_w, pair_w), 0) // HEAD_DIM
        rb = lax.broadcasted_iota(jnp.int32, (pair_w, pair_w), 1) // HEAD_DIM
        head_sum = jnp.where(ra == rb, 1.0, 0.0).astype(BF16)

    def normed(sub):
        return _norm_modulate(x_ref[sub * tm:(sub + 1) * tm, :], g_ref[...], mod_ref, 0).astype(BF16)

    def put_kv(ref, sub, kh, val):
        if use_rope:
            ref[sub * tm:(sub + 1) * tm, kh * HEAD_DIM:(kh + 1) * HEAD_DIM] = val.astype(ref.dtype)
        else:
            ref[pl.ds(sub * tm * N_KV_HEADS + kh, tm, stride=N_KV_HEADS), :] = val.astype(ref.dtype)

    h_next = normed(0)
    for sub in range(n_sub):
        h = h_next
        if sub + 1 < n_sub:
            h_next = normed(sub + 1)
        rows = slice(sub * tm, (sub + 1) * tm)
        if use_rope:
            q_tabs = (qw * cos_ref[rows, :], _swap_halves(qw) * sin_ref[rows, :])
            k_tabs = (kw * cos_ref[rows, :], _swap_halves(kw) * sin_ref[rows, :])
        t2_next = jnp.dot(h, wqk_ref[:, 0:pair_w], preferred_element_type=F32)
        for pair in range(n_pairs):
            t2 = t2_next
            if pair + 1 < n_pairs:
                t2_next = jnp.dot(h, wqk_ref[:, (pair + 1) * pair_w:(pair + 2) * pair_w],
                                  preferred_element_type=F32)
            else:
                t2_next = jnp.dot(h, w_ref[:, nq + nk:], preferred_element_type=F32)
            if use_rope:
                ss2 = jnp.dot((t2 * t2).astype(BF16), head_sum, preferred_element_type=F32)
                r2 = lax.rsqrt(ss2 * (1.0 / HEAD_DIM) + EPS)
            for j in range(2):
                head = 2 * pair + j
                t = t2[:, j * HEAD_DIM:(j + 1) * HEAD_DIM]
                if use_rope:
                    r = r2[:, j * HEAD_DIM:(j + 1) * HEAD_DIM]
                    wc, ws = q_tabs if head < N_HEADS else k_tabs
                    t = (t * wc + _swap_halves(t) * ws) * r
                else:
                    r = lax.rsqrt(jnp.mean(t * t, axis=-1, keepdims=True) + EPS)
                    t = t * r * (qw if head < N_HEADS else kw)
                if head < N_HEADS:
                    q_ref[rows, head * HEAD_DIM:(head + 1) * HEAD_DIM] = t.astype(q_ref.dtype)
                else:
                    put_kv(k_ref, sub, head - N_HEADS, t)
        vv = t2_next
        for kh in range(N_KV_HEADS):
            put_kv(v_ref, sub, kh, vv[:, kh * HEAD_DIM:(kh + 1) * HEAD_DIM])


def _attn_in(x, mod, mod_rows, g, w, qn, kn, rope, tile, casts=()):
    rows = x.shape[0]
    nk = N_KV_HEADS * HEAD_DIM
    if rope is None:
        kv_shape = jax.ShapeDtypeStruct((rows * N_KV_HEADS, HEAD_DIM), F32)
        kv_spec = pl.BlockSpec((tile * N_KV_HEADS, HEAD_DIM), lambda i: (i, 0))
    else:
        kv_shape = jax.ShapeDtypeStruct((rows, nk), BF16)
        kv_spec = pl.BlockSpec((tile, nk), lambda i: (i, 0))
    in_specs = [
        pl.BlockSpec((tile, D_MODEL), lambda i: (i, 0)),
        _mod_spec(mod_rows, tile),
        _const_spec((1, D_MODEL)),
        _const_spec((D_MODEL, QKV_W)),
        _const_spec((1, HEAD_DIM)),
        _const_spec((1, HEAD_DIM)),
    ]
    args = [x, mod, g, w, qn, kn]
    scratch = []
    if rope is not None:
        cos, sin, reorder = rope
        seq_tiles = cos.shape[0] // tile
        in_specs += [pl.BlockSpec((tile, HEAD_DIM), lambda i: (i % seq_tiles, 0))] * 2
        in_specs.append(_const_spec((HEAD_DIM, HEAD_DIM)))
        args += [cos, sin, reorder]
        scratch = [pltpu.VMEM((D_MODEL, (N_HEADS + N_KV_HEADS) * HEAD_DIM), BF16)]
    steps = rows // tile
    c_in, c_out, c_shapes = _cast_plumbing(casts, steps)
    out_specs = [pl.BlockSpec((tile, N_HEADS * HEAD_DIM), lambda i: (i, 0)), kv_spec, kv_spec]
    out_shapes = [jax.ShapeDtypeStruct((rows, N_HEADS * HEAD_DIM), BF16), kv_shape, kv_shape]
    body = functools.partial(_attn_in_kernel, use_rope=rope is not None)
    outs = pl.pallas_call(
        _with_casts(body, len(args), len(out_shapes), len(casts)),
        out_shape=tuple(out_shapes + c_shapes),
        grid=(steps,),
        in_specs=in_specs + c_in,
        out_specs=tuple(out_specs + c_out),
        scratch_shapes=scratch,
        compiler_params=_params(("arbitrary",)),
        name="attn_in_rope" if rope is not None else "attn_in",
    )(*args, *[arr for arr, _ in casts])
    return outs[:3], list(outs[3:])


def _kv_head(ref, e, kv):
    if ref.dtype == BF16:
        return ref[e, :, kv * HEAD_DIM:(kv + 1) * HEAD_DIM]
    tokens = ref.shape[1] // N_KV_HEADS
    return ref[e, pl.ds(kv, tokens, stride=N_KV_HEADS), :].astype(BF16)


def _attn_kernel(q_ref, k_ref, v_ref, *rest, cached):
    if cached:
        ck_hbm, cv_hbm, reorder_ref, o_ref, ck_scr, cv_scr, stage, sems = rest
        b = pl.program_id(0)
        n_b = pl.num_programs(0)

        def head_copies(bi, slot):
            return [pltpu.make_async_copy(src.at[bi, 0, :, kv, :], stage.at[slot, t, kv], sems.at[slot, t, kv])
                    for t, src in enumerate((ck_hbm, cv_hbm)) for kv in range(N_KV_HEADS)]

        @pl.when(pl.program_id(1) == 0)
        def _():
            slot = b % 2

            @pl.when(b == 0)
            def _():
                for cp in head_copies(0, 0):
                    cp.start()

            @pl.when(b + 1 < n_b)
            def _():
                for cp in head_copies(b + 1, 1 - slot):
                    cp.start()

            for cp in head_copies(b, slot):
                cp.wait()
            for kv in range(N_KV_HEADS):
                ck_scr[kv] = jnp.dot(stage[slot, 0, kv].astype(BF16), reorder_ref[...],
                                     preferred_element_type=F32).astype(BF16)
                cv_scr[kv] = stage[slot, 1, kv].astype(BF16)
    else:
        (o_ref,) = rest
    nt = (((1,), (1,)), ((), ()))
    n_elem = q_ref.shape[0]
    ks = [[_kv_head(k_ref, e, kv) for kv in range(N_KV_HEADS)] for e in range(n_elem)]
    vs = [[_kv_head(v_ref, e, kv) for kv in range(N_KV_HEADS)] for e in range(n_elem)]

    def scores(unit):
        e, head = unit
        kv = head // GROUP
        q = q_ref[e, :, head * HEAD_DIM:(head + 1) * HEAD_DIM]
        s = lax.dot_general(q, ks[e][kv], nt, preferred_element_type=F32)
        sc = lax.dot_general(q, ck_scr[kv], nt, preferred_element_type=F32) if cached else None
        return s, sc

    units = [(e, head) for e in range(n_elem) for head in range(N_HEADS)]
    nxt = scores(units[0])
    for ui, (e, head) in enumerate(units):
        kv = head // GROUP
        s, sc = nxt
        if ui + 1 < len(units):
            nxt = scores(units[ui + 1])
        m = jnp.max(s, axis=-1, keepdims=True)
        if cached:
            m = jnp.maximum(m, jnp.max(sc, axis=-1, keepdims=True))
        p = jnp.exp2(s - m)
        l = jnp.sum(p, axis=-1, keepdims=True)
        o = jnp.dot(p.astype(BF16), vs[e][kv], preferred_element_type=F32)
        if cached:
            pc = jnp.exp2(sc - m)
            l = l + jnp.sum(pc, axis=-1, keepdims=True)
            o = o + jnp.dot(pc.astype(BF16), cv_scr[kv], preferred_element_type=F32)
        o_ref[e, :, head * HEAD_DIM:(head + 1) * HEAD_DIM] = (o / l).astype(o_ref.dtype)


def _attention(q, k, v, cache=None, casts=()):
    b, n, width = q.shape
    tq = min(ATTN_Q_TILE, n)
    eb = 1 if cache is not None else max(1, min(b, ATTN_Q_TILE // n))
    assert b % eb == 0
    in_specs = [pl.BlockSpec((eb, tq, width), lambda bi, qi: (bi, qi, 0))]
    in_specs += [pl.BlockSpec((eb,) + k.shape[1:], lambda bi, qi: (bi, 0, 0))] * 2
    args = [q, k, v]
    scratch = []
    if cache is not None:
        past = cache[0].shape[2]
        in_specs += [pl.BlockSpec(memory_space=pl.ANY)] * 2
        in_specs.append(pl.BlockSpec((HEAD_DIM, HEAD_DIM), lambda bi, qi: (0, 0)))
        args += list(cache)
        scratch = [pltpu.VMEM((N_KV_HEADS, past, HEAD_DIM), BF16)] * 2
        scratch += [pltpu.VMEM((2, 2, N_KV_HEADS, past, HEAD_DIM), F32),
                    pltpu.SemaphoreType.DMA((2, 2, N_KV_HEADS))]
    c_in, c_out, c_shapes = _cast_plumbing(casts, b // eb)
    body = functools.partial(_attn_kernel, cached=cache is not None)
    outs = pl.pallas_call(
        _with_casts(body, len(args), 1, len(casts)),
        out_shape=tuple([jax.ShapeDtypeStruct((b, n, width), BF16)] + c_shapes),
        grid=(b // eb, n // tq),
        in_specs=in_specs + c_in,
        out_specs=tuple([pl.BlockSpec((eb, tq, width), lambda bi, qi: (bi, qi, 0))] + c_out),
        scratch_shapes=scratch,
        compiler_params=_params(("arbitrary", "arbitrary")),
        name="attention_cached" if cache is not None else "attention",
    )(*args, *[arr for arr, _ in casts])
    return outs[0], list(outs[1:])


def _post_kernel(a_ref, x_ref, mod_ref, g_ref, wo_ref, w1_ref, w2_ref, *rest, final):
    if final:
        gf_ref, o_ref = rest
    else:
        (o_ref,) = rest
    gate1 = mod_ref[0, 0, :, 2 * D_MODEL:3 * D_MODEL]
    gate2 = mod_ref[0, 0, :, 5 * D_MODEL:6 * D_MODEL]
    n_ff = D_FF // POST_FF_TILE

    def attn_residual(t):
        rows = slice(t * POST_SUB_TILE, (t + 1) * POST_SUB_TILE)
        x = x_ref[rows, :] + gate1 * jnp.dot(a_ref[rows, :], wo_ref[...], preferred_element_type=F32)
        return x, _norm_modulate(x, g_ref[...], mod_ref, 1).astype(BF16)

    n_sub = x_ref.shape[0] // POST_SUB_TILE
    nxt = attn_residual(0)
    for t in range(n_sub):
        x, h = nxt
        if t + 1 < n_sub:
            nxt = attn_residual(t + 1)
        y = None
        for f in range(n_ff):
            u = jnp.dot(h, w1_ref[:, f * POST_FF_TILE:(f + 1) * POST_FF_TILE], preferred_element_type=F32)
            u = jnp.square(jnp.maximum(u, 0.0)).astype(BF16)
            part = jnp.dot(u, w2_ref[f * POST_FF_TILE:(f + 1) * POST_FF_TILE, :], preferred_element_type=F32)
            y = part if y is None else y + part
        x = x + gate2 * y
        if final:
            x = x * lax.rsqrt(jnp.mean(x * x, axis=-1, keepdims=True) + EPS) * gf_ref[...]
        o_ref[t * POST_SUB_TILE:(t + 1) * POST_SUB_TILE, :] = x


def _post(a, x, mod, mod_rows, g, wo, w1, w2, gf=None, casts=()):
    rows = x.shape[0]
    row_block = pl.BlockSpec((POST_ROW_TILE, D_MODEL), lambda i: (i, 0))
    in_specs = [
        row_block,
        row_block,
        _mod_spec(mod_rows, POST_ROW_TILE),
        _const_spec((1, D_MODEL)),
        _const_spec((D_MODEL, D_MODEL)),
        _const_spec((D_MODEL, D_FF)),
        _const_spec((D_FF, D_MODEL)),
    ]
    args = [a, x, mod, g, wo, w1, w2]
    if gf is not None:
        in_specs.append(_const_spec((1, D_MODEL)))
        args.append(gf)
    steps = rows // POST_ROW_TILE
    c_in, c_out, c_shapes = _cast_plumbing(casts, steps)
    body = functools.partial(_post_kernel, final=gf is not None)
    outs = pl.pallas_call(
        _with_casts(body, len(args), 1, len(casts)),
        out_shape=tuple([jax.ShapeDtypeStruct((rows, D_MODEL), F32)] + c_shapes),
        grid=(steps,),
        in_specs=in_specs + c_in,
        out_specs=tuple([row_block] + c_out),
        compiler_params=_params(("parallel",)),
        name="post_final" if gf is not None else "post",
    )(*args, *[arr for arr, _ in casts])
    return outs[0], list(outs[1:])


def _ret_in_kernel(x_ref, mod_ref, g_ref, w_ref, *rest, use_rope):
    if use_rope:
        cos_ref, sin_ref, q_ref, k_ref, v_ref, gate_ref = rest
    else:
        q_ref, k_ref, v_ref, gate_ref = rest
    width = RET_HEADS * RET_DK
    tm = PROJ_SUB_TILE
    n_sub = x_ref.shape[0] // tm

    def normed(sub):
        return _norm_modulate(x_ref[sub * tm:(sub + 1) * tm, :], g_ref[...], mod_ref, 0).astype(BF16)

    h_next = normed(0)
    for sub in range(n_sub):
        h = h_next
        if sub + 1 < n_sub:
            h_next = normed(sub + 1)
        rows = slice(sub * tm, (sub + 1) * tm)
        for part, ref in enumerate((q_ref, k_ref, v_ref, gate_ref)):
            t = jnp.dot(h, w_ref[:, part * width:(part + 1) * width], preferred_element_type=F32)
            if part < 2 and use_rope:
                for c in range(width // V7X_LANES):
                    half = (c % (RET_DK // V7X_LANES)) * V7X_LANES
                    tc = t[:, c * V7X_LANES:(c + 1) * V7X_LANES]
                    tc = (tc * cos_ref[rows, half:half + V7X_LANES]
                          + _swap_halves(tc) * sin_ref[rows, half:half + V7X_LANES])
                    if part == 0:
                        tc = tc * RET_DK ** -0.5
                    ref[rows, c * V7X_LANES:(c + 1) * V7X_LANES] = tc.astype(ref.dtype)
            else:
                if part == 0:
                    t = t * RET_DK ** -0.5
                ref[rows, :] = t.astype(ref.dtype)


def _ret_in(x, mod, mod_rows, g, w, rope, tile, casts=()):
    rows = x.shape[0]
    width = RET_HEADS * RET_DK
    in_specs = [
        pl.BlockSpec((tile, D_MODEL), lambda i: (i, 0)),
        _mod_spec(mod_rows, tile),
        _const_spec((1, D_MODEL)),
        _const_spec((D_MODEL, RET_W)),
    ]
    args = [x, mod, g, w]
    if rope is not None:
        seq_tiles = rope[0].shape[0] // tile
        in_specs += [pl.BlockSpec((tile, RET_DK), lambda i: (i % seq_tiles, 0))] * 2
        args += list(rope)
    out_block = pl.BlockSpec((tile, width), lambda i: (i, 0))
    steps = rows // tile
    c_in, c_out, c_shapes = _cast_plumbing(casts, steps)
    body = functools.partial(_ret_in_kernel, use_rope=rope is not None)
    outs = pl.pallas_call(
        _with_casts(body, len(args), 4, len(casts)),
        out_shape=tuple([jax.ShapeDtypeStruct((rows, width), BF16)] * 3
                        + [jax.ShapeDtypeStruct((rows, width), F32)] + c_shapes),
        grid=(steps,),
        in_specs=in_specs + c_in,
        out_specs=tuple([out_block] * 4 + c_out),
        compiler_params=_params(("parallel",)),
        name="ret_in_rope" if rope is not None else "ret_in",
    )(*args, *[arr for arr, _ in casts])
    return outs[:4], list(outs[4:])


def _log_sigmoid(x):
    return jnp.minimum(x, 0.0) - jnp.log(1.0 + jnp.exp(-jnp.abs(x)))


def _ret_kernel(q_ref, k_ref, v_ref, gate_ref, lg_ref, gn_ref, *rest, n, tq, heads, has_state):
    if has_state:
        s0_ref, y_ref, d_ref = rest
    else:
        y_ref, st_ref, d_ref = rest
    b = pl.program_id(1)
    qi = pl.program_id(2)
    r0 = qi * tq
    nt = (((1,), (1,)), ((), ()))
    sub = min(tq, RET_SUB_TILE)
    lgs = [(_log_sigmoid(lg_ref[hh, 0, 0:1, 0:1]), _log_sigmoid(lg_ref[hh, 1, 0:1, 0:1]))
           for hh in range(heads)]

    @pl.when(b == 0)
    def _():
        ii = r0 + lax.broadcasted_iota(jnp.int32, (tq, n), 0)
        jj = lax.broadcasted_iota(jnp.int32, (tq, n), 1)
        diff = (ii - jj).astype(F32)
        for hh in range(heads):
            lg_f, lg_b = lgs[hh]
            d_ref[hh, qi] = jnp.exp(jnp.where(diff >= 0.0, lg_f, -lg_b) * diff)

    def head_cols(hh):
        return slice(hh * RET_DK, (hh + 1) * RET_DK)

    def scores(unit):
        hh, si = unit
        return lax.dot_general(q_ref[0, si * sub:(si + 1) * sub, head_cols(hh)], k_ref[0, :, head_cols(hh)],
                               nt, preferred_element_type=F32)

    units = [(hh, si) for hh in range(heads) for si in range(tq // sub)]
    nxt = scores(units[0])
    for ui, (hh, si) in enumerate(units):
        s = nxt
        if ui + 1 < len(units):
            nxt = scores(units[ui + 1])
        lg_f, lg_b = lgs[hh]
        cols = head_cols(hh)
        rows = slice(si * sub, (si + 1) * sub)
        v = v_ref[0, :, cols]
        p = (s * d_ref[hh, qi, rows, :]).astype(BF16)
        o = jnp.dot(p, v, preferred_element_type=F32)
        if has_state:
            q = q_ref[0, rows, cols]
            i_col = (r0 + si * sub + lax.broadcasted_iota(jnp.int32, (sub, 1), 0)).astype(F32)
            o = o + (jnp.dot(q, s0_ref[0, 0, 0, hh].astype(BF16), preferred_element_type=F32)
                     * jnp.exp(lg_f * (i_col + 1.0))
                     + jnp.dot(q, s0_ref[0, 0, 1, hh].astype(BF16), preferred_element_type=F32)
                     * jnp.exp(lg_b * (n - i_col)))
        elif si == 0:
            eye = jnp.where(lax.broadcasted_iota(jnp.int32, (RET_DK, RET_DK), 0)
                            == lax.broadcasted_iota(jnp.int32, (RET_DK, RET_DK), 1), 1.0, 0.0).astype(BF16)
            kt = lax.dot_general(eye, k_ref[0, :, cols], nt, preferred_element_type=F32)
            j_row = lax.broadcasted_iota(jnp.int32, (1, n), 1).astype(F32)
            st_ref[0, 0, 0, hh] = jnp.dot((kt * jnp.exp(lg_f * (n - 1.0 - j_row))).astype(BF16), v,
                                          preferred_element_type=F32)
            st_ref[0, 0, 1, hh] = jnp.dot((kt * jnp.exp(lg_b * j_row)).astype(BF16), v,
                                          preferred_element_type=F32)
        mu = jnp.mean(o, axis=-1, keepdims=True)
        oc = o - mu
        var = jnp.mean(oc * oc, axis=-1, keepdims=True)
        on = oc * lax.rsqrt(var + EPS) * gn_ref[:, cols]
        gt = gate_ref[0, rows, cols]
        y_ref[0, rows, cols] = (gt * _sigmoid(gt) * on).astype(y_ref.dtype)


def _retention(q, k, v, gate, lg, gn, state0, heads, tq):
    b, n, _ = q.shape
    assert n % tq == 0 and RET_HEADS % heads == 0 and (state0 is not None or tq == n)
    width = heads * RET_DK
    q_tile = pl.BlockSpec((1, tq, width), lambda hg, bi, qi: (bi, qi, hg))
    seq = pl.BlockSpec((1, n, width), lambda hg, bi, qi: (bi, 0, hg))
    st = pl.BlockSpec((1, 1, 2, heads, RET_DK, RET_DV), lambda hg, bi, qi: (bi, 0, 0, hg, 0, 0))
    in_specs = [q_tile, seq, seq, q_tile,
                pl.BlockSpec((heads, 2, V7X_SUBLANES, V7X_LANES), lambda hg, bi, qi: (hg, 0, 0, 0)),
                pl.BlockSpec((1, width), lambda hg, bi, qi: (0, hg))]
    args = [q, k, v, gate, lg, gn]
    y_shape = jax.ShapeDtypeStruct((b, n, RET_HEADS * RET_DV), BF16)
    if state0 is not None:
        in_specs.append(st)
        args.append(state0)
        out_shape, out_specs = y_shape, q_tile
    else:
        out_shape = (y_shape, jax.ShapeDtypeStruct((b, 1, 2, RET_HEADS, RET_DK, RET_DV), F32))
        out_specs = (q_tile, st)
    return pl.pallas_call(
        functools.partial(_ret_kernel, n=n, tq=tq, heads=heads, has_state=state0 is not None),
        out_shape=out_shape,
        grid=(RET_HEADS // heads, b, n // tq),
        in_specs=in_specs,
        out_specs=out_specs,
        scratch_shapes=[pltpu.VMEM((heads, n // tq, tq, n), F32)],
        compiler_params=_params(("arbitrary", "arbitrary", "arbitrary")),
        name="retention_state" if state0 is not None else "retention",
    )(*args)


def kernel(x_prompt, x_sample, cache_k, cache_v, state_ret, c, c_ctx, w_mod, b_mod, norm_g,
           attn_w_qkv, attn_q_norm, attn_k_norm, attn_w_o, ret_w_qkvg, ret_decay_logit, ret_gn_w,
           ret_w_o, mlp_w1, mlp_w2, final_norm_g):
    bp, sp, d = x_prompt.shape
    bs, ss, _ = x_sample.shape
    depth = w_mod.shape[0]
    assert d == D_MODEL and depth == 2 and bs % V7X_SUBLANES == 0
    assert (bp * sp) % POST_ROW_TILE == 0 and ss % POST_ROW_TILE == 0 and POST_ROW_TILE % ROW_TILE == 0

    mod, (wqkv,) = _modulation(c, c_ctx, w_mod, b_mod, casts=[(attn_w_qkv, 0)])
    ctx_mod = (0, bs)
    lat_mod = (ss, 0)

    xp = x_prompt.reshape(bp * sp, d)
    xs = x_sample.reshape(bs * ss, d)
    nkv = N_KV_HEADS * HEAD_DIM

    g0 = norm_g[0, 0][None, :]
    g1 = norm_g[0, 1][None, :]
    qn = attn_q_norm[0][None, :]
    kn = attn_k_norm[0][None, :]

    (qp, kp, vp), (w1,) = _attn_in(xp, mod, (0,) + ctx_mod, g0, wqkv, qn, kn, None, PROJ_SUB_TILE,
                                   casts=[(mlp_w1, 0)])
    reorder = np.zeros((HEAD_DIM, HEAD_DIM), np.float32)
    reorder[_PAIRED_ORDER, np.arange(HEAD_DIM)] = 1.0
    reorder = jnp.asarray(reorder, BF16)
    (qs, ks, vs), (w2, wo) = _attn_in(xs, mod, (0,) + lat_mod, g0, wqkv, qn, kn,
                                      _rope_tables(ss, HEAD_DIM, True) + (reorder,), ROW_TILE,
                                      casts=[(mlp_w2, 0), (attn_w_o, 0)])

    ap, _ = _attention(qp.reshape(bp, sp, -1), kp.reshape(bp, sp * N_KV_HEADS, HEAD_DIM),
                       vp.reshape(bp, sp * N_KV_HEADS, HEAD_DIM))
    xp, _ = _post(ap.reshape(bp * sp, -1), xp, mod, (0,) + ctx_mod, g1, wo, w1, w2)
    cache = (cache_k, cache_v, reorder)
    as_, (wr,) = _attention(qs.reshape(bs, ss, -1), ks.reshape(bs, ss, nkv), vs.reshape(bs, ss, nkv), cache,
                            casts=[(ret_w_qkvg, 0)])
    xs, _ = _post(as_.reshape(bs * ss, -1), xs, mod, (0,) + lat_mod, g1, wo, w1, w2)

    g0 = norm_g[1, 0][None, :]
    g1 = norm_g[1, 1][None, :]
    gn = ret_gn_w[0][None, :]
    gf = final_norm_g[None, :]
    lg = jnp.broadcast_to(ret_decay_logit[0].T[:, :, None, None],
                          (RET_HEADS, 2, V7X_SUBLANES, V7X_LANES))
    hw = RET_HEADS * RET_DK

    (q, k, v, gate), (w1,) = _ret_in(xp, mod, (1,) + ctx_mod, g0, wr, None, PROJ_SUB_TILE, casts=[(mlp_w1, 1)])
    yp, new_state = _retention(q.reshape(bp, sp, hw), k.reshape(bp, sp, hw), v.reshape(bp, sp, hw),
                               gate.reshape(bp, sp, hw), lg, gn, None, RET_HEADS, sp)
    (q, k, v, gate), (w2, wo) = _ret_in(xs, mod, (1,) + lat_mod, g0, wr, _rope_tables(ss, RET_DK, False),
                                        ROW_TILE, casts=[(mlp_w2, 1), (ret_w_o, 0)])
    ys = _retention(q.reshape(bs, ss, hw), k.reshape(bs, ss, hw), v.reshape(bs, ss, hw),
                    gate.reshape(bs, ss, hw), lg, gn, state_ret, RET_STATE_HEADS, RET_Q_TILE)
    y_prompt, _ = _post(yp.reshape(bp * sp, hw), xp, mod, (1,) + ctx_mod, g1, wo, w1, w2, gf)
    y_sample, _ = _post(ys.reshape(bs * ss, hw), xs, mod, (1,) + lat_mod, g1, wo, w1, w2, gf)

    return (y_prompt.reshape(bp, sp, d),
            y_sample.reshape(bs, ss, d),
            kp.reshape(bp, 1, sp, N_KV_HEADS, HEAD_DIM),
            vp.reshape(bp, 1, sp, N_KV_HEADS, HEAD_DIM),
            new_state)
```

```python
import functools

import numpy as np
import jax
import jax.numpy as jnp
from jax import lax
from jax.experimental import pallas as pl
from jax.experimental.pallas import tpu as pltpu

F32 = jnp.float32
BF16 = jnp.bfloat16

D_MODEL = 1024
GRID_W = 64
N_HEADS = 8
N_KV_HEADS = 2
HEAD_DIM = 128
GROUP = N_HEADS // N_KV_HEADS
ROPE_BASE = 10000.0
RET_HEADS = 4
RET_DK = 256
RET_DV = 256
D_FF = 4 * D_MODEL
Q_SCALE = HEAD_DIM ** -0.5 * float(np.log2(np.e))
EPS = 1e-6
QKV_W = (N_HEADS + 2 * N_KV_HEADS) * HEAD_DIM
RET_W = 2 * RET_HEADS * RET_DK + 2 * RET_HEADS * RET_DV

V7X_LANES = 128
V7X_SUBLANES = 8
MOD_COL_TILE = 2048
ROW_TILE = 1024
PROJ_SUB_TILE = 512
POST_ROW_TILE = 1024
POST_SUB_TILE = 512
POST_FF_TILE = 1024
ATTN_Q_TILE = 512
RET_Q_TILE = 1024
RET_SUB_TILE = 256
RET_STATE_HEADS = 2
VMEM_LIMIT = 56 * 1024 * 1024


def _params(sem, vmem=VMEM_LIMIT):
    return pltpu.CompilerParams(dimension_semantics=sem, vmem_limit_bytes=vmem)


def _const_spec(shape):
    nd = len(shape)
    return pl.BlockSpec(shape, lambda *_: (0,) * nd, pipeline_mode=pl.Buffered(1))


def _sigmoid(x):
    return 1.0 / (1.0 + jnp.exp(-x))


def _cast_plumbing(casts, steps):
    in_specs, out_specs, out_shapes = [], [], []
    for arr, layer in casts:
        _, r, c = arr.shape
        slab = r // steps
        assert slab * steps == r and slab % 16 == 0
        in_specs.append(pl.BlockSpec((1, slab, c), functools.partial(lambda *idx, l: (l, idx[0], 0), l=layer)))
        out_specs.append(pl.BlockSpec((slab, c), lambda *idx: (idx[0], 0)))
        out_shapes.append(jax.ShapeDtypeStruct((r, c), BF16))
    return in_specs, out_specs, out_shapes


def _with_casts(body, n_in, n_out, n_cast):
    def kernel(*refs):
        ins = refs[:n_in]
        cast_in = refs[n_in:n_in + n_cast]
        outs = refs[n_in + n_cast:n_in + n_cast + n_out]
        cast_out = refs[n_in + n_cast + n_out:n_in + 2 * n_cast + n_out]
        scratch = refs[n_in + 2 * n_cast + n_out:]
        for src, dst in zip(cast_in, cast_out):
            dst[...] = src[0].astype(dst.dtype)
        body(*ins, *outs, *scratch)
    return kernel


def _mod_kernel(c_ref, cctx_ref, w_ref, b_ref, o_ref):
    cond = jnp.concatenate([c_ref[...], jnp.broadcast_to(cctx_ref[...], (V7X_SUBLANES, D_MODEL))], axis=0)
    s = (cond * _sigmoid(cond)).astype(BF16)
    bias = b_ref[pl.ds(pl.program_id(0), 1), :]
    o_ref[0, :, 0, :] = jnp.dot(s, w_ref[0].astype(BF16), preferred_element_type=F32) + bias


def _modulation(c, c_ctx, w_mod, b_mod, casts=()):
    depth = w_mod.shape[0]
    rows = c.shape[0] + V7X_SUBLANES
    tn = MOD_COL_TILE
    c_in, c_out, c_shapes = _cast_plumbing(casts, depth)
    outs = pl.pallas_call(
        _with_casts(_mod_kernel, 4, 1, len(casts)),
        out_shape=tuple([jax.ShapeDtypeStruct((depth, rows, 1, 6 * D_MODEL), F32)] + c_shapes),
        grid=(depth, 6 * D_MODEL // tn),
        in_specs=[
            pl.BlockSpec(c.shape, lambda i, j: (0, 0)),
            pl.BlockSpec((1, D_MODEL), lambda i, j: (0, 0)),
            pl.BlockSpec((1, D_MODEL, tn), lambda i, j: (i, 0, j)),
            pl.BlockSpec((depth, tn), lambda i, j: (0, j)),
        ] + c_in,
        out_specs=tuple([pl.BlockSpec((1, rows, 1, tn), lambda i, j: (i, 0, 0, j))] + c_out),
        compiler_params=_params(("arbitrary", "arbitrary")),
        name="modulation",
    )(c, c_ctx.reshape(1, D_MODEL), w_mod, b_mod, *[arr for arr, _ in casts])
    return outs[0], list(outs[1:])


def _norm_modulate(x, g, mod_ref, which):
    y = x * lax.rsqrt(jnp.mean(x * x, axis=-1, keepdims=True) + EPS) * g
    shift = mod_ref[0, 0, :, (3 * which) * D_MODEL:(3 * which + 1) * D_MODEL]
    scale = mod_ref[0, 0, :, (3 * which + 1) * D_MODEL:(3 * which + 2) * D_MODEL]
    return y * (1.0 + scale) + shift


def _mod_spec(mod_rows, tile):
    layer, rows_per_mod_row, first_row = mod_rows
    if rows_per_mod_row:
        assert rows_per_mod_row % tile == 0
        tiles = rows_per_mod_row // tile
        return pl.BlockSpec((1, 1, 1, 6 * D_MODEL), lambda i: (layer, first_row + i // tiles, 0, 0))
    return pl.BlockSpec((1, 1, 1, 6 * D_MODEL), lambda i: (layer, first_row, 0, 0))


def _rope_tables(n, dim, paired):
    quarter = dim // 4
    rows = np.repeat(np.arange(n // GRID_W), GRID_W).astype(np.float64)
    cols = np.tile(np.arange(GRID_W), n // GRID_W).astype(np.float64)
    freqs = ROPE_BASE ** (-np.arange(quarter, dtype=np.float64) / quarter)
    ar = rows[:, None] * freqs
    ac = cols[:, None] * freqs
    if paired:
        cos = np.concatenate([np.cos(ar), np.cos(ac), np.cos(ar), np.cos(ac)], axis=1)
        sin = np.concatenate([-np.sin(ar), -np.sin(ac), np.sin(ar), np.sin(ac)], axis=1)
    else:
        cos = np.concatenate([np.cos(ar), np.cos(ar), np.cos(ac), np.cos(ac)], axis=1)
        sin = np.concatenate([-np.sin(ar), np.sin(ar), -np.sin(ac), np.sin(ac)], axis=1)
    return jnp.asarray(cos, F32), jnp.asarray(sin, F32)


_Q = HEAD_DIM // 4
_PAIRED_ORDER = np.concatenate([np.arange(0, _Q), np.arange(2 * _Q, 3 * _Q),
                                np.arange(_Q, 2 * _Q), np.arange(3 * _Q, 4 * _Q)])


def _paired_lanes(w):
    lane = lax.broadcasted_iota(jnp.int32, w.shape, 1)
    from_c = pltpu.roll(w, V7X_LANES - _Q, axis=1)
    from_b = pltpu.roll(w, _Q, axis=1)
    return jnp.where((lane >= _Q) & (lane < 2 * _Q), from_c,
                     jnp.where((lane >= 2 * _Q) & (lane < 3 * _Q), from_b, w))


def _swap_halves(t):
    return pltpu.roll(t, V7X_LANES // 2, axis=1)


def _attn_in_kernel(x_ref, mod_ref, g_ref, w_ref, qn_ref, kn_ref, *rest, use_rope):
    nq = N_HEADS * HEAD_DIM
    nk = N_KV_HEADS * HEAD_DIM
    if use_rope:
        cos_ref, sin_ref, reorder_ref, q_ref, k_ref, v_ref, wqk_ref = rest

        @pl.when(pl.program_id(0) == 0)
        def _():
            for head in range(N_HEADS + N_KV_HEADS):
                cols = slice(head * HEAD_DIM, (head + 1) * HEAD_DIM)
                wqk_ref[:, cols] = jnp.dot(w_ref[:, cols], reorder_ref[...],
                                           preferred_element_type=F32).astype(BF16)
    else:
        q_ref, k_ref, v_ref = rest
        wqk_ref = w_ref
    tm = PROJ_SUB_TILE
    n_sub = x_ref.shape[0] // tm
    pair_w = 2 * HEAD_DIM
    n_pairs = (N_HEADS + N_KV_HEADS) // 2
    qw = qn_ref[...] * Q_SCALE
    kw = kn_ref[...]
    if use_rope:
        qw = _paired_lanes(qw)
        kw = _paired_lanes(kw)
        ra = lax.broadcasted_iota(jnp.int32, (pair_w, pair_w), 0) // HEAD_DIM
        rb = lax.broadcasted_iota(jnp.int32, (pair_w, pair_w), 1) // HEAD_DIM
        head_sum = jnp.where(ra == rb, 1.0, 0.0).astype(BF16)

    def normed(sub):
        return _norm_modulate(x_ref[sub * tm:(sub + 1) * tm, :], g_ref[...], mod_ref, 0).astype(BF16)

    def put_kv(ref, sub, kh, val):
        if use_rope:
            ref[sub * tm:(sub + 1) * tm, kh * HEAD_DIM:(kh + 1) * HEAD_DIM] = val.astype(ref.dtype)
        else:
            ref[pl.ds(sub * tm * N_KV_HEADS + kh, tm, stride=N_KV_HEADS), :] = val.astype(ref.dtype)

    h_next = normed(0)
    for sub in range(n_sub):
        h = h_next
        if sub + 1 < n_sub:
            h_next = normed(sub + 1)
        rows = slice(sub * tm, (sub + 1) * tm)
        if use_rope:
            q_tabs = (qw * cos_ref[rows, :], _swap_halves(qw) * sin_ref[rows, :])
            k_tabs = (kw * cos_ref[rows, :], _swap_halves(kw) * sin_ref[rows, :])
        t2_next = jnp.dot(h, wqk_ref[:, 0:pair_w], preferred_element_type=F32)
        for pair in range(n_pairs):
            t2 = t2_next
            if pair + 1 < n_pairs:
                t2_next = jnp.dot(h, wqk_ref[:, (pair + 1) * pair_w:(pair + 2) * pair_w],
                                  preferred_element_type=F32)
            else:
                t2_next = jnp.dot(h, w_ref[:, nq + nk:], preferred_element_type=F32)
            if use_rope:
                ss2 = jnp.dot((t2 * t2).astype(BF16), head_sum, preferred_element_type=F32)
                r2 = lax.rsqrt(ss2 * (1.0 / HEAD_DIM) + EPS)
            for j in range(2):
                head = 2 * pair + j
                t = t2[:, j * HEAD_DIM:(j + 1) * HEAD_DIM]
                if use_rope:
                    r = r2[:, j * HEAD_DIM:(j + 1) * HEAD_DIM]
                    wc, ws = q_tabs if head < N_HEADS else k_tabs
                    t = (t * wc + _swap_halves(t) * ws) * r
                else:
                    r = lax.rsqrt(jnp.mean(t * t, axis=-1, keepdims=True) + EPS)
                    t = t * r * (qw if head < N_HEADS else kw)
                if head < N_HEADS:
                    q_ref[rows, head * HEAD_DIM:(head + 1) * HEAD_DIM] = t.astype(q_ref.dtype)
                else:
                    put_kv(k_ref, sub, head - N_HEADS, t)
        vv = t2_next
        for kh in range(N_KV_HEADS):
            put_kv(v_ref, sub, kh, vv[:, kh * HEAD_DIM:(kh + 1) * HEAD_DIM])


def _attn_in(x, mod, mod_rows, g, w, qn, kn, rope, tile, casts=()):
    rows = x.shape[0]
    nk = N_KV_HEADS * HEAD_DIM
    if rope is None:
        kv_shape = jax.ShapeDtypeStruct((rows * N_KV_HEADS, HEAD_DIM), F32)
        kv_spec = pl.BlockSpec((tile * N_KV_HEADS, HEAD_DIM), lambda i: (i, 0))
    else:
        kv_shape = jax.ShapeDtypeStruct((rows, nk), BF16)
        kv_spec = pl.BlockSpec((tile, nk), lambda i: (i, 0))
    in_specs = [
        pl.BlockSpec((tile, D_MODEL), lambda i: (i, 0)),
        _mod_spec(mod_rows, tile),
        _const_spec((1, D_MODEL)),
        _const_spec((D_MODEL, QKV_W)),
        _const_spec((1, HEAD_DIM)),
        _const_spec((1, HEAD_DIM)),
    ]
    args = [x, mod, g, w, qn, kn]
    scratch = []
    if rope is not None:
        cos, sin, reorder = rope
        seq_tiles = cos.shape[0] // tile
        in_specs += [pl.BlockSpec((tile, HEAD_DIM), lambda i: (i % seq_tiles, 0))] * 2
        in_specs.append(_const_spec((HEAD_DIM, HEAD_DIM)))
        args += [cos, sin, reorder]
        scratch = [pltpu.VMEM((D_MODEL, (N_HEADS + N_KV_HEADS) * HEAD_DIM), BF16)]
    steps = rows // tile
    c_in, c_out, c_shapes = _cast_plumbing(casts, steps)
    out_specs = [pl.BlockSpec((tile, N_HEADS * HEAD_DIM), lambda i: (i, 0)), kv_spec, kv_spec]
    out_shapes = [jax.ShapeDtypeStruct((rows, N_HEADS * HEAD_DIM), BF16), kv_shape, kv_shape]
    body = functools.partial(_attn_in_kernel, use_rope=rope is not None)
    outs = pl.pallas_call(
        _with_casts(body, len(args), len(out_shapes), len(casts)),
        out_shape=tuple(out_shapes + c_shapes),
        grid=(steps,),
        in_specs=in_specs + c_in,
        out_specs=tuple(out_specs + c_out),
        scratch_shapes=scratch,
        compiler_params=_params(("arbitrary",)),
        name="attn_in_rope" if rope is not None else "attn_in",
    )(*args, *[arr for arr, _ in casts])
    return outs[:3], list(outs[3:])


def _kv_head(ref, e, kv):
    if ref.dtype == BF16:
        return ref[e, :, kv * HEAD_DIM:(kv + 1) * HEAD_DIM]
    tokens = ref.shape[1] // N_KV_HEADS
    return ref[e, pl.ds(kv, tokens, stride=N_KV_HEADS), :].astype(BF16)


def _attn_kernel(q_ref, k_ref, v_ref, *rest, cached):
    def with_ones(v):
        return jnp.concatenate([v, jnp.ones_like(v)], axis=1)

    if cached:
        ck_hbm, cv_hbm, reorder_ref, o_ref, ck_scr, cv_scr, stage, sems = rest
        b = pl.program_id(0)
        n_b = pl.num_programs(0)

        def head_copies(bi, slot):
            return [pltpu.make_async_copy(src.at[bi, 0, :, kv, :], stage.at[slot, t, kv], sems.at[slot, t, kv])
                    for t, src in enumerate((ck_hbm, cv_hbm)) for kv in range(N_KV_HEADS)]

        @pl.when(pl.program_id(1) == 0)
        def _():
            slot = b % 2

            @pl.when(b == 0)
            def _():
                for cp in head_copies(0, 0):
                    cp.start()

            @pl.when(b + 1 < n_b)
            def _():
                for cp in head_copies(b + 1, 1 - slot):
                    cp.start()

            for cp in head_copies(b, slot):
                cp.wait()
            for kv in range(N_KV_HEADS):
                ck_scr[kv] = jnp.dot(stage[slot, 0, kv].astype(BF16), reorder_ref[...],
                                     preferred_element_type=F32).astype(BF16)
                cv_scr[kv] = with_ones(stage[slot, 1, kv].astype(BF16))
    else:
        (o_ref,) = rest
    nt = (((1,), (1,)), ((), ()))
    n_elem = q_ref.shape[0]
    ks = [[_kv_head(k_ref, e, kv) for kv in range(N_KV_HEADS)] for e in range(n_elem)]
    vs = [[with_ones(_kv_head(v_ref, e, kv)) for kv in range(N_KV_HEADS)] for e in range(n_elem)]

    def scores(unit):
        e, head = unit
        kv = head // GROUP
        q = q_ref[e, :, head * HEAD_DIM:(head + 1) * HEAD_DIM]
        s = lax.dot_general(q, ks[e][kv], nt, preferred_element_type=F32)
        sc = lax.dot_general(q, ck_scr[kv], nt, preferred_element_type=F32) if cached else None
        return s, sc

    units = [(e, head) for e in range(n_elem) for head in range(N_HEADS)]
    nxt = scores(units[0])
    for ui, (e, head) in enumerate(units):
        kv = head // GROUP
        s, sc = nxt
        if ui + 1 < len(units):
            nxt = scores(units[ui + 1])
        m = jnp.max(s, axis=-1, keepdims=True)
        if cached:
            m = jnp.maximum(m, jnp.max(sc, axis=-1, keepdims=True))
        o = jnp.dot(jnp.exp2(s - m).astype(BF16), vs[e][kv], preferred_element_type=F32)
        if cached:
            o = o + jnp.dot(jnp.exp2(sc - m).astype(BF16), cv_scr[kv], preferred_element_type=F32)
        o_ref[e, :, head * HEAD_DIM:(head + 1) * HEAD_DIM] = (o[:, :HEAD_DIM] / o[:, HEAD_DIM:]).astype(o_ref.dtype)


def _attention(q, k, v, cache=None, casts=()):
    b, n, width = q.shape
    tq = min(ATTN_Q_TILE, n)
    eb = 1 if cache is not None else max(1, min(b, ATTN_Q_TILE // n))
    assert b % eb == 0
    in_specs = [pl.BlockSpec((eb, tq, width), lambda bi, qi: (bi, qi, 0))]
    in_specs += [pl.BlockSpec((eb,) + k.shape[1:], lambda bi, qi: (bi, 0, 0))] * 2
    args = [q, k, v]
    scratch = []
    if cache is not None:
        past = cache[0].shape[2]
        in_specs += [pl.BlockSpec(memory_space=pl.ANY)] * 2
        in_specs.append(pl.BlockSpec((HEAD_DIM, HEAD_DIM), lambda bi, qi: (0, 0)))
        args += list(cache)
        scratch = [pltpu.VMEM((N_KV_HEADS, past, HEAD_DIM), BF16),
                   pltpu.VMEM((N_KV_HEADS, past, 2 * HEAD_DIM), BF16)]
        scratch += [pltpu.VMEM((2, 2, N_KV_HEADS, past, HEAD_DIM), F32),
                    pltpu.SemaphoreType.DMA((2, 2, N_KV_HEADS))]
    c_in, c_out, c_shapes = _cast_plumbing(casts, b // eb)
    body = functools.partial(_attn_kernel, cached=cache is not None)
    outs = pl.pallas_call(
        _with_casts(body, len(args), 1, len(casts)),
        out_shape=tuple([jax.ShapeDtypeStruct((b, n, width), BF16)] + c_shapes),
        grid=(b // eb, n // tq),
        in_specs=in_specs + c_in,
        out_specs=tuple([pl.BlockSpec((eb, tq, width), lambda bi, qi: (bi, qi, 0))] + c_out),
        scratch_shapes=scratch,
        compiler_params=_params(("arbitrary", "arbitrary")),
        name="attention_cached" if cache is not None else "attention",
    )(*args, *[arr for arr, _ in casts])
    return outs[0], list(outs[1:])


def _post_kernel(a_ref, x_ref, mod_ref, g_ref, wo_ref, w1_ref, w2_ref, *rest, final):
    if final:
        gf_ref, o_ref = rest
    else:
        (o_ref,) = rest
    gate1 = mod_ref[0, 0, :, 2 * D_MODEL:3 * D_MODEL]
    gate2 = mod_ref[0, 0, :, 5 * D_MODEL:6 * D_MODEL]
    n_ff = D_FF // POST_FF_TILE

    def attn_residual(t):
        rows = slice(t * POST_SUB_TILE, (t + 1) * POST_SUB_TILE)
        x = x_ref[rows, :] + gate1 * jnp.dot(a_ref[rows, :], wo_ref[...], preferred_element_type=F32)
        return x, _norm_modulate(x, g_ref[...], mod_ref, 1).astype(BF16)

    n_sub = x_ref.shape[0] // POST_SUB_TILE
    nxt = attn_residual(0)
    for t in range(n_sub):
        x, h = nxt
        if t + 1 < n_sub:
            nxt = attn_residual(t + 1)
        y = None
        for f in range(n_ff):
            u = jnp.dot(h, w1_ref[:, f * POST_FF_TILE:(f + 1) * POST_FF_TILE], preferred_element_type=F32)
            u = jnp.square(jnp.maximum(u, 0.0)).astype(BF16)
            part = jnp.dot(u, w2_ref[f * POST_FF_TILE:(f + 1) * POST_FF_TILE, :], preferred_element_type=F32)
            y = part if y is None else y + part
        x = x + gate2 * y
        if final:
            x = x * lax.rsqrt(jnp.mean(x * x, axis=-1, keepdims=True) + EPS) * gf_ref[...]
        o_ref[t * POST_SUB_TILE:(t + 1) * POST_SUB_TILE, :] = x


def _post(a, x, mod, mod_rows, g, wo, w1, w2, gf=None, casts=()):
    rows = x.shape[0]
    row_block = pl.BlockSpec((POST_ROW_TILE, D_MODEL), lambda i: (i, 0))
    in_specs = [
        row_block,
        row_block,
        _mod_spec(mod_rows, POST_ROW_TILE),
        _const_spec((1, D_MODEL)),
        _const_spec((D_MODEL, D_MODEL)),
        _const_spec((D_MODEL, D_FF)),
        _const_spec((D_FF, D_MODEL)),
    ]
    args = [a, x, mod, g, wo, w1, w2]
    if gf is not None:
        in_specs.append(_const_spec((1, D_MODEL)))
        args.append(gf)
    steps = rows // POST_ROW_TILE
    c_in, c_out, c_shapes = _cast_plumbing(casts, steps)
    body = functools.partial(_post_kernel, final=gf is not None)
    outs = pl.pallas_call(
        _with_casts(body, len(args), 1, len(casts)),
        out_shape=tuple([jax.ShapeDtypeStruct((rows, D_MODEL), F32)] + c_shapes),
        grid=(steps,),
        in_specs=in_specs + c_in,
        out_specs=tuple([row_block] + c_out),
        compiler_params=_params(("parallel",)),
        name="post_final" if gf is not None else "post",
    )(*args, *[arr for arr, _ in casts])
    return outs[0], list(outs[1:])


def _ret_in_kernel(x_ref, mod_ref, g_ref, w_ref, *rest, use_rope):
    if use_rope:
        cos_ref, sin_ref, q_ref, k_ref, v_ref, gate_ref = rest
    else:
        q_ref, k_ref, v_ref, gate_ref = rest
    width = RET_HEADS * RET_DK
    tm = PROJ_SUB_TILE
    n_sub = x_ref.shape[0] // tm

    def normed(sub):
        return _norm_modulate(x_ref[sub * tm:(sub + 1) * tm, :], g_ref[...], mod_ref, 0).astype(BF16)

    h_next = normed(0)
    for sub in range(n_sub):
        h = h_next
        if sub + 1 < n_sub:
            h_next = normed(sub + 1)
        rows = slice(sub * tm, (sub + 1) * tm)
        for part, ref in enumerate((q_ref, k_ref, v_ref, gate_ref)):
            t = jnp.dot(h, w_ref[:, part * width:(part + 1) * width], preferred_element_type=F32)
            if part < 2 and use_rope:
                for c in range(width // V7X_LANES):
                    half = (c % (RET_DK // V7X_LANES)) * V7X_LANES
                    tc = t[:, c * V7X_LANES:(c + 1) * V7X_LANES]
                    tc = (tc * cos_ref[rows, half:half + V7X_LANES]
                          + _swap_halves(tc) * sin_ref[rows, half:half + V7X_LANES])
                    if part == 0:
                        tc = tc * RET_DK ** -0.5
                    ref[rows, c * V7X_LANES:(c + 1) * V7X_LANES] = tc.astype(ref.dtype)
            else:
                if part == 0:
                    t = t * RET_DK ** -0.5
                ref[rows, :] = t.astype(ref.dtype)


def _ret_in(x, mod, mod_rows, g, w, rope, tile, casts=()):
    rows = x.shape[0]
    width = RET_HEADS * RET_DK
    in_specs = [
        pl.BlockSpec((tile, D_MODEL), lambda i: (i, 0)),
        _mod_spec(mod_rows, tile),
        _const_spec((1, D_MODEL)),
        _const_spec((D_MODEL, RET_W)),
    ]
    args = [x, mod, g, w]
    if rope is not None:
        seq_tiles = rope[0].shape[0] // tile
        in_specs += [pl.BlockSpec((tile, RET_DK), lambda i: (i % seq_tiles, 0))] * 2
        args += list(rope)
    out_block = pl.BlockSpec((tile, width), lambda i: (i, 0))
    steps = rows // tile
    c_in, c_out, c_shapes = _cast_plumbing(casts, steps)
    body = functools.partial(_ret_in_kernel, use_rope=rope is not None)
    outs = pl.pallas_call(
        _with_casts(body, len(args), 4, len(casts)),
        out_shape=tuple([jax.ShapeDtypeStruct((rows, width), BF16)] * 3
                        + [jax.ShapeDtypeStruct((rows, width), F32)] + c_shapes),
        grid=(steps,),
        in_specs=in_specs + c_in,
        out_specs=tuple([out_block] * 4 + c_out),
        compiler_params=_params(("parallel",)),
        name="ret_in_rope" if rope is not None else "ret_in",
    )(*args, *[arr for arr, _ in casts])
    return outs[:4], list(outs[4:])


def _log_sigmoid(x):
    return jnp.minimum(x, 0.0) - jnp.log(1.0 + jnp.exp(-jnp.abs(x)))


def _ret_kernel(q_ref, k_ref, v_ref, gate_ref, lg_ref, gn_ref, *rest, n, tq, heads, has_state):
    if has_state:
        s0_ref, y_ref, d_ref = rest
    else:
        y_ref, st_ref, d_ref = rest
    b = pl.program_id(1)
    qi = pl.program_id(2)
    r0 = qi * tq
    nt = (((1,), (1,)), ((), ()))
    sub = min(tq, RET_SUB_TILE)
    lgs = [(_log_sigmoid(lg_ref[hh, 0, 0:1, 0:1]), _log_sigmoid(lg_ref[hh, 1, 0:1, 0:1]))
           for hh in range(heads)]

    @pl.when(b == 0)
    def _():
        ii = r0 + lax.broadcasted_iota(jnp.int32, (tq, n), 0)
        jj = lax.broadcasted_iota(jnp.int32, (tq, n), 1)
        diff = (ii - jj).astype(F32)
        for hh in range(heads):
            lg_f, lg_b = lgs[hh]
            d_ref[hh, qi] = jnp.exp(jnp.where(diff >= 0.0, lg_f, -lg_b) * diff)

    def head_cols(hh):
        return slice(hh * RET_DK, (hh + 1) * RET_DK)

    def scores(unit):
        hh, si = unit
        return lax.dot_general(q_ref[0, si * sub:(si + 1) * sub, head_cols(hh)], k_ref[0, :, head_cols(hh)],
                               nt, preferred_element_type=F32)

    units = [(hh, si) for hh in range(heads) for si in range(tq // sub)]
    nxt = scores(units[0])
    for ui, (hh, si) in enumerate(units):
        s = nxt
        if ui + 1 < len(units):
            nxt = scores(units[ui + 1])
        lg_f, lg_b = lgs[hh]
        cols = head_cols(hh)
        rows = slice(si * sub, (si + 1) * sub)
        v = v_ref[0, :, cols]
        p = (s * d_ref[hh, qi, rows, :]).astype(BF16)
        o = jnp.dot(p, v, preferred_element_type=F32)
        if has_state:
            q = q_ref[0, rows, cols]
            i_col = (r0 + si * sub + lax.broadcasted_iota(jnp.int32, (sub, 1), 0)).astype(F32)
            o = o + (jnp.dot(q, s0_ref[0, 0, 0, hh].astype(BF16), preferred_element_type=F32)
                     * jnp.exp(lg_f * (i_col + 1.0))
                     + jnp.dot(q, s0_ref[0, 0, 1, hh].astype(BF16), preferred_element_type=F32)
                     * jnp.exp(lg_b * (n - i_col)))
        elif si == 0:
            eye = jnp.where(lax.broadcasted_iota(jnp.int32, (RET_DK, RET_DK), 0)
                            == lax.broadcasted_iota(jnp.int32, (RET_DK, RET_DK), 1), 1.0, 0.0).astype(BF16)
            kt = lax.dot_general(eye, k_ref[0, :, cols], nt, preferred_element_type=F32)
            j_row = lax.broadcasted_iota(jnp.int32, (1, n), 1).astype(F32)
            st_ref[0, 0, 0, hh] = jnp.dot((kt * jnp.exp(lg_f * (n - 1.0 - j_row))).astype(BF16), v,
                                          preferred_element_type=F32)
            st_ref[0, 0, 1, hh] = jnp.dot((kt * jnp.exp(lg_b * j_row)).astype(BF16), v,
                                          preferred_element_type=F32)
        mu = jnp.mean(o, axis=-1, keepdims=True)
        oc = o - mu
        var = jnp.mean(oc * oc, axis=-1, keepdims=True)
        on = oc * lax.rsqrt(var + EPS) * gn_ref[:, cols]
        gt = gate_ref[0, rows, cols]
        y_ref[0, rows, cols] = (gt * _sigmoid(gt) * on).astype(y_ref.dtype)


def _retention(q, k, v, gate, lg, gn, state0, heads, tq):
    b, n, _ = q.shape
    assert n % tq == 0 and RET_HEADS % heads == 0 and (state0 is not None or tq == n)
    width = heads * RET_DK
    q_tile = pl.BlockSpec((1, tq, width), lambda hg, bi, qi: (bi, qi, hg))
    seq = pl.BlockSpec((1, n, width), lambda hg, bi, qi: (bi, 0, hg))
    st = pl.BlockSpec((1, 1, 2, heads, RET_DK, RET_DV), lambda hg, bi, qi: (bi, 0, 0, hg, 0, 0))
    in_specs = [q_tile, seq, seq, q_tile,
                pl.BlockSpec((heads, 2, V7X_SUBLANES, V7X_LANES), lambda hg, bi, qi: (hg, 0, 0, 0)),
                pl.BlockSpec((1, width), lambda hg, bi, qi: (0, hg))]
    args = [q, k, v, gate, lg, gn]
    y_shape = jax.ShapeDtypeStruct((b, n, RET_HEADS * RET_DV), BF16)
    if state0 is not None:
        in_specs.append(st)
        args.append(state0)
        out_shape, out_specs = y_shape, q_tile
    else:
        out_shape = (y_shape, jax.ShapeDtypeStruct((b, 1, 2, RET_HEADS, RET_DK, RET_DV), F32))
        out_specs = (q_tile, st)
    return pl.pallas_call(
        functools.partial(_ret_kernel, n=n, tq=tq, heads=heads, has_state=state0 is not None),
        out_shape=out_shape,
        grid=(RET_HEADS // heads, b, n // tq),
        in_specs=in_specs,
        out_specs=out_specs,
        scratch_shapes=[pltpu.VMEM((heads, n // tq, tq, n), F32)],
        compiler_params=_params(("arbitrary", "arbitrary", "arbitrary")),
        name="retention_state" if state0 is not None else "retention",
    )(*args)


def kernel(x_prompt, x_sample, cache_k, cache_v, state_ret, c, c_ctx, w_mod, b_mod, norm_g,
           attn_w_qkv, attn_q_norm, attn_k_norm, attn_w_o, ret_w_qkvg, ret_decay_logit, ret_gn_w,
           ret_w_o, mlp_w1, mlp_w2, final_norm_g):
    bp, sp, d = x_prompt.shape
    bs, ss, _ = x_sample.shape
    depth = w_mod.shape[0]
    assert d == D_MODEL and depth == 2 and bs % V7X_SUBLANES == 0
    assert (bp * sp) % POST_ROW_TILE == 0 and ss % POST_ROW_TILE == 0 and POST_ROW_TILE % ROW_TILE == 0

    mod, (wqkv,) = _modulation(c, c_ctx, w_mod, b_mod, casts=[(attn_w_qkv, 0)])
    ctx_mod = (0, bs)
    lat_mod = (ss, 0)

    xp = x_prompt.reshape(bp * sp, d)
    xs = x_sample.reshape(bs * ss, d)
    nkv = N_KV_HEADS * HEAD_DIM

    g0 = norm_g[0, 0][None, :]
    g1 = norm_g[0, 1][None, :]
    qn = attn_q_norm[0][None, :]
    kn = attn_k_norm[0][None, :]

    (qp, kp, vp), (w1,) = _attn_in(xp, mod, (0,) + ctx_mod, g0, wqkv, qn, kn, None, PROJ_SUB_TILE,
                                   casts=[(mlp_w1, 0)])
    reorder = np.zeros((HEAD_DIM, HEAD_DIM), np.float32)
    reorder[_PAIRED_ORDER, np.arange(HEAD_DIM)] = 1.0
    reorder = jnp.asarray(reorder, BF16)
    (qs, ks, vs), (w2, wo) = _attn_in(xs, mod, (0,) + lat_mod, g0, wqkv, qn, kn,
                                      _rope_tables(ss, HEAD_DIM, True) + (reorder,), ROW_TILE,
                                      casts=[(mlp_w2, 0), (attn_w_o, 0)])

    ap, _ = _attention(qp.reshape(bp, sp, -1), kp.reshape(bp, sp * N_KV_HEADS, HEAD_DIM),
                       vp.reshape(bp, sp * N_KV_HEADS, HEAD_DIM))
    xp, _ = _post(ap.reshape(bp * sp, -1), xp, mod, (0,) + ctx_mod, g1, wo, w1, w2)
    cache = (cache_k, cache_v, reorder)
    as_, (wr,) = _attention(qs.reshape(bs, ss, -1), ks.reshape(bs, ss, nkv), vs.reshape(bs, ss, nkv), cache,
                            casts=[(ret_w_qkvg, 0)])
    xs, _ = _post(as_.reshape(bs * ss, -1), xs, mod, (0,) + lat_mod, g1, wo, w1, w2)

    g0 = norm_g[1, 0][None, :]
    g1 = norm_g[1, 1][None, :]
    gn = ret_gn_w[0][None, :]
    gf = final_norm_g[None, :]
    lg = jnp.broadcast_to(ret_decay_logit[0].T[:, :, None, None],
                          (RET_HEADS, 2, V7X_SUBLANES, V7X_LANES))
    hw = RET_HEADS * RET_DK

    (q, k, v, gate), (w1,) = _ret_in(xp, mod, (1,) + ctx_mod, g0, wr, None, PROJ_SUB_TILE, casts=[(mlp_w1, 1)])
    yp, new_state = _retention(q.reshape(bp, sp, hw), k.reshape(bp, sp, hw), v.reshape(bp, sp, hw),
                               gate.reshape(bp, sp, hw), lg, gn, None, RET_HEADS, sp)
    (q, k, v, gate), (w2, wo) = _ret_in(xs, mod, (1,) + lat_mod, g0, wr, _rope_tables(ss, RET_DK, False),
                                        ROW_TILE, casts=[(mlp_w2, 1), (ret_w_o, 0)])
    ys = _retention(q.reshape(bs, ss, hw), k.reshape(bs, ss, hw), v.reshape(bs, ss, hw),
                    gate.reshape(bs, ss, hw), lg, gn, state_ret, RET_STATE_HEADS, RET_Q_TILE)
    y_prompt, _ = _post(yp.reshape(bp * sp, hw), xp, mod, (1,) + ctx_mod, g1, wo, w1, w2, gf)
    y_sample, _ = _post(ys.reshape(bs * ss, hw), xs, mod, (1,) + lat_mod, g1, wo, w1, w2, gf)

    return (y_prompt.reshape(bp, sp, d),
            y_sample.reshape(bs, ss, d),
            kp.reshape(bp, 1, sp, N_KV_HEADS, HEAD_DIM),
            vp.reshape(bp, 1, sp, N_KV_HEADS, HEAD_DIM),
            new_state)
```

```python
import functools

import numpy as np
import jax
import jax.numpy as jnp
from jax import lax
from jax.experimental import pallas as pl
from jax.experimental.pallas import tpu as pltpu

F32 = jnp.float32
BF16 = jnp.bfloat16

D_MODEL = 1024
GRID_W = 64
N_HEADS = 8
N_KV_HEADS = 2
HEAD_DIM = 128
GROUP = N_HEADS // N_KV_HEADS
ROPE_BASE = 10000.0
RET_HEADS = 4
RET_DK = 256
RET_DV = 256
D_FF = 4 * D_MODEL
Q_SCALE = HEAD_DIM ** -0.5 * float(np.log2(np.e))
EPS = 1e-6
QKV_W = (N_HEADS + 2 * N_KV_HEADS) * HEAD_DIM
RET_W = 2 * RET_HEADS * RET_DK + 2 * RET_HEADS * RET_DV

V7X_LANES = 128
V7X_SUBLANES = 8
MOD_COL_TILE = 2048
ROW_TILE = 1024
PROJ_SUB_TILE = 512
POST_ROW_TILE = 1024
POST_SUB_TILE = 512
POST_FF_TILE = 1024
ATTN_Q_TILE = 512
RET_Q_TILE = 1024
RET_SUB_TILE = 256
RET_STATE_HEADS = 2
VMEM_LIMIT = 56 * 1024 * 1024


def _params(sem, vmem=VMEM_LIMIT):
    return pltpu.CompilerParams(dimension_semantics=sem, vmem_limit_bytes=vmem)


def _const_spec(shape):
    nd = len(shape)
    return pl.BlockSpec(shape, lambda *_: (0,) * nd, pipeline_mode=pl.Buffered(1))


def _sigmoid(x):
    return 1.0 / (1.0 + jnp.exp(-x))


def _cast_plumbing(casts, steps):
    in_specs, out_specs, out_shapes = [], [], []
    for arr, layer in casts:
        _, r, c = arr.shape
        slab = r // steps
        assert slab * steps == r and slab % 16 == 0
        in_specs.append(pl.BlockSpec((1, slab, c), functools.partial(lambda *idx, l: (l, idx[0], 0), l=layer)))
        out_specs.append(pl.BlockSpec((slab, c), lambda *idx: (idx[0], 0)))
        out_shapes.append(jax.ShapeDtypeStruct((r, c), BF16))
    return in_specs, out_specs, out_shapes


def _with_casts(body, n_in, n_out, n_cast):
    def kernel(*refs):
        ins = refs[:n_in]
        cast_in = refs[n_in:n_in + n_cast]
        outs = refs[n_in + n_cast:n_in + n_cast + n_out]
        cast_out = refs[n_in + n_cast + n_out:n_in + 2 * n_cast + n_out]
        scratch = refs[n_in + 2 * n_cast + n_out:]
        for src, dst in zip(cast_in, cast_out):
            dst[...] = src[0].astype(dst.dtype)
        body(*ins, *outs, *scratch)
    return kernel


def _mod_kernel(c_ref, cctx_ref, w_ref, b_ref, o_ref):
    cond = jnp.concatenate([c_ref[...], jnp.broadcast_to(cctx_ref[...], (V7X_SUBLANES, D_MODEL))], axis=0)
    s = (cond * _sigmoid(cond)).astype(BF16)
    bias = b_ref[pl.ds(pl.program_id(0), 1), :]
    o_ref[0, :, 0, :] = jnp.dot(s, w_ref[0].astype(BF16), preferred_element_type=F32) + bias


def _modulation(c, c_ctx, w_mod, b_mod, casts=()):
    depth = w_mod.shape[0]
    rows = c.shape[0] + V7X_SUBLANES
    tn = MOD_COL_TILE
    c_in, c_out, c_shapes = _cast_plumbing(casts, depth)
    outs = pl.pallas_call(
        _with_casts(_mod_kernel, 4, 1, len(casts)),
        out_shape=tuple([jax.ShapeDtypeStruct((depth, rows, 1, 6 * D_MODEL), F32)] + c_shapes),
        grid=(depth, 6 * D_MODEL // tn),
        in_specs=[
            pl.BlockSpec(c.shape, lambda i, j: (0, 0)),
            pl.BlockSpec((1, D_MODEL), lambda i, j: (0, 0)),
            pl.BlockSpec((1, D_MODEL, tn), lambda i, j: (i, 0, j)),
            pl.BlockSpec((depth, tn), lambda i, j: (0, j)),
        ] + c_in,
        out_specs=tuple([pl.BlockSpec((1, rows, 1, tn), lambda i, j: (i, 0, 0, j))] + c_out),
        compiler_params=_params(("arbitrary", "arbitrary")),
        name="modulation",
    )(c, c_ctx.reshape(1, D_MODEL), w_mod, b_mod, *[arr for arr, _ in casts])
    return outs[0], list(outs[1:])


def _norm_modulate(x, g, mod_ref, which):
    y = x * lax.rsqrt(jnp.mean(x * x, axis=-1, keepdims=True) + EPS) * g
    shift = mod_ref[0, 0, :, (3 * which) * D_MODEL:(3 * which + 1) * D_MODEL]
    scale = mod_ref[0, 0, :, (3 * which + 1) * D_MODEL:(3 * which + 2) * D_MODEL]
    return y * (1.0 + scale) + shift


def _mod_spec(mod_rows, tile):
    layer, rows_per_mod_row, first_row = mod_rows
    if rows_per_mod_row:
        assert rows_per_mod_row % tile == 0
        tiles = rows_per_mod_row // tile
        return pl.BlockSpec((1, 1, 1, 6 * D_MODEL), lambda i: (layer, first_row + i // tiles, 0, 0))
    return pl.BlockSpec((1, 1, 1, 6 * D_MODEL), lambda i: (layer, first_row, 0, 0))


def _rope_tables(n, dim, paired):
    quarter = dim // 4
    rows = np.repeat(np.arange(n // GRID_W), GRID_W).astype(np.float64)
    cols = np.tile(np.arange(GRID_W), n // GRID_W).astype(np.float64)
    freqs = ROPE_BASE ** (-np.arange(quarter, dtype=np.float64) / quarter)
    ar = rows[:, None] * freqs
    ac = cols[:, None] * freqs
    if paired:
        cos = np.concatenate([np.cos(ar), np.cos(ac), np.cos(ar), np.cos(ac)], axis=1)
        sin = np.concatenate([-np.sin(ar), -np.sin(ac), np.sin(ar), np.sin(ac)], axis=1)
    else:
        cos = np.concatenate([np.cos(ar), np.cos(ar), np.cos(ac), np.cos(ac)], axis=1)
        sin = np.concatenate([-np.sin(ar), np.sin(ar), -np.sin(ac), np.sin(ac)], axis=1)
    return jnp.asarray(cos, F32), jnp.asarray(sin, F32)


_Q = HEAD_DIM // 4
_PAIRED_ORDER = np.concatenate([np.arange(0, _Q), np.arange(2 * _Q, 3 * _Q),
                                np.arange(_Q, 2 * _Q), np.arange(3 * _Q, 4 * _Q)])


def _paired_lanes(w):
    lane = lax.broadcasted_iota(jnp.int32, w.shape, 1)
    from_c = pltpu.roll(w, V7X_LANES - _Q, axis=1)
    from_b = pltpu.roll(w, _Q, axis=1)
    return jnp.where((lane >= _Q) & (lane < 2 * _Q), from_c,
                     jnp.where((lane >= 2 * _Q) & (lane < 3 * _Q), from_b, w))


def _swap_halves(t):
    return pltpu.roll(t, V7X_LANES // 2, axis=1)


def _attn_in_kernel(x_ref, mod_ref, g_ref, w_ref, qn_ref, kn_ref, *rest, use_rope):
    nq = N_HEADS * HEAD_DIM
    nk = N_KV_HEADS * HEAD_DIM
    if use_rope:
        cos_ref, sin_ref, reorder_ref, q_ref, k_ref, v_ref, wqk_ref = rest

        @pl.when(pl.program_id(0) == 0)
        def _():
            for head in range(N_HEADS + N_KV_HEADS):
                cols = slice(head * HEAD_DIM, (head + 1) * HEAD_DIM)
                wqk_ref[:, cols] = jnp.dot(w_ref[:, cols], reorder_ref[...],
                                           preferred_element_type=F32).astype(BF16)
    else:
        q_ref, k_ref, v_ref = rest
        wqk_ref = w_ref
    tm = PROJ_SUB_TILE
    n_sub = x_ref.shape[0] // tm
    pair_w = 2 * HEAD_DIM
    n_pairs = (N_HEADS + N_KV_HEADS) // 2
    qw = qn_ref[...] * Q_SCALE
    kw = kn_ref[...]
    if use_rope:
        qw = _paired_lanes(qw)
        kw = _paired_lanes(kw)
        ra = lax.broadcasted_iota(jnp.int32, (pair_w, pair_w), 0) // HEAD_DIM
        rb = lax.broadcasted_iota(jnp.int32, (pair_w, pair_w), 1) // HEAD_DIM
        head_sum = jnp.where(ra == rb, 1.0, 0.0).astype(BF16)

    def normed(sub):
        return _norm_modulate(x_ref[sub * tm:(sub + 1) * tm, :], g_ref[...], mod_ref, 0).astype(BF16)

    def put_kv(ref, sub, kh, val):
        if use_rope:
            ref[sub * tm:(sub + 1) * tm, kh * HEAD_DIM:(kh + 1) * HEAD_DIM] = val.astype(ref.dtype)
        else:
            ref[pl.ds(sub * tm * N_KV_HEADS + kh, tm, stride=N_KV_HEADS), :] = val.astype(ref.dtype)

    h_next = normed(0)
    for sub in range(n_sub):
        h = h_next
        if sub + 1 < n_sub:
            h_next = normed(sub + 1)
        rows = slice(sub * tm, (sub + 1) * tm)
        if use_rope:
            q_tabs = (qw * cos_ref[rows, :], _swap_halves(qw) * sin_ref[rows, :])
            k_tabs = (kw * cos_ref[rows, :], _swap_halves(kw) * sin_ref[rows, :])
        t2_next = jnp.dot(h, wqk_ref[:, 0:pair_w], preferred_element_type=F32)
        for pair in range(n_pairs):
            t2 = t2_next
            if pair + 1 < n_pairs:
                t2_next = jnp.dot(h, wqk_ref[:, (pair + 1) * pair_w:(pair + 2) * pair_w],
                                  preferred_element_type=F32)
            else:
                t2_next = jnp.dot(h, w_ref[:, nq + nk:], preferred_element_type=F32)
            if use_rope:
                ss2 = jnp.dot((t2 * t2).astype(BF16), head_sum, preferred_element_type=F32)
                r2 = lax.rsqrt(ss2 * (1.0 / HEAD_DIM) + EPS)
            for j in range(2):
                head = 2 * pair + j
                t = t2[:, j * HEAD_DIM:(j + 1) * HEAD_DIM]
                if use_rope:
                    r = r2[:, j * HEAD_DIM:(j + 1) * HEAD_DIM]
                    wc, ws = q_tabs if head < N_HEADS else k_tabs
                    t = (t * wc + _swap_halves(t) * ws) * r
                else:
                    r = lax.rsqrt(jnp.mean(t * t, axis=-1, keepdims=True) + EPS)
                    t = t * r * (qw if head < N_HEADS else kw)
                if head < N_HEADS:
                    q_ref[rows, head * HEAD_DIM:(head + 1) * HEAD_DIM] = t.astype(q_ref.dtype)
                else:
                    put_kv(k_ref, sub, head - N_HEADS, t)
        vv = t2_next
        for kh in range(N_KV_HEADS):
            put_kv(v_ref, sub, kh, vv[:, kh * HEAD_DIM:(kh + 1) * HEAD_DIM])


def _attn_in(x, mod, mod_rows, g, w, qn, kn, rope, tile, casts=()):
    rows = x.shape[0]
    nk = N_KV_HEADS * HEAD_DIM
    if rope is None:
        kv_shape = jax.ShapeDtypeStruct((rows * N_KV_HEADS, HEAD_DIM), F32)
        kv_spec = pl.BlockSpec((tile * N_KV_HEADS, HEAD_DIM), lambda i: (i, 0))
    else:
        kv_shape = jax.ShapeDtypeStruct((rows, nk), BF16)
        kv_spec = pl.BlockSpec((tile, nk), lambda i: (i, 0))
    in_specs = [
        pl.BlockSpec((tile, D_MODEL), lambda i: (i, 0)),
        _mod_spec(mod_rows, tile),
        _const_spec((1, D_MODEL)),
        _const_spec((D_MODEL, QKV_W)),
        _const_spec((1, HEAD_DIM)),
        _const_spec((1, HEAD_DIM)),
    ]
    args = [x, mod, g, w, qn, kn]
    scratch = []
    if rope is not None:
        cos, sin, reorder = rope
        seq_tiles = cos.shape[0] // tile
        in_specs += [pl.BlockSpec((tile, HEAD_DIM), lambda i: (i % seq_tiles, 0))] * 2
        in_specs.append(_const_spec((HEAD_DIM, HEAD_DIM)))
        args += [cos, sin, reorder]
        scratch = [pltpu.VMEM((D_MODEL, (N_HEADS + N_KV_HEADS) * HEAD_DIM), BF16)]
    steps = rows // tile
    c_in, c_out, c_shapes = _cast_plumbing(casts, steps)
    out_specs = [pl.BlockSpec((tile, N_HEADS * HEAD_DIM), lambda i: (i, 0)), kv_spec, kv_spec]
    out_shapes = [jax.ShapeDtypeStruct((rows, N_HEADS * HEAD_DIM), BF16), kv_shape, kv_shape]
    body = functools.partial(_attn_in_kernel, use_rope=rope is not None)
    outs = pl.pallas_call(
        _with_casts(body, len(args), len(out_shapes), len(casts)),
        out_shape=tuple(out_shapes + c_shapes),
        grid=(steps,),
        in_specs=in_specs + c_in,
        out_specs=tuple(out_specs + c_out),
        scratch_shapes=scratch,
        compiler_params=_params(("arbitrary",)),
        name="attn_in_rope" if rope is not None else "attn_in",
    )(*args, *[arr for arr, _ in casts])
    return outs[:3], list(outs[3:])


def _kv_head(ref, e, kv):
    if ref.dtype == BF16:
        return ref[e, :, kv * HEAD_DIM:(kv + 1) * HEAD_DIM]
    tokens = ref.shape[1] // N_KV_HEADS
    return ref[e, pl.ds(kv, tokens, stride=N_KV_HEADS), :].astype(BF16)


def _attn_kernel(q_ref, k_ref, v_ref, *rest, cached):
    def with_ones(v):
        return jnp.concatenate([v, jnp.ones_like(v)], axis=1)

    if cached:
        ck_hbm, cv_hbm, reorder_ref, o_ref, ck_scr, cv_scr, stage, sems = rest
        b = pl.program_id(0)
        n_b = pl.num_programs(0)

        def head_copies(bi, slot):
            return [pltpu.make_async_copy(src.at[bi, 0, :, kv, :], stage.at[slot, t, kv], sems.at[slot, t, kv])
                    for t, src in enumerate((ck_hbm, cv_hbm)) for kv in range(N_KV_HEADS)]

        @pl.when(pl.program_id(1) == 0)
        def _():
            slot = b % 2

            @pl.when(b == 0)
            def _():
                for cp in head_copies(0, 0):
                    cp.start()

            @pl.when(b + 1 < n_b)
            def _():
                for cp in head_copies(b + 1, 1 - slot):
                    cp.start()

            for cp in head_copies(b, slot):
                cp.wait()
            for kv in range(N_KV_HEADS):
                ck_scr[kv] = jnp.dot(stage[slot, 0, kv].astype(BF16), reorder_ref[...],
                                     preferred_element_type=F32).astype(BF16)
                cv_scr[kv] = with_ones(stage[slot, 1, kv].astype(BF16))
    else:
        (o_ref,) = rest
    nt = (((1,), (1,)), ((), ()))
    n_elem = q_ref.shape[0]
    ks = [[_kv_head(k_ref, e, kv) for kv in range(N_KV_HEADS)] for e in range(n_elem)]
    vs = [[with_ones(_kv_head(v_ref, e, kv)) for kv in range(N_KV_HEADS)] for e in range(n_elem)]

    def scores(unit):
        e, head = unit
        kv = head // GROUP
        q = q_ref[e, :, head * HEAD_DIM:(head + 1) * HEAD_DIM]
        s = lax.dot_general(q, ks[e][kv], nt, preferred_element_type=F32)
        sc = lax.dot_general(q, ck_scr[kv], nt, preferred_element_type=F32) if cached else None
        return s, sc

    units = [(e, head) for e in range(n_elem) for head in range(N_HEADS)]
    nxt = scores(units[0])
    for ui, (e, head) in enumerate(units):
        kv = head // GROUP
        s, sc = nxt
        if ui + 1 < len(units):
            nxt = scores(units[ui + 1])
        m = jnp.max(s, axis=-1, keepdims=True)
        if cached:
            m = jnp.maximum(m, jnp.max(sc, axis=-1, keepdims=True))
        o = jnp.dot(jnp.exp2(s - m).astype(BF16), vs[e][kv], preferred_element_type=F32)
        if cached:
            o = o + jnp.dot(jnp.exp2(sc - m).astype(BF16), cv_scr[kv], preferred_element_type=F32)
        o_ref[e, :, head * HEAD_DIM:(head + 1) * HEAD_DIM] = (o[:, :HEAD_DIM] / o[:, HEAD_DIM:]).astype(o_ref.dtype)


def _attention(q, k, v, cache=None, casts=()):
    b, n, width = q.shape
    tq = min(ATTN_Q_TILE, n)
    eb = 1 if cache is not None else max(1, min(b, ATTN_Q_TILE // n))
    assert b % eb == 0
    in_specs = [pl.BlockSpec((eb, tq, width), lambda bi, qi: (bi, qi, 0))]
    in_specs += [pl.BlockSpec((eb,) + k.shape[1:], lambda bi, qi: (bi, 0, 0))] * 2
    args = [q, k, v]
    scratch = []
    if cache is not None:
        past = cache[0].shape[2]
        in_specs += [pl.BlockSpec(memory_space=pl.ANY)] * 2
        in_specs.append(pl.BlockSpec((HEAD_DIM, HEAD_DIM), lambda bi, qi: (0, 0)))
        args += list(cache)
        scratch = [pltpu.VMEM((N_KV_HEADS, past, HEAD_DIM), BF16),
                   pltpu.VMEM((N_KV_HEADS, past, 2 * HEAD_DIM), BF16)]
        scratch += [pltpu.VMEM((2, 2, N_KV_HEADS, past, HEAD_DIM), F32),
                    pltpu.SemaphoreType.DMA((2, 2, N_KV_HEADS))]
    c_in, c_out, c_shapes = _cast_plumbing(casts, b // eb)
    body = functools.partial(_attn_kernel, cached=cache is not None)
    outs = pl.pallas_call(
        _with_casts(body, len(args), 1, len(casts)),
        out_shape=tuple([jax.ShapeDtypeStruct((b, n, width), BF16)] + c_shapes),
        grid=(b // eb, n // tq),
        in_specs=in_specs + c_in,
        out_specs=tuple([pl.BlockSpec((eb, tq, width), lambda bi, qi: (bi, qi, 0))] + c_out),
        scratch_shapes=scratch,
        compiler_params=_params(("arbitrary", "arbitrary")),
        name="attention_cached" if cache is not None else "attention",
    )(*args, *[arr for arr, _ in casts])
    return outs[0], list(outs[1:])


def _post_kernel(a_ref, x_ref, mod_ref, g_ref, wo_ref, w1_ref, w2_ref, *rest, final):
    if final:
        gf_ref, o_ref = rest
    else:
        (o_ref,) = rest
    gate1 = mod_ref[0, 0, :, 2 * D_MODEL:3 * D_MODEL]
    gate2 = mod_ref[0, 0, :, 5 * D_MODEL:6 * D_MODEL]
    n_ff = D_FF // POST_FF_TILE

    def attn_residual(t):
        rows = slice(t * POST_SUB_TILE, (t + 1) * POST_SUB_TILE)
        x = x_ref[rows, :] + gate1 * jnp.dot(a_ref[rows, :], wo_ref[...], preferred_element_type=F32)
        return x, _norm_modulate(x, g_ref[...], mod_ref, 1).astype(BF16)

    n_sub = x_ref.shape[0] // POST_SUB_TILE
    nxt = attn_residual(0)
    for t in range(n_sub):
        x, h = nxt
        if t + 1 < n_sub:
            nxt = attn_residual(t + 1)
        y = None
        for f in range(n_ff):
            u = jnp.dot(h, w1_ref[:, f * POST_FF_TILE:(f + 1) * POST_FF_TILE], preferred_element_type=F32)
            u = jnp.square(jnp.maximum(u, 0.0)).astype(BF16)
            part = jnp.dot(u, w2_ref[f * POST_FF_TILE:(f + 1) * POST_FF_TILE, :], preferred_element_type=F32)
            y = part if y is None else y + part
        x = x + gate2 * y
        if final:
            x = x * lax.rsqrt(jnp.mean(x * x, axis=-1, keepdims=True) + EPS) * gf_ref[...]
        o_ref[t * POST_SUB_TILE:(t + 1) * POST_SUB_TILE, :] = x


def _post(a, x, mod, mod_rows, g, wo, w1, w2, gf=None, casts=()):
    rows = x.shape[0]
    row_block = pl.BlockSpec((POST_ROW_TILE, D_MODEL), lambda i: (i, 0))
    in_specs = [
        row_block,
        row_block,
        _mod_spec(mod_rows, POST_ROW_TILE),
        _const_spec((1, D_MODEL)),
        _const_spec((D_MODEL, D_MODEL)),
        _const_spec((D_MODEL, D_FF)),
        _const_spec((D_FF, D_MODEL)),
    ]
    args = [a, x, mod, g, wo, w1, w2]
    if gf is not None:
        in_specs.append(_const_spec((1, D_MODEL)))
        args.append(gf)
    steps = rows // POST_ROW_TILE
    c_in, c_out, c_shapes = _cast_plumbing(casts, steps)
    body = functools.partial(_post_kernel, final=gf is not None)
    outs = pl.pallas_call(
        _with_casts(body, len(args), 1, len(casts)),
        out_shape=tuple([jax.ShapeDtypeStruct((rows, D_MODEL), F32)] + c_shapes),
        grid=(steps,),
        in_specs=in_specs + c_in,
        out_specs=tuple([row_block] + c_out),
        compiler_params=_params(("parallel",)),
        name="post_final" if gf is not None else "post",
    )(*args, *[arr for arr, _ in casts])
    return outs[0], list(outs[1:])


def _ret_in_kernel(x_ref, mod_ref, g_ref, w_ref, *rest, use_rope):
    if use_rope:
        cos_ref, sin_ref, q_ref, k_ref, v_ref, gate_ref = rest
    else:
        q_ref, k_ref, v_ref, gate_ref = rest
    width = RET_HEADS * RET_DK
    tm = PROJ_SUB_TILE
    n_sub = x_ref.shape[0] // tm

    def normed(sub):
        return _norm_modulate(x_ref[sub * tm:(sub + 1) * tm, :], g_ref[...], mod_ref, 0).astype(BF16)

    h_next = normed(0)
    for sub in range(n_sub):
        h = h_next
        if sub + 1 < n_sub:
            h_next = normed(sub + 1)
        rows = slice(sub * tm, (sub + 1) * tm)
        for part, ref in enumerate((q_ref, k_ref, v_ref, gate_ref)):
            t = jnp.dot(h, w_ref[:, part * width:(part + 1) * width], preferred_element_type=F32)
            if part < 2 and use_rope:
                for c in range(width // V7X_LANES):
                    half = (c % (RET_DK // V7X_LANES)) * V7X_LANES
                    tc = t[:, c * V7X_LANES:(c + 1) * V7X_LANES]
                    tc = (tc * cos_ref[rows, half:half + V7X_LANES]
                          + _swap_halves(tc) * sin_ref[rows, half:half + V7X_LANES])
                    if part == 0:
                        tc = tc * RET_DK ** -0.5
                    ref[rows, c * V7X_LANES:(c + 1) * V7X_LANES] = tc.astype(ref.dtype)
            else:
                if part == 0:
                    t = t * RET_DK ** -0.5
                ref[rows, :] = t.astype(ref.dtype)


def _ret_in(x, mod, mod_rows, g, w, rope, tile, casts=()):
    rows = x.shape[0]
    width = RET_HEADS * RET_DK
    in_specs = [
        pl.BlockSpec((tile, D_MODEL), lambda i: (i, 0)),
        _mod_spec(mod_rows, tile),
        _const_spec((1, D_MODEL)),
        _const_spec((D_MODEL, RET_W)),
    ]
    args = [x, mod, g, w]
    if rope is not None:
        seq_tiles = rope[0].shape[0] // tile
        in_specs += [pl.BlockSpec((tile, RET_DK), lambda i: (i % seq_tiles, 0))] * 2
        args += list(rope)
    out_block = pl.BlockSpec((tile, width), lambda i: (i, 0))
    steps = rows // tile
    c_in, c_out, c_shapes = _cast_plumbing(casts, steps)
    body = functools.partial(_ret_in_kernel, use_rope=rope is not None)
    outs = pl.pallas_call(
        _with_casts(body, len(args), 4, len(casts)),
        out_shape=tuple([jax.ShapeDtypeStruct((rows, width), BF16)] * 3
                        + [jax.ShapeDtypeStruct((rows, width), F32)] + c_shapes),
        grid=(steps,),
        in_specs=in_specs + c_in,
        out_specs=tuple([out_block] * 4 + c_out),
        compiler_params=_params(("parallel",)),
        name="ret_in_rope" if rope is not None else "ret_in",
    )(*args, *[arr for arr, _ in casts])
    return outs[:4], list(outs[4:])


def _log_sigmoid(x):
    return jnp.minimum(x, 0.0) - jnp.log(1.0 + jnp.exp(-jnp.abs(x)))


def _ret_kernel(q_ref, k_ref, v_ref, gate_ref, lg_ref, gn_ref, *rest, n, tq, heads, has_state):
    if has_state:
        s0_ref, y_ref, d_ref = rest
    else:
        y_ref, st_ref, d_ref = rest
    b = pl.program_id(1)
    qi = pl.program_id(2)
    r0 = qi * tq
    nt = (((1,), (1,)), ((), ()))
    sub = min(tq, RET_SUB_TILE)
    lgs = [(_log_sigmoid(lg_ref[hh, 0, 0:1, 0:1]), _log_sigmoid(lg_ref[hh, 1, 0:1, 0:1]))
           for hh in range(heads)]

    @pl.when(b == 0)
    def _():
        ii = r0 + lax.broadcasted_iota(jnp.int32, (tq, n), 0)
        jj = lax.broadcasted_iota(jnp.int32, (tq, n), 1)
        diff = (ii - jj).astype(F32)
        for hh in range(heads):
            lg_f, lg_b = lgs[hh]
            d_ref[hh, qi] = jnp.exp(jnp.where(diff >= 0.0, lg_f, -lg_b) * diff)

    def head_cols(hh):
        return slice(hh * RET_DK, (hh + 1) * RET_DK)

    def scores(unit):
        hh, si = unit
        return lax.dot_general(q_ref[0, si * sub:(si + 1) * sub, head_cols(hh)], k_ref[0, :, head_cols(hh)],
                               nt, preferred_element_type=F32)

    units = [(hh, si) for hh in range(heads) for si in range(tq // sub)]
    nxt = scores(units[0])
    for ui, (hh, si) in enumerate(units):
        s = nxt
        if ui + 1 < len(units):
            nxt = scores(units[ui + 1])
        lg_f, lg_b = lgs[hh]
        cols = head_cols(hh)
        rows = slice(si * sub, (si + 1) * sub)
        v = v_ref[0, :, cols]
        p = (s * d_ref[hh, qi, rows, :]).astype(BF16)
        o = jnp.dot(p, v, preferred_element_type=F32)
        if has_state:
            q = q_ref[0, rows, cols]
            i_col = (r0 + si * sub + lax.broadcasted_iota(jnp.int32, (sub, 1), 0)).astype(F32)
            o = o + (jnp.dot(q, s0_ref[0, 0, 0, hh].astype(BF16), preferred_element_type=F32)
                     * jnp.exp(lg_f * (i_col + 1.0))
                     + jnp.dot(q, s0_ref[0, 0, 1, hh].astype(BF16), preferred_element_type=F32)
                     * jnp.exp(lg_b * (n - i_col)))
        elif si == 0:
            eye = jnp.where(lax.broadcasted_iota(jnp.int32, (RET_DK, RET_DK), 0)
                            == lax.broadcasted_iota(jnp.int32, (RET_DK, RET_DK), 1), 1.0, 0.0).astype(BF16)
            kt = lax.dot_general(eye, k_ref[0, :, cols], nt, preferred_element_type=F32)
            j_row = lax.broadcasted_iota(jnp.int32, (1, n), 1).astype(F32)
            st_ref[0, 0, 0, hh] = jnp.dot((kt * jnp.exp(lg_f * (n - 1.0 - j_row))).astype(BF16), v,
                                          preferred_element_type=F32)
            st_ref[0, 0, 1, hh] = jnp.dot((kt * jnp.exp(lg_b * j_row)).astype(BF16), v,
                                          preferred_element_type=F32)
        mu = jnp.mean(o, axis=-1, keepdims=True)
        oc = o - mu
        var = jnp.mean(oc * oc, axis=-1, keepdims=True)
        on = oc * lax.rsqrt(var + EPS) * gn_ref[:, cols]
        gt = gate_ref[0, rows, cols]
        y_ref[0, rows, cols] = (gt * _sigmoid(gt) * on).astype(y_ref.dtype)


def _retention(q, k, v, gate, lg, gn, state0, heads, tq):
    b, n, _ = q.shape
    assert n % tq == 0 and RET_HEADS % heads == 0 and (state0 is not None or tq == n)
    width = heads * RET_DK
    q_tile = pl.BlockSpec((1, tq, width), lambda hg, bi, qi: (bi, qi, hg))
    seq = pl.BlockSpec((1, n, width), lambda hg, bi, qi: (bi, 0, hg))
    st = pl.BlockSpec((1, 1, 2, heads, RET_DK, RET_DV), lambda hg, bi, qi: (bi, 0, 0, hg, 0, 0))
    in_specs = [q_tile, seq, seq, q_tile,
                pl.BlockSpec((heads, 2, V7X_SUBLANES, V7X_LANES), lambda hg, bi, qi: (hg, 0, 0, 0)),
                pl.BlockSpec((1, width), lambda hg, bi, qi: (0, hg))]
    args = [q, k, v, gate, lg, gn]
    y_shape = jax.ShapeDtypeStruct((b, n, RET_HEADS * RET_DV), BF16)
    if state0 is not None:
        in_specs.append(st)
        args.append(state0)
        out_shape, out_specs = y_shape, q_tile
    else:
        out_shape = (y_shape, jax.ShapeDtypeStruct((b, 1, 2, RET_HEADS, RET_DK, RET_DV), F32))
        out_specs = (q_tile, st)
    return pl.pallas_call(
        functools.partial(_ret_kernel, n=n, tq=tq, heads=heads, has_state=state0 is not None),
        out_shape=out_shape,
        grid=(RET_HEADS // heads, b, n // tq),
        in_specs=in_specs,
        out_specs=out_specs,
        scratch_shapes=[pltpu.VMEM((heads, n // tq, tq, n), F32)],
        compiler_params=_params(("arbitrary", "arbitrary", "arbitrary")),
        name="retention_state" if state0 is not None else "retention",
    )(*args)


def kernel(x_prompt, x_sample, cache_k, cache_v, state_ret, c, c_ctx, w_mod, b_mod, norm_g,
           attn_w_qkv, attn_q_norm, attn_k_norm, attn_w_o, ret_w_qkvg, ret_decay_logit, ret_gn_w,
           ret_w_o, mlp_w1, mlp_w2, final_norm_g):
    bp, sp, d = x_prompt.shape
    bs, ss, _ = x_sample.shape
    depth = w_mod.shape[0]
    assert d == D_MODEL and depth == 2 and bs % V7X_SUBLANES == 0
    assert (bp * sp) % POST_ROW_TILE == 0 and ss % POST_ROW_TILE == 0 and POST_ROW_TILE % ROW_TILE == 0

    mod, (wqkv,) = _modulation(c, c_ctx, w_mod, b_mod, casts=[(attn_w_qkv, 0)])
    ctx_mod = (0, bs)
    lat_mod = (ss, 0)

    xp = x_prompt.reshape(bp * sp, d)
    xs = x_sample.reshape(bs * ss, d)
    nkv = N_KV_HEADS * HEAD_DIM

    g0 = norm_g[0, 0][None, :]
    g1 = norm_g[0, 1][None, :]
    qn = attn_q_norm[0][None, :]
    kn = attn_k_norm[0][None, :]

    (qp, kp, vp), _ = _attn_in(xp, mod, (0,) + ctx_mod, g0, wqkv, qn, kn, None, PROJ_SUB_TILE)
    reorder = np.zeros((HEAD_DIM, HEAD_DIM), np.float32)
    reorder[_PAIRED_ORDER, np.arange(HEAD_DIM)] = 1.0
    reorder = jnp.asarray(reorder, BF16)
    (qs, ks, vs), (w1, w2, wo) = _attn_in(xs, mod, (0,) + lat_mod, g0, wqkv, qn, kn,
                                          _rope_tables(ss, HEAD_DIM, True) + (reorder,), ROW_TILE,
                                          casts=[(mlp_w1, 0), (mlp_w2, 0), (attn_w_o, 0)])

    ap, _ = _attention(qp.reshape(bp, sp, -1), kp.reshape(bp, sp * N_KV_HEADS, HEAD_DIM),
                       vp.reshape(bp, sp * N_KV_HEADS, HEAD_DIM))
    xp, _ = _post(ap.reshape(bp * sp, -1), xp, mod, (0,) + ctx_mod, g1, wo, w1, w2)
    cache = (cache_k, cache_v, reorder)
    as_, (wr,) = _attention(qs.reshape(bs, ss, -1), ks.reshape(bs, ss, nkv), vs.reshape(bs, ss, nkv), cache,
                            casts=[(ret_w_qkvg, 0)])
    xs, _ = _post(as_.reshape(bs * ss, -1), xs, mod, (0,) + lat_mod, g1, wo, w1, w2)

    g0 = norm_g[1, 0][None, :]
    g1 = norm_g[1, 1][None, :]
    gn = ret_gn_w[0][None, :]
    gf = final_norm_g[None, :]
    lg = jnp.broadcast_to(ret_decay_logit[0].T[:, :, None, None],
                          (RET_HEADS, 2, V7X_SUBLANES, V7X_LANES))
    hw = RET_HEADS * RET_DK

    (q, k, v, gate), (w1,) = _ret_in(xp, mod, (1,) + ctx_mod, g0, wr, None, PROJ_SUB_TILE, casts=[(mlp_w1, 1)])
    yp, new_state = _retention(q.reshape(bp, sp, hw), k.reshape(bp, sp, hw), v.reshape(bp, sp, hw),
                               gate.reshape(bp, sp, hw), lg, gn, None, RET_HEADS, sp)
    (q, k, v, gate), (w2, wo) = _ret_in(xs, mod, (1,) + lat_mod, g0, wr, _rope_tables(ss, RET_DK, False),
                                        ROW_TILE, casts=[(mlp_w2, 1), (ret_w_o, 0)])
    ys = _retention(q.reshape(bs, ss, hw), k.reshape(bs, ss, hw), v.reshape(bs, ss, hw),
                    gate.reshape(bs, ss, hw), lg, gn, state_ret, RET_STATE_HEADS, RET_Q_TILE)
    y_prompt, _ = _post(yp.reshape(bp * sp, hw), xp, mod, (1,) + ctx_mod, g1, wo, w1, w2, gf)
    y_sample, _ = _post(ys.reshape(bs * ss, hw), xs, mod, (1,) + lat_mod, g1, wo, w1, w2, gf)

    return (y_prompt.reshape(bp, sp, d),
            y_sample.reshape(bs, ss, d),
            kp.reshape(bp, 1, sp, N_KV_HEADS, HEAD_DIM),
            vp.reshape(bp, 1, sp, N_KV_HEADS, HEAD_DIM),
            new_state)
```

```python
import functools

import numpy as np
import jax
import jax.numpy as jnp
from jax import lax
from jax.experimental import pallas as pl
from jax.experimental.pallas import tpu as pltpu

F32 = jnp.float32
BF16 = jnp.bfloat16

D_MODEL = 1024
GRID_W = 64
N_HEADS = 8
N_KV_HEADS = 2
HEAD_DIM = 128
GROUP = N_HEADS // N_KV_HEADS
ROPE_BASE = 10000.0
RET_HEADS = 4
RET_DK = 256
RET_DV = 256
D_FF = 4 * D_MODEL
Q_SCALE = HEAD_DIM ** -0.5 * float(np.log2(np.e))
EPS = 1e-6
QKV_W = (N_HEADS + 2 * N_KV_HEADS) * HEAD_DIM
RET_W = 2 * RET_HEADS * RET_DK + 2 * RET_HEADS * RET_DV

V7X_LANES = 128
V7X_SUBLANES = 8
MOD_COL_TILE = 2048
ROW_TILE = 1024
PROJ_SUB_TILE = 512
POST_ROW_TILE = 1024
POST_SUB_TILE = 512
POST_FF_TILE = 1024
ATTN_Q_TILE = 512
RET_Q_TILE = 1024
RET_SUB_TILE = 256
RET_STATE_HEADS = 2
VMEM_LIMIT = 56 * 1024 * 1024


def _params(sem, vmem=VMEM_LIMIT):
    return pltpu.CompilerParams(dimension_semantics=sem, vmem_limit_bytes=vmem)


def _const_spec(shape):
    nd = len(shape)
    return pl.BlockSpec(shape, lambda *_: (0,) * nd, pipeline_mode=pl.Buffered(1))


def _sigmoid(x):
    return 1.0 / (1.0 + jnp.exp(-x))


def _cast_plumbing(casts, steps):
    in_specs, out_specs, out_shapes = [], [], []
    for arr, layer in casts:
        _, r, c = arr.shape
        slab = r // steps
        assert slab * steps == r and slab % 16 == 0
        in_specs.append(pl.BlockSpec((1, slab, c), functools.partial(lambda *idx, l: (l, idx[0], 0), l=layer)))
        out_specs.append(pl.BlockSpec((slab, c), lambda *idx: (idx[0], 0)))
        out_shapes.append(jax.ShapeDtypeStruct((r, c), BF16))
    return in_specs, out_specs, out_shapes


def _with_casts(body, n_in, n_out, n_cast):
    def kernel(*refs):
        ins = refs[:n_in]
        cast_in = refs[n_in:n_in + n_cast]
        outs = refs[n_in + n_cast:n_in + n_cast + n_out]
        cast_out = refs[n_in + n_cast + n_out:n_in + 2 * n_cast + n_out]
        scratch = refs[n_in + 2 * n_cast + n_out:]
        for src, dst in zip(cast_in, cast_out):
            dst[...] = src[0].astype(dst.dtype)
        body(*ins, *outs, *scratch)
    return kernel


def _mod_kernel(c_ref, cctx_ref, w_ref, b_ref, o_ref):
    cond = jnp.concatenate([c_ref[...], jnp.broadcast_to(cctx_ref[...], (V7X_SUBLANES, D_MODEL))], axis=0)
    s = (cond * _sigmoid(cond)).astype(BF16)
    bias = b_ref[pl.ds(pl.program_id(0), 1), :]
    o_ref[0, :, 0, :] = jnp.dot(s, w_ref[0].astype(BF16), preferred_element_type=F32) + bias


def _modulation(c, c_ctx, w_mod, b_mod, casts=()):
    depth = w_mod.shape[0]
    rows = c.shape[0] + V7X_SUBLANES
    tn = MOD_COL_TILE
    c_in, c_out, c_shapes = _cast_plumbing(casts, depth)
    outs = pl.pallas_call(
        _with_casts(_mod_kernel, 4, 1, len(casts)),
        out_shape=tuple([jax.ShapeDtypeStruct((depth, rows, 1, 6 * D_MODEL), F32)] + c_shapes),
        grid=(depth, 6 * D_MODEL // tn),
        in_specs=[
            pl.BlockSpec(c.shape, lambda i, j: (0, 0)),
            pl.BlockSpec((1, D_MODEL), lambda i, j: (0, 0)),
            pl.BlockSpec((1, D_MODEL, tn), lambda i, j: (i, 0, j)),
            pl.BlockSpec((depth, tn), lambda i, j: (0, j)),
        ] + c_in,
        out_specs=tuple([pl.BlockSpec((1, rows, 1, tn), lambda i, j: (i, 0, 0, j))] + c_out),
        compiler_params=_params(("arbitrary", "arbitrary")),
        name="modulation",
    )(c, c_ctx.reshape(1, D_MODEL), w_mod, b_mod, *[arr for arr, _ in casts])
    return outs[0], list(outs[1:])


def _norm_modulate(x, g, mod_ref, which):
    y = x * lax.rsqrt(jnp.mean(x * x, axis=-1, keepdims=True) + EPS) * g
    shift = mod_ref[0, 0, :, (3 * which) * D_MODEL:(3 * which + 1) * D_MODEL]
    scale = mod_ref[0, 0, :, (3 * which + 1) * D_MODEL:(3 * which + 2) * D_MODEL]
    return y * (1.0 + scale) + shift


def _mod_spec(mod_rows, tile):
    layer, rows_per_mod_row, first_row = mod_rows
    if rows_per_mod_row:
        assert rows_per_mod_row % tile == 0
        tiles = rows_per_mod_row // tile
        return pl.BlockSpec((1, 1, 1, 6 * D_MODEL), lambda i: (layer, first_row + i // tiles, 0, 0))
    return pl.BlockSpec((1, 1, 1, 6 * D_MODEL), lambda i: (layer, first_row, 0, 0))


def _rope_tables(n, dim, paired):
    quarter = dim // 4
    rows = np.repeat(np.arange(n // GRID_W), GRID_W).astype(np.float64)
    cols = np.tile(np.arange(GRID_W), n // GRID_W).astype(np.float64)
    freqs = ROPE_BASE ** (-np.arange(quarter, dtype=np.float64) / quarter)
    ar = rows[:, None] * freqs
    ac = cols[:, None] * freqs
    if paired:
        cos = np.concatenate([np.cos(ar), np.cos(ac), np.cos(ar), np.cos(ac)], axis=1)
        sin = np.concatenate([-np.sin(ar), -np.sin(ac), np.sin(ar), np.sin(ac)], axis=1)
    else:
        cos = np.concatenate([np.cos(ar), np.cos(ar), np.cos(ac), np.cos(ac)], axis=1)
        sin = np.concatenate([-np.sin(ar), np.sin(ar), -np.sin(ac), np.sin(ac)], axis=1)
    return jnp.asarray(cos, F32), jnp.asarray(sin, F32)


_Q = HEAD_DIM // 4
_PAIRED_ORDER = np.concatenate([np.arange(0, _Q), np.arange(2 * _Q, 3 * _Q),
                                np.arange(_Q, 2 * _Q), np.arange(3 * _Q, 4 * _Q)])


def _paired_lanes(w):
    lane = lax.broadcasted_iota(jnp.int32, w.shape, 1)
    from_c = pltpu.roll(w, V7X_LANES - _Q, axis=1)
    from_b = pltpu.roll(w, _Q, axis=1)
    return jnp.where((lane >= _Q) & (lane < 2 * _Q), from_c,
                     jnp.where((lane >= 2 * _Q) & (lane < 3 * _Q), from_b, w))


def _swap_halves(t):
    return pltpu.roll(t, V7X_LANES // 2, axis=1)


def _attn_in_kernel(x_ref, mod_ref, g_ref, w_ref, qn_ref, kn_ref, *rest, use_rope):
    nq = N_HEADS * HEAD_DIM
    nk = N_KV_HEADS * HEAD_DIM
    if use_rope:
        cos_ref, sin_ref, reorder_ref, q_ref, k_ref, v_ref, wqk_ref = rest

        @pl.when(pl.program_id(0) == 0)
        def _():
            for head in range(N_HEADS + N_KV_HEADS):
                cols = slice(head * HEAD_DIM, (head + 1) * HEAD_DIM)
                wqk_ref[:, cols] = jnp.dot(w_ref[:, cols], reorder_ref[...],
                                           preferred_element_type=F32).astype(BF16)
    else:
        q_ref, k_ref, v_ref = rest
        wqk_ref = w_ref
    tm = PROJ_SUB_TILE
    n_sub = x_ref.shape[0] // tm
    pair_w = 2 * HEAD_DIM
    n_pairs = (N_HEADS + N_KV_HEADS) // 2
    qw = qn_ref[...] * Q_SCALE
    kw = kn_ref[...]
    if use_rope:
        qw = _paired_lanes(qw)
        kw = _paired_lanes(kw)
        ra = lax.broadcasted_iota(jnp.int32, (pair_w, pair_w), 0) // HEAD_DIM
        rb = lax.broadcasted_iota(jnp.int32, (pair_w, pair_w), 1) // HEAD_DIM
        head_sum = jnp.where(ra == rb, 1.0, 0.0).astype(BF16)

    def normed(sub):
        return _norm_modulate(x_ref[sub * tm:(sub + 1) * tm, :], g_ref[...], mod_ref, 0).astype(BF16)

    def put_kv(ref, sub, kh, val):
        if use_rope:
            ref[sub * tm:(sub + 1) * tm, kh * HEAD_DIM:(kh + 1) * HEAD_DIM] = val.astype(ref.dtype)
        else:
            ref[pl.ds(sub * tm * N_KV_HEADS + kh, tm, stride=N_KV_HEADS), :] = val.astype(ref.dtype)

    h_next = normed(0)
    for sub in range(n_sub):
        h = h_next
        if sub + 1 < n_sub:
            h_next = normed(sub + 1)
        rows = slice(sub * tm, (sub + 1) * tm)
        if use_rope:
            q_tabs = (qw * cos_ref[rows, :], _swap_halves(qw) * sin_ref[rows, :])
            k_tabs = (kw * cos_ref[rows, :], _swap_halves(kw) * sin_ref[rows, :])
        t2_next = jnp.dot(h, wqk_ref[:, 0:pair_w], preferred_element_type=F32)
        for pair in range(n_pairs):
            t2 = t2_next
            if pair + 1 < n_pairs:
                t2_next = jnp.dot(h, wqk_ref[:, (pair + 1) * pair_w:(pair + 2) * pair_w],
                                  preferred_element_type=F32)
            else:
                t2_next = jnp.dot(h, w_ref[:, nq + nk:], preferred_element_type=F32)
            if use_rope:
                ss2 = jnp.dot((t2 * t2).astype(BF16), head_sum, preferred_element_type=F32)
                r2 = lax.rsqrt(ss2 * (1.0 / HEAD_DIM) + EPS)
            for j in range(2):
                head = 2 * pair + j
                t = t2[:, j * HEAD_DIM:(j + 1) * HEAD_DIM]
                if use_rope:
                    r = r2[:, j * HEAD_DIM:(j + 1) * HEAD_DIM]
                    wc, ws = q_tabs if head < N_HEADS else k_tabs
                    t = (t * wc + _swap_halves(t) * ws) * r
                else:
                    r = lax.rsqrt(jnp.mean(t * t, axis=-1, keepdims=True) + EPS)
                    t = t * r * (qw if head < N_HEADS else kw)
                if head < N_HEADS:
                    q_ref[rows, head * HEAD_DIM:(head + 1) * HEAD_DIM] = t.astype(q_ref.dtype)
                else:
                    put_kv(k_ref, sub, head - N_HEADS, t)
        vv = t2_next
        for kh in range(N_KV_HEADS):
            put_kv(v_ref, sub, kh, vv[:, kh * HEAD_DIM:(kh + 1) * HEAD_DIM])


def _attn_in(x, mod, mod_rows, g, w, qn, kn, rope, tile, casts=()):
    rows = x.shape[0]
    nk = N_KV_HEADS * HEAD_DIM
    if rope is None:
        kv_shape = jax.ShapeDtypeStruct((rows * N_KV_HEADS, HEAD_DIM), F32)
        kv_spec = pl.BlockSpec((tile * N_KV_HEADS, HEAD_DIM), lambda i: (i, 0))
    else:
        kv_shape = jax.ShapeDtypeStruct((rows, nk), BF16)
        kv_spec = pl.BlockSpec((tile, nk), lambda i: (i, 0))
    in_specs = [
        pl.BlockSpec((tile, D_MODEL), lambda i: (i, 0)),
        _mod_spec(mod_rows, tile),
        _const_spec((1, D_MODEL)),
        _const_spec((D_MODEL, QKV_W)),
        _const_spec((1, HEAD_DIM)),
        _const_spec((1, HEAD_DIM)),
    ]
    args = [x, mod, g, w, qn, kn]
    scratch = []
    if rope is not None:
        cos, sin, reorder = rope
        seq_tiles = cos.shape[0] // tile
        in_specs += [pl.BlockSpec((tile, HEAD_DIM), lambda i: (i % seq_tiles, 0))] * 2
        in_specs.append(_const_spec((HEAD_DIM, HEAD_DIM)))
        args += [cos, sin, reorder]
        scratch = [pltpu.VMEM((D_MODEL, (N_HEADS + N_KV_HEADS) * HEAD_DIM), BF16)]
    steps = rows // tile
    c_in, c_out, c_shapes = _cast_plumbing(casts, steps)
    out_specs = [pl.BlockSpec((tile, N_HEADS * HEAD_DIM), lambda i: (i, 0)), kv_spec, kv_spec]
    out_shapes = [jax.ShapeDtypeStruct((rows, N_HEADS * HEAD_DIM), BF16), kv_shape, kv_shape]
    body = functools.partial(_attn_in_kernel, use_rope=rope is not None)
    outs = pl.pallas_call(
        _with_casts(body, len(args), len(out_shapes), len(casts)),
        out_shape=tuple(out_shapes + c_shapes),
        grid=(steps,),
        in_specs=in_specs + c_in,
        out_specs=tuple(out_specs + c_out),
        scratch_shapes=scratch,
        compiler_params=_params(("arbitrary",)),
        name="attn_in_rope" if rope is not None else "attn_in",
    )(*args, *[arr for arr, _ in casts])
    return outs[:3], list(outs[3:])


def _kv_head(ref, e, kv):
    if ref.dtype == BF16:
        return ref[e, :, kv * HEAD_DIM:(kv + 1) * HEAD_DIM]
    tokens = ref.shape[1] // N_KV_HEADS
    return ref[e, pl.ds(kv, tokens, stride=N_KV_HEADS), :].astype(BF16)


def _attn_kernel(q_ref, k_ref, v_ref, *rest, cached):
    def with_ones(v):
        return jnp.concatenate([v, jnp.ones_like(v)], axis=1)

    if cached:
        ck_hbm, cv_hbm, reorder_ref, o_ref, ck_scr, cv_scr, stage, sems = rest
        b = pl.program_id(0)
        n_b = pl.num_programs(0)

        def head_copies(bi, slot):
            return [pltpu.make_async_copy(src.at[bi, 0, :, kv, :], stage.at[slot, t, kv], sems.at[slot, t, kv])
                    for t, src in enumerate((ck_hbm, cv_hbm)) for kv in range(N_KV_HEADS)]

        @pl.when(pl.program_id(1) == 0)
        def _():
            slot = b % 2

            @pl.when(b == 0)
            def _():
                for cp in head_copies(0, 0):
                    cp.start()

            @pl.when(b + 1 < n_b)
            def _():
                for cp in head_copies(b + 1, 1 - slot):
                    cp.start()

            for cp in head_copies(b, slot):
                cp.wait()
            for kv in range(N_KV_HEADS):
                ck_scr[kv] = jnp.dot(stage[slot, 0, kv].astype(BF16), reorder_ref[...],
                                     preferred_element_type=F32).astype(BF16)
                cv_scr[kv] = with_ones(stage[slot, 1, kv].astype(BF16))
    else:
        (o_ref,) = rest
    nt = (((1,), (1,)), ((), ()))
    n_elem = q_ref.shape[0]
    ks = [[_kv_head(k_ref, e, kv) for kv in range(N_KV_HEADS)] for e in range(n_elem)]
    vs = [[with_ones(_kv_head(v_ref, e, kv)) for kv in range(N_KV_HEADS)] for e in range(n_elem)]

    def scores(unit):
        e, head = unit
        kv = head // GROUP
        q = q_ref[e, :, head * HEAD_DIM:(head + 1) * HEAD_DIM]
        s = lax.dot_general(q, ks[e][kv], nt, preferred_element_type=F32)
        sc = lax.dot_general(q, ck_scr[kv], nt, preferred_element_type=F32) if cached else None
        return s, sc

    units = [(e, head) for e in range(n_elem) for head in range(N_HEADS)]
    nxt = scores(units[0])
    for ui, (e, head) in enumerate(units):
        kv = head // GROUP
        s, sc = nxt
        if ui + 1 < len(units):
            nxt = scores(units[ui + 1])
        m = jnp.max(s, axis=-1, keepdims=True)
        if cached:
            m = jnp.maximum(m, jnp.max(sc, axis=-1, keepdims=True))
        o = jnp.dot(jnp.exp2(s - m).astype(BF16), vs[e][kv], preferred_element_type=F32)
        if cached:
            o = o + jnp.dot(jnp.exp2(sc - m).astype(BF16), cv_scr[kv], preferred_element_type=F32)
        o_ref[e, :, head * HEAD_DIM:(head + 1) * HEAD_DIM] = (o[:, :HEAD_DIM] / o[:, HEAD_DIM:]).astype(o_ref.dtype)


def _attention(q, k, v, cache=None, casts=()):
    b, n, width = q.shape
    tq = min(ATTN_Q_TILE, n)
    eb = 1 if cache is not None else max(1, min(b, ATTN_Q_TILE // n))
    assert b % eb == 0
    in_specs = [pl.BlockSpec((eb, tq, width), lambda bi, qi: (bi, qi, 0))]
    in_specs += [pl.BlockSpec((eb,) + k.shape[1:], lambda bi, qi: (bi, 0, 0))] * 2
    args = [q, k, v]
    scratch = []
    if cache is not None:
        past = cache[0].shape[2]
        in_specs += [pl.BlockSpec(memory_space=pl.ANY)] * 2
        in_specs.append(pl.BlockSpec((HEAD_DIM, HEAD_DIM), lambda bi, qi: (0, 0)))
        args += list(cache)
        scratch = [pltpu.VMEM((N_KV_HEADS, past, HEAD_DIM), BF16),
                   pltpu.VMEM((N_KV_HEADS, past, 2 * HEAD_DIM), BF16)]
        scratch += [pltpu.VMEM((2, 2, N_KV_HEADS, past, HEAD_DIM), F32),
                    pltpu.SemaphoreType.DMA((2, 2, N_KV_HEADS))]
    c_in, c_out, c_shapes = _cast_plumbing(casts, b // eb)
    body = functools.partial(_attn_kernel, cached=cache is not None)
    outs = pl.pallas_call(
        _with_casts(body, len(args), 1, len(casts)),
        out_shape=tuple([jax.ShapeDtypeStruct((b, n, width), BF16)] + c_shapes),
        grid=(b // eb, n // tq),
        in_specs=in_specs + c_in,
        out_specs=tuple([pl.BlockSpec((eb, tq, width), lambda bi, qi: (bi, qi, 0))] + c_out),
        scratch_shapes=scratch,
        compiler_params=_params(("arbitrary", "arbitrary")),
        name="attention_cached" if cache is not None else "attention",
    )(*args, *[arr for arr, _ in casts])
    return outs[0], list(outs[1:])


def _post_kernel(a_ref, x_ref, mod_ref, g_ref, wo_ref, w1_ref, w2_ref, *rest, final):
    if final:
        gf_ref, o_ref = rest
    else:
        (o_ref,) = rest
    gate1 = mod_ref[0, 0, :, 2 * D_MODEL:3 * D_MODEL]
    gate2 = mod_ref[0, 0, :, 5 * D_MODEL:6 * D_MODEL]
    n_ff = D_FF // POST_FF_TILE

    def attn_residual(t):
        rows = slice(t * POST_SUB_TILE, (t + 1) * POST_SUB_TILE)
        x = x_ref[rows, :] + gate1 * jnp.dot(a_ref[rows, :], wo_ref[...], preferred_element_type=F32)
        return x, _norm_modulate(x, g_ref[...], mod_ref, 1).astype(BF16)

    n_sub = x_ref.shape[0] // POST_SUB_TILE
    nxt = attn_residual(0)
    for t in range(n_sub):
        x, h = nxt
        if t + 1 < n_sub:
            nxt = attn_residual(t + 1)
        y = None
        for f in range(n_ff):
            u = jnp.dot(h, w1_ref[:, f * POST_FF_TILE:(f + 1) * POST_FF_TILE], preferred_element_type=F32)
            u = jnp.square(jnp.maximum(u, 0.0)).astype(BF16)
            part = jnp.dot(u, w2_ref[f * POST_FF_TILE:(f + 1) * POST_FF_TILE, :], preferred_element_type=F32)
            y = part if y is None else y + part
        x = x + gate2 * y
        if final:
            x = x * lax.rsqrt(jnp.mean(x * x, axis=-1, keepdims=True) + EPS) * gf_ref[...]
        o_ref[t * POST_SUB_TILE:(t + 1) * POST_SUB_TILE, :] = x


def _post(a, x, mod, mod_rows, g, wo, w1, w2, gf=None, casts=()):
    rows = x.shape[0]
    row_block = pl.BlockSpec((POST_ROW_TILE, D_MODEL), lambda i: (i, 0))
    in_specs = [
        row_block,
        row_block,
        _mod_spec(mod_rows, POST_ROW_TILE),
        _const_spec((1, D_MODEL)),
        _const_spec((D_MODEL, D_MODEL)),
        _const_spec((D_MODEL, D_FF)),
        _const_spec((D_FF, D_MODEL)),
    ]
    args = [a, x, mod, g, wo, w1, w2]
    if gf is not None:
        in_specs.append(_const_spec((1, D_MODEL)))
        args.append(gf)
    steps = rows // POST_ROW_TILE
    c_in, c_out, c_shapes = _cast_plumbing(casts, steps)
    body = functools.partial(_post_kernel, final=gf is not None)
    outs = pl.pallas_call(
        _with_casts(body, len(args), 1, len(casts)),
        out_shape=tuple([jax.ShapeDtypeStruct((rows, D_MODEL), F32)] + c_shapes),
        grid=(steps,),
        in_specs=in_specs + c_in,
        out_specs=tuple([row_block] + c_out),
        compiler_params=_params(("parallel",)),
        name="post_final" if gf is not None else "post",
    )(*args, *[arr for arr, _ in casts])
    return outs[0], list(outs[1:])


def _ret_in_kernel(x_ref, mod_ref, g_ref, w_ref, *rest, use_rope):
    if use_rope:
        cos_ref, sin_ref, q_ref, k_ref, v_ref, gate_ref = rest
    else:
        q_ref, k_ref, v_ref, gate_ref = rest
    width = RET_HEADS * RET_DK
    tm = PROJ_SUB_TILE
    n_sub = x_ref.shape[0] // tm

    def normed(sub):
        return _norm_modulate(x_ref[sub * tm:(sub + 1) * tm, :], g_ref[...], mod_ref, 0).astype(BF16)

    h_next = normed(0)
    for sub in range(n_sub):
        h = h_next
        if sub + 1 < n_sub:
            h_next = normed(sub + 1)
        rows = slice(sub * tm, (sub + 1) * tm)
        for part, ref in enumerate((q_ref, k_ref, v_ref, gate_ref)):
            t = jnp.dot(h, w_ref[:, part * width:(part + 1) * width], preferred_element_type=F32)
            if part < 2 and use_rope:
                for c in range(width // V7X_LANES):
                    half = (c % (RET_DK // V7X_LANES)) * V7X_LANES
                    tc = t[:, c * V7X_LANES:(c + 1) * V7X_LANES]
                    tc = (tc * cos_ref[rows, half:half + V7X_LANES]
                          + _swap_halves(tc) * sin_ref[rows, half:half + V7X_LANES])
                    if part == 0:
                        tc = tc * RET_DK ** -0.5
                    ref[rows, c * V7X_LANES:(c + 1) * V7X_LANES] = tc.astype(ref.dtype)
            else:
                if part == 0:
                    t = t * RET_DK ** -0.5
                ref[rows, :] = t.astype(ref.dtype)


def _ret_in(x, mod, mod_rows, g, w, rope, tile, casts=()):
    rows = x.shape[0]
    width = RET_HEADS * RET_DK
    in_specs = [
        pl.BlockSpec((tile, D_MODEL), lambda i: (i, 0)),
        _mod_spec(mod_rows, tile),
        _const_spec((1, D_MODEL)),
        _const_spec((D_MODEL, RET_W)),
    ]
    args = [x, mod, g, w]
    if rope is not None:
        seq_tiles = rope[0].shape[0] // tile
        in_specs += [pl.BlockSpec((tile, RET_DK), lambda i: (i % seq_tiles, 0))] * 2
        args += list(rope)
    out_block = pl.BlockSpec((tile, width), lambda i: (i, 0))
    steps = rows // tile
    c_in, c_out, c_shapes = _cast_plumbing(casts, steps)
    body = functools.partial(_ret_in_kernel, use_rope=rope is not None)
    outs = pl.pallas_call(
        _with_casts(body, len(args), 4, len(casts)),
        out_shape=tuple([jax.ShapeDtypeStruct((rows, width), BF16)] * 3
                        + [jax.ShapeDtypeStruct((rows, width), F32)] + c_shapes),
        grid=(steps,),
        in_specs=in_specs + c_in,
        out_specs=tuple([out_block] * 4 + c_out),
        compiler_params=_params(("parallel",)),
        name="ret_in_rope" if rope is not None else "ret_in",
    )(*args, *[arr for arr, _ in casts])
    return outs[:4], list(outs[4:])


def _log_sigmoid(x):
    return jnp.minimum(x, 0.0) - jnp.log(1.0 + jnp.exp(-jnp.abs(x)))


def _ret_kernel(q_ref, k_ref, v_ref, gate_ref, lg_ref, gn_ref, *rest, n, tq, heads, has_state):
    if has_state:
        s0_ref, y_ref, d_ref = rest
    else:
        y_ref, st_ref, d_ref = rest
    b = pl.program_id(1)
    qi = pl.program_id(2)
    r0 = qi * tq
    nt = (((1,), (1,)), ((), ()))
    sub = min(tq, RET_SUB_TILE)
    lgs = [(_log_sigmoid(lg_ref[hh, 0, 0:1, 0:1]), _log_sigmoid(lg_ref[hh, 1, 0:1, 0:1]))
           for hh in range(heads)]

    @pl.when(b == 0)
    def _():
        ii = r0 + lax.broadcasted_iota(jnp.int32, (tq, n), 0)
        jj = lax.broadcasted_iota(jnp.int32, (tq, n), 1)
        diff = (ii - jj).astype(F32)
        for hh in range(heads):
            lg_f, lg_b = lgs[hh]
            d_ref[hh, qi] = jnp.exp(jnp.where(diff >= 0.0, lg_f, -lg_b) * diff)

    def head_cols(hh):
        return slice(hh * RET_DK, (hh + 1) * RET_DK)

    def scores(unit):
        hh, si = unit
        return lax.dot_general(q_ref[0, si * sub:(si + 1) * sub, head_cols(hh)], k_ref[0, :, head_cols(hh)],
                               nt, preferred_element_type=F32)

    units = [(hh, si) for hh in range(heads) for si in range(tq // sub)]
    nxt = scores(units[0])
    for ui, (hh, si) in enumerate(units):
        s = nxt
        if ui + 1 < len(units):
            nxt = scores(units[ui + 1])
        lg_f, lg_b = lgs[hh]
        cols = head_cols(hh)
        rows = slice(si * sub, (si + 1) * sub)
        v = v_ref[0, :, cols]
        p = (s * d_ref[hh, qi, rows, :]).astype(BF16)
        o = jnp.dot(p, v, preferred_element_type=F32)
        if has_state:
            q = q_ref[0, rows, cols]
            i_col = (r0 + si * sub + lax.broadcasted_iota(jnp.int32, (sub, 1), 0)).astype(F32)
            o = o + (jnp.dot(q, s0_ref[0, 0, 0, hh].astype(BF16), preferred_element_type=F32)
                     * jnp.exp(lg_f * (i_col + 1.0))
                     + jnp.dot(q, s0_ref[0, 0, 1, hh].astype(BF16), preferred_element_type=F32)
                     * jnp.exp(lg_b * (n - i_col)))
        elif si == 0:
            eye = jnp.where(lax.broadcasted_iota(jnp.int32, (RET_DK, RET_DK), 0)
                            == lax.broadcasted_iota(jnp.int32, (RET_DK, RET_DK), 1), 1.0, 0.0).astype(BF16)
            kt = lax.dot_general(eye, k_ref[0, :, cols], nt, preferred_element_type=F32)
            j_row = lax.broadcasted_iota(jnp.int32, (1, n), 1).astype(F32)
            st_ref[0, 0, 0, hh] = jnp.dot((kt * jnp.exp(lg_f * (n - 1.0 - j_row))).astype(BF16), v,
                                          preferred_element_type=F32)
            st_ref[0, 0, 1, hh] = jnp.dot((kt * jnp.exp(lg_b * j_row)).astype(BF16), v,
                                          preferred_element_type=F32)
        mu = jnp.mean(o, axis=-1, keepdims=True)
        oc = o - mu
        var = jnp.mean(oc * oc, axis=-1, keepdims=True)
        on = oc * lax.rsqrt(var + EPS) * gn_ref[:, cols]
        gt = gate_ref[0, rows, cols]
        y_ref[0, rows, cols] = (gt * _sigmoid(gt) * on).astype(y_ref.dtype)


def _retention(q, k, v, gate, lg, gn, state0, heads, tq):
    b, n, _ = q.shape
    assert n % tq == 0 and RET_HEADS % heads == 0 and (state0 is not None or tq == n)
    width = heads * RET_DK
    q_tile = pl.BlockSpec((1, tq, width), lambda hg, bi, qi: (bi, qi, hg))
    seq = pl.BlockSpec((1, n, width), lambda hg, bi, qi: (bi, 0, hg))
    st = pl.BlockSpec((1, 1, 2, heads, RET_DK, RET_DV), lambda hg, bi, qi: (bi, 0, 0, hg, 0, 0))
    in_specs = [q_tile, seq, seq, q_tile,
                pl.BlockSpec((heads, 2, V7X_SUBLANES, V7X_LANES), lambda hg, bi, qi: (hg, 0, 0, 0)),
                pl.BlockSpec((1, width), lambda hg, bi, qi: (0, hg))]
    args = [q, k, v, gate, lg, gn]
    y_shape = jax.ShapeDtypeStruct((b, n, RET_HEADS * RET_DV), BF16)
    if state0 is not None:
        in_specs.append(st)
        args.append(state0)
        out_shape, out_specs = y_shape, q_tile
    else:
        out_shape = (y_shape, jax.ShapeDtypeStruct((b, 1, 2, RET_HEADS, RET_DK, RET_DV), F32))
        out_specs = (q_tile, st)
    return pl.pallas_call(
        functools.partial(_ret_kernel, n=n, tq=tq, heads=heads, has_state=state0 is not None),
        out_shape=out_shape,
        grid=(RET_HEADS // heads, b, n // tq),
        in_specs=in_specs,
        out_specs=out_specs,
        scratch_shapes=[pltpu.VMEM((heads, n // tq, tq, n), F32)],
        compiler_params=_params(("arbitrary", "arbitrary", "arbitrary")),
        name="retention_state" if state0 is not None else "retention",
    )(*args)


def kernel(x_prompt, x_sample, cache_k, cache_v, state_ret, c, c_ctx, w_mod, b_mod, norm_g,
           attn_w_qkv, attn_q_norm, attn_k_norm, attn_w_o, ret_w_qkvg, ret_decay_logit, ret_gn_w,
           ret_w_o, mlp_w1, mlp_w2, final_norm_g):
    bp, sp, d = x_prompt.shape
    bs, ss, _ = x_sample.shape
    depth = w_mod.shape[0]
    assert d == D_MODEL and depth == 2 and bs % V7X_SUBLANES == 0
    assert (bp * sp) % POST_ROW_TILE == 0 and ss % POST_ROW_TILE == 0 and POST_ROW_TILE % ROW_TILE == 0

    mod, (wqkv,) = _modulation(c, c_ctx, w_mod, b_mod, casts=[(attn_w_qkv, 0)])
    ctx_mod = (0, bs)
    lat_mod = (ss, 0)

    xp = x_prompt.reshape(bp * sp, d)
    xs = x_sample.reshape(bs * ss, d)
    nkv = N_KV_HEADS * HEAD_DIM

    g0 = norm_g[0, 0][None, :]
    g1 = norm_g[0, 1][None, :]
    qn = attn_q_norm[0][None, :]
    kn = attn_k_norm[0][None, :]

    (qp, kp, vp), _ = _attn_in(xp, mod, (0,) + ctx_mod, g0, wqkv, qn, kn, None, ROW_TILE)
    reorder = np.zeros((HEAD_DIM, HEAD_DIM), np.float32)
    reorder[_PAIRED_ORDER, np.arange(HEAD_DIM)] = 1.0
    reorder = jnp.asarray(reorder, BF16)
    (qs, ks, vs), (w1, w2, wo) = _attn_in(xs, mod, (0,) + lat_mod, g0, wqkv, qn, kn,
                                          _rope_tables(ss, HEAD_DIM, True) + (reorder,), ROW_TILE,
                                          casts=[(mlp_w1, 0), (mlp_w2, 0), (attn_w_o, 0)])

    ap, _ = _attention(qp.reshape(bp, sp, -1), kp.reshape(bp, sp * N_KV_HEADS, HEAD_DIM),
                       vp.reshape(bp, sp * N_KV_HEADS, HEAD_DIM))
    xp, _ = _post(ap.reshape(bp * sp, -1), xp, mod, (0,) + ctx_mod, g1, wo, w1, w2)
    cache = (cache_k, cache_v, reorder)
    as_, (wr,) = _attention(qs.reshape(bs, ss, -1), ks.reshape(bs, ss, nkv), vs.reshape(bs, ss, nkv), cache,
                            casts=[(ret_w_qkvg, 0)])
    xs, _ = _post(as_.reshape(bs * ss, -1), xs, mod, (0,) + lat_mod, g1, wo, w1, w2)

    g0 = norm_g[1, 0][None, :]
    g1 = norm_g[1, 1][None, :]
    gn = ret_gn_w[0][None, :]
    gf = final_norm_g[None, :]
    lg = jnp.broadcast_to(ret_decay_logit[0].T[:, :, None, None],
                          (RET_HEADS, 2, V7X_SUBLANES, V7X_LANES))
    hw = RET_HEADS * RET_DK

    (q, k, v, gate), _ = _ret_in(xp, mod, (1,) + ctx_mod, g0, wr, None, ROW_TILE)
    yp, new_state = _retention(q.reshape(bp, sp, hw), k.reshape(bp, sp, hw), v.reshape(bp, sp, hw),
                               gate.reshape(bp, sp, hw), lg, gn, None, RET_HEADS, sp)
    (q, k, v, gate), (w1, w2, wo) = _ret_in(xs, mod, (1,) + lat_mod, g0, wr, _rope_tables(ss, RET_DK, False),
                                            ROW_TILE, casts=[(mlp_w1, 1), (mlp_w2, 1), (ret_w_o, 0)])
    ys = _retention(q.reshape(bs, ss, hw), k.reshape(bs, ss, hw), v.reshape(bs, ss, hw),
                    gate.reshape(bs, ss, hw), lg, gn, state_ret, RET_STATE_HEADS, RET_Q_TILE)
    y_prompt, _ = _post(yp.reshape(bp * sp, hw), xp, mod, (1,) + ctx_mod, g1, wo, w1, w2, gf)
    y_sample, _ = _post(ys.reshape(bs * ss, hw), xs, mod, (1,) + lat_mod, g1, wo, w1, w2, gf)

    return (y_prompt.reshape(bp, sp, d),
            y_sample.reshape(bs, ss, d),
            kp.reshape(bp, 1, sp, N_KV_HEADS, HEAD_DIM),
            vp.reshape(bp, 1, sp, N_KV_HEADS, HEAD_DIM),
            new_state)
```

```python
import functools

import numpy as np
import jax
import jax.numpy as jnp
from jax import lax
from jax.experimental import pallas as pl
from jax.experimental.pallas import tpu as pltpu

F32 = jnp.float32
BF16 = jnp.bfloat16

D_MODEL = 1024
GRID_W = 64
N_HEADS = 8
N_KV_HEADS = 2
HEAD_DIM = 128
GROUP = N_HEADS // N_KV_HEADS
ROPE_BASE = 10000.0
RET_HEADS = 4
RET_DK = 256
RET_DV = 256
D_FF = 4 * D_MODEL
Q_SCALE = HEAD_DIM ** -0.5 * float(np.log2(np.e))
EPS = 1e-6
QKV_W = (N_HEADS + 2 * N_KV_HEADS) * HEAD_DIM
RET_W = 2 * RET_HEADS * RET_DK + 2 * RET_HEADS * RET_DV

V7X_LANES = 128
V7X_SUBLANES = 8
MOD_COL_TILE = 2048
ROW_TILE = 1024
PROJ_SUB_TILE = 512
POST_ROW_TILE = 1024
POST_SUB_TILE = 512
POST_FF_TILE = 1024
ATTN_Q_TILE = 1024
RET_Q_TILE = 1024
RET_SUB_TILE = 256
RET_STATE_HEADS = 2
RET_CTX_ELEMS = 4
ATTN_CTX_ROWS = 1024
VMEM_LIMIT = 56 * 1024 * 1024


def _params(sem, vmem=VMEM_LIMIT):
    return pltpu.CompilerParams(dimension_semantics=sem, vmem_limit_bytes=vmem)


def _const_spec(shape):
    nd = len(shape)
    return pl.BlockSpec(shape, lambda *_: (0,) * nd, pipeline_mode=pl.Buffered(1))


def _sigmoid(x):
    return 1.0 / (1.0 + jnp.exp(-x))


def _cast_plumbing(casts, steps):
    in_specs, out_specs, out_shapes = [], [], []
    for arr, layer in casts:
        _, r, c = arr.shape
        slab = r // steps
        assert slab * steps == r and slab % 16 == 0
        in_specs.append(pl.BlockSpec((1, slab, c), functools.partial(lambda *idx, l: (l, idx[0], 0), l=layer)))
        out_specs.append(pl.BlockSpec((slab, c), lambda *idx: (idx[0], 0)))
        out_shapes.append(jax.ShapeDtypeStruct((r, c), BF16))
    return in_specs, out_specs, out_shapes


def _with_casts(body, n_in, n_out, n_cast):
    def kernel(*refs):
        ins = refs[:n_in]
        cast_in = refs[n_in:n_in + n_cast]
        outs = refs[n_in + n_cast:n_in + n_cast + n_out]
        cast_out = refs[n_in + n_cast + n_out:n_in + 2 * n_cast + n_out]
        scratch = refs[n_in + 2 * n_cast + n_out:]
        for src, dst in zip(cast_in, cast_out):
            dst[...] = src[0].astype(dst.dtype)
        body(*ins, *outs, *scratch)
    return kernel


def _mod_kernel(c_ref, cctx_ref, w_ref, b_ref, o_ref):
    cond = jnp.concatenate([c_ref[...], jnp.broadcast_to(cctx_ref[...], (V7X_SUBLANES, D_MODEL))], axis=0)
    s = (cond * _sigmoid(cond)).astype(BF16)
    bias = b_ref[pl.ds(pl.program_id(0), 1), :]
    o_ref[0, :, 0, :] = jnp.dot(s, w_ref[0].astype(BF16), preferred_element_type=F32) + bias


def _modulation(c, c_ctx, w_mod, b_mod, casts=()):
    depth = w_mod.shape[0]
    rows = c.shape[0] + V7X_SUBLANES
    tn = MOD_COL_TILE
    c_in, c_out, c_shapes = _cast_plumbing(casts, depth)
    outs = pl.pallas_call(
        _with_casts(_mod_kernel, 4, 1, len(casts)),
        out_shape=tuple([jax.ShapeDtypeStruct((depth, rows, 1, 6 * D_MODEL), F32)] + c_shapes),
        grid=(depth, 6 * D_MODEL // tn),
        in_specs=[
            pl.BlockSpec(c.shape, lambda i, j: (0, 0)),
            pl.BlockSpec((1, D_MODEL), lambda i, j: (0, 0)),
            pl.BlockSpec((1, D_MODEL, tn), lambda i, j: (i, 0, j)),
            pl.BlockSpec((depth, tn), lambda i, j: (0, j)),
        ] + c_in,
        out_specs=tuple([pl.BlockSpec((1, rows, 1, tn), lambda i, j: (i, 0, 0, j))] + c_out),
        compiler_params=_params(("arbitrary", "arbitrary")),
        name="modulation",
    )(c, c_ctx.reshape(1, D_MODEL), w_mod, b_mod, *[arr for arr, _ in casts])
    return outs[0], list(outs[1:])


def _norm_modulate(x, g, mod_ref, which):
    y = x * lax.rsqrt(jnp.mean(x * x, axis=-1, keepdims=True) + EPS) * g
    shift = mod_ref[0, 0, :, (3 * which) * D_MODEL:(3 * which + 1) * D_MODEL]
    scale = mod_ref[0, 0, :, (3 * which + 1) * D_MODEL:(3 * which + 2) * D_MODEL]
    return y * (1.0 + scale) + shift


def _mod_spec(mod_rows, tile):
    layer, rows_per_mod_row, first_row = mod_rows
    if rows_per_mod_row:
        assert rows_per_mod_row % tile == 0
        tiles = rows_per_mod_row // tile
        return pl.BlockSpec((1, 1, 1, 6 * D_MODEL), lambda i: (layer, first_row + i // tiles, 0, 0))
    return pl.BlockSpec((1, 1, 1, 6 * D_MODEL), lambda i: (layer, first_row, 0, 0))


def _rope_tables(n, dim, paired):
    quarter = dim // 4
    rows = np.repeat(np.arange(n // GRID_W), GRID_W).astype(np.float64)
    cols = np.tile(np.arange(GRID_W), n // GRID_W).astype(np.float64)
    freqs = ROPE_BASE ** (-np.arange(quarter, dtype=np.float64) / quarter)
    ar = rows[:, None] * freqs
    ac = cols[:, None] * freqs
    if paired:
        cos = np.concatenate([np.cos(ar), np.cos(ac), np.cos(ar), np.cos(ac)], axis=1)
        sin = np.concatenate([-np.sin(ar), -np.sin(ac), np.sin(ar), np.sin(ac)], axis=1)
    else:
        cos = np.concatenate([np.cos(ar), np.cos(ar), np.cos(ac), np.cos(ac)], axis=1)
        sin = np.concatenate([-np.sin(ar), np.sin(ar), -np.sin(ac), np.sin(ac)], axis=1)
    return jnp.asarray(cos, F32), jnp.asarray(sin, F32)


_Q = HEAD_DIM // 4
_PAIRED_ORDER = np.concatenate([np.arange(0, _Q), np.arange(2 * _Q, 3 * _Q),
                                np.arange(_Q, 2 * _Q), np.arange(3 * _Q, 4 * _Q)])


def _paired_lanes(w):
    lane = lax.broadcasted_iota(jnp.int32, w.shape, 1)
    from_c = pltpu.roll(w, V7X_LANES - _Q, axis=1)
    from_b = pltpu.roll(w, _Q, axis=1)
    return jnp.where((lane >= _Q) & (lane < 2 * _Q), from_c,
                     jnp.where((lane >= 2 * _Q) & (lane < 3 * _Q), from_b, w))


def _swap_halves(t):
    return pltpu.roll(t, V7X_LANES // 2, axis=1)


def _attn_in_kernel(x_ref, mod_ref, g_ref, w_ref, qn_ref, kn_ref, *rest, use_rope):
    nq = N_HEADS * HEAD_DIM
    nk = N_KV_HEADS * HEAD_DIM
    if use_rope:
        cos_ref, sin_ref, reorder_ref, q_ref, k_ref, v_ref, wqk_ref = rest

        @pl.when(pl.program_id(0) == 0)
        def _():
            for head in range(N_HEADS + N_KV_HEADS):
                cols = slice(head * HEAD_DIM, (head + 1) * HEAD_DIM)
                wqk_ref[:, cols] = jnp.dot(w_ref[:, cols], reorder_ref[...],
                                           preferred_element_type=F32).astype(BF16)
    else:
        q_ref, k_ref, v_ref = rest
        wqk_ref = w_ref
    tm = PROJ_SUB_TILE
    n_sub = x_ref.shape[0] // tm
    pair_w = 2 * HEAD_DIM
    n_pairs = (N_HEADS + N_KV_HEADS) // 2
    qw = qn_ref[...] * Q_SCALE
    kw = kn_ref[...]
    if use_rope:
        qw = _paired_lanes(qw)
        kw = _paired_lanes(kw)
        ra = lax.broadcasted_iota(jnp.int32, (pair_w, pair_w), 0) // HEAD_DIM
        rb = lax.broadcasted_iota(jnp.int32, (pair_w, pair_w), 1) // HEAD_DIM
        head_sum = jnp.where(ra == rb, 1.0, 0.0).astype(BF16)

    def normed(sub):
        return _norm_modulate(x_ref[sub * tm:(sub + 1) * tm, :], g_ref[...], mod_ref, 0).astype(BF16)

    def put_kv(ref, sub, kh, val):
        if use_rope:
            ref[sub * tm:(sub + 1) * tm, kh * HEAD_DIM:(kh + 1) * HEAD_DIM] = val.astype(ref.dtype)
        else:
            ref[pl.ds(sub * tm * N_KV_HEADS + kh, tm, stride=N_KV_HEADS), :] = val.astype(ref.dtype)

    h_next = normed(0)
    for sub in range(n_sub):
        h = h_next
        if sub + 1 < n_sub:
            h_next = normed(sub + 1)
        rows = slice(sub * tm, (sub + 1) * tm)
        if use_rope:
            q_tabs = (qw * cos_ref[rows, :], _swap_halves(qw) * sin_ref[rows, :])
            k_tabs = (kw * cos_ref[rows, :], _swap_halves(kw) * sin_ref[rows, :])
        t2_next = jnp.dot(h, wqk_ref[:, 0:pair_w], preferred_element_type=F32)
        for pair in range(n_pairs):
            t2 = t2_next
            if pair + 1 < n_pairs:
                t2_next = jnp.dot(h, wqk_ref[:, (pair + 1) * pair_w:(pair + 2) * pair_w],
                                  preferred_element_type=F32)
            else:
                t2_next = jnp.dot(h, w_ref[:, nq + nk:], preferred_element_type=F32)
            if use_rope:
                ss2 = jnp.dot((t2 * t2).astype(BF16), head_sum, preferred_element_type=F32)
                r2 = lax.rsqrt(ss2 * (1.0 / HEAD_DIM) + EPS)
            for j in range(2):
                head = 2 * pair + j
                t = t2[:, j * HEAD_DIM:(j + 1) * HEAD_DIM]
                if use_rope:
                    r = r2[:, j * HEAD_DIM:(j + 1) * HEAD_DIM]
                    wc, ws = q_tabs if head < N_HEADS else k_tabs
                    t = (t * wc + _swap_halves(t) * ws) * r
                else:
                    r = lax.rsqrt(jnp.mean(t * t, axis=-1, keepdims=True) + EPS)
                    t = t * r * (qw if head < N_HEADS else kw)
                if head < N_HEADS:
                    q_ref[rows, head * HEAD_DIM:(head + 1) * HEAD_DIM] = t.astype(q_ref.dtype)
                else:
                    put_kv(k_ref, sub, head - N_HEADS, t)
        vv = t2_next
        for kh in range(N_KV_HEADS):
            put_kv(v_ref, sub, kh, vv[:, kh * HEAD_DIM:(kh + 1) * HEAD_DIM])


def _attn_in(x, mod, mod_rows, g, w, qn, kn, rope, tile, casts=()):
    rows = x.shape[0]
    nk = N_KV_HEADS * HEAD_DIM
    if rope is None:
        kv_shape = jax.ShapeDtypeStruct((rows * N_KV_HEADS, HEAD_DIM), F32)
        kv_spec = pl.BlockSpec((tile * N_KV_HEADS, HEAD_DIM), lambda i: (i, 0))
    else:
        kv_shape = jax.ShapeDtypeStruct((rows, nk), BF16)
        kv_spec = pl.BlockSpec((tile, nk), lambda i: (i, 0))
    in_specs = [
        pl.BlockSpec((tile, D_MODEL), lambda i: (i, 0)),
        _mod_spec(mod_rows, tile),
        _const_spec((1, D_MODEL)),
        _const_spec((D_MODEL, QKV_W)),
        _const_spec((1, HEAD_DIM)),
        _const_spec((1, HEAD_DIM)),
    ]
    args = [x, mod, g, w, qn, kn]
    scratch = []
    if rope is not None:
        cos, sin, reorder = rope
        seq_tiles = cos.shape[0] // tile
        in_specs += [pl.BlockSpec((tile, HEAD_DIM), lambda i: (i % seq_tiles, 0))] * 2
        in_specs.append(_const_spec((HEAD_DIM, HEAD_DIM)))
        args += [cos, sin, reorder]
        scratch = [pltpu.VMEM((D_MODEL, (N_HEADS + N_KV_HEADS) * HEAD_DIM), BF16)]
    steps = rows // tile
    c_in, c_out, c_shapes = _cast_plumbing(casts, steps)
    out_specs = [pl.BlockSpec((tile, N_HEADS * HEAD_DIM), lambda i: (i, 0)), kv_spec, kv_spec]
    out_shapes = [jax.ShapeDtypeStruct((rows, N_HEADS * HEAD_DIM), BF16), kv_shape, kv_shape]
    body = functools.partial(_attn_in_kernel, use_rope=rope is not None)
    outs = pl.pallas_call(
        _with_casts(body, len(args), len(out_shapes), len(casts)),
        out_shape=tuple(out_shapes + c_shapes),
        grid=(steps,),
        in_specs=in_specs + c_in,
        out_specs=tuple(out_specs + c_out),
        scratch_shapes=scratch,
        compiler_params=_params(("arbitrary",)),
        name="attn_in_rope" if rope is not None else "attn_in",
    )(*args, *[arr for arr, _ in casts])
    return outs[:3], list(outs[3:])


def _kv_head(ref, e, kv):
    if ref.dtype == BF16:
        return ref[e, :, kv * HEAD_DIM:(kv + 1) * HEAD_DIM]
    tokens = ref.shape[1] // N_KV_HEADS
    return ref[e, pl.ds(kv, tokens, stride=N_KV_HEADS), :].astype(BF16)


def _attn_kernel(q_ref, k_ref, v_ref, *rest, cached):
    def with_ones(v):
        return jnp.concatenate([v, jnp.ones_like(v)], axis=1)

    if cached:
        ck_hbm, cv_hbm, reorder_ref, o_ref, ck_scr, cv_scr, stage, sems = rest
        b = pl.program_id(0)
        n_b = pl.num_programs(0)

        def head_copies(bi, slot):
            return [pltpu.make_async_copy(src.at[bi, 0, :, kv, :], stage.at[slot, t, kv], sems.at[slot, t, kv])
                    for t, src in enumerate((ck_hbm, cv_hbm)) for kv in range(N_KV_HEADS)]

        @pl.when(pl.program_id(1) == 0)
        def _():
            slot = b % 2

            @pl.when(b == 0)
            def _():
                for cp in head_copies(0, 0):
                    cp.start()

            @pl.when(b + 1 < n_b)
            def _():
                for cp in head_copies(b + 1, 1 - slot):
                    cp.start()

            for cp in head_copies(b, slot):
                cp.wait()
            for kv in range(N_KV_HEADS):
                ck_scr[kv] = jnp.dot(stage[slot, 0, kv].astype(BF16), reorder_ref[...],
                                     preferred_element_type=F32).astype(BF16)
                cv_scr[kv] = with_ones(stage[slot, 1, kv].astype(BF16))
    else:
        (o_ref,) = rest
    nt = (((1,), (1,)), ((), ()))
    n_elem = q_ref.shape[0]
    ks = [[_kv_head(k_ref, e, kv) for kv in range(N_KV_HEADS)] for e in range(n_elem)]
    vs = [[with_ones(_kv_head(v_ref, e, kv)) for kv in range(N_KV_HEADS)] for e in range(n_elem)]

    def scores(unit):
        e, head = unit
        kv = head // GROUP
        q = q_ref[e, :, head * HEAD_DIM:(head + 1) * HEAD_DIM]
        s = lax.dot_general(q, ks[e][kv], nt, preferred_element_type=F32)
        sc = lax.dot_general(q, ck_scr[kv], nt, preferred_element_type=F32) if cached else None
        return s, sc

    units = [(e, head) for e in range(n_elem) for head in range(N_HEADS)]
    nxt = scores(units[0])
    for ui, (e, head) in enumerate(units):
        kv = head // GROUP
        s, sc = nxt
        if ui + 1 < len(units):
            nxt = scores(units[ui + 1])
        m = jnp.max(s, axis=-1, keepdims=True)
        if cached:
            m = jnp.maximum(m, jnp.max(sc, axis=-1, keepdims=True))
        o = jnp.dot(jnp.exp2(s - m).astype(BF16), vs[e][kv], preferred_element_type=F32)
        if cached:
            o = o + jnp.dot(jnp.exp2(sc - m).astype(BF16), cv_scr[kv], preferred_element_type=F32)
        o_ref[e, :, head * HEAD_DIM:(head + 1) * HEAD_DIM] = (o[:, :HEAD_DIM] / o[:, HEAD_DIM:]).astype(o_ref.dtype)


def _attention(q, k, v, cache=None, casts=()):
    b, n, width = q.shape
    tq = min(ATTN_Q_TILE, n)
    eb = 1 if cache is not None else max(1, min(b, ATTN_CTX_ROWS // n))
    assert b % eb == 0
    in_specs = [pl.BlockSpec((eb, tq, width), lambda bi, qi: (bi, qi, 0))]
    in_specs += [pl.BlockSpec((eb,) + k.shape[1:], lambda bi, qi: (bi, 0, 0))] * 2
    args = [q, k, v]
    scratch = []
    if cache is not None:
        past = cache[0].shape[2]
        in_specs += [pl.BlockSpec(memory_space=pl.ANY)] * 2
        in_specs.append(pl.BlockSpec((HEAD_DIM, HEAD_DIM), lambda bi, qi: (0, 0)))
        args += list(cache)
        scratch = [pltpu.VMEM((N_KV_HEADS, past, HEAD_DIM), BF16),
                   pltpu.VMEM((N_KV_HEADS, past, 2 * HEAD_DIM), BF16)]
        scratch += [pltpu.VMEM((2, 2, N_KV_HEADS, past, HEAD_DIM), F32),
                    pltpu.SemaphoreType.DMA((2, 2, N_KV_HEADS))]
    c_in, c_out, c_shapes = _cast_plumbing(casts, b // eb)
    body = functools.partial(_attn_kernel, cached=cache is not None)
    outs = pl.pallas_call(
        _with_casts(body, len(args), 1, len(casts)),
        out_shape=tuple([jax.ShapeDtypeStruct((b, n, width), BF16)] + c_shapes),
        grid=(b // eb, n // tq),
        in_specs=in_specs + c_in,
        out_specs=tuple([pl.BlockSpec((eb, tq, width), lambda bi, qi: (bi, qi, 0))] + c_out),
        scratch_shapes=scratch,
        compiler_params=_params(("arbitrary", "arbitrary")),
        name="attention_cached" if cache is not None else "attention",
    )(*args, *[arr for arr, _ in casts])
    return outs[0], list(outs[1:])


def _post_kernel(a_ref, x_ref, mod_ref, g_ref, wo_ref, w1_ref, w2_ref, *rest, final):
    if final:
        gf_ref, o_ref = rest
    else:
        (o_ref,) = rest
    gate1 = mod_ref[0, 0, :, 2 * D_MODEL:3 * D_MODEL]
    gate2 = mod_ref[0, 0, :, 5 * D_MODEL:6 * D_MODEL]
    n_ff = D_FF // POST_FF_TILE

    def attn_residual(t):
        rows = slice(t * POST_SUB_TILE, (t + 1) * POST_SUB_TILE)
        x = x_ref[rows, :] + gate1 * jnp.dot(a_ref[rows, :], wo_ref[...], preferred_element_type=F32)
        return x, _norm_modulate(x, g_ref[...], mod_ref, 1).astype(BF16)

    n_sub = x_ref.shape[0] // POST_SUB_TILE
    nxt = attn_residual(0)
    for t in range(n_sub):
        x, h = nxt
        if t + 1 < n_sub:
            nxt = attn_residual(t + 1)
        y = None
        for f in range(n_ff):
            u = jnp.dot(h, w1_ref[:, f * POST_FF_TILE:(f + 1) * POST_FF_TILE], preferred_element_type=F32)
            u = jnp.square(jnp.maximum(u, 0.0)).astype(BF16)
            part = jnp.dot(u, w2_ref[f * POST_FF_TILE:(f + 1) * POST_FF_TILE, :], preferred_element_type=F32)
            y = part if y is None else y + part
        x = x + gate2 * y
        if final:
            x = x * lax.rsqrt(jnp.mean(x * x, axis=-1, keepdims=True) + EPS) * gf_ref[...]
        o_ref[t * POST_SUB_TILE:(t + 1) * POST_SUB_TILE, :] = x


def _post(a, x, mod, mod_rows, g, wo, w1, w2, gf=None, casts=()):
    rows = x.shape[0]
    row_block = pl.BlockSpec((POST_ROW_TILE, D_MODEL), lambda i: (i, 0))
    in_specs = [
        row_block,
        row_block,
        _mod_spec(mod_rows, POST_ROW_TILE),
        _const_spec((1, D_MODEL)),
        _const_spec((D_MODEL, D_MODEL)),
        _const_spec((D_MODEL, D_FF)),
        _const_spec((D_FF, D_MODEL)),
    ]
    args = [a, x, mod, g, wo, w1, w2]
    if gf is not None:
        in_specs.append(_const_spec((1, D_MODEL)))
        args.append(gf)
    steps = rows // POST_ROW_TILE
    c_in, c_out, c_shapes = _cast_plumbing(casts, steps)
    body = functools.partial(_post_kernel, final=gf is not None)
    outs = pl.pallas_call(
        _with_casts(body, len(args), 1, len(casts)),
        out_shape=tuple([jax.ShapeDtypeStruct((rows, D_MODEL), F32)] + c_shapes),
        grid=(steps,),
        in_specs=in_specs + c_in,
        out_specs=tuple([row_block] + c_out),
        compiler_params=_params(("parallel",)),
        name="post_final" if gf is not None else "post",
    )(*args, *[arr for arr, _ in casts])
    return outs[0], list(outs[1:])


def _ret_in_kernel(x_ref, mod_ref, g_ref, w_ref, *rest, use_rope):
    if use_rope:
        cos_ref, sin_ref, q_ref, k_ref, v_ref, gate_ref = rest
    else:
        q_ref, k_ref, v_ref, gate_ref = rest
    width = RET_HEADS * RET_DK
    tm = PROJ_SUB_TILE
    n_sub = x_ref.shape[0] // tm

    def normed(sub):
        return _norm_modulate(x_ref[sub * tm:(sub + 1) * tm, :], g_ref[...], mod_ref, 0).astype(BF16)

    h_next = normed(0)
    for sub in range(n_sub):
        h = h_next
        if sub + 1 < n_sub:
            h_next = normed(sub + 1)
        rows = slice(sub * tm, (sub + 1) * tm)
        for part, ref in enumerate((q_ref, k_ref, v_ref, gate_ref)):
            t = jnp.dot(h, w_ref[:, part * width:(part + 1) * width], preferred_element_type=F32)
            if part < 2 and use_rope:
                for c in range(width // V7X_LANES):
                    half = (c % (RET_DK // V7X_LANES)) * V7X_LANES
                    tc = t[:, c * V7X_LANES:(c + 1) * V7X_LANES]
                    tc = (tc * cos_ref[rows, half:half + V7X_LANES]
                          + _swap_halves(tc) * sin_ref[rows, half:half + V7X_LANES])
                    if part == 0:
                        tc = tc * RET_DK ** -0.5
                    ref[rows, c * V7X_LANES:(c + 1) * V7X_LANES] = tc.astype(ref.dtype)
            else:
                if part == 0:
                    t = t * RET_DK ** -0.5
                ref[rows, :] = t.astype(ref.dtype)


def _ret_in(x, mod, mod_rows, g, w, rope, tile, casts=()):
    rows = x.shape[0]
    width = RET_HEADS * RET_DK
    in_specs = [
        pl.BlockSpec((tile, D_MODEL), lambda i: (i, 0)),
        _mod_spec(mod_rows, tile),
        _const_spec((1, D_MODEL)),
        _const_spec((D_MODEL, RET_W)),
    ]
    args = [x, mod, g, w]
    if rope is not None:
        seq_tiles = rope[0].shape[0] // tile
        in_specs += [pl.BlockSpec((tile, RET_DK), lambda i: (i % seq_tiles, 0))] * 2
        args += list(rope)
    out_block = pl.BlockSpec((tile, width), lambda i: (i, 0))
    steps = rows // tile
    c_in, c_out, c_shapes = _cast_plumbing(casts, steps)
    body = functools.partial(_ret_in_kernel, use_rope=rope is not None)
    outs = pl.pallas_call(
        _with_casts(body, len(args), 4, len(casts)),
        out_shape=tuple([jax.ShapeDtypeStruct((rows, width), BF16)] * 3
                        + [jax.ShapeDtypeStruct((rows, width), F32)] + c_shapes),
        grid=(steps,),
        in_specs=in_specs + c_in,
        out_specs=tuple([out_block] * 4 + c_out),
        compiler_params=_params(("parallel",)),
        name="ret_in_rope" if rope is not None else "ret_in",
    )(*args, *[arr for arr, _ in casts])
    return outs[:4], list(outs[4:])


def _log_sigmoid(x):
    return jnp.minimum(x, 0.0) - jnp.log(1.0 + jnp.exp(-jnp.abs(x)))


def _ret_kernel(q_ref, k_ref, v_ref, gate_ref, lg_ref, gn_ref, *rest, n, tq, heads, has_state):
    if has_state:
        s0_ref, y_ref, d_ref = rest
    else:
        y_ref, st_ref, d_ref = rest
    b = pl.program_id(1)
    qi = pl.program_id(2)
    r0 = qi * tq
    nt = (((1,), (1,)), ((), ()))
    sub = min(tq, RET_SUB_TILE)
    lgs = [(_log_sigmoid(lg_ref[hh, 0, 0:1, 0:1]), _log_sigmoid(lg_ref[hh, 1, 0:1, 0:1]))
           for hh in range(heads)]

    @pl.when(b == 0)
    def _():
        ii = r0 + lax.broadcasted_iota(jnp.int32, (tq, n), 0)
        jj = lax.broadcasted_iota(jnp.int32, (tq, n), 1)
        diff = (ii - jj).astype(F32)
        for hh in range(heads):
            lg_f, lg_b = lgs[hh]
            d_ref[hh, qi] = jnp.exp(jnp.where(diff >= 0.0, lg_f, -lg_b) * diff)

    def head_cols(hh):
        return slice(hh * RET_DK, (hh + 1) * RET_DK)

    def scores(unit):
        e, hh, si = unit
        return lax.dot_general(q_ref[e, si * sub:(si + 1) * sub, head_cols(hh)], k_ref[e, :, head_cols(hh)],
                               nt, preferred_element_type=F32)

    units = [(e, hh, si) for e in range(q_ref.shape[0]) for hh in range(heads) for si in range(tq // sub)]
    nxt = scores(units[0])
    for ui, (e, hh, si) in enumerate(units):
        s = nxt
        if ui + 1 < len(units):
            nxt = scores(units[ui + 1])
        lg_f, lg_b = lgs[hh]
        cols = head_cols(hh)
        rows = slice(si * sub, (si + 1) * sub)
        v = v_ref[e, :, cols]
        p = (s * d_ref[hh, qi, rows, :]).astype(BF16)
        o = jnp.dot(p, v, preferred_element_type=F32)
        if has_state:
            q = q_ref[e, rows, cols]
            i_col = (r0 + si * sub + lax.broadcasted_iota(jnp.int32, (sub, 1), 0)).astype(F32)
            o = o + (jnp.dot(q, s0_ref[e, 0, 0, hh].astype(BF16), preferred_element_type=F32)
                     * jnp.exp(lg_f * (i_col + 1.0))
                     + jnp.dot(q, s0_ref[e, 0, 1, hh].astype(BF16), preferred_element_type=F32)
                     * jnp.exp(lg_b * (n - i_col)))
        elif si == 0:
            eye = jnp.where(lax.broadcasted_iota(jnp.int32, (RET_DK, RET_DK), 0)
                            == lax.broadcasted_iota(jnp.int32, (RET_DK, RET_DK), 1), 1.0, 0.0).astype(BF16)
            kt = lax.dot_general(eye, k_ref[e, :, cols], nt, preferred_element_type=F32)
            j_row = lax.broadcasted_iota(jnp.int32, (1, n), 1).astype(F32)
            st_ref[e, 0, 0, hh] = jnp.dot((kt * jnp.exp(lg_f * (n - 1.0 - j_row))).astype(BF16), v,
                                          preferred_element_type=F32)
            st_ref[e, 0, 1, hh] = jnp.dot((kt * jnp.exp(lg_b * j_row)).astype(BF16), v,
                                          preferred_element_type=F32)
        mu = jnp.mean(o, axis=-1, keepdims=True)
        oc = o - mu
        var = jnp.mean(oc * oc, axis=-1, keepdims=True)
        on = oc * lax.rsqrt(var + EPS) * gn_ref[:, cols]
        gt = gate_ref[e, rows, cols]
        y_ref[e, rows, cols] = (gt * _sigmoid(gt) * on).astype(y_ref.dtype)


def _retention(q, k, v, gate, lg, gn, state0, heads, tq, elems=1):
    b, n, _ = q.shape
    assert n % tq == 0 and RET_HEADS % heads == 0 and (state0 is not None or tq == n) and b % elems == 0
    width = heads * RET_DK
    q_tile = pl.BlockSpec((elems, tq, width), lambda hg, bi, qi: (bi, qi, hg))
    seq = pl.BlockSpec((elems, n, width), lambda hg, bi, qi: (bi, 0, hg))
    st = pl.BlockSpec((elems, 1, 2, heads, RET_DK, RET_DV), lambda hg, bi, qi: (bi, 0, 0, hg, 0, 0))
    in_specs = [q_tile, seq, seq, q_tile,
                pl.BlockSpec((heads, 2, V7X_SUBLANES, V7X_LANES), lambda hg, bi, qi: (hg, 0, 0, 0)),
                pl.BlockSpec((1, width), lambda hg, bi, qi: (0, hg))]
    args = [q, k, v, gate, lg, gn]
    y_shape = jax.ShapeDtypeStruct((b, n, RET_HEADS * RET_DV), BF16)
    if state0 is not None:
        in_specs.append(st)
        args.append(state0)
        out_shape, out_specs = y_shape, q_tile
    else:
        out_shape = (y_shape, jax.ShapeDtypeStruct((b, 1, 2, RET_HEADS, RET_DK, RET_DV), F32))
        out_specs = (q_tile, st)
    return pl.pallas_call(
        functools.partial(_ret_kernel, n=n, tq=tq, heads=heads, has_state=state0 is not None),
        out_shape=out_shape,
        grid=(RET_HEADS // heads, b // elems, n // tq),
        in_specs=in_specs,
        out_specs=out_specs,
        scratch_shapes=[pltpu.VMEM((heads, n // tq, tq, n), F32)],
        compiler_params=_params(("arbitrary", "arbitrary", "arbitrary")),
        name="retention_state" if state0 is not None else "retention",
    )(*args)


def kernel(x_prompt, x_sample, cache_k, cache_v, state_ret, c, c_ctx, w_mod, b_mod, norm_g,
           attn_w_qkv, attn_q_norm, attn_k_norm, attn_w_o, ret_w_qkvg, ret_decay_logit, ret_gn_w,
           ret_w_o, mlp_w1, mlp_w2, final_norm_g):
    bp, sp, d = x_prompt.shape
    bs, ss, _ = x_sample.shape
    depth = w_mod.shape[0]
    assert d == D_MODEL and depth == 2 and bs % V7X_SUBLANES == 0
    assert (bp * sp) % POST_ROW_TILE == 0 and ss % POST_ROW_TILE == 0 and POST_ROW_TILE % ROW_TILE == 0

    mod, (wqkv,) = _modulation(c, c_ctx, w_mod, b_mod, casts=[(attn_w_qkv, 0)])
    ctx_mod = (0, bs)
    lat_mod = (ss, 0)

    xp = x_prompt.reshape(bp * sp, d)
    xs = x_sample.reshape(bs * ss, d)
    nkv = N_KV_HEADS * HEAD_DIM

    g0 = norm_g[0, 0][None, :]
    g1 = norm_g[0, 1][None, :]
    qn = attn_q_norm[0][None, :]
    kn = attn_k_norm[0][None, :]

    (qp, kp, vp), _ = _attn_in(xp, mod, (0,) + ctx_mod, g0, wqkv, qn, kn, None, PROJ_SUB_TILE)
    reorder = np.zeros((HEAD_DIM, HEAD_DIM), np.float32)
    reorder[_PAIRED_ORDER, np.arange(HEAD_DIM)] = 1.0
    reorder = jnp.asarray(reorder, BF16)
    (qs, ks, vs), (w1, w2, wo) = _attn_in(xs, mod, (0,) + lat_mod, g0, wqkv, qn, kn,
                                          _rope_tables(ss, HEAD_DIM, True) + (reorder,), ROW_TILE,
                                          casts=[(mlp_w1, 0), (mlp_w2, 0), (attn_w_o, 0)])

    ap, _ = _attention(qp.reshape(bp, sp, -1), kp.reshape(bp, sp * N_KV_HEADS, HEAD_DIM),
                       vp.reshape(bp, sp * N_KV_HEADS, HEAD_DIM))
    xp, _ = _post(ap.reshape(bp * sp, -1), xp, mod, (0,) + ctx_mod, g1, wo, w1, w2)
    cache = (cache_k, cache_v, reorder)
    as_, (wr,) = _attention(qs.reshape(bs, ss, -1), ks.reshape(bs, ss, nkv), vs.reshape(bs, ss, nkv), cache,
                            casts=[(ret_w_qkvg, 0)])
    xs, _ = _post(as_.reshape(bs * ss, -1), xs, mod, (0,) + lat_mod, g1, wo, w1, w2)

    g0 = norm_g[1, 0][None, :]
    g1 = norm_g[1, 1][None, :]
    gn = ret_gn_w[0][None, :]
    gf = final_norm_g[None, :]
    lg = jnp.broadcast_to(ret_decay_logit[0].T[:, :, None, None],
                          (RET_HEADS, 2, V7X_SUBLANES, V7X_LANES))
    hw = RET_HEADS * RET_DK

    (q, k, v, gate), (w1,) = _ret_in(xp, mod, (1,) + ctx_mod, g0, wr, None, PROJ_SUB_TILE, casts=[(mlp_w1, 1)])
    yp, new_state = _retention(q.reshape(bp, sp, hw), k.reshape(bp, sp, hw), v.reshape(bp, sp, hw),
                               gate.reshape(bp, sp, hw), lg, gn, None, RET_HEADS, sp, RET_CTX_ELEMS)
    (q, k, v, gate), (w2, wo) = _ret_in(xs, mod, (1,) + lat_mod, g0, wr, _rope_tables(ss, RET_DK, False),
                                        ROW_TILE, casts=[(mlp_w2, 1), (ret_w_o, 0)])
    ys = _retention(q.reshape(bs, ss, hw), k.reshape(bs, ss, hw), v.reshape(bs, ss, hw),
                    gate.reshape(bs, ss, hw), lg, gn, state_ret, RET_STATE_HEADS, RET_Q_TILE)
    y_prompt, _ = _post(yp.reshape(bp * sp, hw), xp, mod, (1,) + ctx_mod, g1, wo, w1, w2, gf)
    y_sample, _ = _post(ys.reshape(bs * ss, hw), xs, mod, (1,) + lat_mod, g1, wo, w1, w2, gf)

    return (y_prompt.reshape(bp, sp, d),
            y_sample.reshape(bs, ss, d),
            kp.reshape(bp, 1, sp, N_KV_HEADS, HEAD_DIM),
            vp.reshape(bp, 1, sp, N_KV_HEADS, HEAD_DIM),
            new_state)
```

```python
import functools

import numpy as np
import jax
import jax.numpy as jnp
from jax import lax
from jax.experimental import pallas as pl
from jax.experimental.pallas import tpu as pltpu

F32 = jnp.float32
BF16 = jnp.bfloat16

D_MODEL = 1024
GRID_W = 64
N_HEADS = 8
N_KV_HEADS = 2
HEAD_DIM = 128
GROUP = N_HEADS // N_KV_HEADS
ROPE_BASE = 10000.0
RET_HEADS = 4
RET_DK = 256
RET_DV = 256
D_FF = 4 * D_MODEL
Q_SCALE = HEAD_DIM ** -0.5 * float(np.log2(np.e))
EPS = 1e-6
QKV_W = (N_HEADS + 2 * N_KV_HEADS) * HEAD_DIM
RET_W = 2 * RET_HEADS * RET_DK + 2 * RET_HEADS * RET_DV

V7X_LANES = 128
V7X_SUBLANES = 8
MOD_COL_TILE = 2048
ROW_TILE = 1024
PROJ_SUB_TILE = 512
POST_ROW_TILE = 1024
POST_SUB_TILE = 512
POST_FF_TILE = 1024
ATTN_Q_TILE = 1024
RET_Q_TILE = 1024
RET_SUB_TILE = 256
RET_STATE_HEADS = 2
RET_CTX_ELEMS = 4
ATTN_CTX_ROWS = 1024
VMEM_LIMIT = 56 * 1024 * 1024


def _params(sem, vmem=VMEM_LIMIT):
    return pltpu.CompilerParams(dimension_semantics=sem, vmem_limit_bytes=vmem)


def _const_spec(shape):
    nd = len(shape)
    return pl.BlockSpec(shape, lambda *_: (0,) * nd, pipeline_mode=pl.Buffered(1))


def _sigmoid(x):
    return 1.0 / (1.0 + jnp.exp(-x))


def _cast_plumbing(casts, steps):
    in_specs, out_specs, out_shapes = [], [], []
    for arr, layer in casts:
        _, r, c = arr.shape
        slab = r // steps
        assert slab * steps == r and slab % 16 == 0
        in_specs.append(pl.BlockSpec((1, slab, c), functools.partial(lambda *idx, l: (l, idx[0], 0), l=layer)))
        out_specs.append(pl.BlockSpec((slab, c), lambda *idx: (idx[0], 0)))
        out_shapes.append(jax.ShapeDtypeStruct((r, c), BF16))
    return in_specs, out_specs, out_shapes


def _with_casts(body, n_in, n_out, n_cast):
    def kernel(*refs):
        ins = refs[:n_in]
        cast_in = refs[n_in:n_in + n_cast]
        outs = refs[n_in + n_cast:n_in + n_cast + n_out]
        cast_out = refs[n_in + n_cast + n_out:n_in + 2 * n_cast + n_out]
        scratch = refs[n_in + 2 * n_cast + n_out:]
        for src, dst in zip(cast_in, cast_out):
            dst[...] = src[0].astype(dst.dtype)
        body(*ins, *outs, *scratch)
    return kernel


def _mod_kernel(c_ref, cctx_ref, w_ref, b_ref, o_ref):
    cond = jnp.concatenate([c_ref[...], jnp.broadcast_to(cctx_ref[...], (V7X_SUBLANES, D_MODEL))], axis=0)
    s = (cond * _sigmoid(cond)).astype(BF16)
    bias = b_ref[pl.ds(pl.program_id(0), 1), :]
    o_ref[0, :, 0, :] = jnp.dot(s, w_ref[0].astype(BF16), preferred_element_type=F32) + bias


def _modulation(c, c_ctx, w_mod, b_mod, casts=()):
    depth = w_mod.shape[0]
    rows = c.shape[0] + V7X_SUBLANES
    tn = MOD_COL_TILE
    c_in, c_out, c_shapes = _cast_plumbing(casts, depth)
    outs = pl.pallas_call(
        _with_casts(_mod_kernel, 4, 1, len(casts)),
        out_shape=tuple([jax.ShapeDtypeStruct((depth, rows, 1, 6 * D_MODEL), F32)] + c_shapes),
        grid=(depth, 6 * D_MODEL // tn),
        in_specs=[
            pl.BlockSpec(c.shape, lambda i, j: (0, 0)),
            pl.BlockSpec((1, D_MODEL), lambda i, j: (0, 0)),
            pl.BlockSpec((1, D_MODEL, tn), lambda i, j: (i, 0, j)),
            pl.BlockSpec((depth, tn), lambda i, j: (0, j)),
        ] + c_in,
        out_specs=tuple([pl.BlockSpec((1, rows, 1, tn), lambda i, j: (i, 0, 0, j))] + c_out),
        compiler_params=_params(("arbitrary", "arbitrary")),
        name="modulation",
    )(c, c_ctx.reshape(1, D_MODEL), w_mod, b_mod, *[arr for arr, _ in casts])
    return outs[0], list(outs[1:])


def _norm_modulate(x, g, mod_ref, which):
    y = x * lax.rsqrt(jnp.mean(x * x, axis=-1, keepdims=True) + EPS) * g
    shift = mod_ref[0, 0, :, (3 * which) * D_MODEL:(3 * which + 1) * D_MODEL]
    scale = mod_ref[0, 0, :, (3 * which + 1) * D_MODEL:(3 * which + 2) * D_MODEL]
    return y * (1.0 + scale) + shift


def _mod_spec(mod_rows, tile):
    layer, rows_per_mod_row, first_row = mod_rows
    if rows_per_mod_row:
        assert rows_per_mod_row % tile == 0
        tiles = rows_per_mod_row // tile
        return pl.BlockSpec((1, 1, 1, 6 * D_MODEL), lambda i: (layer, first_row + i // tiles, 0, 0))
    return pl.BlockSpec((1, 1, 1, 6 * D_MODEL), lambda i: (layer, first_row, 0, 0))


def _rope_tables(n, dim, paired):
    quarter = dim // 4
    rows = np.repeat(np.arange(n // GRID_W), GRID_W).astype(np.float64)
    cols = np.tile(np.arange(GRID_W), n // GRID_W).astype(np.float64)
    freqs = ROPE_BASE ** (-np.arange(quarter, dtype=np.float64) / quarter)
    ar = rows[:, None] * freqs
    ac = cols[:, None] * freqs
    if paired:
        cos = np.concatenate([np.cos(ar), np.cos(ac), np.cos(ar), np.cos(ac)], axis=1)
        sin = np.concatenate([-np.sin(ar), -np.sin(ac), np.sin(ar), np.sin(ac)], axis=1)
    else:
        cos = np.concatenate([np.cos(ar), np.cos(ar), np.cos(ac), np.cos(ac)], axis=1)
        sin = np.concatenate([-np.sin(ar), np.sin(ar), -np.sin(ac), np.sin(ac)], axis=1)
    return jnp.asarray(cos, F32), jnp.asarray(sin, F32)


_Q = HEAD_DIM // 4
_PAIRED_ORDER = np.concatenate([np.arange(0, _Q), np.arange(2 * _Q, 3 * _Q),
                                np.arange(_Q, 2 * _Q), np.arange(3 * _Q, 4 * _Q)])


def _paired_lanes(w):
    lane = lax.broadcasted_iota(jnp.int32, w.shape, 1)
    from_c = pltpu.roll(w, V7X_LANES - _Q, axis=1)
    from_b = pltpu.roll(w, _Q, axis=1)
    return jnp.where((lane >= _Q) & (lane < 2 * _Q), from_c,
                     jnp.where((lane >= 2 * _Q) & (lane < 3 * _Q), from_b, w))


def _swap_halves(t):
    return pltpu.roll(t, V7X_LANES // 2, axis=1)


def _attn_in_kernel(x_ref, mod_ref, g_ref, w_ref, qn_ref, kn_ref, *rest, use_rope):
    nq = N_HEADS * HEAD_DIM
    nk = N_KV_HEADS * HEAD_DIM
    if use_rope:
        cos_ref, sin_ref, reorder_ref, q_ref, k_ref, v_ref, wqk_ref = rest

        @pl.when(pl.program_id(0) == 0)
        def _():
            for head in range(N_HEADS + N_KV_HEADS):
                cols = slice(head * HEAD_DIM, (head + 1) * HEAD_DIM)
                wqk_ref[:, cols] = jnp.dot(w_ref[:, cols], reorder_ref[...],
                                           preferred_element_type=F32).astype(BF16)
    else:
        q_ref, k_ref, v_ref = rest
        wqk_ref = w_ref
    tm = PROJ_SUB_TILE
    n_sub = x_ref.shape[0] // tm
    pair_w = 2 * HEAD_DIM
    n_pairs = (N_HEADS + N_KV_HEADS) // 2
    qw = qn_ref[...] * Q_SCALE
    kw = kn_ref[...]
    if use_rope:
        qw = _paired_lanes(qw)
        kw = _paired_lanes(kw)
        ra = lax.broadcasted_iota(jnp.int32, (pair_w, pair_w), 0) // HEAD_DIM
        rb = lax.broadcasted_iota(jnp.int32, (pair_w, pair_w), 1) // HEAD_DIM
        head_sum = jnp.where(ra == rb, 1.0, 0.0).astype(BF16)

    def normed(sub):
        return _norm_modulate(x_ref[sub * tm:(sub + 1) * tm, :], g_ref[...], mod_ref, 0).astype(BF16)

    def put_kv(ref, sub, kh, val):
        if use_rope:
            ref[sub * tm:(sub + 1) * tm, kh * HEAD_DIM:(kh + 1) * HEAD_DIM] = val.astype(ref.dtype)
        else:
            ref[pl.ds(sub * tm * N_KV_HEADS + kh, tm, stride=N_KV_HEADS), :] = val.astype(ref.dtype)

    h_next = normed(0)
    for sub in range(n_sub):
        h = h_next
        if sub + 1 < n_sub:
            h_next = normed(sub + 1)
        rows = slice(sub * tm, (sub + 1) * tm)
        if use_rope:
            q_tabs = (qw * cos_ref[rows, :], _swap_halves(qw) * sin_ref[rows, :])
            k_tabs = (kw * cos_ref[rows, :], _swap_halves(kw) * sin_ref[rows, :])
        t2_next = jnp.dot(h, wqk_ref[:, 0:pair_w], preferred_element_type=F32)
        for pair in range(n_pairs):
            t2 = t2_next
            if pair + 1 < n_pairs:
                t2_next = jnp.dot(h, wqk_ref[:, (pair + 1) * pair_w:(pair + 2) * pair_w],
                                  preferred_element_type=F32)
            else:
                t2_next = jnp.dot(h, w_ref[:, nq + nk:], preferred_element_type=F32)
            if use_rope:
                ss2 = jnp.dot((t2 * t2).astype(BF16), head_sum, preferred_element_type=F32)
                r2 = lax.rsqrt(ss2 * (1.0 / HEAD_DIM) + EPS)
            for j in range(2):
                head = 2 * pair + j
                t = t2[:, j * HEAD_DIM:(j + 1) * HEAD_DIM]
                if use_rope:
                    r = r2[:, j * HEAD_DIM:(j + 1) * HEAD_DIM]
                    wc, ws = q_tabs if head < N_HEADS else k_tabs
                    t = (t * wc + _swap_halves(t) * ws) * r
                else:
                    r = lax.rsqrt(jnp.mean(t * t, axis=-1, keepdims=True) + EPS)
                    t = t * r * (qw if head < N_HEADS else kw)
                if head < N_HEADS:
                    q_ref[rows, head * HEAD_DIM:(head + 1) * HEAD_DIM] = t.astype(q_ref.dtype)
                else:
                    put_kv(k_ref, sub, head - N_HEADS, t)
        vv = t2_next
        for kh in range(N_KV_HEADS):
            put_kv(v_ref, sub, kh, vv[:, kh * HEAD_DIM:(kh + 1) * HEAD_DIM])


def _attn_in(x, mod, mod_rows, g, w, qn, kn, rope, tile, casts=()):
    rows = x.shape[0]
    nk = N_KV_HEADS * HEAD_DIM
    if rope is None:
        kv_shape = jax.ShapeDtypeStruct((rows * N_KV_HEADS, HEAD_DIM), F32)
        kv_spec = pl.BlockSpec((tile * N_KV_HEADS, HEAD_DIM), lambda i: (i, 0))
    else:
        kv_shape = jax.ShapeDtypeStruct((rows, nk), BF16)
        kv_spec = pl.BlockSpec((tile, nk), lambda i: (i, 0))
    in_specs = [
        pl.BlockSpec((tile, D_MODEL), lambda i: (i, 0)),
        _mod_spec(mod_rows, tile),
        _const_spec((1, D_MODEL)),
        _const_spec((D_MODEL, QKV_W)),
        _const_spec((1, HEAD_DIM)),
        _const_spec((1, HEAD_DIM)),
    ]
    args = [x, mod, g, w, qn, kn]
    scratch = []
    if rope is not None:
        cos, sin, reorder = rope
        seq_tiles = cos.shape[0] // tile
        in_specs += [pl.BlockSpec((tile, HEAD_DIM), lambda i: (i % seq_tiles, 0))] * 2
        in_specs.append(_const_spec((HEAD_DIM, HEAD_DIM)))
        args += [cos, sin, reorder]
        scratch = [pltpu.VMEM((D_MODEL, (N_HEADS + N_KV_HEADS) * HEAD_DIM), BF16)]
    steps = rows // tile
    c_in, c_out, c_shapes = _cast_plumbing(casts, steps)
    out_specs = [pl.BlockSpec((tile, N_HEADS * HEAD_DIM), lambda i: (i, 0)), kv_spec, kv_spec]
    out_shapes = [jax.ShapeDtypeStruct((rows, N_HEADS * HEAD_DIM), BF16), kv_shape, kv_shape]
    body = functools.partial(_attn_in_kernel, use_rope=rope is not None)
    outs = pl.pallas_call(
        _with_casts(body, len(args), len(out_shapes), len(casts)),
        out_shape=tuple(out_shapes + c_shapes),
        grid=(steps,),
        in_specs=in_specs + c_in,
        out_specs=tuple(out_specs + c_out),
        scratch_shapes=scratch,
        compiler_params=_params(("arbitrary",)),
        name="attn_in_rope" if rope is not None else "attn_in",
    )(*args, *[arr for arr, _ in casts])
    return outs[:3], list(outs[3:])


def _kv_head(ref, e, kv):
    if ref.dtype == BF16:
        return ref[e, :, kv * HEAD_DIM:(kv + 1) * HEAD_DIM]
    tokens = ref.shape[1] // N_KV_HEADS
    return ref[e, pl.ds(kv, tokens, stride=N_KV_HEADS), :].astype(BF16)


def _attn_kernel(q_ref, k_ref, v_ref, *rest, cached):
    def with_ones(v):
        return jnp.concatenate([v, jnp.ones_like(v)], axis=1)

    if cached:
        ck_hbm, cv_hbm, reorder_ref, o_ref, ck_scr, cv_scr, stage, sems = rest
        b = pl.program_id(0)
        n_b = pl.num_programs(0)

        def head_copies(bi, slot):
            return [pltpu.make_async_copy(src.at[bi, 0, :, kv, :], stage.at[slot, t, kv], sems.at[slot, t, kv])
                    for t, src in enumerate((ck_hbm, cv_hbm)) for kv in range(N_KV_HEADS)]

        @pl.when(pl.program_id(1) == 0)
        def _():
            slot = b % 2

            @pl.when(b == 0)
            def _():
                for cp in head_copies(0, 0):
                    cp.start()

            @pl.when(b + 1 < n_b)
            def _():
                for cp in head_copies(b + 1, 1 - slot):
                    cp.start()

            for cp in head_copies(b, slot):
                cp.wait()
            for kv in range(N_KV_HEADS):
                ck_scr[kv] = jnp.dot(stage[slot, 0, kv].astype(BF16), reorder_ref[...],
                                     preferred_element_type=F32).astype(BF16)
                cv_scr[kv] = with_ones(stage[slot, 1, kv].astype(BF16))
    else:
        (o_ref,) = rest
    nt = (((1,), (1,)), ((), ()))
    n_elem = q_ref.shape[0]
    ks = [[_kv_head(k_ref, e, kv) for kv in range(N_KV_HEADS)] for e in range(n_elem)]
    vs = [[with_ones(_kv_head(v_ref, e, kv)) for kv in range(N_KV_HEADS)] for e in range(n_elem)]

    def scores(unit):
        e, head = unit
        kv = head // GROUP
        q = q_ref[e, :, head * HEAD_DIM:(head + 1) * HEAD_DIM]
        s = lax.dot_general(q, ks[e][kv], nt, preferred_element_type=F32)
        sc = lax.dot_general(q, ck_scr[kv], nt, preferred_element_type=F32) if cached else None
        return s, sc

    units = [(e, head) for e in range(n_elem) for head in range(N_HEADS)]
    nxt = scores(units[0])
    for ui, (e, head) in enumerate(units):
        kv = head // GROUP
        s, sc = nxt
        if ui + 1 < len(units):
            nxt = scores(units[ui + 1])
        m = jnp.max(s, axis=-1, keepdims=True)
        if cached:
            m = jnp.maximum(m, jnp.max(sc, axis=-1, keepdims=True))
        o = jnp.dot(jnp.exp2(s - m).astype(BF16), vs[e][kv], preferred_element_type=F32)
        if cached:
            o = o + jnp.dot(jnp.exp2(sc - m).astype(BF16), cv_scr[kv], preferred_element_type=F32)
        o_ref[e, :, head * HEAD_DIM:(head + 1) * HEAD_DIM] = (o[:, :HEAD_DIM] / o[:, HEAD_DIM:]).astype(o_ref.dtype)


def _attention(q, k, v, cache=None, casts=()):
    b, n, width = q.shape
    tq = min(ATTN_Q_TILE, n)
    eb = 1 if cache is not None else max(1, min(b, ATTN_CTX_ROWS // n))
    assert b % eb == 0
    in_specs = [pl.BlockSpec((eb, tq, width), lambda bi, qi: (bi, qi, 0))]
    in_specs += [pl.BlockSpec((eb,) + k.shape[1:], lambda bi, qi: (bi, 0, 0))] * 2
    args = [q, k, v]
    scratch = []
    if cache is not None:
        past = cache[0].shape[2]
        in_specs += [pl.BlockSpec(memory_space=pl.ANY)] * 2
        in_specs.append(pl.BlockSpec((HEAD_DIM, HEAD_DIM), lambda bi, qi: (0, 0)))
        args += list(cache)
        scratch = [pltpu.VMEM((N_KV_HEADS, past, HEAD_DIM), BF16),
                   pltpu.VMEM((N_KV_HEADS, past, 2 * HEAD_DIM), BF16)]
        scratch += [pltpu.VMEM((2, 2, N_KV_HEADS, past, HEAD_DIM), F32),
                    pltpu.SemaphoreType.DMA((2, 2, N_KV_HEADS))]
    c_in, c_out, c_shapes = _cast_plumbing(casts, b // eb)
    body = functools.partial(_attn_kernel, cached=cache is not None)
    outs = pl.pallas_call(
        _with_casts(body, len(args), 1, len(casts)),
        out_shape=tuple([jax.ShapeDtypeStruct((b, n, width), BF16)] + c_shapes),
        grid=(b // eb, n // tq),
        in_specs=in_specs + c_in,
        out_specs=tuple([pl.BlockSpec((eb, tq, width), lambda bi, qi: (bi, qi, 0))] + c_out),
        scratch_shapes=scratch,
        compiler_params=_params(("arbitrary", "arbitrary")),
        name="attention_cached" if cache is not None else "attention",
    )(*args, *[arr for arr, _ in casts])
    return outs[0], list(outs[1:])


def _post_kernel(a_ref, x_ref, mod_ref, g_ref, wo_ref, w1_ref, w2_ref, *rest, final):
    if final:
        gf_ref, o_ref = rest
    else:
        (o_ref,) = rest
    gate1 = mod_ref[0, 0, :, 2 * D_MODEL:3 * D_MODEL]
    gate2 = mod_ref[0, 0, :, 5 * D_MODEL:6 * D_MODEL]
    n_ff = D_FF // POST_FF_TILE

    def attn_residual(t):
        rows = slice(t * POST_SUB_TILE, (t + 1) * POST_SUB_TILE)
        x = x_ref[rows, :] + gate1 * jnp.dot(a_ref[rows, :], wo_ref[...], preferred_element_type=F32)
        return x, _norm_modulate(x, g_ref[...], mod_ref, 1).astype(BF16)

    n_sub = x_ref.shape[0] // POST_SUB_TILE
    nxt = attn_residual(0)
    for t in range(n_sub):
        x, h = nxt
        if t + 1 < n_sub:
            nxt = attn_residual(t + 1)
        y = None
        for f in range(n_ff):
            u = jnp.dot(h, w1_ref[:, f * POST_FF_TILE:(f + 1) * POST_FF_TILE], preferred_element_type=F32)
            u = jnp.square(jnp.maximum(u, 0.0)).astype(BF16)
            part = jnp.dot(u, w2_ref[f * POST_FF_TILE:(f + 1) * POST_FF_TILE, :], preferred_element_type=F32)
            y = part if y is None else y + part
        x = x + gate2 * y
        if final:
            x = x * lax.rsqrt(jnp.mean(x * x, axis=-1, keepdims=True) + EPS) * gf_ref[...]
        o_ref[t * POST_SUB_TILE:(t + 1) * POST_SUB_TILE, :] = x


def _post(a, x, mod, mod_rows, g, wo, w1, w2, gf=None, casts=()):
    rows = x.shape[0]
    row_block = pl.BlockSpec((POST_ROW_TILE, D_MODEL), lambda i: (i, 0))
    in_specs = [
        row_block,
        row_block,
        _mod_spec(mod_rows, POST_ROW_TILE),
        _const_spec((1, D_MODEL)),
        _const_spec((D_MODEL, D_MODEL)),
        _const_spec((D_MODEL, D_FF)),
        _const_spec((D_FF, D_MODEL)),
    ]
    args = [a, x, mod, g, wo, w1, w2]
    if gf is not None:
        in_specs.append(_const_spec((1, D_MODEL)))
        args.append(gf)
    steps = rows // POST_ROW_TILE
    c_in, c_out, c_shapes = _cast_plumbing(casts, steps)
    body = functools.partial(_post_kernel, final=gf is not None)
    outs = pl.pallas_call(
        _with_casts(body, len(args), 1, len(casts)),
        out_shape=tuple([jax.ShapeDtypeStruct((rows, D_MODEL), F32)] + c_shapes),
        grid=(steps,),
        in_specs=in_specs + c_in,
        out_specs=tuple([row_block] + c_out),
        compiler_params=_params(("parallel",)),
        name="post_final" if gf is not None else "post",
    )(*args, *[arr for arr, _ in casts])
    return outs[0], list(outs[1:])


def _ret_in_kernel(x_ref, mod_ref, g_ref, w_ref, *rest, use_rope):
    if use_rope:
        cos_ref, sin_ref, q_ref, k_ref, v_ref, gate_ref = rest
    else:
        q_ref, k_ref, v_ref, gate_ref = rest
    width = RET_HEADS * RET_DK
    tm = PROJ_SUB_TILE
    n_sub = x_ref.shape[0] // tm

    def normed(sub):
        return _norm_modulate(x_ref[sub * tm:(sub + 1) * tm, :], g_ref[...], mod_ref, 0).astype(BF16)

    h_next = normed(0)
    for sub in range(n_sub):
        h = h_next
        if sub + 1 < n_sub:
            h_next = normed(sub + 1)
        rows = slice(sub * tm, (sub + 1) * tm)
        for part, ref in enumerate((q_ref, k_ref, v_ref, gate_ref)):
            t = jnp.dot(h, w_ref[:, part * width:(part + 1) * width], preferred_element_type=F32)
            if part < 2 and use_rope:
                for c in range(width // V7X_LANES):
                    half = (c % (RET_DK // V7X_LANES)) * V7X_LANES
                    tc = t[:, c * V7X_LANES:(c + 1) * V7X_LANES]
                    tc = (tc * cos_ref[rows, half:half + V7X_LANES]
                          + _swap_halves(tc) * sin_ref[rows, half:half + V7X_LANES])
                    if part == 0:
                        tc = tc * RET_DK ** -0.5
                    ref[rows, c * V7X_LANES:(c + 1) * V7X_LANES] = tc.astype(ref.dtype)
            else:
                if part == 0:
                    t = t * RET_DK ** -0.5
                ref[rows, :] = t.astype(ref.dtype)


def _ret_in(x, mod, mod_rows, g, w, rope, tile, casts=()):
    rows = x.shape[0]
    width = RET_HEADS * RET_DK
    in_specs = [
        pl.BlockSpec((tile, D_MODEL), lambda i: (i, 0)),
        _mod_spec(mod_rows, tile),
        _const_spec((1, D_MODEL)),
        _const_spec((D_MODEL, RET_W)),
    ]
    args = [x, mod, g, w]
    if rope is not None:
        seq_tiles = rope[0].shape[0] // tile
        in_specs += [pl.BlockSpec((tile, RET_DK), lambda i: (i % seq_tiles, 0))] * 2
        args += list(rope)
    out_block = pl.BlockSpec((tile, width), lambda i: (i, 0))
    steps = rows // tile
    c_in, c_out, c_shapes = _cast_plumbing(casts, steps)
    body = functools.partial(_ret_in_kernel, use_rope=rope is not None)
    outs = pl.pallas_call(
        _with_casts(body, len(args), 4, len(casts)),
        out_shape=tuple([jax.ShapeDtypeStruct((rows, width), BF16)] * 3
                        + [jax.ShapeDtypeStruct((rows, width), F32)] + c_shapes),
        grid=(steps,),
        in_specs=in_specs + c_in,
        out_specs=tuple([out_block] * 4 + c_out),
        compiler_params=_params(("parallel",)),
        name="ret_in_rope" if rope is not None else "ret_in",
    )(*args, *[arr for arr, _ in casts])
    return outs[:4], list(outs[4:])


def _log_sigmoid(x):
    return jnp.minimum(x, 0.0) - jnp.log(1.0 + jnp.exp(-jnp.abs(x)))


def _ret_kernel(q_ref, k_ref, v_ref, gate_ref, lg_ref, gn_ref, *rest, n, tq, heads, has_state):
    if has_state:
        s0_ref, y_ref, d_ref = rest
    else:
        y_ref, st_ref, d_ref = rest
    b = pl.program_id(1)
    qi = pl.program_id(2)
    r0 = qi * tq
    nt = (((1,), (1,)), ((), ()))
    sub = min(tq, RET_SUB_TILE)
    lgs = [(_log_sigmoid(lg_ref[hh, 0, 0:1, 0:1]), _log_sigmoid(lg_ref[hh, 1, 0:1, 0:1]))
           for hh in range(heads)]

    @pl.when(b == 0)
    def _():
        ii = r0 + lax.broadcasted_iota(jnp.int32, (tq, n), 0)
        jj = lax.broadcasted_iota(jnp.int32, (tq, n), 1)
        diff = (ii - jj).astype(F32)
        for hh in range(heads):
            lg_f, lg_b = lgs[hh]
            d_ref[hh, qi] = jnp.exp(jnp.where(diff >= 0.0, lg_f, -lg_b) * diff)

    def head_cols(hh):
        return slice(hh * RET_DK, (hh + 1) * RET_DK)

    def scores(unit):
        e, hh, si = unit
        return lax.dot_general(q_ref[e, si * sub:(si + 1) * sub, head_cols(hh)], k_ref[e, :, head_cols(hh)],
                               nt, preferred_element_type=F32)

    units = [(e, hh, si) for e in range(q_ref.shape[0]) for hh in range(heads) for si in range(tq // sub)]
    nxt = scores(units[0])
    for ui, (e, hh, si) in enumerate(units):
        s = nxt
        if ui + 1 < len(units):
            nxt = scores(units[ui + 1])
        lg_f, lg_b = lgs[hh]
        cols = head_cols(hh)
        rows = slice(si * sub, (si + 1) * sub)
        v = v_ref[e, :, cols]
        p = (s * d_ref[hh, qi, rows, :]).astype(BF16)
        o = jnp.dot(p, v, preferred_element_type=F32)
        if has_state:
            q = q_ref[e, rows, cols]
            i_col = (r0 + si * sub + lax.broadcasted_iota(jnp.int32, (sub, 1), 0)).astype(F32)
            o = o + (jnp.dot(q, s0_ref[e, 0, 0, hh].astype(BF16), preferred_element_type=F32)
                     * jnp.exp(lg_f * (i_col + 1.0))
                     + jnp.dot(q, s0_ref[e, 0, 1, hh].astype(BF16), preferred_element_type=F32)
                     * jnp.exp(lg_b * (n - i_col)))
        elif si == 0:
            eye = jnp.where(lax.broadcasted_iota(jnp.int32, (RET_DK, RET_DK), 0)
                            == lax.broadcasted_iota(jnp.int32, (RET_DK, RET_DK), 1), 1.0, 0.0).astype(BF16)
            kt = lax.dot_general(eye, k_ref[e, :, cols], nt, preferred_element_type=F32)
            j_row = lax.broadcasted_iota(jnp.int32, (1, n), 1).astype(F32)
            st_ref[e, 0, 0, hh] = jnp.dot((kt * jnp.exp(lg_f * (n - 1.0 - j_row))).astype(BF16), v,
                                          preferred_element_type=F32)
            st_ref[e, 0, 1, hh] = jnp.dot((kt * jnp.exp(lg_b * j_row)).astype(BF16), v,
                                          preferred_element_type=F32)
        mu = jnp.mean(o, axis=-1, keepdims=True)
        oc = o - mu
        var = jnp.mean(oc * oc, axis=-1, keepdims=True)
        on = oc * lax.rsqrt(var + EPS) * gn_ref[:, cols]
        gt = gate_ref[e, rows, cols]
        y_ref[e, rows, cols] = (gt * _sigmoid(gt) * on).astype(y_ref.dtype)


def _retention(q, k, v, gate, lg, gn, state0, heads, tq, elems=1):
    b, n, _ = q.shape
    assert n % tq == 0 and RET_HEADS % heads == 0 and (state0 is not None or tq == n) and b % elems == 0
    width = heads * RET_DK
    q_tile = pl.BlockSpec((elems, tq, width), lambda hg, bi, qi: (bi, qi, hg))
    seq = pl.BlockSpec((elems, n, width), lambda hg, bi, qi: (bi, 0, hg))
    st = pl.BlockSpec((elems, 1, 2, heads, RET_DK, RET_DV), lambda hg, bi, qi: (bi, 0, 0, hg, 0, 0))
    in_specs = [q_tile, seq, seq, q_tile,
                pl.BlockSpec((heads, 2, V7X_SUBLANES, V7X_LANES), lambda hg, bi, qi: (hg, 0, 0, 0)),
                pl.BlockSpec((1, width), lambda hg, bi, qi: (0, hg))]
    args = [q, k, v, gate, lg, gn]
    y_shape = jax.ShapeDtypeStruct((b, n, RET_HEADS * RET_DV), BF16)
    if state0 is not None:
        in_specs.append(st)
        args.append(state0)
        out_shape, out_specs = y_shape, q_tile
    else:
        out_shape = (y_shape, jax.ShapeDtypeStruct((b, 1, 2, RET_HEADS, RET_DK, RET_DV), F32))
        out_specs = (q_tile, st)
    return pl.pallas_call(
        functools.partial(_ret_kernel, n=n, tq=tq, heads=heads, has_state=state0 is not None),
        out_shape=out_shape,
        grid=(RET_HEADS // heads, b // elems, n // tq),
        in_specs=in_specs,
        out_specs=out_specs,
        scratch_shapes=[pltpu.VMEM((heads, n // tq, tq, n), F32)],
        compiler_params=_params(("arbitrary", "arbitrary", "arbitrary")),
        name="retention_state" if state0 is not None else "retention",
    )(*args)


def kernel(x_prompt, x_sample, cache_k, cache_v, state_ret, c, c_ctx, w_mod, b_mod, norm_g,
           attn_w_qkv, attn_q_norm, attn_k_norm, attn_w_o, ret_w_qkvg, ret_decay_logit, ret_gn_w,
           ret_w_o, mlp_w1, mlp_w2, final_norm_g):
    bp, sp, d = x_prompt.shape
    bs, ss, _ = x_sample.shape
    depth = w_mod.shape[0]
    assert d == D_MODEL and depth == 2 and bs % V7X_SUBLANES == 0
    assert (bp * sp) % POST_ROW_TILE == 0 and ss % POST_ROW_TILE == 0 and POST_ROW_TILE % ROW_TILE == 0

    mod, (wqkv,) = _modulation(c, c_ctx, w_mod, b_mod, casts=[(attn_w_qkv, 0)])
    ctx_mod = (0, bs)
    lat_mod = (ss, 0)

    xp = x_prompt.reshape(bp * sp, d)
    xs = x_sample.reshape(bs * ss, d)
    nkv = N_KV_HEADS * HEAD_DIM

    g0 = norm_g[0, 0][None, :]
    g1 = norm_g[0, 1][None, :]
    qn = attn_q_norm[0][None, :]
    kn = attn_k_norm[0][None, :]

    (qp, kp, vp), _ = _attn_in(xp, mod, (0,) + ctx_mod, g0, wqkv, qn, kn, None, PROJ_SUB_TILE)
    reorder = np.zeros((HEAD_DIM, HEAD_DIM), np.float32)
    reorder[_PAIRED_ORDER, np.arange(HEAD_DIM)] = 1.0
    reorder = jnp.asarray(reorder, BF16)
    (qs, ks, vs), (w1, wo) = _attn_in(xs, mod, (0,) + lat_mod, g0, wqkv, qn, kn,
                                      _rope_tables(ss, HEAD_DIM, True) + (reorder,), ROW_TILE,
                                      casts=[(mlp_w1, 0), (attn_w_o, 0)])

    ap, _ = _attention(qp.reshape(bp, sp, -1), kp.reshape(bp, sp * N_KV_HEADS, HEAD_DIM),
                       vp.reshape(bp, sp * N_KV_HEADS, HEAD_DIM))
    cache = (cache_k, cache_v, reorder)
    as_, (w2, wr) = _attention(qs.reshape(bs, ss, -1), ks.reshape(bs, ss, nkv), vs.reshape(bs, ss, nkv), cache,
                               casts=[(mlp_w2, 0), (ret_w_qkvg, 0)])
    xp, _ = _post(ap.reshape(bp * sp, -1), xp, mod, (0,) + ctx_mod, g1, wo, w1, w2)
    xs, _ = _post(as_.reshape(bs * ss, -1), xs, mod, (0,) + lat_mod, g1, wo, w1, w2)

    g0 = norm_g[1, 0][None, :]
    g1 = norm_g[1, 1][None, :]
    gn = ret_gn_w[0][None, :]
    gf = final_norm_g[None, :]
    lg = jnp.broadcast_to(ret_decay_logit[0].T[:, :, None, None],
                          (RET_HEADS, 2, V7X_SUBLANES, V7X_LANES))
    hw = RET_HEADS * RET_DK

    (q, k, v, gate), (w1,) = _ret_in(xp, mod, (1,) + ctx_mod, g0, wr, None, PROJ_SUB_TILE, casts=[(mlp_w1, 1)])
    yp, new_state = _retention(q.reshape(bp, sp, hw), k.reshape(bp, sp, hw), v.reshape(bp, sp, hw),
                               gate.reshape(bp, sp, hw), lg, gn, None, RET_HEADS, sp, RET_CTX_ELEMS)
    (q, k, v, gate), (w2, wo) = _ret_in(xs, mod, (1,) + lat_mod, g0, wr, _rope_tables(ss, RET_DK, False),
                                        ROW_TILE, casts=[(mlp_w2, 1), (ret_w_o, 0)])
    ys = _retention(q.reshape(bs, ss, hw), k.reshape(bs, ss, hw), v.reshape(bs, ss, hw),
                    gate.reshape(bs, ss, hw), lg, gn, state_ret, RET_STATE_HEADS, RET_Q_TILE)
    y_prompt, _ = _post(yp.reshape(bp * sp, hw), xp, mod, (1,) + ctx_mod, g1, wo, w1, w2, gf)
    y_sample, _ = _post(ys.reshape(bs * ss, hw), xs, mod, (1,) + lat_mod, g1, wo, w1, w2, gf)

    return (y_prompt.reshape(bp, sp, d),
            y_sample.reshape(bs, ss, d),
            kp.reshape(bp, 1, sp, N_KV_HEADS, HEAD_DIM),
            vp.reshape(bp, 1, sp, N_KV_HEADS, HEAD_DIM),
            new_state)
```

```python
import functools

import numpy as np
import jax
import jax.numpy as jnp
from jax import lax
from jax.experimental import pallas as pl
from jax.experimental.pallas import tpu as pltpu

F32 = jnp.float32
BF16 = jnp.bfloat16

D_MODEL = 1024
GRID_W = 64
N_HEADS = 8
N_KV_HEADS = 2
HEAD_DIM = 128
GROUP = N_HEADS // N_KV_HEADS
ROPE_BASE = 10000.0
RET_HEADS = 4
RET_DK = 256
RET_DV = 256
D_FF = 4 * D_MODEL
Q_SCALE = HEAD_DIM ** -0.5 * float(np.log2(np.e))
EPS = 1e-6
QKV_W = (N_HEADS + 2 * N_KV_HEADS) * HEAD_DIM
RET_W = 2 * RET_HEADS * RET_DK + 2 * RET_HEADS * RET_DV

V7X_LANES = 128
V7X_SUBLANES = 8
MOD_COL_TILE = 2048
ROW_TILE = 1024
PROJ_SUB_TILE = 512
POST_ROW_TILE = 1024
POST_SUB_TILE = 512
POST_FF_TILE = 1024
ATTN_Q_TILE = 1024
RET_Q_TILE = 1024
RET_SUB_TILE = 256
RET_STATE_HEADS = 2
RET_CTX_ELEMS = 4
ATTN_CTX_ROWS = 1024
VMEM_LIMIT = 56 * 1024 * 1024


def _params(sem, vmem=VMEM_LIMIT):
    return pltpu.CompilerParams(dimension_semantics=sem, vmem_limit_bytes=vmem)


def _const_spec(shape):
    nd = len(shape)
    return pl.BlockSpec(shape, lambda *_: (0,) * nd, pipeline_mode=pl.Buffered(1))


def _sigmoid(x):
    return 1.0 / (1.0 + jnp.exp(-x))


def _cast_plumbing(casts, steps):
    in_specs, out_specs, out_shapes = [], [], []
    for arr, layer in casts:
        _, r, c = arr.shape
        slab = r // steps
        assert slab * steps == r and slab % 16 == 0
        in_specs.append(pl.BlockSpec((1, slab, c), functools.partial(lambda *idx, l: (l, idx[0], 0), l=layer)))
        out_specs.append(pl.BlockSpec((slab, c), lambda *idx: (idx[0], 0)))
        out_shapes.append(jax.ShapeDtypeStruct((r, c), BF16))
    return in_specs, out_specs, out_shapes


def _with_casts(body, n_in, n_out, n_cast):
    def kernel(*refs):
        ins = refs[:n_in]
        cast_in = refs[n_in:n_in + n_cast]
        outs = refs[n_in + n_cast:n_in + n_cast + n_out]
        cast_out = refs[n_in + n_cast + n_out:n_in + 2 * n_cast + n_out]
        scratch = refs[n_in + 2 * n_cast + n_out:]
        for src, dst in zip(cast_in, cast_out):
            dst[...] = src[0].astype(dst.dtype)
        body(*ins, *outs, *scratch)
    return kernel


def _mod_kernel(c_ref, cctx_ref, w_ref, b_ref, o_ref):
    cond = jnp.concatenate([c_ref[...], jnp.broadcast_to(cctx_ref[...], (V7X_SUBLANES, D_MODEL))], axis=0)
    s = (cond * _sigmoid(cond)).astype(BF16)
    bias = b_ref[pl.ds(pl.program_id(0), 1), :]
    o_ref[0, :, 0, :] = jnp.dot(s, w_ref[0].astype(BF16), preferred_element_type=F32) + bias


def _modulation(c, c_ctx, w_mod, b_mod, casts=()):
    depth = w_mod.shape[0]
    rows = c.shape[0] + V7X_SUBLANES
    tn = MOD_COL_TILE
    c_in, c_out, c_shapes = _cast_plumbing(casts, depth)
    outs = pl.pallas_call(
        _with_casts(_mod_kernel, 4, 1, len(casts)),
        out_shape=tuple([jax.ShapeDtypeStruct((depth, rows, 1, 6 * D_MODEL), F32)] + c_shapes),
        grid=(depth, 6 * D_MODEL // tn),
        in_specs=[
            pl.BlockSpec(c.shape, lambda i, j: (0, 0)),
            pl.BlockSpec((1, D_MODEL), lambda i, j: (0, 0)),
            pl.BlockSpec((1, D_MODEL, tn), lambda i, j: (i, 0, j)),
            pl.BlockSpec((depth, tn), lambda i, j: (0, j)),
        ] + c_in,
        out_specs=tuple([pl.BlockSpec((1, rows, 1, tn), lambda i, j: (i, 0, 0, j))] + c_out),
        compiler_params=_params(("arbitrary", "arbitrary")),
        name="modulation",
    )(c, c_ctx.reshape(1, D_MODEL), w_mod, b_mod, *[arr for arr, _ in casts])
    return outs[0], list(outs[1:])


def _norm_modulate(x, g, mod_ref, which):
    y = x * lax.rsqrt(jnp.mean(x * x, axis=-1, keepdims=True) + EPS) * g
    shift = mod_ref[0, 0, :, (3 * which) * D_MODEL:(3 * which + 1) * D_MODEL]
    scale = mod_ref[0, 0, :, (3 * which + 1) * D_MODEL:(3 * which + 2) * D_MODEL]
    return y * (1.0 + scale) + shift


def _mod_spec(mod_rows, tile):
    layer, rows_per_mod_row, first_row = mod_rows
    if rows_per_mod_row:
        assert rows_per_mod_row % tile == 0
        tiles = rows_per_mod_row // tile
        return pl.BlockSpec((1, 1, 1, 6 * D_MODEL), lambda i: (layer, first_row + i // tiles, 0, 0))
    return pl.BlockSpec((1, 1, 1, 6 * D_MODEL), lambda i: (layer, first_row, 0, 0))


def _rope_tables(n, dim, paired):
    quarter = dim // 4
    rows = np.repeat(np.arange(n // GRID_W), GRID_W).astype(np.float64)
    cols = np.tile(np.arange(GRID_W), n // GRID_W).astype(np.float64)
    freqs = ROPE_BASE ** (-np.arange(quarter, dtype=np.float64) / quarter)
    ar = rows[:, None] * freqs
    ac = cols[:, None] * freqs
    if paired:
        cos = np.concatenate([np.cos(ar), np.cos(ac), np.cos(ar), np.cos(ac)], axis=1)
        sin = np.concatenate([-np.sin(ar), -np.sin(ac), np.sin(ar), np.sin(ac)], axis=1)
    else:
        cos = np.concatenate([np.cos(ar), np.cos(ar), np.cos(ac), np.cos(ac)], axis=1)
        sin = np.concatenate([-np.sin(ar), np.sin(ar), -np.sin(ac), np.sin(ac)], axis=1)
    return jnp.asarray(cos, F32), jnp.asarray(sin, F32)


_Q = HEAD_DIM // 4
_PAIRED_ORDER = np.concatenate([np.arange(0, _Q), np.arange(2 * _Q, 3 * _Q),
                                np.arange(_Q, 2 * _Q), np.arange(3 * _Q, 4 * _Q)])


def _paired_lanes(w):
    lane = lax.broadcasted_iota(jnp.int32, w.shape, 1)
    from_c = pltpu.roll(w, V7X_LANES - _Q, axis=1)
    from_b = pltpu.roll(w, _Q, axis=1)
    return jnp.where((lane >= _Q) & (lane < 2 * _Q), from_c,
                     jnp.where((lane >= 2 * _Q) & (lane < 3 * _Q), from_b, w))


def _swap_halves(t):
    return pltpu.roll(t, V7X_LANES // 2, axis=1)


def _attn_in_kernel(x_ref, mod_ref, g_ref, w_ref, qn_ref, kn_ref, *rest, use_rope):
    nq = N_HEADS * HEAD_DIM
    nk = N_KV_HEADS * HEAD_DIM
    if use_rope:
        cos_ref, sin_ref, reorder_ref, q_ref, k_ref, v_ref, wqk_ref = rest

        @pl.when(pl.program_id(0) == 0)
        def _():
            for head in range(N_HEADS + N_KV_HEADS):
                cols = slice(head * HEAD_DIM, (head + 1) * HEAD_DIM)
                wqk_ref[:, cols] = jnp.dot(w_ref[:, cols], reorder_ref[...],
                                           preferred_element_type=F32).astype(BF16)
    else:
        q_ref, k_ref, v_ref = rest
        wqk_ref = w_ref
    tm = PROJ_SUB_TILE
    n_sub = x_ref.shape[0] // tm
    pair_w = 2 * HEAD_DIM
    n_pairs = (N_HEADS + N_KV_HEADS) // 2
    qw = qn_ref[...] * Q_SCALE
    kw = kn_ref[...]
    if use_rope:
        qw = _paired_lanes(qw)
        kw = _paired_lanes(kw)
        ra = lax.broadcasted_iota(jnp.int32, (pair_w, pair_w), 0) // HEAD_DIM
        rb = lax.broadcasted_iota(jnp.int32, (pair_w, pair_w), 1) // HEAD_DIM
        head_sum = jnp.where(ra == rb, 1.0, 0.0).astype(BF16)

    def normed(sub):
        return _norm_modulate(x_ref[sub * tm:(sub + 1) * tm, :], g_ref[...], mod_ref, 0).astype(BF16)

    def put_kv(ref, sub, kh, val):
        if use_rope:
            ref[sub * tm:(sub + 1) * tm, kh * HEAD_DIM:(kh + 1) * HEAD_DIM] = val.astype(ref.dtype)
        else:
            ref[pl.ds(sub * tm * N_KV_HEADS + kh, tm, stride=N_KV_HEADS), :] = val.astype(ref.dtype)

    h_next = normed(0)
    for sub in range(n_sub):
        h = h_next
        if sub + 1 < n_sub:
            h_next = normed(sub + 1)
        rows = slice(sub * tm, (sub + 1) * tm)
        if use_rope:
            q_tabs = (qw * cos_ref[rows, :], _swap_halves(qw) * sin_ref[rows, :])
            k_tabs = (kw * cos_ref[rows, :], _swap_halves(kw) * sin_ref[rows, :])
        t2_next = jnp.dot(h, wqk_ref[:, 0:pair_w], preferred_element_type=F32)
        for pair in range(n_pairs):
            t2 = t2_next
            if pair + 1 < n_pairs:
                t2_next = jnp.dot(h, wqk_ref[:, (pair + 1) * pair_w:(pair + 2) * pair_w],
                                  preferred_element_type=F32)
            else:
                t2_next = jnp.dot(h, w_ref[:, nq + nk:], preferred_element_type=F32)
            if use_rope:
                ss2 = jnp.dot((t2 * t2).astype(BF16), head_sum, preferred_element_type=F32)
                r2 = lax.rsqrt(ss2 * (1.0 / HEAD_DIM) + EPS)
            for j in range(2):
                head = 2 * pair + j
                t = t2[:, j * HEAD_DIM:(j + 1) * HEAD_DIM]
                if use_rope:
                    r = r2[:, j * HEAD_DIM:(j + 1) * HEAD_DIM]
                    wc, ws = q_tabs if head < N_HEADS else k_tabs
                    t = (t * wc + _swap_halves(t) * ws) * r
                else:
                    r = lax.rsqrt(jnp.mean(t * t, axis=-1, keepdims=True) + EPS)
                    t = t * r * (qw if head < N_HEADS else kw)
                if head < N_HEADS:
                    q_ref[rows, head * HEAD_DIM:(head + 1) * HEAD_DIM] = t.astype(q_ref.dtype)
                else:
                    put_kv(k_ref, sub, head - N_HEADS, t)
        vv = t2_next
        for kh in range(N_KV_HEADS):
            put_kv(v_ref, sub, kh, vv[:, kh * HEAD_DIM:(kh + 1) * HEAD_DIM])


def _attn_in(x, mod, mod_rows, g, w, qn, kn, rope, tile, casts=()):
    rows = x.shape[0]
    nk = N_KV_HEADS * HEAD_DIM
    if rope is None:
        kv_shape = jax.ShapeDtypeStruct((rows * N_KV_HEADS, HEAD_DIM), F32)
        kv_spec = pl.BlockSpec((tile * N_KV_HEADS, HEAD_DIM), lambda i: (i, 0))
    else:
        kv_shape = jax.ShapeDtypeStruct((rows, nk), BF16)
        kv_spec = pl.BlockSpec((tile, nk), lambda i: (i, 0))
    in_specs = [
        pl.BlockSpec((tile, D_MODEL), lambda i: (i, 0)),
        _mod_spec(mod_rows, tile),
        _const_spec((1, D_MODEL)),
        _const_spec((D_MODEL, QKV_W)),
        _const_spec((1, HEAD_DIM)),
        _const_spec((1, HEAD_DIM)),
    ]
    args = [x, mod, g, w, qn, kn]
    scratch = []
    if rope is not None:
        cos, sin, reorder = rope
        seq_tiles = cos.shape[0] // tile
        in_specs += [pl.BlockSpec((tile, HEAD_DIM), lambda i: (i % seq_tiles, 0))] * 2
        in_specs.append(_const_spec((HEAD_DIM, HEAD_DIM)))
        args += [cos, sin, reorder]
        scratch = [pltpu.VMEM((D_MODEL, (N_HEADS + N_KV_HEADS) * HEAD_DIM), BF16)]
    steps = rows // tile
    c_in, c_out, c_shapes = _cast_plumbing(casts, steps)
    out_specs = [pl.BlockSpec((tile, N_HEADS * HEAD_DIM), lambda i: (i, 0)), kv_spec, kv_spec]
    out_shapes = [jax.ShapeDtypeStruct((rows, N_HEADS * HEAD_DIM), BF16), kv_shape, kv_shape]
    body = functools.partial(_attn_in_kernel, use_rope=rope is not None)
    outs = pl.pallas_call(
        _with_casts(body, len(args), len(out_shapes), len(casts)),
        out_shape=tuple(out_shapes + c_shapes),
        grid=(steps,),
        in_specs=in_specs + c_in,
        out_specs=tuple(out_specs + c_out),
        scratch_shapes=scratch,
        compiler_params=_params(("arbitrary",)),
        name="attn_in_rope" if rope is not None else "attn_in",
    )(*args, *[arr for arr, _ in casts])
    return outs[:3], list(outs[3:])


def _kv_head(ref, e, kv):
    if ref.dtype == BF16:
        return ref[e, :, kv * HEAD_DIM:(kv + 1) * HEAD_DIM]
    tokens = ref.shape[1] // N_KV_HEADS
    return ref[e, pl.ds(kv, tokens, stride=N_KV_HEADS), :].astype(BF16)


def _attn_kernel(q_ref, k_ref, v_ref, *rest, cached):
    def with_ones(v):
        return jnp.concatenate([v, jnp.ones_like(v)], axis=1)

    if cached:
        ck_hbm, cv_hbm, reorder_ref, o_ref, ck_scr, cv_scr, stage, sems = rest
        b = pl.program_id(0)
        n_b = pl.num_programs(0)

        def head_copies(bi, slot):
            return [pltpu.make_async_copy(src.at[bi, 0, :, kv, :], stage.at[slot, t, kv], sems.at[slot, t, kv])
                    for t, src in enumerate((ck_hbm, cv_hbm)) for kv in range(N_KV_HEADS)]

        @pl.when(pl.program_id(1) == 0)
        def _():
            slot = b % 2

            @pl.when(b == 0)
            def _():
                for cp in head_copies(0, 0):
                    cp.start()

            @pl.when(b + 1 < n_b)
            def _():
                for cp in head_copies(b + 1, 1 - slot):
                    cp.start()

            for cp in head_copies(b, slot):
                cp.wait()
            for kv in range(N_KV_HEADS):
                ck_scr[kv] = jnp.dot(stage[slot, 0, kv].astype(BF16), reorder_ref[...],
                                     preferred_element_type=F32).astype(BF16)
                cv_scr[kv] = with_ones(stage[slot, 1, kv].astype(BF16))
    else:
        (o_ref,) = rest
    nt = (((1,), (1,)), ((), ()))
    n_elem = q_ref.shape[0]
    ks = [[_kv_head(k_ref, e, kv) for kv in range(N_KV_HEADS)] for e in range(n_elem)]
    vs = [[with_ones(_kv_head(v_ref, e, kv)) for kv in range(N_KV_HEADS)] for e in range(n_elem)]

    def scores(unit):
        e, head = unit
        kv = head // GROUP
        q = q_ref[e, :, head * HEAD_DIM:(head + 1) * HEAD_DIM]
        s = lax.dot_general(q, ks[e][kv], nt, preferred_element_type=F32)
        sc = lax.dot_general(q, ck_scr[kv], nt, preferred_element_type=F32) if cached else None
        return s, sc

    units = [(e, head) for e in range(n_elem) for head in range(N_HEADS)]
    nxt = scores(units[0])
    for ui, (e, head) in enumerate(units):
        kv = head // GROUP
        s, sc = nxt
        if ui + 1 < len(units):
            nxt = scores(units[ui + 1])
        m = jnp.max(s, axis=-1, keepdims=True)
        if cached:
            m = jnp.maximum(m, jnp.max(sc, axis=-1, keepdims=True))
        o = jnp.dot(jnp.exp2(s - m).astype(BF16), vs[e][kv], preferred_element_type=F32)
        if cached:
            o = o + jnp.dot(jnp.exp2(sc - m).astype(BF16), cv_scr[kv], preferred_element_type=F32)
        o_ref[e, :, head * HEAD_DIM:(head + 1) * HEAD_DIM] = (o[:, :HEAD_DIM] / o[:, HEAD_DIM:]).astype(o_ref.dtype)


def _attention(q, k, v, cache=None, casts=()):
    b, n, width = q.shape
    tq = min(ATTN_Q_TILE, n)
    eb = 1 if cache is not None else max(1, min(b, ATTN_CTX_ROWS // n))
    assert b % eb == 0
    in_specs = [pl.BlockSpec((eb, tq, width), lambda bi, qi: (bi, qi, 0))]
    in_specs += [pl.BlockSpec((eb,) + k.shape[1:], lambda bi, qi: (bi, 0, 0))] * 2
    args = [q, k, v]
    scratch = []
    if cache is not None:
        past = cache[0].shape[2]
        in_specs += [pl.BlockSpec(memory_space=pl.ANY)] * 2
        in_specs.append(pl.BlockSpec((HEAD_DIM, HEAD_DIM), lambda bi, qi: (0, 0)))
        args += list(cache)
        scratch = [pltpu.VMEM((N_KV_HEADS, past, HEAD_DIM), BF16),
                   pltpu.VMEM((N_KV_HEADS, past, 2 * HEAD_DIM), BF16)]
        scratch += [pltpu.VMEM((2, 2, N_KV_HEADS, past, HEAD_DIM), F32),
                    pltpu.SemaphoreType.DMA((2, 2, N_KV_HEADS))]
    c_in, c_out, c_shapes = _cast_plumbing(casts, b // eb)
    body = functools.partial(_attn_kernel, cached=cache is not None)
    outs = pl.pallas_call(
        _with_casts(body, len(args), 1, len(casts)),
        out_shape=tuple([jax.ShapeDtypeStruct((b, n, width), BF16)] + c_shapes),
        grid=(b // eb, n // tq),
        in_specs=in_specs + c_in,
        out_specs=tuple([pl.BlockSpec((eb, tq, width), lambda bi, qi: (bi, qi, 0))] + c_out),
        scratch_shapes=scratch,
        compiler_params=_params(("arbitrary", "arbitrary")),
        name="attention_cached" if cache is not None else "attention",
    )(*args, *[arr for arr, _ in casts])
    return outs[0], list(outs[1:])


def _post_kernel(a_ref, x_ref, mod_ref, g_ref, wo_ref, w1_ref, w2_ref, *rest, final):
    if final:
        gf_ref, o_ref = rest
    else:
        (o_ref,) = rest
    gate1 = mod_ref[0, 0, :, 2 * D_MODEL:3 * D_MODEL]
    gate2 = mod_ref[0, 0, :, 5 * D_MODEL:6 * D_MODEL]
    n_ff = D_FF // POST_FF_TILE

    def attn_residual(t):
        rows = slice(t * POST_SUB_TILE, (t + 1) * POST_SUB_TILE)
        x = x_ref[rows, :] + gate1 * jnp.dot(a_ref[rows, :], wo_ref[...], preferred_element_type=F32)
        return x, _norm_modulate(x, g_ref[...], mod_ref, 1).astype(BF16)

    n_sub = x_ref.shape[0] // POST_SUB_TILE
    nxt = attn_residual(0)
    for t in range(n_sub):
        x, h = nxt
        if t + 1 < n_sub:
            nxt = attn_residual(t + 1)
        y = None
        for f in range(n_ff):
            u = jnp.dot(h, w1_ref[:, f * POST_FF_TILE:(f + 1) * POST_FF_TILE], preferred_element_type=F32)
            u = jnp.square(jnp.maximum(u, 0.0)).astype(BF16)
            part = jnp.dot(u, w2_ref[f * POST_FF_TILE:(f + 1) * POST_FF_TILE, :], preferred_element_type=F32)
            y = part if y is None else y + part
        x = x + gate2 * y
        if final:
            x = x * lax.rsqrt(jnp.mean(x * x, axis=-1, keepdims=True) + EPS) * gf_ref[...]
        o_ref[t * POST_SUB_TILE:(t + 1) * POST_SUB_TILE, :] = x


def _post(a, x, mod, mod_rows, g, wo, w1, w2, gf=None, casts=()):
    rows = x.shape[0]
    row_block = pl.BlockSpec((POST_ROW_TILE, D_MODEL), lambda i: (i, 0))
    in_specs = [
        row_block,
        row_block,
        _mod_spec(mod_rows, POST_ROW_TILE),
        _const_spec((1, D_MODEL)),
        _const_spec((D_MODEL, D_MODEL)),
        _const_spec((D_MODEL, D_FF)),
        _const_spec((D_FF, D_MODEL)),
    ]
    args = [a, x, mod, g, wo, w1, w2]
    if gf is not None:
        in_specs.append(_const_spec((1, D_MODEL)))
        args.append(gf)
    steps = rows // POST_ROW_TILE
    c_in, c_out, c_shapes = _cast_plumbing(casts, steps)
    body = functools.partial(_post_kernel, final=gf is not None)
    outs = pl.pallas_call(
        _with_casts(body, len(args), 1, len(casts)),
        out_shape=tuple([jax.ShapeDtypeStruct((rows, D_MODEL), F32)] + c_shapes),
        grid=(steps,),
        in_specs=in_specs + c_in,
        out_specs=tuple([row_block] + c_out),
        compiler_params=_params(("parallel",)),
        name="post_final" if gf is not None else "post",
    )(*args, *[arr for arr, _ in casts])
    return outs[0], list(outs[1:])


def _ret_in_kernel(x_ref, mod_ref, g_ref, w_ref, *rest, use_rope):
    if use_rope:
        cos_ref, sin_ref, q_ref, k_ref, v_ref, gate_ref = rest
    else:
        q_ref, k_ref, v_ref, gate_ref = rest
    width = RET_HEADS * RET_DK
    tm = PROJ_SUB_TILE
    n_sub = x_ref.shape[0] // tm

    def normed(sub):
        return _norm_modulate(x_ref[sub * tm:(sub + 1) * tm, :], g_ref[...], mod_ref, 0).astype(BF16)

    h_next = normed(0)
    for sub in range(n_sub):
        h = h_next
        if sub + 1 < n_sub:
            h_next = normed(sub + 1)
        rows = slice(sub * tm, (sub + 1) * tm)
        for part, ref in enumerate((q_ref, k_ref, v_ref, gate_ref)):
            t = jnp.dot(h, w_ref[:, part * width:(part + 1) * width], preferred_element_type=F32)
            if part < 2 and use_rope:
                for c in range(width // V7X_LANES):
                    half = (c % (RET_DK // V7X_LANES)) * V7X_LANES
                    tc = t[:, c * V7X_LANES:(c + 1) * V7X_LANES]
                    tc = (tc * cos_ref[rows, half:half + V7X_LANES]
                          + _swap_halves(tc) * sin_ref[rows, half:half + V7X_LANES])
                    if part == 0:
                        tc = tc * RET_DK ** -0.5
                    ref[rows, c * V7X_LANES:(c + 1) * V7X_LANES] = tc.astype(ref.dtype)
            else:
                if part == 0:
                    t = t * RET_DK ** -0.5
                ref[rows, :] = t.astype(ref.dtype)


def _ret_in(x, mod, mod_rows, g, w, rope, tile, casts=()):
    rows = x.shape[0]
    width = RET_HEADS * RET_DK
    in_specs = [
        pl.BlockSpec((tile, D_MODEL), lambda i: (i, 0)),
        _mod_spec(mod_rows, tile),
        _const_spec((1, D_MODEL)),
        _const_spec((D_MODEL, RET_W)),
    ]
    args = [x, mod, g, w]
    if rope is not None:
        seq_tiles = rope[0].shape[0] // tile
        in_specs += [pl.BlockSpec((tile, RET_DK), lambda i: (i % seq_tiles, 0))] * 2
        args += list(rope)
    out_block = pl.BlockSpec((tile, width), lambda i: (i, 0))
    steps = rows // tile
    c_in, c_out, c_shapes = _cast_plumbing(casts, steps)
    body = functools.partial(_ret_in_kernel, use_rope=rope is not None)
    outs = pl.pallas_call(
        _with_casts(body, len(args), 4, len(casts)),
        out_shape=tuple([jax.ShapeDtypeStruct((rows, width), BF16)] * 3
                        + [jax.ShapeDtypeStruct((rows, width), F32)] + c_shapes),
        grid=(steps,),
        in_specs=in_specs + c_in,
        out_specs=tuple([out_block] * 4 + c_out),
        compiler_params=_params(("parallel",)),
        name="ret_in_rope" if rope is not None else "ret_in",
    )(*args, *[arr for arr, _ in casts])
    return outs[:4], list(outs[4:])


def _log_sigmoid(x):
    return jnp.minimum(x, 0.0) - jnp.log(1.0 + jnp.exp(-jnp.abs(x)))


def _ret_kernel(q_ref, k_ref, v_ref, gate_ref, lg_ref, gn_ref, *rest, n, tq, heads, has_state):
    if has_state:
        s0_ref, y_ref, d_ref = rest
    else:
        y_ref, st_ref, d_ref = rest
    b = pl.program_id(1)
    qi = pl.program_id(2)
    r0 = qi * tq
    nt = (((1,), (1,)), ((), ()))
    sub = min(tq, RET_SUB_TILE)
    lgs = [(_log_sigmoid(lg_ref[hh, 0, 0:1, 0:1]), _log_sigmoid(lg_ref[hh, 1, 0:1, 0:1]))
           for hh in range(heads)]

    @pl.when(b == 0)
    def _():
        ii = r0 + lax.broadcasted_iota(jnp.int32, (tq, n), 0)
        jj = lax.broadcasted_iota(jnp.int32, (tq, n), 1)
        diff = (ii - jj).astype(F32)
        for hh in range(heads):
            lg_f, lg_b = lgs[hh]
            d_ref[hh, qi] = jnp.exp(jnp.where(diff >= 0.0, lg_f, -lg_b) * diff)

    def head_cols(hh):
        return slice(hh * RET_DK, (hh + 1) * RET_DK)

    def scores(unit):
        e, hh, si = unit
        return lax.dot_general(q_ref[e, si * sub:(si + 1) * sub, head_cols(hh)], k_ref[e, :, head_cols(hh)],
                               nt, preferred_element_type=F32)

    units = [(e, hh, si) for e in range(q_ref.shape[0]) for hh in range(heads) for si in range(tq // sub)]
    nxt = scores(units[0])
    for ui, (e, hh, si) in enumerate(units):
        s = nxt
        if ui + 1 < len(units):
            nxt = scores(units[ui + 1])
        lg_f, lg_b = lgs[hh]
        cols = head_cols(hh)
        rows = slice(si * sub, (si + 1) * sub)
        v = v_ref[e, :, cols]
        p = (s * d_ref[hh, qi, rows, :]).astype(BF16)
        o = jnp.dot(p, v, preferred_element_type=F32)
        if has_state:
            q = q_ref[e, rows, cols]
            i_col = (r0 + si * sub + lax.broadcasted_iota(jnp.int32, (sub, 1), 0)).astype(F32)
            o = o + (jnp.dot(q, s0_ref[e, 0, 0, hh].astype(BF16), preferred_element_type=F32)
                     * jnp.exp(lg_f * (i_col + 1.0))
                     + jnp.dot(q, s0_ref[e, 0, 1, hh].astype(BF16), preferred_element_type=F32)
                     * jnp.exp(lg_b * (n - i_col)))
        elif si == 0:
            eye = jnp.where(lax.broadcasted_iota(jnp.int32, (RET_DK, RET_DK), 0)
                            == lax.broadcasted_iota(jnp.int32, (RET_DK, RET_DK), 1), 1.0, 0.0).astype(BF16)
            kt = lax.dot_general(eye, k_ref[e, :, cols], nt, preferred_element_type=F32)
            j_row = lax.broadcasted_iota(jnp.int32, (1, n), 1).astype(F32)
            st_ref[e, 0, 0, hh] = jnp.dot((kt * jnp.exp(lg_f * (n - 1.0 - j_row))).astype(BF16), v,
                                          preferred_element_type=F32)
            st_ref[e, 0, 1, hh] = jnp.dot((kt * jnp.exp(lg_b * j_row)).astype(BF16), v,
                                          preferred_element_type=F32)
        mu = jnp.mean(o, axis=-1, keepdims=True)
        oc = o - mu
        var = jnp.mean(oc * oc, axis=-1, keepdims=True)
        on = oc * lax.rsqrt(var + EPS) * gn_ref[:, cols]
        gt = gate_ref[e, rows, cols]
        y_ref[e, rows, cols] = (gt * _sigmoid(gt) * on).astype(y_ref.dtype)


def _retention(q, k, v, gate, lg, gn, state0, heads, tq, elems=1):
    b, n, _ = q.shape
    assert n % tq == 0 and RET_HEADS % heads == 0 and (state0 is not None or tq == n) and b % elems == 0
    width = heads * RET_DK
    q_tile = pl.BlockSpec((elems, tq, width), lambda hg, bi, qi: (bi, qi, hg))
    seq = pl.BlockSpec((elems, n, width), lambda hg, bi, qi: (bi, 0, hg))
    st = pl.BlockSpec((elems, 1, 2, heads, RET_DK, RET_DV), lambda hg, bi, qi: (bi, 0, 0, hg, 0, 0))
    in_specs = [q_tile, seq, seq, q_tile,
                pl.BlockSpec((heads, 2, V7X_SUBLANES, V7X_LANES), lambda hg, bi, qi: (hg, 0, 0, 0)),
                pl.BlockSpec((1, width), lambda hg, bi, qi: (0, hg))]
    args = [q, k, v, gate, lg, gn]
    y_shape = jax.ShapeDtypeStruct((b, n, RET_HEADS * RET_DV), BF16)
    if state0 is not None:
        in_specs.append(st)
        args.append(state0)
        out_shape, out_specs = y_shape, q_tile
    else:
        out_shape = (y_shape, jax.ShapeDtypeStruct((b, 1, 2, RET_HEADS, RET_DK, RET_DV), F32))
        out_specs = (q_tile, st)
    return pl.pallas_call(
        functools.partial(_ret_kernel, n=n, tq=tq, heads=heads, has_state=state0 is not None),
        out_shape=out_shape,
        grid=(RET_HEADS // heads, b // elems, n // tq),
        in_specs=in_specs,
        out_specs=out_specs,
        scratch_shapes=[pltpu.VMEM((heads, n // tq, tq, n), F32)],
        compiler_params=_params(("arbitrary", "arbitrary", "arbitrary")),
        name="retention_state" if state0 is not None else "retention",
    )(*args)


def kernel(x_prompt, x_sample, cache_k, cache_v, state_ret, c, c_ctx, w_mod, b_mod, norm_g,
           attn_w_qkv, attn_q_norm, attn_k_norm, attn_w_o, ret_w_qkvg, ret_decay_logit, ret_gn_w,
           ret_w_o, mlp_w1, mlp_w2, final_norm_g):
    bp, sp, d = x_prompt.shape
    bs, ss, _ = x_sample.shape
    depth = w_mod.shape[0]
    assert d == D_MODEL and depth == 2 and bs % V7X_SUBLANES == 0
    assert (bp * sp) % POST_ROW_TILE == 0 and ss % POST_ROW_TILE == 0 and POST_ROW_TILE % ROW_TILE == 0

    mod, (wqkv,) = _modulation(c, c_ctx, w_mod, b_mod, casts=[(attn_w_qkv, 0)])
    ctx_mod = (0, bs)
    lat_mod = (ss, 0)

    xp = x_prompt.reshape(bp * sp, d)
    xs = x_sample.reshape(bs * ss, d)
    nkv = N_KV_HEADS * HEAD_DIM

    g0 = norm_g[0, 0][None, :]
    g1 = norm_g[0, 1][None, :]
    qn = attn_q_norm[0][None, :]
    kn = attn_k_norm[0][None, :]

    (qp, kp, vp), _ = _attn_in(xp, mod, (0,) + ctx_mod, g0, wqkv, qn, kn, None, PROJ_SUB_TILE)
    reorder = np.zeros((HEAD_DIM, HEAD_DIM), np.float32)
    reorder[_PAIRED_ORDER, np.arange(HEAD_DIM)] = 1.0
    reorder = jnp.asarray(reorder, BF16)
    (qs, ks, vs), (wo,) = _attn_in(xs, mod, (0,) + lat_mod, g0, wqkv, qn, kn,
                                   _rope_tables(ss, HEAD_DIM, True) + (reorder,), ROW_TILE,
                                   casts=[(attn_w_o, 0)])

    ap, _ = _attention(qp.reshape(bp, sp, -1), kp.reshape(bp, sp * N_KV_HEADS, HEAD_DIM),
                       vp.reshape(bp, sp * N_KV_HEADS, HEAD_DIM))
    cache = (cache_k, cache_v, reorder)
    as_, (w1, w2, wr) = _attention(qs.reshape(bs, ss, -1), ks.reshape(bs, ss, nkv), vs.reshape(bs, ss, nkv),
                                   cache, casts=[(mlp_w1, 0), (mlp_w2, 0), (ret_w_qkvg, 0)])
    xp, _ = _post(ap.reshape(bp * sp, -1), xp, mod, (0,) + ctx_mod, g1, wo, w1, w2)
    xs, _ = _post(as_.reshape(bs * ss, -1), xs, mod, (0,) + lat_mod, g1, wo, w1, w2)

    g0 = norm_g[1, 0][None, :]
    g1 = norm_g[1, 1][None, :]
    gn = ret_gn_w[0][None, :]
    gf = final_norm_g[None, :]
    lg = jnp.broadcast_to(ret_decay_logit[0].T[:, :, None, None],
                          (RET_HEADS, 2, V7X_SUBLANES, V7X_LANES))
    hw = RET_HEADS * RET_DK

    (q, k, v, gate), (w1,) = _ret_in(xp, mod, (1,) + ctx_mod, g0, wr, None, PROJ_SUB_TILE, casts=[(mlp_w1, 1)])
    yp, new_state = _retention(q.reshape(bp, sp, hw), k.reshape(bp, sp, hw), v.reshape(bp, sp, hw),
                               gate.reshape(bp, sp, hw), lg, gn, None, RET_HEADS, sp, RET_CTX_ELEMS)
    (q, k, v, gate), (w2, wo) = _ret_in(xs, mod, (1,) + lat_mod, g0, wr, _rope_tables(ss, RET_DK, False),
                                        ROW_TILE, casts=[(mlp_w2, 1), (ret_w_o, 0)])
    ys = _retention(q.reshape(bs, ss, hw), k.reshape(bs, ss, hw), v.reshape(bs, ss, hw),
                    gate.reshape(bs, ss, hw), lg, gn, state_ret, RET_STATE_HEADS, RET_Q_TILE)
    y_prompt, _ = _post(yp.reshape(bp * sp, hw), xp, mod, (1,) + ctx_mod, g1, wo, w1, w2, gf)
    y_sample, _ = _post(ys.reshape(bs * ss, hw), xs, mod, (1,) + lat_mod, g1, wo, w1, w2, gf)

    return (y_prompt.reshape(bp, sp, d),
            y_sample.reshape(bs, ss, d),
            kp.reshape(bp, 1, sp, N_KV_HEADS, HEAD_DIM),
            vp.reshape(bp, 1, sp, N_KV_HEADS, HEAD_DIM),
            new_state)
```

```python
import functools

import numpy as np
import jax
import jax.numpy as jnp
from jax import lax
from jax.experimental import pallas as pl
from jax.experimental.pallas import tpu as pltpu

F32 = jnp.float32
BF16 = jnp.bfloat16

D_MODEL = 1024
GRID_W = 64
N_HEADS = 8
N_KV_HEADS = 2
HEAD_DIM = 128
GROUP = N_HEADS // N_KV_HEADS
ROPE_BASE = 10000.0
RET_HEADS = 4
RET_DK = 256
RET_DV = 256
D_FF = 4 * D_MODEL
Q_SCALE = HEAD_DIM ** -0.5 * float(np.log2(np.e))
EPS = 1e-6
QKV_W = (N_HEADS + 2 * N_KV_HEADS) * HEAD_DIM
RET_W = 2 * RET_HEADS * RET_DK + 2 * RET_HEADS * RET_DV

V7X_LANES = 128
V7X_SUBLANES = 8
MOD_COL_TILE = 2048
ROW_TILE = 1024
PROJ_SUB_TILE = 512
POST_ROW_TILE = 1024
POST_SUB_TILE = 512
POST_FF_TILE = 1024
ATTN_Q_TILE = 1024
RET_Q_TILE = 1024
RET_SUB_TILE = 256
RET_KEY_CHUNK = 512
RET_STATE_HEADS = 2
RET_CTX_ELEMS = 4
ATTN_CTX_ROWS = 1024
VMEM_LIMIT = 56 * 1024 * 1024


def _params(sem, vmem=VMEM_LIMIT):
    return pltpu.CompilerParams(dimension_semantics=sem, vmem_limit_bytes=vmem)


def _const_spec(shape):
    nd = len(shape)
    return pl.BlockSpec(shape, lambda *_: (0,) * nd, pipeline_mode=pl.Buffered(1))


def _sigmoid(x):
    return 1.0 / (1.0 + jnp.exp(-x))


def _cast_plumbing(casts, steps):
    in_specs, out_specs, out_shapes = [], [], []
    for arr, layer in casts:
        _, r, c = arr.shape
        slab = r // steps
        assert slab * steps == r and slab % 16 == 0
        in_specs.append(pl.BlockSpec((1, slab, c), functools.partial(lambda *idx, l: (l, idx[0], 0), l=layer)))
        out_specs.append(pl.BlockSpec((slab, c), lambda *idx: (idx[0], 0)))
        out_shapes.append(jax.ShapeDtypeStruct((r, c), BF16))
    return in_specs, out_specs, out_shapes


def _with_casts(body, n_in, n_out, n_cast):
    def kernel(*refs):
        ins = refs[:n_in]
        cast_in = refs[n_in:n_in + n_cast]
        outs = refs[n_in + n_cast:n_in + n_cast + n_out]
        cast_out = refs[n_in + n_cast + n_out:n_in + 2 * n_cast + n_out]
        scratch = refs[n_in + 2 * n_cast + n_out:]
        for src, dst in zip(cast_in, cast_out):
            dst[...] = src[0].astype(dst.dtype)
        body(*ins, *outs, *scratch)
    return kernel


def _mod_kernel(c_ref, cctx_ref, w_ref, b_ref, o_ref):
    cond = jnp.concatenate([c_ref[...], jnp.broadcast_to(cctx_ref[...], (V7X_SUBLANES, D_MODEL))], axis=0)
    s = (cond * _sigmoid(cond)).astype(BF16)
    bias = b_ref[pl.ds(pl.program_id(0), 1), :]
    o_ref[0, :, 0, :] = jnp.dot(s, w_ref[0].astype(BF16), preferred_element_type=F32) + bias


def _modulation(c, c_ctx, w_mod, b_mod, casts=()):
    depth = w_mod.shape[0]
    rows = c.shape[0] + V7X_SUBLANES
    tn = MOD_COL_TILE
    c_in, c_out, c_shapes = _cast_plumbing(casts, depth)
    outs = pl.pallas_call(
        _with_casts(_mod_kernel, 4, 1, len(casts)),
        out_shape=tuple([jax.ShapeDtypeStruct((depth, rows, 1, 6 * D_MODEL), F32)] + c_shapes),
        grid=(depth, 6 * D_MODEL // tn),
        in_specs=[
            pl.BlockSpec(c.shape, lambda i, j: (0, 0)),
            pl.BlockSpec((1, D_MODEL), lambda i, j: (0, 0)),
            pl.BlockSpec((1, D_MODEL, tn), lambda i, j: (i, 0, j)),
            pl.BlockSpec((depth, tn), lambda i, j: (0, j)),
        ] + c_in,
        out_specs=tuple([pl.BlockSpec((1, rows, 1, tn), lambda i, j: (i, 0, 0, j))] + c_out),
        compiler_params=_params(("arbitrary", "arbitrary")),
        name="modulation",
    )(c, c_ctx.reshape(1, D_MODEL), w_mod, b_mod, *[arr for arr, _ in casts])
    return outs[0], list(outs[1:])


def _norm_modulate(x, g, mod_ref, which):
    y = x * lax.rsqrt(jnp.mean(x * x, axis=-1, keepdims=True) + EPS) * g
    shift = mod_ref[0, 0, :, (3 * which) * D_MODEL:(3 * which + 1) * D_MODEL]
    scale = mod_ref[0, 0, :, (3 * which + 1) * D_MODEL:(3 * which + 2) * D_MODEL]
    return y * (1.0 + scale) + shift


def _mod_spec(mod_rows, tile):
    layer, rows_per_mod_row, first_row = mod_rows
    if rows_per_mod_row:
        assert rows_per_mod_row % tile == 0
        tiles = rows_per_mod_row // tile
        return pl.BlockSpec((1, 1, 1, 6 * D_MODEL), lambda i: (layer, first_row + i // tiles, 0, 0))
    return pl.BlockSpec((1, 1, 1, 6 * D_MODEL), lambda i: (layer, first_row, 0, 0))


def _rope_tables(n, dim, paired):
    quarter = dim // 4
    rows = np.repeat(np.arange(n // GRID_W), GRID_W).astype(np.float64)
    cols = np.tile(np.arange(GRID_W), n // GRID_W).astype(np.float64)
    freqs = ROPE_BASE ** (-np.arange(quarter, dtype=np.float64) / quarter)
    ar = rows[:, None] * freqs
    ac = cols[:, None] * freqs
    if paired:
        cos = np.concatenate([np.cos(ar), np.cos(ac), np.cos(ar), np.cos(ac)], axis=1)
        sin = np.concatenate([-np.sin(ar), -np.sin(ac), np.sin(ar), np.sin(ac)], axis=1)
    else:
        cos = np.concatenate([np.cos(ar), np.cos(ar), np.cos(ac), np.cos(ac)], axis=1)
        sin = np.concatenate([-np.sin(ar), np.sin(ar), -np.sin(ac), np.sin(ac)], axis=1)
    return jnp.asarray(cos, F32), jnp.asarray(sin, F32)


_Q = HEAD_DIM // 4
_PAIRED_ORDER = np.concatenate([np.arange(0, _Q), np.arange(2 * _Q, 3 * _Q),
                                np.arange(_Q, 2 * _Q), np.arange(3 * _Q, 4 * _Q)])


def _paired_lanes(w):
    lane = lax.broadcasted_iota(jnp.int32, w.shape, 1)
    from_c = pltpu.roll(w, V7X_LANES - _Q, axis=1)
    from_b = pltpu.roll(w, _Q, axis=1)
    return jnp.where((lane >= _Q) & (lane < 2 * _Q), from_c,
                     jnp.where((lane >= 2 * _Q) & (lane < 3 * _Q), from_b, w))


def _swap_halves(t):
    return pltpu.roll(t, V7X_LANES // 2, axis=1)


def _attn_in_kernel(x_ref, mod_ref, g_ref, w_ref, qn_ref, kn_ref, *rest, use_rope):
    nq = N_HEADS * HEAD_DIM
    nk = N_KV_HEADS * HEAD_DIM
    if use_rope:
        cos_ref, sin_ref, reorder_ref, q_ref, k_ref, v_ref, wqk_ref = rest

        @pl.when(pl.program_id(0) == 0)
        def _():
            for head in range(N_HEADS + N_KV_HEADS):
                cols = slice(head * HEAD_DIM, (head + 1) * HEAD_DIM)
                wqk_ref[:, cols] = jnp.dot(w_ref[:, cols], reorder_ref[...],
                                           preferred_element_type=F32).astype(BF16)
    else:
        q_ref, k_ref, v_ref = rest
        wqk_ref = w_ref
    tm = PROJ_SUB_TILE
    n_sub = x_ref.shape[0] // tm
    pair_w = 2 * HEAD_DIM
    n_pairs = (N_HEADS + N_KV_HEADS) // 2
    qw = qn_ref[...] * Q_SCALE
    kw = kn_ref[...]
    if use_rope:
        qw = _paired_lanes(qw)
        kw = _paired_lanes(kw)
        ra = lax.broadcasted_iota(jnp.int32, (pair_w, pair_w), 0) // HEAD_DIM
        rb = lax.broadcasted_iota(jnp.int32, (pair_w, pair_w), 1) // HEAD_DIM
        head_sum = jnp.where(ra == rb, 1.0, 0.0).astype(BF16)

    def normed(sub):
        return _norm_modulate(x_ref[sub * tm:(sub + 1) * tm, :], g_ref[...], mod_ref, 0).astype(BF16)

    def put_kv(ref, sub, kh, val):
        if use_rope:
            ref[sub * tm:(sub + 1) * tm, kh * HEAD_DIM:(kh + 1) * HEAD_DIM] = val.astype(ref.dtype)
        else:
            ref[pl.ds(sub * tm * N_KV_HEADS + kh, tm, stride=N_KV_HEADS), :] = val.astype(ref.dtype)

    h_next = normed(0)
    for sub in range(n_sub):
        h = h_next
        if sub + 1 < n_sub:
            h_next = normed(sub + 1)
        rows = slice(sub * tm, (sub + 1) * tm)
        if use_rope:
            q_tabs = (qw * cos_ref[rows, :], _swap_halves(qw) * sin_ref[rows, :])
            k_tabs = (kw * cos_ref[rows, :], _swap_halves(kw) * sin_ref[rows, :])
        t2_next = jnp.dot(h, wqk_ref[:, 0:pair_w], preferred_element_type=F32)
        for pair in range(n_pairs):
            t2 = t2_next
            if pair + 1 < n_pairs:
                t2_next = jnp.dot(h, wqk_ref[:, (pair + 1) * pair_w:(pair + 2) * pair_w],
                                  preferred_element_type=F32)
            else:
                t2_next = jnp.dot(h, w_ref[:, nq + nk:], preferred_element_type=F32)
            if use_rope:
                ss2 = jnp.dot((t2 * t2).astype(BF16), head_sum, preferred_element_type=F32)
                r2 = lax.rsqrt(ss2 * (1.0 / HEAD_DIM) + EPS)
            for j in range(2):
                head = 2 * pair + j
                t = t2[:, j * HEAD_DIM:(j + 1) * HEAD_DIM]
                if use_rope:
                    r = r2[:, j * HEAD_DIM:(j + 1) * HEAD_DIM]
                    wc, ws = q_tabs if head < N_HEADS else k_tabs
                    t = (t * wc + _swap_halves(t) * ws) * r
                else:
                    r = lax.rsqrt(jnp.mean(t * t, axis=-1, keepdims=True) + EPS)
                    t = t * r * (qw if head < N_HEADS else kw)
                if head < N_HEADS:
                    q_ref[rows, head * HEAD_DIM:(head + 1) * HEAD_DIM] = t.astype(q_ref.dtype)
                else:
                    put_kv(k_ref, sub, head - N_HEADS, t)
        vv = t2_next
        for kh in range(N_KV_HEADS):
            put_kv(v_ref, sub, kh, vv[:, kh * HEAD_DIM:(kh + 1) * HEAD_DIM])


def _attn_in(x, mod, mod_rows, g, w, qn, kn, rope, tile, casts=()):
    rows = x.shape[0]
    nk = N_KV_HEADS * HEAD_DIM
    if rope is None:
        kv_shape = jax.ShapeDtypeStruct((rows * N_KV_HEADS, HEAD_DIM), F32)
        kv_spec = pl.BlockSpec((tile * N_KV_HEADS, HEAD_DIM), lambda i: (i, 0))
    else:
        kv_shape = jax.ShapeDtypeStruct((rows, nk), BF16)
        kv_spec = pl.BlockSpec((tile, nk), lambda i: (i, 0))
    in_specs = [
        pl.BlockSpec((tile, D_MODEL), lambda i: (i, 0)),
        _mod_spec(mod_rows, tile),
        _const_spec((1, D_MODEL)),
        _const_spec((D_MODEL, QKV_W)),
        _const_spec((1, HEAD_DIM)),
        _const_spec((1, HEAD_DIM)),
    ]
    args = [x, mod, g, w, qn, kn]
    scratch = []
    if rope is not None:
        cos, sin, reorder = rope
        seq_tiles = cos.shape[0] // tile
        in_specs += [pl.BlockSpec((tile, HEAD_DIM), lambda i: (i % seq_tiles, 0))] * 2
        in_specs.append(_const_spec((HEAD_DIM, HEAD_DIM)))
        args += [cos, sin, reorder]
        scratch = [pltpu.VMEM((D_MODEL, (N_HEADS + N_KV_HEADS) * HEAD_DIM), BF16)]
    steps = rows // tile
    c_in, c_out, c_shapes = _cast_plumbing(casts, steps)
    out_specs = [pl.BlockSpec((tile, N_HEADS * HEAD_DIM), lambda i: (i, 0)), kv_spec, kv_spec]
    out_shapes = [jax.ShapeDtypeStruct((rows, N_HEADS * HEAD_DIM), BF16), kv_shape, kv_shape]
    body = functools.partial(_attn_in_kernel, use_rope=rope is not None)
    outs = pl.pallas_call(
        _with_casts(body, len(args), len(out_shapes), len(casts)),
        out_shape=tuple(out_shapes + c_shapes),
        grid=(steps,),
        in_specs=in_specs + c_in,
        out_specs=tuple(out_specs + c_out),
        scratch_shapes=scratch,
        compiler_params=_params(("arbitrary",)),
        name="attn_in_rope" if rope is not None else "attn_in",
    )(*args, *[arr for arr, _ in casts])
    return outs[:3], list(outs[3:])


def _kv_head(ref, e, kv):
    if ref.dtype == BF16:
        return ref[e, :, kv * HEAD_DIM:(kv + 1) * HEAD_DIM]
    tokens = ref.shape[1] // N_KV_HEADS
    return ref[e, pl.ds(kv, tokens, stride=N_KV_HEADS), :].astype(BF16)


def _attn_kernel(q_ref, k_ref, v_ref, *rest, cached):
    def with_ones(v):
        return jnp.concatenate([v, jnp.ones_like(v)], axis=1)

    if cached:
        ck_hbm, cv_hbm, reorder_ref, o_ref, ck_scr, cv_scr, stage, sems = rest
        b = pl.program_id(0)
        n_b = pl.num_programs(0)

        def head_copies(bi, slot):
            return [pltpu.make_async_copy(src.at[bi, 0, :, kv, :], stage.at[slot, t, kv], sems.at[slot, t, kv])
                    for t, src in enumerate((ck_hbm, cv_hbm)) for kv in range(N_KV_HEADS)]

        @pl.when(pl.program_id(1) == 0)
        def _():
            slot = b % 2

            @pl.when(b == 0)
            def _():
                for cp in head_copies(0, 0):
                    cp.start()

            @pl.when(b + 1 < n_b)
            def _():
                for cp in head_copies(b + 1, 1 - slot):
                    cp.start()

            for cp in head_copies(b, slot):
                cp.wait()
            for kv in range(N_KV_HEADS):
                ck_scr[kv] = jnp.dot(stage[slot, 0, kv].astype(BF16), reorder_ref[...],
                                     preferred_element_type=F32).astype(BF16)
                cv_scr[kv] = with_ones(stage[slot, 1, kv].astype(BF16))
    else:
        (o_ref,) = rest
    nt = (((1,), (1,)), ((), ()))
    n_elem = q_ref.shape[0]
    ks = [[_kv_head(k_ref, e, kv) for kv in range(N_KV_HEADS)] for e in range(n_elem)]
    vs = [[with_ones(_kv_head(v_ref, e, kv)) for kv in range(N_KV_HEADS)] for e in range(n_elem)]

    def scores(unit):
        e, head = unit
        kv = head // GROUP
        q = q_ref[e, :, head * HEAD_DIM:(head + 1) * HEAD_DIM]
        s = lax.dot_general(q, ks[e][kv], nt, preferred_element_type=F32)
        sc = lax.dot_general(q, ck_scr[kv], nt, preferred_element_type=F32) if cached else None
        return s, sc

    units = [(e, head) for e in range(n_elem) for head in range(N_HEADS)]
    nxt = scores(units[0])
    for ui, (e, head) in enumerate(units):
        kv = head // GROUP
        s, sc = nxt
        if ui + 1 < len(units):
            nxt = scores(units[ui + 1])
        m = jnp.max(s, axis=-1, keepdims=True)
        if cached:
            m = jnp.maximum(m, jnp.max(sc, axis=-1, keepdims=True))
        o = jnp.dot(jnp.exp2(s - m).astype(BF16), vs[e][kv], preferred_element_type=F32)
        if cached:
            o = o + jnp.dot(jnp.exp2(sc - m).astype(BF16), cv_scr[kv], preferred_element_type=F32)
        o_ref[e, :, head * HEAD_DIM:(head + 1) * HEAD_DIM] = (o[:, :HEAD_DIM] / o[:, HEAD_DIM:]).astype(o_ref.dtype)


def _attention(q, k, v, cache=None, casts=()):
    b, n, width = q.shape
    tq = min(ATTN_Q_TILE, n)
    eb = 1 if cache is not None else max(1, min(b, ATTN_CTX_ROWS // n))
    assert b % eb == 0
    in_specs = [pl.BlockSpec((eb, tq, width), lambda bi, qi: (bi, qi, 0))]
    in_specs += [pl.BlockSpec((eb,) + k.shape[1:], lambda bi, qi: (bi, 0, 0))] * 2
    args = [q, k, v]
    scratch = []
    if cache is not None:
        past = cache[0].shape[2]
        in_specs += [pl.BlockSpec(memory_space=pl.ANY)] * 2
        in_specs.append(pl.BlockSpec((HEAD_DIM, HEAD_DIM), lambda bi, qi: (0, 0)))
        args += list(cache)
        scratch = [pltpu.VMEM((N_KV_HEADS, past, HEAD_DIM), BF16),
                   pltpu.VMEM((N_KV_HEADS, past, 2 * HEAD_DIM), BF16)]
        scratch += [pltpu.VMEM((2, 2, N_KV_HEADS, past, HEAD_DIM), F32),
                    pltpu.SemaphoreType.DMA((2, 2, N_KV_HEADS))]
    c_in, c_out, c_shapes = _cast_plumbing(casts, b // eb)
    body = functools.partial(_attn_kernel, cached=cache is not None)
    outs = pl.pallas_call(
        _with_casts(body, len(args), 1, len(casts)),
        out_shape=tuple([jax.ShapeDtypeStruct((b, n, width), BF16)] + c_shapes),
        grid=(b // eb, n // tq),
        in_specs=in_specs + c_in,
        out_specs=tuple([pl.BlockSpec((eb, tq, width), lambda bi, qi: (bi, qi, 0))] + c_out),
        scratch_shapes=scratch,
        compiler_params=_params(("arbitrary", "arbitrary")),
        name="attention_cached" if cache is not None else "attention",
    )(*args, *[arr for arr, _ in casts])
    return outs[0], list(outs[1:])


def _post_kernel(a_ref, x_ref, mod_ref, g_ref, wo_ref, w1_ref, w2_ref, *rest, final):
    if final:
        gf_ref, o_ref = rest
    else:
        (o_ref,) = rest
    gate1 = mod_ref[0, 0, :, 2 * D_MODEL:3 * D_MODEL]
    gate2 = mod_ref[0, 0, :, 5 * D_MODEL:6 * D_MODEL]
    n_ff = D_FF // POST_FF_TILE

    def attn_residual(t):
        rows = slice(t * POST_SUB_TILE, (t + 1) * POST_SUB_TILE)
        x = x_ref[rows, :] + gate1 * jnp.dot(a_ref[rows, :], wo_ref[...], preferred_element_type=F32)
        return x, _norm_modulate(x, g_ref[...], mod_ref, 1).astype(BF16)

    n_sub = x_ref.shape[0] // POST_SUB_TILE
    nxt = attn_residual(0)
    for t in range(n_sub):
        x, h = nxt
        if t + 1 < n_sub:
            nxt = attn_residual(t + 1)
        y = None
        for f in range(n_ff):
            u = jnp.dot(h, w1_ref[:, f * POST_FF_TILE:(f + 1) * POST_FF_TILE], preferred_element_type=F32)
            u = jnp.square(jnp.maximum(u, 0.0)).astype(BF16)
            part = jnp.dot(u, w2_ref[f * POST_FF_TILE:(f + 1) * POST_FF_TILE, :], preferred_element_type=F32)
            y = part if y is None else y + part
        x = x + gate2 * y
        if final:
            x = x * lax.rsqrt(jnp.mean(x * x, axis=-1, keepdims=True) + EPS) * gf_ref[...]
        o_ref[t * POST_SUB_TILE:(t + 1) * POST_SUB_TILE, :] = x


def _post(a, x, mod, mod_rows, g, wo, w1, w2, gf=None, casts=()):
    rows = x.shape[0]
    row_block = pl.BlockSpec((POST_ROW_TILE, D_MODEL), lambda i: (i, 0))
    in_specs = [
        row_block,
        row_block,
        _mod_spec(mod_rows, POST_ROW_TILE),
        _const_spec((1, D_MODEL)),
        _const_spec((D_MODEL, D_MODEL)),
        _const_spec((D_MODEL, D_FF)),
        _const_spec((D_FF, D_MODEL)),
    ]
    args = [a, x, mod, g, wo, w1, w2]
    if gf is not None:
        in_specs.append(_const_spec((1, D_MODEL)))
        args.append(gf)
    steps = rows // POST_ROW_TILE
    c_in, c_out, c_shapes = _cast_plumbing(casts, steps)
    body = functools.partial(_post_kernel, final=gf is not None)
    outs = pl.pallas_call(
        _with_casts(body, len(args), 1, len(casts)),
        out_shape=tuple([jax.ShapeDtypeStruct((rows, D_MODEL), F32)] + c_shapes),
        grid=(steps,),
        in_specs=in_specs + c_in,
        out_specs=tuple([row_block] + c_out),
        compiler_params=_params(("parallel",)),
        name="post_final" if gf is not None else "post",
    )(*args, *[arr for arr, _ in casts])
    return outs[0], list(outs[1:])


def _ret_in_kernel(x_ref, mod_ref, g_ref, w_ref, *rest, use_rope):
    if use_rope:
        cos_ref, sin_ref, q_ref, k_ref, v_ref, gate_ref = rest
    else:
        q_ref, k_ref, v_ref, gate_ref = rest
    width = RET_HEADS * RET_DK
    tm = PROJ_SUB_TILE
    n_sub = x_ref.shape[0] // tm

    def normed(sub):
        return _norm_modulate(x_ref[sub * tm:(sub + 1) * tm, :], g_ref[...], mod_ref, 0).astype(BF16)

    h_next = normed(0)
    for sub in range(n_sub):
        h = h_next
        if sub + 1 < n_sub:
            h_next = normed(sub + 1)
        rows = slice(sub * tm, (sub + 1) * tm)
        for part, ref in enumerate((q_ref, k_ref, v_ref, gate_ref)):
            t = jnp.dot(h, w_ref[:, part * width:(part + 1) * width], preferred_element_type=F32)
            if part < 2 and use_rope:
                for c in range(width // V7X_LANES):
                    half = (c % (RET_DK // V7X_LANES)) * V7X_LANES
                    tc = t[:, c * V7X_LANES:(c + 1) * V7X_LANES]
                    tc = (tc * cos_ref[rows, half:half + V7X_LANES]
                          + _swap_halves(tc) * sin_ref[rows, half:half + V7X_LANES])
                    if part == 0:
                        tc = tc * RET_DK ** -0.5
                    ref[rows, c * V7X_LANES:(c + 1) * V7X_LANES] = tc.astype(ref.dtype)
            else:
                if part == 0:
                    t = t * RET_DK ** -0.5
                ref[rows, :] = t.astype(ref.dtype)


def _ret_in(x, mod, mod_rows, g, w, rope, tile, casts=()):
    rows = x.shape[0]
    width = RET_HEADS * RET_DK
    in_specs = [
        pl.BlockSpec((tile, D_MODEL), lambda i: (i, 0)),
        _mod_spec(mod_rows, tile),
        _const_spec((1, D_MODEL)),
        _const_spec((D_MODEL, RET_W)),
    ]
    args = [x, mod, g, w]
    if rope is not None:
        seq_tiles = rope[0].shape[0] // tile
        in_specs += [pl.BlockSpec((tile, RET_DK), lambda i: (i % seq_tiles, 0))] * 2
        args += list(rope)
    out_block = pl.BlockSpec((tile, width), lambda i: (i, 0))
    steps = rows // tile
    c_in, c_out, c_shapes = _cast_plumbing(casts, steps)
    body = functools.partial(_ret_in_kernel, use_rope=rope is not None)
    outs = pl.pallas_call(
        _with_casts(body, len(args), 4, len(casts)),
        out_shape=tuple([jax.ShapeDtypeStruct((rows, width), BF16)] * 3
                        + [jax.ShapeDtypeStruct((rows, width), F32)] + c_shapes),
        grid=(steps,),
        in_specs=in_specs + c_in,
        out_specs=tuple([out_block] * 4 + c_out),
        compiler_params=_params(("parallel",)),
        name="ret_in_rope" if rope is not None else "ret_in",
    )(*args, *[arr for arr, _ in casts])
    return outs[:4], list(outs[4:])


def _log_sigmoid(x):
    return jnp.minimum(x, 0.0) - jnp.log(1.0 + jnp.exp(-jnp.abs(x)))


def _ret_kernel(q_ref, k_ref, v_ref, gate_ref, lg_ref, gn_ref, *rest, n, tq, heads, has_state):
    if has_state:
        s0_ref, y_ref, d_ref = rest
    else:
        y_ref, st_ref, d_ref = rest
    b = pl.program_id(1)
    qi = pl.program_id(2)
    r0 = qi * tq
    nt = (((1,), (1,)), ((), ()))
    sub = min(tq, RET_SUB_TILE)
    lgs = [(_log_sigmoid(lg_ref[hh, 0, 0:1, 0:1]), _log_sigmoid(lg_ref[hh, 1, 0:1, 0:1]))
           for hh in range(heads)]

    @pl.when(b == 0)
    def _():
        ii = r0 + lax.broadcasted_iota(jnp.int32, (tq, n), 0)
        jj = lax.broadcasted_iota(jnp.int32, (tq, n), 1)
        diff = (ii - jj).astype(F32)
        for hh in range(heads):
            lg_f, lg_b = lgs[hh]
            d_ref[hh, qi] = jnp.exp(jnp.where(diff >= 0.0, lg_f, -lg_b) * diff)

    def head_cols(hh):
        return slice(hh * RET_DK, (hh + 1) * RET_DK)

    kc = min(n, RET_KEY_CHUNK)
    key_chunks = [slice(c * kc, (c + 1) * kc) for c in range(n // kc)]

    def scores(unit):
        e, hh, si = unit
        q = q_ref[e, si * sub:(si + 1) * sub, head_cols(hh)]
        return [lax.dot_general(q, k_ref[e, ks, head_cols(hh)], nt, preferred_element_type=F32)
                for ks in key_chunks]

    units = [(e, hh, si) for e in range(q_ref.shape[0]) for hh in range(heads) for si in range(tq // sub)]
    nxt = scores(units[0])
    for ui, (e, hh, si) in enumerate(units):
        s = nxt
        if ui + 1 < len(units):
            nxt = scores(units[ui + 1])
        lg_f, lg_b = lgs[hh]
        cols = head_cols(hh)
        rows = slice(si * sub, (si + 1) * sub)
        v = v_ref[e, :, cols]
        o = None
        for s_c, ks in zip(s, key_chunks):
            part = jnp.dot((s_c * d_ref[hh, qi, rows, ks]).astype(BF16), v_ref[e, ks, cols],
                           preferred_element_type=F32)
            o = part if o is None else o + part
        if has_state:
            q = q_ref[e, rows, cols]
            i_col = (r0 + si * sub + lax.broadcasted_iota(jnp.int32, (sub, 1), 0)).astype(F32)
            o = o + (jnp.dot(q, s0_ref[e, 0, 0, hh].astype(BF16), preferred_element_type=F32)
                     * jnp.exp(lg_f * (i_col + 1.0))
                     + jnp.dot(q, s0_ref[e, 0, 1, hh].astype(BF16), preferred_element_type=F32)
                     * jnp.exp(lg_b * (n - i_col)))
        elif si == 0:
            eye = jnp.where(lax.broadcasted_iota(jnp.int32, (RET_DK, RET_DK), 0)
                            == lax.broadcasted_iota(jnp.int32, (RET_DK, RET_DK), 1), 1.0, 0.0).astype(BF16)
            kt = lax.dot_general(eye, k_ref[e, :, cols], nt, preferred_element_type=F32)
            j_row = lax.broadcasted_iota(jnp.int32, (1, n), 1).astype(F32)
            st_ref[e, 0, 0, hh] = jnp.dot((kt * jnp.exp(lg_f * (n - 1.0 - j_row))).astype(BF16), v,
                                          preferred_element_type=F32)
            st_ref[e, 0, 1, hh] = jnp.dot((kt * jnp.exp(lg_b * j_row)).astype(BF16), v,
                                          preferred_element_type=F32)
        mu = jnp.mean(o, axis=-1, keepdims=True)
        oc = o - mu
        var = jnp.mean(oc * oc, axis=-1, keepdims=True)
        on = oc * lax.rsqrt(var + EPS) * gn_ref[:, cols]
        gt = gate_ref[e, rows, cols]
        y_ref[e, rows, cols] = (gt * _sigmoid(gt) * on).astype(y_ref.dtype)


def _retention(q, k, v, gate, lg, gn, state0, heads, tq, elems=1):
    b, n, _ = q.shape
    assert n % tq == 0 and RET_HEADS % heads == 0 and (state0 is not None or tq == n) and b % elems == 0
    width = heads * RET_DK
    q_tile = pl.BlockSpec((elems, tq, width), lambda hg, bi, qi: (bi, qi, hg))
    seq = pl.BlockSpec((elems, n, width), lambda hg, bi, qi: (bi, 0, hg))
    st = pl.BlockSpec((elems, 1, 2, heads, RET_DK, RET_DV), lambda hg, bi, qi: (bi, 0, 0, hg, 0, 0))
    in_specs = [q_tile, seq, seq, q_tile,
                pl.BlockSpec((heads, 2, V7X_SUBLANES, V7X_LANES), lambda hg, bi, qi: (hg, 0, 0, 0)),
                pl.BlockSpec((1, width), lambda hg, bi, qi: (0, hg))]
    args = [q, k, v, gate, lg, gn]
    y_shape = jax.ShapeDtypeStruct((b, n, RET_HEADS * RET_DV), BF16)
    if state0 is not None:
        in_specs.append(st)
        args.append(state0)
        out_shape, out_specs = y_shape, q_tile
    else:
        out_shape = (y_shape, jax.ShapeDtypeStruct((b, 1, 2, RET_HEADS, RET_DK, RET_DV), F32))
        out_specs = (q_tile, st)
    return pl.pallas_call(
        functools.partial(_ret_kernel, n=n, tq=tq, heads=heads, has_state=state0 is not None),
        out_shape=out_shape,
        grid=(RET_HEADS // heads, b // elems, n // tq),
        in_specs=in_specs,
        out_specs=out_specs,
        scratch_shapes=[pltpu.VMEM((heads, n // tq, tq, n), F32)],
        compiler_params=_params(("arbitrary", "arbitrary", "arbitrary")),
        name="retention_state" if state0 is not None else "retention",
    )(*args)


def kernel(x_prompt, x_sample, cache_k, cache_v, state_ret, c, c_ctx, w_mod, b_mod, norm_g,
           attn_w_qkv, attn_q_norm, attn_k_norm, attn_w_o, ret_w_qkvg, ret_decay_logit, ret_gn_w,
           ret_w_o, mlp_w1, mlp_w2, final_norm_g):
    bp, sp, d = x_prompt.shape
    bs, ss, _ = x_sample.shape
    depth = w_mod.shape[0]
    assert d == D_MODEL and depth == 2 and bs % V7X_SUBLANES == 0
    assert (bp * sp) % POST_ROW_TILE == 0 and ss % POST_ROW_TILE == 0 and POST_ROW_TILE % ROW_TILE == 0

    mod, (wqkv,) = _modulation(c, c_ctx, w_mod, b_mod, casts=[(attn_w_qkv, 0)])
    ctx_mod = (0, bs)
    lat_mod = (ss, 0)

    xp = x_prompt.reshape(bp * sp, d)
    xs = x_sample.reshape(bs * ss, d)
    nkv = N_KV_HEADS * HEAD_DIM

    g0 = norm_g[0, 0][None, :]
    g1 = norm_g[0, 1][None, :]
    qn = attn_q_norm[0][None, :]
    kn = attn_k_norm[0][None, :]

    (qp, kp, vp), _ = _attn_in(xp, mod, (0,) + ctx_mod, g0, wqkv, qn, kn, None, PROJ_SUB_TILE)
    reorder = np.zeros((HEAD_DIM, HEAD_DIM), np.float32)
    reorder[_PAIRED_ORDER, np.arange(HEAD_DIM)] = 1.0
    reorder = jnp.asarray(reorder, BF16)
    (qs, ks, vs), (w1, wo) = _attn_in(xs, mod, (0,) + lat_mod, g0, wqkv, qn, kn,
                                      _rope_tables(ss, HEAD_DIM, True) + (reorder,), ROW_TILE,
                                      casts=[(mlp_w1, 0), (attn_w_o, 0)])

    ap, _ = _attention(qp.reshape(bp, sp, -1), kp.reshape(bp, sp * N_KV_HEADS, HEAD_DIM),
                       vp.reshape(bp, sp * N_KV_HEADS, HEAD_DIM))
    cache = (cache_k, cache_v, reorder)
    as_, (w2, wr) = _attention(qs.reshape(bs, ss, -1), ks.reshape(bs, ss, nkv), vs.reshape(bs, ss, nkv), cache,
                               casts=[(mlp_w2, 0), (ret_w_qkvg, 0)])
    xp, _ = _post(ap.reshape(bp * sp, -1), xp, mod, (0,) + ctx_mod, g1, wo, w1, w2)
    xs, _ = _post(as_.reshape(bs * ss, -1), xs, mod, (0,) + lat_mod, g1, wo, w1, w2)

    g0 = norm_g[1, 0][None, :]
    g1 = norm_g[1, 1][None, :]
    gn = ret_gn_w[0][None, :]
    gf = final_norm_g[None, :]
    lg = jnp.broadcast_to(ret_decay_logit[0].T[:, :, None, None],
                          (RET_HEADS, 2, V7X_SUBLANES, V7X_LANES))
    hw = RET_HEADS * RET_DK

    (q, k, v, gate), (w1,) = _ret_in(xp, mod, (1,) + ctx_mod, g0, wr, None, PROJ_SUB_TILE, casts=[(mlp_w1, 1)])
    yp, new_state = _retention(q.reshape(bp, sp, hw), k.reshape(bp, sp, hw), v.reshape(bp, sp, hw),
                               gate.reshape(bp, sp, hw), lg, gn, None, RET_HEADS, sp, RET_CTX_ELEMS)
    (q, k, v, gate), (w2, wo) = _ret_in(xs, mod, (1,) + lat_mod, g0, wr, _rope_tables(ss, RET_DK, False),
                                        ROW_TILE, casts=[(mlp_w2, 1), (ret_w_o, 0)])
    ys = _retention(q.reshape(bs, ss, hw), k.reshape(bs, ss, hw), v.reshape(bs, ss, hw),
                    gate.reshape(bs, ss, hw), lg, gn, state_ret, RET_STATE_HEADS, RET_Q_TILE)
    y_prompt, _ = _post(yp.reshape(bp * sp, hw), xp, mod, (1,) + ctx_mod, g1, wo, w1, w2, gf)
    y_sample, _ = _post(ys.reshape(bs * ss, hw), xs, mod, (1,) + lat_mod, g1, wo, w1, w2, gf)

    return (y_prompt.reshape(bp, sp, d),
            y_sample.reshape(bs, ss, d),
            kp.reshape(bp, 1, sp, N_KV_HEADS, HEAD_DIM),
            vp.reshape(bp, 1, sp, N_KV_HEADS, HEAD_DIM),
            new_state)
```
